```python
import math
import numpy as np
import jax
import jax.numpy as jnp
from jax import lax

D_MODEL = 1024
BATCH = 16
SEQ = 256
DEPTH = 2
DEC_BATCH = 4
DEC_SEQ = 1024
PAST_LEN = 512

GRID_W = 64
HEAD_DIM = 64
GROUP_W = D_MODEL // 4
MIX_W = 4 * GROUP_W
H_M = GROUP_W // HEAD_DIM
H_R = GROUP_W // HEAD_DIM
H_A = GROUP_W // HEAD_DIM
A_SUB = HEAD_DIM // 2
S5_GC = 16
S5_G = GROUP_W // S5_GC
S5_P = 64
D_FF = ((8 * D_MODEL // 3 + 255) // 256) * 256
CHUNK = 64
Q_BLOCK = 128
ROPE_BASE = 10000.0
EPS = 1e-6
PROJ_SIZES = (GROUP_W, GROUP_W, GROUP_W, GROUP_W, 2 * H_M, 2 * H_M,
              GROUP_W, GROUP_W, GROUP_W, GROUP_W,
              GROUP_W, GROUP_W, GROUP_W,
              GROUP_W)
PROJ_W = 12 * GROUP_W + 4 * H_M

kernel_name = "hybrid_diffusion_parallel_groups_step"


def rmsnorm(x, g):
    xf = x.astype(jnp.float32)
    y = xf * lax.rsqrt(jnp.mean(xf * xf, axis=-1, keepdims=True) + EPS)
    return (y * g.astype(jnp.float32)).astype(x.dtype)


def head_groupnorm(x, g):
    xf = x.astype(jnp.float32)
    mu = jnp.mean(xf, axis=-1, keepdims=True)
    xc = xf - mu
    var = jnp.mean(xc * xc, axis=-1, keepdims=True)
    return xc * lax.rsqrt(var + EPS) * g.astype(jnp.float32)


def _to_chunks(a, nc):
    b, h = a.shape[:2]
    return jnp.moveaxis(a.reshape(b, h, nc, CHUNK, *a.shape[3:]), 2, 0)


def _from_chunks(a):
    nc, b, h, l = a.shape[:4]
    return jnp.moveaxis(a, 0, 2).reshape(b, h, nc * l, *a.shape[4:])


def mlstm_chunkwise(q, k, v, ig, lf, c0, n0, m0):
    f32 = jnp.float32
    nc = q.shape[2] // CHUNK
    tri = jnp.tril(jnp.ones((CHUNK, CHUNK), dtype=bool))

    def step(carry, inp):
        c_st, n_st, m_st = carry
        qc, kc, vc, ic, fc = inp
        b = jnp.cumsum(fc, axis=-1)
        dlog = jnp.where(tri, b[..., :, None] - b[..., None, :] + ic[..., None, :], -jnp.inf)
        inter = b + m_st[..., None]
        m_row = jnp.maximum(jnp.max(dlog, axis=-1), inter)
        w = jnp.exp(dlog - m_row[..., None])
        a_inter = jnp.exp(inter - m_row)
        s = jnp.einsum('bhld,bhsd->bhls', qc, kc) * w
        num = jnp.einsum('bhls,bhse->bhle', s, vc) + a_inter[..., None] * jnp.einsum('bhld,bhde->bhle', qc, c_st)
        den = jnp.sum(s, axis=-1) + a_inter * jnp.einsum('bhld,bhd->bhl', qc, n_st)
        h = num / jnp.maximum(jnp.abs(den), jnp.exp(-m_row))[..., None]
        m_new = m_row[..., -1]
        wk = jnp.exp(b[..., -1:] - b + ic - m_new[..., None])
        dec = jnp.exp(b[..., -1] + m_st - m_new)
        c_new = dec[..., None, None] * c_st + jnp.einsum('bhs,bhsd,bhse->bhde', wk, kc, vc)
        n_new = dec[..., None] * n_st + jnp.einsum('bhs,bhsd->bhd', wk, kc)
        return (c_new, n_new, m_new), h

    xs = tuple(_to_chunks(a, nc) for a in (q, k, v, ig, lf))
    carry0 = (c0.astype(f32), n0.astype(f32), m0.astype(f32))
    (c_t, n_t, m_t), hs = lax.scan(step, carry0, xs)
    return _from_chunks(hs), (c_t, n_t, m_t)


def retention_chunkwise(q, k, v, log_g, s0):
    nc = q.shape[2] // CHUNK
    pos = jnp.arange(CHUNK, dtype=jnp.float32)
    diff = pos[:, None] - pos[None, :]
    tri = diff >= 0
    dmat = jnp.where(tri, jnp.exp(jnp.where(tri, diff, 0.0)[None] * log_g[:, None, None]), 0.0)
    xi = jnp.exp((pos + 1.0)[None] * log_g[:, None])
    zeta = jnp.exp((CHUNK - 1.0 - pos)[None] * log_g[:, None])
    g_chunk = jnp.exp(CHUNK * log_g)[:, None, None]

    def step(s, inp):
        qc, kc, vc = inp
        att = jnp.einsum('bhld,bhsd->bhls', qc, kc) * dmat
        o = jnp.einsum('bhls,bhse->bhle', att, vc) + xi[..., None] * jnp.einsum('bhld,bhde->bhle', qc, s)
        s_new = g_chunk * s + jnp.einsum('hs,bhsd,bhse->bhde', zeta, kc, vc)
        return s_new, o

    s_t, os_ = lax.scan(step, s0.astype(jnp.float32), tuple(_to_chunks(a, nc) for a in (q, k, v)))
    return _from_chunks(os_), s_t


def axial_rope_tables(n_tokens):
    rows = n_tokens // GRID_W
    row = jnp.repeat(jnp.arange(rows, dtype=jnp.float32), GRID_W)
    col = jnp.tile(jnp.arange(GRID_W, dtype=jnp.float32), rows)
    n_freq = A_SUB // 4
    inv = ROPE_BASE ** (-jnp.arange(n_freq, dtype=jnp.float32) / n_freq)
    ang = jnp.stack([row[:, None] * inv, col[:, None] * inv], axis=1)
    return jnp.cos(ang), jnp.sin(ang)


def apply_axial_rope(x, cos, sin):
    b, t, h, m, _ = x.shape
    xr = x.astype(jnp.float32).reshape(b, t, h, m, 2, 2, A_SUB // 4)
    x1, x2 = xr[..., 0, :], xr[..., 1, :]
    c = cos[None, :, None, None]
    s = sin[None, :, None, None]
    out = jnp.stack([x1 * c - x2 * s, x2 * c + x1 * s], axis=-2)
    return out.reshape(x.shape).astype(x.dtype)


def diff_attend(q, k, v, lam):
    b, tq = q.shape[:2]
    nb = tq // Q_BLOCK
    qb = jnp.moveaxis(q.reshape(b, nb, Q_BLOCK, *q.shape[2:]), 1, 0)
    scale = A_SUB ** -0.5

    def blk(qi):
        s = jnp.einsum('bqhcd,bkhcd->bhcqk', qi, k).astype(jnp.float32) * scale
        p = jax.nn.softmax(s, axis=-1)
        a = p[:, :, 0] - lam * p[:, :, 1]
        return jnp.einsum('bhqk,bkhd->bqhd', a, v.astype(jnp.float32))

    o = lax.map(blk, qb)
    return jnp.moveaxis(o, 0, 1).reshape(b, tq, H_A, HEAD_DIM)


def s5_discretize(lre, lim, log_dt, bre, bim, cre, cim):
    f32 = jnp.float32
    lam = lax.complex(lre.astype(f32), lim.astype(f32))
    dt = jnp.exp(log_dt.astype(f32))[:, None]
    lam_bar = jnp.exp(lam * dt)
    bmat = lax.complex(bre.astype(f32), bim.astype(f32))
    b_bar = ((lam_bar - 1.0) / lam)[..., None] * bmat
    cmat = lax.complex(cre.astype(f32), cim.astype(f32))
    return lam_bar, b_bar, cmat


def s5_scan(u, lam_bar, b_bar, cmat, x0):
    bu = jnp.einsum('gpc,btgc->btgp', b_bar, u.astype(jnp.complex64))
    bu = bu.at[:, 0].add(lam_bar * x0)
    a = jnp.broadcast_to(lam_bar, bu.shape)

    def combine(e1, e2):
        a1, b1 = e1
        a2, b2 = e2
        return a1 * a2, a2 * b1 + b2

    _, xs = lax.associative_scan(combine, (a, bu), axis=1)
    y = jnp.einsum('gcp,btgp->btgc', cmat, xs).real
    return y, xs[:, -1]


def mixer(h, p, lam_init, st):
    f32 = jnp.float32
    B, T, _ = h.shape
    split_idx = np.cumsum(PROJ_SIZES)[:-1].tolist()
    z = jnp.einsum('btd,de->bte', h, p['w_in'])
    (mq, mk, mv, mo, mi, mf, rq, rk, rv, rg, aq, ak, av, su) = jnp.split(z, split_idx, axis=-1)

    def heads(a):
        return a.astype(f32).reshape(B, T, -1, HEAD_DIM).transpose(0, 2, 1, 3)

    def tflip(a):
        return jnp.flip(a, axis=2)

    qm, km, vm = heads(mq), heads(mk) * HEAD_DIM ** -0.5, heads(mv)
    i_bias, f_bias = jnp.split(p['m_gate_bias'], 2)
    ig = (mi + i_bias).astype(f32).reshape(B, T, 2, H_M).transpose(2, 0, 3, 1)
    lf = jax.nn.log_sigmoid((mf + f_bias).astype(f32)).reshape(B, T, 2, H_M).transpose(2, 0, 3, 1)
    if st is None:
        zero_m = (jnp.zeros((B, H_M, HEAD_DIM, HEAD_DIM), f32), jnp.zeros((B, H_M, HEAD_DIM), f32),
                  jnp.zeros((B, H_M), f32))
        m_init = [zero_m, zero_m]
    else:
        m_init = [(st[0][:, d], st[1][:, d], st[2][:, d]) for d in range(2)]
    hf, msf = mlstm_chunkwise(qm, km, vm, ig[0], lf[0], *m_init[0])
    hb, msb = mlstm_chunkwise(tflip(qm), tflip(km), tflip(vm), tflip(ig[1]), tflip(lf[1]), *m_init[1])
    hm = (hf + tflip(hb)).transpose(0, 2, 1, 3)
    o_gate = jax.nn.sigmoid(mo.astype(f32)).reshape(B, T, H_M, HEAD_DIM)
    m_out = head_groupnorm(o_gate * hm, p['m_norm'].reshape(H_M, HEAD_DIM)).reshape(B, T, GROUP_W)

    qr, kr, vr = heads(rq), heads(rk) * HEAD_DIM ** -0.5, heads(rv)
    log_g = jax.nn.log_sigmoid(p['r_decay_logit'].astype(f32))
    if st is None:
        r_init = [jnp.zeros((B, H_R, HEAD_DIM, HEAD_DIM), f32)] * 2
    else:
        r_init = [st[3][:, d] for d in range(2)]
    of, rsf = retention_chunkwise(qr, kr, vr, log_g[0], r_init[0])
    ob, rsb = retention_chunkwise(tflip(qr), tflip(kr), tflip(vr), log_g[1], r_init[1])
    ro = (of + tflip(ob)).transpose(0, 2, 1, 3)
    r_out = jax.nn.silu(rg.astype(f32)) * head_groupnorm(ro, p['r_norm'].reshape(H_R, HEAD_DIM)).reshape(B, T, GROUP_W)

    qa = aq.reshape(B, T, H_A, 2, A_SUB)
    ka = ak.reshape(B, T, H_A, 2, A_SUB)
    va = av.reshape(B, T, H_A, HEAD_DIM)
    if st is None:
        k_all, v_all = ka, va
    else:
        cos, sin = axial_rope_tables(T)
        qa = apply_axial_rope(qa, cos, sin)
        k_ctx = st[4].reshape(B, -1, H_A, 2, A_SUB)
        k_all = jnp.concatenate([k_ctx.astype(f32), apply_axial_rope(ka, cos, sin).astype(f32)], axis=1)
        v_all = jnp.concatenate([st[5].astype(f32), va.astype(f32)], axis=1)
    lam = (jnp.exp(jnp.sum(p['a_lam_q1'] * p['a_lam_k1']).astype(f32))
           - jnp.exp(jnp.sum(p['a_lam_q2'] * p['a_lam_k2']).astype(f32)) + lam_init)
    ao = diff_attend(qa, k_all, v_all, lam)
    a_out = (rmsnorm(ao, p['a_norm'].reshape(H_A, HEAD_DIM)) * (1.0 - lam_init)).reshape(B, T, GROUP_W)

    u = su.astype(f32).reshape(B, T, S5_G, S5_GC)
    y_s5 = p['s5_d'].astype(f32).reshape(S5_G, S5_GC) * u
    s5_last = []
    for d in range(2):
        lam_bar, b_bar, cmat = s5_discretize(p['s5_lam_re'][d], p['s5_lam_im'][d], p['s5_log_dt'][d],
                                             p['s5_b_re'][d], p['s5_b_im'][d], p['s5_c_re'][d], p['s5_c_im'][d])
        if st is None:
            x0 = jnp.zeros((B, S5_G, S5_P), jnp.complex64)
        else:
            x0 = lax.complex(st[6][:, d].astype(f32), st[7][:, d].astype(f32))
        ud = u if d == 0 else jnp.flip(u, axis=1)
        yd, xl = s5_scan(ud, lam_bar, b_bar, cmat, x0)
        y_s5 = y_s5 + (yd if d == 0 else jnp.flip(yd, axis=1))
        s5_last.append(xl)
    gs = jax.nn.gelu(y_s5.reshape(B, T, GROUP_W))
    s_out = gs * jax.nn.sigmoid(jnp.einsum('btc,ce->bte', gs, p['s5_w_glu'].astype(f32)) + p['s5_b_glu'].astype(f32))

    mixed = jnp.concatenate([m_out, r_out, a_out, s_out], axis=-1).astype(h.dtype)
    out = jnp.einsum('btc,cd->btd', mixed, p['w_out'])
    if st is not None:
        return out, None
    new_st = (jnp.stack([msf[0], msb[0]], axis=1), jnp.stack([msf[1], msb[1]], axis=1),
              jnp.stack([msf[2], msb[2]], axis=1), jnp.stack([rsf, rsb], axis=1),
              ka.reshape(B, T, H_A, HEAD_DIM), va,
              jnp.stack([s5_last[0].real, s5_last[1].real], axis=1),
              jnp.stack([s5_last[0].imag, s5_last[1].imag], axis=1))
    return out, new_st


def block(x, cond, p, lam_init, st):
    mod = jnp.einsum('bd,de->be', jax.nn.silu(cond), p['w_ada']) + p['b_ada']
    sh1, sc1, g1, sh2, sc2, g2 = jnp.split(mod[:, None, :], 6, axis=-1)
    h = rmsnorm(x, p['n_mix_pre']) * (1.0 + sc1) + sh1
    y, new_st = mixer(h, p, lam_init, st)
    x = x + g1 * rmsnorm(y, p['n_mix_post'])
    h = rmsnorm(x, p['n_ffn_pre']) * (1.0 + sc2) + sh2
    f = jnp.einsum('btf,fd->btd', jax.nn.silu(h @ p['w_gate']) * (h @ p['w_up']), p['w_down'])
    x = x + g2 * rmsnorm(f, p['n_ffn_post'])
    return x, new_st


def setup_inputs(seed: int = 0) -> dict:
    key = jax.random.key(seed)
    ks = iter(jax.random.split(key, 64))
    f32 = jnp.float32
    L = DEPTH

    def nrm(shape, s=1.0):
        return s * jax.random.normal(next(ks), shape, f32)

    d = {}
    d['x_prompt'] = nrm((BATCH, SEQ, D_MODEL))
    d['x_sample'] = nrm((DEC_BATCH, DEC_SEQ, D_MODEL))
    d['state_mlstm_C'] = nrm((DEC_BATCH, L, 2, H_M, HEAD_DIM, HEAD_DIM), 0.5)
    d['state_mlstm_n'] = nrm((DEC_BATCH, L, 2, H_M, HEAD_DIM), 0.5)
    d['state_mlstm_m'] = nrm((DEC_BATCH, L, 2, H_M), 1.0)
    d['state_ret'] = nrm((DEC_BATCH, L, 2, H_R, HEAD_DIM, HEAD_DIM), 0.5)
    d['cache_diff_k'] = nrm((DEC_BATCH, L, PAST_LEN, H_A, HEAD_DIM))
    d['cache_diff_v'] = nrm((DEC_BATCH, L, PAST_LEN, H_A, HEAD_DIM))
    d['state_s5_re'] = nrm((DEC_BATCH, L, 2, S5_G, S5_P), 0.1)
    d['state_s5_im'] = nrm((DEC_BATCH, L, 2, S5_G, S5_P), 0.1)
    d['c'] = nrm((DEC_BATCH, D_MODEL))
    d['c_ctx'] = nrm((D_MODEL,))
    d['w_ada'] = nrm((L, D_MODEL, 6 * D_MODEL), 0.5 * D_MODEL ** -0.5)
    d['b_ada'] = nrm((L, 6 * D_MODEL), 0.02)
    d['n_mix_pre'] = 1.0 + nrm((L, D_MODEL), 0.05)
    d['n_mix_post'] = 1.0 + nrm((L, D_MODEL), 0.05)
    d['n_ffn_pre'] = 1.0 + nrm((L, D_MODEL), 0.05)
    d['n_ffn_post'] = 1.0 + nrm((L, D_MODEL), 0.05)
    d['w_in'] = nrm((L, D_MODEL, PROJ_W), D_MODEL ** -0.5)
    d['w_out'] = nrm((L, MIX_W, D_MODEL), MIX_W ** -0.5)
    f_bias = jnp.tile(jnp.linspace(3.0, 6.0, H_M, dtype=f32), 2)[None] + nrm((L, 2 * H_M), 0.1)
    d['m_gate_bias'] = jnp.concatenate([nrm((L, 2 * H_M), 0.1), f_bias], axis=-1)
    d['m_norm'] = 1.0 + nrm((L, GROUP_W), 0.05)
    d['r_decay_logit'] = jnp.log(2.0 ** (5.0 + jnp.arange(H_R, dtype=f32)) - 1.0) + nrm((L, 2, H_R), 0.1)
    d['r_norm'] = 1.0 + nrm((L, GROUP_W), 0.05)
    d['a_lam_q1'] = nrm((L, A_SUB), 0.1)
    d['a_lam_k1'] = nrm((L, A_SUB), 0.1)
    d['a_lam_q2'] = nrm((L, A_SUB), 0.1)
    d['a_lam_k2'] = nrm((L, A_SUB), 0.1)
    d['a_norm'] = 1.0 + nrm((L, GROUP_W), 0.05)
    d['s5_lam_re'] = -0.5 + nrm((L, 2, S5_G, S5_P), 0.01)
    d['s5_lam_im'] = math.pi * jnp.arange(S5_P, dtype=f32) + nrm((L, 2, S5_G, S5_P), 0.01)
    d['s5_log_dt'] = jax.random.uniform(next(ks), (L, 2, S5_G), f32, math.log(1e-3), math.log(1e-1))
    d['s5_b_re'] = nrm((L, 2, S5_G, S5_P, S5_GC), (2 * S5_GC) ** -0.5)
    d['s5_b_im'] = nrm((L, 2, S5_G, S5_P, S5_GC), (2 * S5_GC) ** -0.5)
    d['s5_c_re'] = nrm((L, 2, S5_G, S5_GC, S5_P), 0.5)
    d['s5_c_im'] = nrm((L, 2, S5_G, S5_GC, S5_P), 0.5)
    d['s5_d'] = nrm((L, GROUP_W))
    d['s5_w_glu'] = nrm((L, GROUP_W, GROUP_W), GROUP_W ** -0.5)
    d['s5_b_glu'] = nrm((L, GROUP_W), 0.02)
    d['w_ffn_gate'] = nrm((L, D_MODEL, D_FF), D_MODEL ** -0.5)
    d['w_ffn_up'] = nrm((L, D_MODEL, D_FF), D_MODEL ** -0.5)
    d['w_ffn_down'] = nrm((L, D_FF, D_MODEL), D_FF ** -0.5)
    return d


def reference(x_prompt, x_sample, state_mlstm_C, state_mlstm_n, state_mlstm_m, state_ret,
              cache_diff_k, cache_diff_v, state_s5_re, state_s5_im, c, c_ctx,
              w_ada, b_ada, n_mix_pre, n_mix_post, n_ffn_pre, n_ffn_post, w_in, w_out,
              m_gate_bias, m_norm, r_decay_logit, r_norm, a_lam_q1, a_lam_k1, a_lam_q2, a_lam_k2,
              a_norm, s5_lam_re, s5_lam_im, s5_log_dt, s5_b_re, s5_b_im, s5_c_re, s5_c_im, s5_d,
              s5_w_glu, s5_b_glu, w_ffn_gate, w_ffn_up, w_ffn_down):
    params = [dict(w_ada=w_ada[l], b_ada=b_ada[l], n_mix_pre=n_mix_pre[l], n_mix_post=n_mix_post[l],
                   n_ffn_pre=n_ffn_pre[l], n_ffn_post=n_ffn_post[l], w_in=w_in[l], w_out=w_out[l],
                   m_gate_bias=m_gate_bias[l], m_norm=m_norm[l], r_decay_logit=r_decay_logit[l],
                   r_norm=r_norm[l], a_lam_q1=a_lam_q1[l], a_lam_k1=a_lam_k1[l], a_lam_q2=a_lam_q2[l],
                   a_lam_k2=a_lam_k2[l], a_norm=a_norm[l], s5_lam_re=s5_lam_re[l], s5_lam_im=s5_lam_im[l],
                   s5_log_dt=s5_log_dt[l], s5_b_re=s5_b_re[l], s5_b_im=s5_b_im[l], s5_c_re=s5_c_re[l],
                   s5_c_im=s5_c_im[l], s5_d=s5_d[l], s5_w_glu=s5_w_glu[l], s5_b_glu=s5_b_glu[l],
                   w_gate=w_ffn_gate[l], w_up=w_ffn_up[l], w_down=w_ffn_down[l])
              for l in range(DEPTH)]
    lam_inits = [0.8 - 0.6 * math.exp(-0.3 * l) for l in range(DEPTH)]

    x = x_prompt
    layer_states = []
    for l in range(DEPTH):
        x, st = block(x, c_ctx[None, :], params[l], lam_inits[l], None)
        layer_states.append(st)
    y_prompt = x
    new_mlstm_C = jnp.stack([s[0] for s in layer_states], axis=1)
    new_mlstm_n = jnp.stack([s[1] for s in layer_states], axis=1)
    new_mlstm_m = jnp.stack([s[2] for s in layer_states], axis=1)
    new_ret = jnp.stack([s[3] for s in layer_states], axis=1)
    new_diff_k = jnp.stack([s[4] for s in layer_states], axis=1)
    new_diff_v = jnp.stack([s[5] for s in layer_states], axis=1)
    new_s5_re = jnp.stack([s[6] for s in layer_states], axis=1)
    new_s5_im = jnp.stack([s[7] for s in layer_states], axis=1)

    x = x_sample
    for l in range(DEPTH):
        st = (state_mlstm_C[:, l], state_mlstm_n[:, l], state_mlstm_m[:, l], state_ret[:, l],
              cache_diff_k[:, l], cache_diff_v[:, l], state_s5_re[:, l], state_s5_im[:, l])
        x, _ = block(x, c, params[l], lam_inits[l], st)
    y_sample = x
    return (y_prompt, y_sample, new_mlstm_C, new_mlstm_n, new_mlstm_m, new_ret,
            new_diff_k, new_diff_v, new_s5_re, new_s5_im)
```

```python
import functools
import math

import jax
import jax.numpy as jnp
from jax import lax
from jax.experimental import pallas as pl
from jax.experimental.pallas import tpu as pltpu

F32 = jnp.float32
BF16 = jnp.bfloat16
HI = lax.Precision.HIGHEST

D_MODEL = 1024
DEPTH = 2
GRID_W = 64
HEAD_DIM = 64
GROUP_W = 256
N_HEADS = 4
A_SUB = 32
S5_GC = 16
S5_G = 16
S5_P = 64
D_FF = 2816
ROPE_BASE = 10000.0
EPS = 1e-6
N_MAIN = 12 * GROUP_W
LANES = 128
CHUNK = 256
ROW_TILE = 256
S5_L = 16
NEG = -1e30
VMEM_LIMIT = 56 * 1024 * 1024


def _dot(a, b, precision=None):
    return jnp.dot(a, b, preferred_element_type=F32, precision=precision)


def _dot_nt(a, b):
    return lax.dot_general(a, b, (((1,), (1,)), ((), ())), preferred_element_type=F32)


def _dot_tn(a, b):
    return lax.dot_general(a, b, (((0,), (0,)), ((), ())), preferred_element_type=F32)


def _log_sigmoid(x):
    return jnp.minimum(x, 0.0) - jnp.log(1.0 + jnp.exp(-jnp.abs(x)))


def _silu(x):
    return x * jax.nn.sigmoid(x)


def _rms(x, g):
    return x * lax.rsqrt(jnp.mean(x * x, axis=-1, keepdims=True) + EPS) * g


def _params(sem=None):
    return pltpu.CompilerParams(dimension_semantics=sem, vmem_limit_bytes=VMEM_LIMIT)


def _const_spec(shape):
    n = len(shape)
    return pl.BlockSpec(shape, lambda *_: (0,) * n)


def _ada_kernel(c_ref, w_ref, b_ref, o_ref):
    o_ref[...] = _dot(_silu(c_ref[...]), w_ref[...], HI) + b_ref[...]


def _ada(cond8, w_ada, b_ada):
    tn = 1536
    return pl.pallas_call(
        _ada_kernel,
        grid=(DEPTH, 6 * D_MODEL // tn),
        in_specs=[pl.BlockSpec((8, D_MODEL), lambda l, j: (0, 0)),
                  pl.BlockSpec((None, D_MODEL, tn), lambda l, j: (l, 0, j)),
                  pl.BlockSpec((None, 1, tn), lambda l, j: (l, 0, j))],
        out_specs=pl.BlockSpec((None, 8, tn), lambda l, j: (l, 0, j)),
        out_shape=jax.ShapeDtypeStruct((DEPTH, 8, 6 * D_MODEL), F32),
        compiler_params=_params(("arbitrary", "arbitrary")),
        name="ada",
    )(cond8, w_ada, b_ada.reshape(DEPTH, 1, 6 * D_MODEL))


def _in_proj_kernel(x_ref, mod_ref, g_ref, w_ref, z_ref, gate_ref):
    h = _rms(x_ref[...], g_ref[...]) * (1.0 + mod_ref[:, D_MODEL:2 * D_MODEL]) + mod_ref[:, 0:D_MODEL]
    hb = h.astype(BF16)
    z_ref[...] = _dot(hb, w_ref[:, :N_MAIN])
    gate_ref[...] = _dot(hb, w_ref[:, N_MAIN:])


def _in_proj(x, mod_l, mod_row, gain, w):
    n = x.shape[0]
    return pl.pallas_call(
        _in_proj_kernel,
        grid=(n // ROW_TILE,),
        in_specs=[pl.BlockSpec((ROW_TILE, D_MODEL), lambda i: (i, 0)),
                  pl.BlockSpec((None, 1, 6 * D_MODEL), lambda i: (mod_row(i), 0, 0)),
                  _const_spec((1, D_MODEL)),
                  _const_spec((D_MODEL, N_MAIN + LANES))],
        out_specs=[pl.BlockSpec((ROW_TILE, N_MAIN), lambda i: (i, 0)),
                   pl.BlockSpec((ROW_TILE, LANES), lambda i: (i, 0))],
        out_shape=[jax.ShapeDtypeStruct((n, N_MAIN), F32),
                   jax.ShapeDtypeStruct((n, LANES), F32)],
        compiler_params=_params(("arbitrary",)),
        name="in_proj",
    )(x, mod_l, gain, w)


def _tri_masks(n):
    row = lax.broadcasted_iota(jnp.int32, (n, n), 0)
    col = lax.broadcasted_iota(jnp.int32, (n, n), 1)
    return row >= col, row <= col


def _head_groupnorm(x, g):
    mu = jnp.mean(x, axis=-1, keepdims=True)
    xc = x - mu
    var = jnp.mean(xc * xc, axis=-1, keepdims=True)
    return xc * lax.rsqrt(var + EPS) * g


def _hs(h):
    return slice(HEAD_DIM * h, HEAD_DIM * (h + 1))


def _mlstm_kernel(*refs, n_chunk, has_state):
    if has_state:
        q_ref, k_ref, v_ref, o_ref, g_ref, bias_ref, gain_ref, c0_ref, n0_ref, m0_ref, out_ref = refs
    else:
        q_ref, k_ref, v_ref, o_ref, g_ref, bias_ref, gain_ref, out_ref, cn_ref, nn_ref, mn_ref = refs
    L = CHUNK
    tril, triu = _tri_masks(L)
    lane = lax.broadcasted_iota(jnp.int32, (L, LANES), 1)

    bcols, wcols, wrows = [], [], []
    for c in range(n_chunk):
        rows = slice(c * L, (c + 1) * L)
        p = g_ref[rows, :] + bias_ref[...]
        lf = jnp.where((lane >= 8) & (lane < 16), _log_sigmoid(p), 0.0)
        cum_f = _dot(tril.astype(F32), lf, HI)
        cum_b = _dot(triu.astype(F32), lf, HI)
        bcol = jnp.where(lane < 4, pltpu.roll(cum_f, LANES - 8, 1), pltpu.roll(cum_b, LANES - 8, 1))
        w = p - bcol
        bcols.append(bcol)
        wcols.append(w)
        wrows.append(w.T)

    for h in range(N_HEADS):
        hs = _hs(h)
        h_dir = []
        for d in range(2):
            tri = tril if d == 0 else triu
            last = L - 1 if d == 0 else 0
            j = 4 * d + h
            if has_state:
                c_st = c0_ref[d, h]
                n_st = n0_ref[d, h:h + 1, :]
                m_st = m0_ref[d:d + 1, h:h + 1]
            else:
                c_st = n_st = None
                m_st = jnp.zeros((1, 1), F32)
            outs = [None] * n_chunk
            order = range(n_chunk) if d == 0 else range(n_chunk - 1, -1, -1)
            for ci, c in enumerate(order):
                rows = slice(c * L, (c + 1) * L)
                qh = q_ref[rows, hs]
                kh = k_ref[rows, hs] * (HEAD_DIM ** -0.5)
                vh = v_ref[rows, hs].astype(BF16)
                qb = qh.astype(BF16)
                bcol = bcols[c][:, j:j + 1]
                dlog = jnp.where(tri, bcol + wrows[c][j:j + 1, :], NEG)
                inter = bcol + m_st
                m_row = jnp.maximum(jnp.max(dlog, axis=-1, keepdims=True), inter)
                s = _dot_nt(qb, kh.astype(BF16)) * jnp.exp(dlog - m_row)
                den = jnp.sum(s, axis=-1, keepdims=True)
                num = _dot(s.astype(BF16), vh)
                if c_st is not None:
                    a_inter = jnp.exp(inter - m_row)
                    num = num + a_inter * _dot(qb, c_st.astype(BF16))
                    den = den + a_inter * jnp.sum(qh * n_st, axis=-1, keepdims=True)
                outs[c] = num / jnp.maximum(jnp.abs(den), jnp.exp(-m_row))
                need_state = (not has_state) or ci < n_chunk - 1
                if need_state:
                    m_new = m_row[last:last + 1, :]
                    wk = jnp.exp(bcols[c][last:last + 1, j:j + 1] + wcols[c][:, j:j + 1] - m_new)
                    kw = kh * wk
                    c_new = _dot_tn(kw.astype(BF16), vh)
                    n_new = jnp.sum(kw, axis=0, keepdims=True)
                    if c_st is not None:
                        dec = jnp.exp(bcols[c][last:last + 1, j:j + 1] + m_st - m_new)
                        c_new = c_new + dec * c_st
                        n_new = n_new + dec * n_st
                    c_st, n_st, m_st = c_new, n_new, m_new
            if not has_state:
                cn_ref[d, h] = c_st
                nn_ref[d, h:h + 1, :] = n_st
                mn_ref[d:d + 1, h:h + 1] = m_st
            h_dir.append(outs)
        for c in range(n_chunk):
            rows = slice(c * L, (c + 1) * L)
            gated = jax.nn.sigmoid(o_ref[rows, hs]) * (h_dir[0][c] + h_dir[1][c])
            out_ref[rows, hs] = _head_groupnorm(gated, gain_ref[:, hs])


def _zspec(t, row0, colblk):
    return pl.BlockSpec((t, GROUP_W), lambda b: (row0 + b, colblk))


def _mlstm(z, gates, bias128, gain, n_seq, t, state):
    has_state = state is not None
    in_specs = [_zspec(t, 0, 0), _zspec(t, 0, 1), _zspec(t, 0, 2), _zspec(t, 0, 3),
                pl.BlockSpec((t, LANES), lambda b: (b, 0)),
                _const_spec((1, LANES)), _const_spec((1, GROUP_W))]
    args = [z, z, z, z, gates, bias128, gain]
    out_specs = [pl.BlockSpec((t, GROUP_W), lambda b: (b, 0))]
    out_shape = [jax.ShapeDtypeStruct((n_seq * t, GROUP_W), F32)]
    if has_state:
        c0, n0, m0 = state
        in_specs += [pl.BlockSpec((None, 2, N_HEADS, HEAD_DIM, HEAD_DIM), lambda b: (b, 0, 0, 0, 0)),
                     pl.BlockSpec((None, 2, N_HEADS, HEAD_DIM), lambda b: (b, 0, 0, 0)),
                     pl.BlockSpec((None, 2, N_HEADS), lambda b: (b, 0, 0))]
        args += [c0, n0, m0]
    else:
        out_specs += [pl.BlockSpec((None, 2, N_HEADS, HEAD_DIM, HEAD_DIM), lambda b: (b, 0, 0, 0, 0)),
                      pl.BlockSpec((None, 2, N_HEADS, HEAD_DIM), lambda b: (b, 0, 0, 0)),
                      pl.BlockSpec((None, 2, N_HEADS), lambda b: (b, 0, 0))]
        out_shape += [jax.ShapeDtypeStruct((n_seq, 2, N_HEADS, HEAD_DIM, HEAD_DIM), F32),
                      jax.ShapeDtypeStruct((n_seq, 2, N_HEADS, HEAD_DIM), F32),
                      jax.ShapeDtypeStruct((n_seq, 2, N_HEADS), F32)]
    return pl.pallas_call(
        functools.partial(_mlstm_kernel, n_chunk=t // CHUNK, has_state=has_state),
        grid=(n_seq,), in_specs=in_specs, out_specs=out_specs, out_shape=out_shape,
        compiler_params=_params(("arbitrary",)), name="mlstm",
    )(*args)


def _ret_kernel(*refs, n_chunk, has_state):
    if has_state:
        q_ref, k_ref, v_ref, g_ref, lg_ref, gain_ref, s0_ref, out_ref = refs
    else:
        q_ref, k_ref, v_ref, g_ref, lg_ref, gain_ref, out_ref, sn_ref = refs
    L = CHUNK
    row = lax.broadcasted_iota(jnp.int32, (L, L), 0)
    col = lax.broadcasted_iota(jnp.int32, (L, L), 1)
    diff = (row - col).astype(F32)
    pos = lax.broadcasted_iota(jnp.int32, (L, 1), 0).astype(F32)
    log_g = _log_sigmoid(lg_ref[...])

    for h in range(N_HEADS):
        hs = _hs(h)
        lgf = log_g[0:1, h:h + 1]
        lgb = log_g[1:2, h:h + 1]
        dmat = (jnp.where(diff >= 0, jnp.exp(jnp.maximum(diff, 0.0) * lgf), 0.0)
                + jnp.where(diff <= 0, jnp.exp(jnp.maximum(-diff, 0.0) * lgb), 0.0))
        zeta_f = jnp.exp((L - 1.0 - pos) * lgf)
        zeta_b = jnp.exp(pos * lgb)
        xi_f = jnp.exp((pos + 1.0) * lgf)
        xi_b = jnp.exp((L - pos) * lgb)
        gl_f = jnp.exp(L * lgf)
        gl_b = jnp.exp(L * lgb)

        def chunk(c):
            rows = slice(c * L, (c + 1) * L)
            qb = q_ref[rows, hs].astype(BF16)
            kh = k_ref[rows, hs] * (HEAD_DIM ** -0.5)
            vb = v_ref[rows, hs].astype(BF16)
            return rows, qb, kh, vb

        sf = [None] * (n_chunk + 1)
        sb = [None] * (n_chunk + 1)
        if has_state:
            sf[0] = s0_ref[0, h]
            sb[n_chunk] = s0_ref[1, h]
        for c in range(n_chunk):
            if (not has_state) or c < n_chunk - 1:
                _, _, kh, vb = chunk(c)
                upd = _dot_tn((kh * zeta_f).astype(BF16), vb)
                sf[c + 1] = upd if sf[c] is None else upd + gl_f * sf[c]
        for c in range(n_chunk - 1, -1, -1):
            if (not has_state) or c > 0:
                _, _, kh, vb = chunk(c)
                upd = _dot_tn((kh * zeta_b).astype(BF16), vb)
                sb[c] = upd if sb[c + 1] is None else upd + gl_b * sb[c + 1]
        if not has_state:
            sn_ref[0, h] = sf[n_chunk]
            sn_ref[1, h] = sb[0]

        for c in range(n_chunk):
            rows, qb, kh, vb = chunk(c)
            att = _dot_nt(qb, kh.astype(BF16)) * dmat
            o = _dot(att.astype(BF16), vb)
            if sf[c] is not None and has_state:
                o = o + xi_f * _dot(qb, sf[c].astype(BF16))
            if sb[c + 1] is not None and has_state:
                o = o + xi_b * _dot(qb, sb[c + 1].astype(BF16))
            out_ref[rows, hs] = _silu(g_ref[rows, hs]) * _head_groupnorm(o, gain_ref[:, hs])


def _retention(z, lg8, gain, n_seq, t, state):
    has_state = state is not None
    in_specs = [_zspec(t, 0, 4), _zspec(t, 0, 5), _zspec(t, 0, 6), _zspec(t, 0, 7),
                _const_spec((8, LANES)), _const_spec((1, GROUP_W))]
    args = [z, z, z, z, lg8, gain]
    out_specs = [pl.BlockSpec((t, GROUP_W), lambda b: (b, 0))]
    out_shape = [jax.ShapeDtypeStruct((n_seq * t, GROUP_W), F32)]
    st_spec = pl.BlockSpec((None, 2, N_HEADS, HEAD_DIM, HEAD_DIM), lambda b: (b, 0, 0, 0, 0))
    if has_state:
        in_specs.append(st_spec)
        args.append(state)
    else:
        out_specs.append(st_spec)
        out_shape.append(jax.ShapeDtypeStruct((n_seq, 2, N_HEADS, HEAD_DIM, HEAD_DIM), F32))
    return pl.pallas_call(
        functools.partial(_ret_kernel, n_chunk=t // CHUNK, has_state=has_state),
        grid=(n_seq,), in_specs=in_specs, out_specs=out_specs, out_shape=out_shape,
        compiler_params=_params(("arbitrary",)), name="retention",
    )(*args)


def _lambda(lam_ref, lam_init):
    v = lam_ref[...]
    s1 = jnp.sum(v[0:1, :] * v[1:2, :], axis=-1, keepdims=True)
    s2 = jnp.sum(v[2:3, :] * v[3:4, :], axis=-1, keepdims=True)
    return jnp.exp(s1) - jnp.exp(s2) + lam_init


def _first_half(shape):
    lane = lax.broadcasted_iota(jnp.int32, shape, 1)
    return (lane % (A_SUB // 2)) < (A_SUB // 4)


def _rope(x, cos, sin_signed):
    n = x.shape[1]
    first_half = _first_half(x.shape)
    partner = jnp.where(first_half, pltpu.roll(x, n - A_SUB // 4, 1), pltpu.roll(x, A_SUB // 4, 1))
    return x * cos + partner * sin_signed


def _attn_kernel(*refs, t, has_cache, lam_init):
    if has_cache:
        q_ref, k_ref, v_ref, kc_ref, vc_ref, lam_ref, gain_ref, out_ref, kr_ref, cos_ref, sin_ref = refs
    else:
        q_ref, k_ref, v_ref, lam_ref, gain_ref, out_ref = refs
    lam = _lambda(lam_ref, lam_init)
    scale = A_SUB ** -0.5
    qb_rows = CHUNK

    if has_cache:
        @pl.when(pl.program_id(0) == 0)
        def _():
            tok = lax.broadcasted_iota(jnp.int32, (t, GROUP_W), 0)
            lane = lax.broadcasted_iota(jnp.int32, (t, GROUP_W), 1)
            pos = jnp.where((lane % A_SUB) < (A_SUB // 2), tok // GRID_W, tok % GRID_W).astype(F32)
            n_freq = A_SUB // 4
            inv = jnp.exp((lane % n_freq).astype(F32) * (-math.log(ROPE_BASE) / n_freq))
            ang = pos * inv
            cos_ref[...] = jnp.cos(ang)
            sin_ref[...] = jnp.where(_first_half((t, GROUP_W)), -jnp.sin(ang), jnp.sin(ang))

        kr_ref[...] = _rope(k_ref[...], cos_ref[...], sin_ref[...]).astype(BF16)

    for qi in range(t // qb_rows):
        rows = slice(qi * qb_rows, (qi + 1) * qb_rows)
        q = q_ref[rows, :]
        if has_cache:
            q = _rope(q, cos_ref[rows, :], sin_ref[rows, :])
        q = (q * scale).astype(BF16)
        for h in range(N_HEADS):
            hs = _hs(h)
            probs = []
            for m in range(2):
                ms = slice(HEAD_DIM * h + A_SUB * m, HEAD_DIM * h + A_SUB * (m + 1))
                qm = q[:, ms]
                if has_cache:
                    s_c = _dot_nt(qm, kc_ref[:, ms].astype(BF16))
                    s_l = _dot_nt(qm, kr_ref[:, ms])
                    mx = jnp.maximum(jnp.max(s_c, axis=-1, keepdims=True), jnp.max(s_l, axis=-1, keepdims=True))
                    e_c = jnp.exp(s_c - mx)
                    e_l = jnp.exp(s_l - mx)
                    inv_l = 1.0 / (jnp.sum(e_c, axis=-1, keepdims=True) + jnp.sum(e_l, axis=-1, keepdims=True))
                    probs.append((e_c * inv_l, e_l * inv_l))
                else:
                    s_l = _dot_nt(qm, k_ref[:, ms].astype(BF16))
                    e_l = jnp.exp(s_l - jnp.max(s_l, axis=-1, keepdims=True))
                    probs.append((None, e_l * (1.0 / jnp.sum(e_l, axis=-1, keepdims=True))))
            a_l = probs[0][1] - lam * probs[1][1]
            o = _dot(a_l.astype(BF16), v_ref[:, hs].astype(BF16))
            if has_cache:
                a_c = probs[0][0] - lam * probs[1][0]
                o = o + _dot(a_c.astype(BF16), vc_ref[:, hs].astype(BF16))
            out_ref[rows, hs] = _rms(o, gain_ref[:, hs]) * (1.0 - lam_init)


def _attention(z, lam8, gain, n_seq, t, cache, lam_init):
    has_cache = cache is not None
    in_specs = [_zspec(t, 0, 8), _zspec(t, 0, 9), _zspec(t, 0, 10)]
    args = [z, z, z]
    scratch = []
    if has_cache:
        kc, vc = cache
        p = kc.shape[1]
        in_specs += [pl.BlockSpec((None, p, GROUP_W), lambda b: (b, 0, 0))] * 2
        args += [kc, vc]
        scratch = [pltpu.VMEM((t, GROUP_W), BF16), pltpu.VMEM((t, GROUP_W), F32), pltpu.VMEM((t, GROUP_W), F32)]
    in_specs += [_const_spec((8, LANES)), _const_spec((1, GROUP_W))]
    args += [lam8, gain]
    return pl.pallas_call(
        functools.partial(_attn_kernel, t=t, has_cache=has_cache, lam_init=lam_init),
        grid=(n_seq,), in_specs=in_specs,
        out_specs=pl.BlockSpec((t, GROUP_W), lambda b: (b, 0)),
        out_shape=jax.ShapeDtypeStruct((n_seq * t, GROUP_W), F32),
        scratch_shapes=scratch,
        compiler_params=_params(("arbitrary",)), name="diff_attention",
    )(*args)


def _cpow(lre, lim, dt, e):
    mag = jnp.exp(lre * dt * e)
    ang = lim * dt * e
    return mag * jnp.cos(ang), mag * jnp.sin(ang)


def _zoh_coef(lre, lim, dt):
    br, bi = _cpow(lre, lim, dt, 1.0)
    den = lre * lre + lim * lim
    return ((br - 1.0) * lre + bi * lim) / den, (bi * lre - (br - 1.0) * lim) / den


def _toeplitz_rows(kall):
    n = S5_L * S5_GC
    lane = lax.broadcasted_iota(jnp.int32, (S5_GC, n), 1)
    pieces = []
    for s in range(S5_L):
        shifted = kall if s == 0 else pltpu.roll(kall, S5_GC * s, 1)
        pieces.append(jnp.where(lane >= S5_GC * s, shifted, 0.0))
    return jnp.concatenate(pieces, axis=0)


def _s5_prep_kernel(lam_c_re, lam_c_im, lam_r_re, lam_r_im, ldt_ref, b_t_re, b_t_im, b_tile_re, b_tile_im,
                    c_re, c_im, c_tile_re, c_tile_im, d_ref, t_ref, bcf_ref, bcb_ref, ccf_ref, ccb_ref, lam_ref):
    n = S5_L * S5_GC
    lane_blk = (lax.broadcasted_iota(jnp.int32, (S5_P, n), 1) // S5_GC).astype(F32)
    row_blk = (lax.broadcasted_iota(jnp.int32, (n, S5_P), 0) // S5_GC).astype(F32)

    def operands(d):
        dt = jnp.exp(ldt_ref[d])
        lcr, lci = lam_c_re[d], lam_c_im[d]
        lrr, lri = lam_r_re[d], lam_r_im[d]
        return dt, lcr, lci, lrr, lri

    dt, lcr, lci, lrr, lri = operands(0)
    kr, ki = _zoh_coef(lrr, lri, dt)
    bbt_re = kr * b_t_re[0] - ki * b_t_im[0]
    bbt_im = kr * b_t_im[0] + ki * b_t_re[0]
    pr, pi = _cpow(lcr, lci, dt, lane_blk)
    cl_re = c_tile_re[0] * pr - c_tile_im[0] * pi
    cl_im = c_tile_re[0] * pi + c_tile_im[0] * pr
    t_f = _toeplitz_rows(_dot(bbt_re, cl_re, HI) - _dot(bbt_im, cl_im, HI))
    bt_re = jnp.concatenate([bbt_re] * S5_L, axis=0)
    bt_im = jnp.concatenate([bbt_im] * S5_L, axis=0)
    pr, pi = _cpow(lrr, lri, dt, S5_L - 1.0 - row_blk)
    bcf_ref[...] = jnp.concatenate([bt_re * pr - bt_im * pi, bt_re * pi + bt_im * pr], axis=1).astype(BF16)
    pr, pi = _cpow(lcr, lci, dt, lane_blk + 1.0)
    ccf_ref[...] = jnp.concatenate([c_tile_re[0] * pr - c_tile_im[0] * pi,
                                    -(c_tile_re[0] * pi + c_tile_im[0] * pr)], axis=0).astype(BF16)
    pr, pi = _cpow(lrr, lri, dt, float(S5_L))
    lam_ref[0:1, :] = jnp.concatenate([pr, pr], axis=1)
    lam_ref[1:2, :] = jnp.concatenate([-pi, pi], axis=1)

    dt, lcr, lci, lrr, lri = operands(1)
    kr, ki = _zoh_coef(lcr, lci, dt)
    bb_re = kr * b_tile_re[1] - ki * b_tile_im[1]
    bb_im = kr * b_tile_im[1] + ki * b_tile_re[1]
    pr, pi = _cpow(lcr, lci, dt, lane_blk)
    bl_re = bb_re * pr - bb_im * pi
    bl_im = bb_re * pi + bb_im * pr
    t_b = _toeplitz_rows(_dot(c_re[1], bl_re, HI) - _dot(c_im[1], bl_im, HI)).T
    kr, ki = _zoh_coef(lrr, lri, dt)
    bbt_re = kr * b_t_re[1] - ki * b_t_im[1]
    bbt_im = kr * b_t_im[1] + ki * b_t_re[1]
    bt_re = jnp.concatenate([bbt_re] * S5_L, axis=0)
    bt_im = jnp.concatenate([bbt_im] * S5_L, axis=0)
    pr, pi = _cpow(lrr, lri, dt, row_blk)
    bcb_ref[...] = jnp.concatenate([bt_re * pr - bt_im * pi, bt_re * pi + bt_im * pr], axis=1).astype(BF16)
    pr, pi = _cpow(lcr, lci, dt, S5_L - lane_blk)
    ccb_ref[...] = jnp.concatenate([c_tile_re[1] * pr - c_tile_im[1] * pi,
                                    -(c_tile_re[1] * pi + c_tile_im[1] * pr)], axis=0).astype(BF16)
    pr, pi = _cpow(lrr, lri, dt, float(S5_L))
    lam_ref[2:3, :] = jnp.concatenate([pr, pr], axis=1)
    lam_ref[3:4, :] = jnp.concatenate([-pi, pi], axis=1)
    lam_ref[4:8, :] = jnp.zeros((4, 2 * S5_P), F32)

    eye = (lax.broadcasted_iota(jnp.int32, (n, n), 0) == lax.broadcasted_iota(jnp.int32, (n, n), 1))
    t_ref[...] = (t_f + t_b + jnp.where(eye, d_ref[...], 0.0)).astype(BF16)


def _s5_prep(lam_re, lam_im, log_dt, b_re, b_im, c_re, c_im, s5_d):
    n = S5_L * S5_GC
    lg = (DEPTH, 2, S5_G)
    bc = lambda a, shape: jnp.broadcast_to(a, lg + shape)
    swap = lambda a: jnp.swapaxes(a, -1, -2)
    ins = [bc(lam_re[..., :, None], (S5_P, n)), bc(lam_im[..., :, None], (S5_P, n)),
           lam_re[..., None, :], lam_im[..., None, :], log_dt[..., None, None],
           swap(b_re), swap(b_im), jnp.tile(b_re, (1, 1, 1, 1, S5_L)), jnp.tile(b_im, (1, 1, 1, 1, S5_L)),
           c_re, c_im, jnp.tile(swap(c_re), (1, 1, 1, 1, S5_L)), jnp.tile(swap(c_im), (1, 1, 1, 1, S5_L))]
    d_tile = jnp.tile(s5_d.reshape(DEPTH, S5_G, 1, S5_GC), (1, 1, 1, S5_L))

    def dir_spec(a):
        return pl.BlockSpec((None, 2, None) + a.shape[3:], lambda l, g: (l, 0, g, 0, 0))

    def out(rows, cols, dtype):
        return (pl.BlockSpec((None, None, rows, cols), lambda l, g: (l, g, 0, 0)),
                jax.ShapeDtypeStruct((DEPTH, S5_G, rows, cols), dtype))

    outs = [out(n, n, BF16), out(n, 2 * S5_P, BF16), out(n, 2 * S5_P, BF16),
            out(2 * S5_P, n, BF16), out(2 * S5_P, n, BF16), out(8, 2 * S5_P, F32)]
    return pl.pallas_call(
        _s5_prep_kernel,
        grid=(DEPTH, S5_G),
        in_specs=[dir_spec(a) for a in ins] + [pl.BlockSpec((None, None, 1, n), lambda l, g: (l, g, 0, 0))],
        out_specs=[o[0] for o in outs], out_shape=[o[1] for o in outs],
        compiler_params=_params(("arbitrary", "arbitrary")), name="s5_prep",
    )(*ins, d_tile)


def _s5_kernel(*refs, n_seq, n_k, has_state):
    if has_state:
        (u_ref, t_ref, bcf_ref, bcb_ref, ccf_ref, ccb_ref, lam_ref, x0_ref,
         y_ref, inj_f, inj_b, xin_f, xin_b) = refs
    else:
        (u_ref, t_ref, bcf_ref, bcb_ref, ccf_ref, ccb_ref, lam_ref,
         y_ref, xf_ref, xb_ref, inj_f, inj_b, xin_f, xin_b) = refs
    w = 2 * S5_P
    for g in range(S5_G):
        cols = slice(w * g, w * (g + 1))
        u = u_ref[g]
        inj_f[:, cols] = _dot(u, bcf_ref[g])
        inj_b[:, cols] = _dot(u, bcb_ref[g])

    width = S5_G * w
    is_im = (lax.broadcasted_iota(jnp.int32, (n_seq, width), 1) % w) >= S5_P

    def cmul(a, bs, x):
        swapped = jnp.where(is_im, pltpu.roll(x, S5_P, 1), pltpu.roll(x, width - S5_P, 1))
        return a * x + bs * swapped

    def coef(r):
        return jnp.concatenate([lam_ref[g, r:r + 1, :] for g in range(S5_G)], axis=1)

    for d, (inj, xin) in enumerate(((inj_f, xin_f), (inj_b, xin_b))):
        a, bs = coef(2 * d), coef(2 * d + 1)
        x = x0_ref[d] if has_state else jnp.zeros((n_seq, width), F32)
        for k in (range(n_k) if d == 0 else range(n_k - 1, -1, -1)):
            rows = slice(k * n_seq, (k + 1) * n_seq)
            xin[rows, :] = x
            x = cmul(a, bs, x) + inj[rows, :]
        if not has_state:
            (xf_ref if d == 0 else xb_ref)[...] = x

    for g in range(S5_G):
        cols = slice(w * g, w * (g + 1))
        y_ref[g] = (_dot(u_ref[g], t_ref[g]) + _dot(xin_f[:, cols].astype(BF16), ccf_ref[g])
                    + _dot(xin_b[:, cols].astype(BF16), ccb_ref[g]))


def _s5(u, ops_l, n_seq, n_k, x0):
    has_state = x0 is not None
    n = S5_L * S5_GC
    r = n_seq * n_k
    width = S5_G * 2 * S5_P
    args = [u] + list(ops_l) + ([x0] if has_state else [])
    out_shape = [jax.ShapeDtypeStruct((S5_G, r, n), F32)]
    if not has_state:
        out_shape += [jax.ShapeDtypeStruct((n_seq, width), F32)] * 2
    return pl.pallas_call(
        functools.partial(_s5_kernel, n_seq=n_seq, n_k=n_k, has_state=has_state),
        out_shape=out_shape,
        scratch_shapes=[pltpu.VMEM((r, width), F32), pltpu.VMEM((r, width), F32),
                        pltpu.VMEM((r, width), F32), pltpu.VMEM((r, width), F32)],
        compiler_params=_params(), name="s5_scan",
    )(*args)


def _out_ffn_kernel(x_ref, m_ref, r_ref, a_ref, y_ref, mod_ref, wglu_ref, bglu_ref, wout_ref,
                    n_post_ref, n_pre_ref, n_fpost_ref, wg_ref, wu_ref, wd_ref, o_ref):
    def mod(i):
        return mod_ref[:, i * D_MODEL:(i + 1) * D_MODEL]

    y = y_ref[...]
    gs = 0.5 * y * (1.0 + jnp.tanh(math.sqrt(2.0 / math.pi) * (y + 0.044715 * (y * y * y))))
    s_out = gs * jax.nn.sigmoid(_dot(gs.astype(BF16), wglu_ref[...]) + bglu_ref[...])
    mixed = jnp.concatenate([m_ref[...], r_ref[...], a_ref[...], s_out], axis=1).astype(BF16)
    x1 = x_ref[...] + mod(2) * _rms(_dot(mixed, wout_ref[...]), n_post_ref[...])
    h = (_rms(x1, n_pre_ref[...]) * (1.0 + mod(4)) + mod(3)).astype(BF16)
    act = (_silu(_dot(h, wg_ref[...])) * _dot(h, wu_ref[...])).astype(BF16)
    o_ref[...] = x1 + mod(5) * _rms(_dot(act, wd_ref[...]), n_fpost_ref[...])


def _out_ffn(x, m_out, r_out, a_out, y_s5, mod_l, mod_row, wl):
    n = x.shape[0]
    row = lambda w: pl.BlockSpec((ROW_TILE, w), lambda i: (i, 0))

    def resident(shape):
        return pl.BlockSpec(shape, lambda i: (0,) * len(shape), pipeline_mode=pl.Buffered(1))

    return pl.pallas_call(
        _out_ffn_kernel,
        grid=(n // ROW_TILE,),
        in_specs=[row(D_MODEL), row(GROUP_W), row(GROUP_W), row(GROUP_W), row(GROUP_W),
                  pl.BlockSpec((None, 1, 6 * D_MODEL), lambda i: (mod_row(i), 0, 0)),
                  resident((GROUP_W, GROUP_W)), resident((1, GROUP_W)), resident((D_MODEL, D_MODEL)),
                  resident((1, D_MODEL)), resident((1, D_MODEL)), resident((1, D_MODEL)),
                  resident((D_MODEL, D_FF)), resident((D_MODEL, D_FF)), resident((D_FF, D_MODEL))],
        out_specs=row(D_MODEL),
        out_shape=jax.ShapeDtypeStruct((n, D_MODEL), F32),
        compiler_params=_params(("arbitrary",)), name="out_ffn",
    )(x, m_out, r_out, a_out, y_s5, mod_l, wl['w_glu'], wl['b_glu'], wl['w_out'],
      wl['n_mix_post'], wl['n_ffn_pre'], wl['n_ffn_post'], wl['w_gate'], wl['w_up'], wl['w_down'])


def _s5_layout(su, n_seq, t):
    n_k = t // S5_L
    u = su.reshape(n_seq, n_k, S5_L, S5_G, S5_GC).transpose(3, 1, 0, 2, 4)
    return u.reshape(S5_G, n_k * n_seq, S5_L * S5_GC).astype(BF16)


def _s5_unlayout(y, n_seq, t):
    n_k = t // S5_L
    y = y.reshape(S5_G, n_k, n_seq, S5_L, S5_GC).transpose(2, 1, 3, 0, 4)
    return y.reshape(n_seq * t, GROUP_W)


def _layer(x, n_seq, t, mod_l, mod_row, wl, s5_ops, lam_init, st):
    z, gates = _in_proj(x, mod_l, mod_row, wl['n_mix_pre'], wl['w_in'])
    m_res = _mlstm(z, gates, wl['m_bias'], wl['m_norm'], n_seq, t, None if st is None else st[0:3])
    r_res = _retention(z, wl['r_decay'], wl['r_norm'], n_seq, t, None if st is None else st[3])
    a_out = _attention(z, wl['a_lam'], wl['a_norm'], n_seq, t, None if st is None else st[4:6], lam_init)
    u = _s5_layout(z[:, 11 * GROUP_W:], n_seq, t)
    s_res = _s5(u, s5_ops, n_seq, t // S5_L, None if st is None else st[6])
    y_s5 = _s5_unlayout(s_res[0], n_seq, t)
    x_new = _out_ffn(x, m_res[0], r_res[0], a_out, y_s5, mod_l, mod_row, wl)
    if st is not None:
        return x_new, None
    k_new = z[:, 9 * GROUP_W:10 * GROUP_W].reshape(n_seq, t, N_HEADS, HEAD_DIM)
    v_new = z[:, 10 * GROUP_W:11 * GROUP_W].reshape(n_seq, t, N_HEADS, HEAD_DIM)
    xs = jnp.stack([s_res[1], s_res[2]], axis=1).reshape(n_seq, 2, S5_G, 2, S5_P)
    return x_new, (m_res[1], m_res[2], m_res[3], r_res[1], k_new, v_new, xs[:, :, :, 0], xs[:, :, :, 1])


def _pad_rows(a, rows=8, cols=LANES):
    return jnp.zeros((rows, cols), F32).at[:a.shape[0], :a.shape[1]].set(a)


def kernel(x_prompt, x_sample, state_mlstm_C, state_mlstm_n, state_mlstm_m, state_ret, cache_diff_k, cache_diff_v, state_s5_re, state_s5_im, c, c_ctx, w_ada, b_ada, n_mix_pre, n_mix_post, n_ffn_pre, n_ffn_post, w_in, w_out, m_gate_bias, m_norm, r_decay_logit, r_norm, a_lam_q1, a_lam_k1, a_lam_q2, a_lam_k2, a_norm, s5_lam_re, s5_lam_im, s5_log_dt, s5_b_re, s5_b_im, s5_c_re, s5_c_im, s5_d, s5_w_glu, s5_b_glu, w_ffn_gate, w_ffn_up, w_ffn_down):
    n_ctx, t_ctx, _ = x_prompt.shape
    n_lat, t_lat, _ = x_sample.shape
    past = cache_diff_k.shape[2]

    cond8 = jnp.concatenate([c, c_ctx[None, :], jnp.zeros((8 - n_lat - 1, D_MODEL), F32)], axis=0)
    mod = _ada(cond8, w_ada, b_ada).reshape(DEPTH, 8, 1, 6 * D_MODEL)
    s5_all = _s5_prep(s5_lam_re, s5_lam_im, s5_log_dt, s5_b_re, s5_b_im, s5_c_re, s5_c_im, s5_d)

    n_gate = 4 * N_HEADS
    layers = []
    for l in range(DEPTH):
        w = w_in[l]
        w_perm = jnp.concatenate([w[:, :4 * GROUP_W], w[:, 4 * GROUP_W + n_gate:],
                                  w[:, 4 * GROUP_W:4 * GROUP_W + n_gate],
                                  jnp.zeros((D_MODEL, LANES - n_gate), F32)], axis=1).astype(BF16)
        row = lambda a: a[l][None, :]
        layers.append(dict(
            w_in=w_perm, w_out=w_out[l].astype(BF16), w_glu=s5_w_glu[l].astype(BF16), b_glu=row(s5_b_glu),
            w_gate=w_ffn_gate[l].astype(BF16), w_up=w_ffn_up[l].astype(BF16), w_down=w_ffn_down[l].astype(BF16),
            n_mix_pre=row(n_mix_pre), n_mix_post=row(n_mix_post), n_ffn_pre=row(n_ffn_pre),
            n_ffn_post=row(n_ffn_post), m_norm=row(m_norm), r_norm=row(r_norm), a_norm=row(a_norm),
            m_bias=_pad_rows(m_gate_bias[l][None, :], 1), r_decay=_pad_rows(r_decay_logit[l]),
            a_lam=_pad_rows(jnp.stack([a_lam_q1[l], a_lam_k1[l], a_lam_q2[l], a_lam_k2[l]]))))
    lam_inits = [0.8 - 0.6 * math.exp(-0.3 * l) for l in range(DEPTH)]

    x = x_prompt.reshape(n_ctx * t_ctx, D_MODEL)
    new_states = []
    for l in range(DEPTH):
        x, st = _layer(x, n_ctx, t_ctx, mod[l], lambda i: n_lat, layers[l],
                       [a[l] for a in s5_all], lam_inits[l], None)
        new_states.append(st)
    y_prompt = x.reshape(n_ctx, t_ctx, D_MODEL)

    x = x_sample.reshape(n_lat * t_lat, D_MODEL)
    tiles_per_seq = t_lat // ROW_TILE
    for l in range(DEPTH):
        x0 = jnp.stack([state_s5_re[:, l], state_s5_im[:, l]], axis=3)
        x0 = x0.transpose(1, 0, 2, 3, 4).reshape(2, n_lat, S5_G * 2 * S5_P)
        st = (state_mlstm_C[:, l], state_mlstm_n[:, l], state_mlstm_m[:, l], state_ret[:, l],
              cache_diff_k[:, l].reshape(n_lat, past, GROUP_W), cache_diff_v[:, l].reshape(n_lat, past, GROUP_W), x0)
        x, _ = _layer(x, n_lat, t_lat, mod[l], lambda i: i // tiles_per_seq, layers[l],
                      [a[l] for a in s5_all], lam_inits[l], st)
    y_sample = x.reshape(n_lat, t_lat, D_MODEL)

    outs = [jnp.stack([s[i] for s in new_states], axis=1) for i in range(8)]
    return (y_prompt, y_sample, *outs)
```

```python
import functools
import math

import jax
import jax.numpy as jnp
from jax import lax
from jax.experimental import pallas as pl
from jax.experimental.pallas import tpu as pltpu

F32 = jnp.float32
BF16 = jnp.bfloat16
HI = lax.Precision.HIGHEST

D_MODEL = 1024
DEPTH = 2
GRID_W = 64
HEAD_DIM = 64
GROUP_W = 256
N_HEADS = 4
A_SUB = 32
S5_GC = 16
S5_G = 16
S5_P = 64
D_FF = 2816
ROPE_BASE = 10000.0
EPS = 1e-6
N_MAIN = 12 * GROUP_W
LANES = 128
CHUNK = 256
ROW_TILE = 256
S5_L = 16
NEG = -1e30
VMEM_LIMIT = 56 * 1024 * 1024


def _dot(a, b, precision=None):
    return jnp.dot(a, b, preferred_element_type=F32, precision=precision)


def _dot_nt(a, b):
    return lax.dot_general(a, b, (((1,), (1,)), ((), ())), preferred_element_type=F32)


def _log_sigmoid(x):
    return jnp.minimum(x, 0.0) - jnp.log(1.0 + jnp.exp(-jnp.abs(x)))


def _silu(x):
    return x * jax.nn.sigmoid(x)


def _rms(x, g):
    return x * lax.rsqrt(jnp.mean(x * x, axis=-1, keepdims=True) + EPS) * g


def _params(sem=None):
    return pltpu.CompilerParams(dimension_semantics=sem, vmem_limit_bytes=VMEM_LIMIT)


def _const_spec(shape):
    n = len(shape)
    return pl.BlockSpec(shape, lambda *_: (0,) * n)


def _ada_kernel(c_ref, w_ref, b_ref, o_ref):
    o_ref[...] = _dot(_silu(c_ref[...]), w_ref[...], HI) + b_ref[...]


def _ada(cond8, w_ada, b_ada):
    tn = 1536
    return pl.pallas_call(
        _ada_kernel,
        grid=(DEPTH, 6 * D_MODEL // tn),
        in_specs=[pl.BlockSpec((8, D_MODEL), lambda l, j: (0, 0)),
                  pl.BlockSpec((None, D_MODEL, tn), lambda l, j: (l, 0, j)),
                  pl.BlockSpec((None, 1, tn), lambda l, j: (l, 0, j))],
        out_specs=pl.BlockSpec((None, 8, tn), lambda l, j: (l, 0, j)),
        out_shape=jax.ShapeDtypeStruct((DEPTH, 8, 6 * D_MODEL), F32),
        compiler_params=_params(("arbitrary", "arbitrary")),
        name="ada",
    )(cond8, w_ada, b_ada.reshape(DEPTH, 1, 6 * D_MODEL))


def _in_proj_kernel(x_ref, mod_ref, g_ref, w_ref, z_ref, gate_ref):
    h = _rms(x_ref[...], g_ref[...]) * (1.0 + mod_ref[:, D_MODEL:2 * D_MODEL]) + mod_ref[:, 0:D_MODEL]
    hb = h.astype(BF16)
    z_ref[...] = _dot(hb, w_ref[:, :N_MAIN])
    gate_ref[...] = _dot(hb, w_ref[:, N_MAIN:])


def _in_proj(x, mod_l, mod_row, gain, w):
    n = x.shape[0]
    return pl.pallas_call(
        _in_proj_kernel,
        grid=(n // ROW_TILE,),
        in_specs=[pl.BlockSpec((ROW_TILE, D_MODEL), lambda i: (i, 0)),
                  pl.BlockSpec((None, 1, 6 * D_MODEL), lambda i: (mod_row(i), 0, 0)),
                  _const_spec((1, D_MODEL)),
                  _const_spec((D_MODEL, N_MAIN + LANES))],
        out_specs=[pl.BlockSpec((ROW_TILE, N_MAIN), lambda i: (i, 0)),
                   pl.BlockSpec((ROW_TILE, LANES), lambda i: (i, 0))],
        out_shape=[jax.ShapeDtypeStruct((n, N_MAIN), F32),
                   jax.ShapeDtypeStruct((n, LANES), F32)],
        compiler_params=_params(("arbitrary",)),
        name="in_proj",
    )(x, mod_l, gain, w)


def _tri_masks(n):
    row = lax.broadcasted_iota(jnp.int32, (n, n), 0)
    col = lax.broadcasted_iota(jnp.int32, (n, n), 1)
    return row >= col, row <= col


def _head_mean_matrix():
    r = lax.broadcasted_iota(jnp.int32, (GROUP_W, GROUP_W), 0) // HEAD_DIM
    c = lax.broadcasted_iota(jnp.int32, (GROUP_W, GROUP_W), 1) // HEAD_DIM
    return jnp.where(r == c, 1.0 / HEAD_DIM, 0.0).astype(BF16)


def _head_mean(x, j):
    hi = x.astype(BF16)
    lo = (x - hi.astype(F32)).astype(BF16)
    return _dot(hi, j) + _dot(lo, j)


def _head_groupnorm(x, g, j):
    xc = x - _head_mean(x, j)
    return xc * lax.rsqrt(_head_mean(xc * xc, j) + EPS) * g


def _head_rms(x, g, j):
    return x * lax.rsqrt(_head_mean(x * x, j) + EPS) * g


def _hs(h):
    return slice(HEAD_DIM * h, HEAD_DIM * (h + 1))


def _v_ext(v_ref, rows, hs):
    ones = jnp.ones((rows.stop - rows.start, HEAD_DIM), BF16)
    return jnp.concatenate([v_ref[rows, hs].astype(BF16), ones], axis=1)


def _zspec(t, colblk):
    return pl.BlockSpec((t, GROUP_W), lambda b: (b, colblk))


def _scan_max(x, reverse):
    n = x.shape[1]
    lane = lax.broadcasted_iota(jnp.int32, x.shape, 1)
    sh = 1
    while sh < n:
        if reverse:
            x = jnp.maximum(x, jnp.where(lane < n - sh, pltpu.roll(x, n - sh, 1), NEG))
        else:
            x = jnp.maximum(x, jnp.where(lane >= sh, pltpu.roll(x, sh, 1), NEG))
        sh *= 2
    return x


def _ends(x, is_fwd):
    return jnp.where(is_fwd, x[:, x.shape[1] - 1:], x[:, 0:1])


def _mlstm_kernel(*refs, n_chunk, has_state):
    if has_state:
        q_ref, k_ref, v_ref, o_ref, g_ref, bias_ref, gain_ref, s0_ref, m0_ref, out_ref = refs
    else:
        q_ref, k_ref, v_ref, o_ref, g_ref, bias_ref, gain_ref, out_ref, sn_ref, mn_ref = refs
    L = CHUNK
    nu = 2 * N_HEADS
    tril, triu = _tri_masks(L)
    is_fwd = lax.broadcasted_iota(jnp.int32, (nu, 1), 0) < N_HEADS

    b8, w8, mcum8 = [], [], []
    for c in range(n_chunk):
        p_t = (g_ref[c * L:(c + 1) * L, :] + bias_ref[...]).T
        lf = _log_sigmoid(p_t[nu:2 * nu, :])
        b = jnp.where(is_fwd, _dot(lf, triu.astype(F32), HI), _dot(lf, tril.astype(F32), HI))
        w = p_t[0:nu, :] - b
        b8.append(b)
        w8.append(w)
        mcum8.append(jnp.where(is_fwd, _scan_max(w, False), _scan_max(w, True)))

    m0 = m0_ref[...] if has_state else jnp.zeros((nu, 1), F32)
    m_in_f, m_in_b = [None] * n_chunk, [None] * n_chunk
    m = m0
    for c in range(n_chunk):
        m_in_f[c] = m
        m = (b8[c] + jnp.maximum(mcum8[c], m))[:, L - 1:]
    m_fin_f = m
    m = m0
    for c in range(n_chunk - 1, -1, -1):
        m_in_b[c] = m
        m = (b8[c] + jnp.maximum(mcum8[c], m))[:, 0:1]
    if not has_state:
        mn_ref[...] = jnp.where(is_fwd, m_fin_f, m)

    wk8, dec8, cols = [], [], []
    for c in range(n_chunk):
        m_in = jnp.where(is_fwd, m_in_f[c], m_in_b[c])
        g = jnp.maximum(mcum8[c], m_in)
        m_row = b8[c] + g
        m_new, b_last = _ends(m_row, is_fwd), _ends(b8[c], is_fwd)
        wk8.append(jnp.exp(b_last + w8[c] - m_new))
        dec8.append(jnp.exp(b_last + m_in - m_new))
        stats = jnp.concatenate([g, jnp.exp(m_in - g), jnp.exp(-m_row), jnp.zeros((LANES - 3 * nu, L), F32)], axis=0)
        cols.append(stats.T)

    k_t = [(k_ref[c * L:(c + 1) * L, :] * (HEAD_DIM ** -0.5)).T for c in range(n_chunk)]

    h_all = []
    for h in range(N_HEADS):
        hs = _hs(h)
        qb = [q_ref[c * L:(c + 1) * L, hs].astype(BF16) for c in range(n_chunk)]
        kt = [k_t[c][hs, :] for c in range(n_chunk)]
        vx = [_v_ext(v_ref, slice(c * L, (c + 1) * L), hs) for c in range(n_chunk)]
        qk = [_dot(qb[c], kt[c].astype(BF16)) for c in range(n_chunk)]
        h_sum = [None] * n_chunk
        for d in range(2):
            tri = tril if d == 0 else triu
            j = N_HEADS * d + h
            s_ext = s0_ref[d, h] if has_state else None
            order = range(n_chunk) if d == 0 else range(n_chunk - 1, -1, -1)
            for ci, c in enumerate(order):
                wgt = jnp.exp(jnp.where(tri, w8[c][j:j + 1, :] - cols[c][:, j:j + 1], NEG))
                tot = _dot((qk[c] * wgt).astype(BF16), vx[c])
                if s_ext is not None:
                    tot = tot + cols[c][:, nu + j:nu + j + 1] * _dot(qb[c], s_ext.astype(BF16))
                den = jnp.maximum(jnp.abs(tot[:, HEAD_DIM:HEAD_DIM + 1]), cols[c][:, 2 * nu + j:2 * nu + j + 1])
                hd = tot[:, 0:HEAD_DIM] / den
                h_sum[c] = hd if h_sum[c] is None else h_sum[c] + hd
                if (not has_state) or ci < n_chunk - 1:
                    upd = _dot((kt[c] * wk8[c][j:j + 1, :]).astype(BF16), vx[c])
                    s_ext = upd if s_ext is None else upd + dec8[c][j:j + 1, :] * s_ext
            if not has_state:
                sn_ref[d, h] = s_ext
        h_all.append(h_sum)
    jm = _head_mean_matrix()
    for c in range(n_chunk):
        rows = slice(c * L, (c + 1) * L)
        gated = jax.nn.sigmoid(o_ref[rows, :]) * jnp.concatenate([h_all[h][c] for h in range(N_HEADS)], axis=1)
        out_ref[rows, :] = _head_groupnorm(gated, gain_ref[...], jm)


def _mlstm(z, gates, bias128, gain, n_seq, t, state):
    has_state = state is not None
    nu = 2 * N_HEADS
    in_specs = [_zspec(t, 0), _zspec(t, 1), _zspec(t, 2), _zspec(t, 3),
                pl.BlockSpec((t, LANES), lambda b: (b, 0)),
                _const_spec((1, LANES)), _const_spec((1, GROUP_W))]
    args = [z, z, z, z, gates, bias128, gain]
    out_specs = [pl.BlockSpec((t, GROUP_W), lambda b: (b, 0))]
    out_shape = [jax.ShapeDtypeStruct((n_seq * t, GROUP_W), F32)]
    st_specs = [pl.BlockSpec((None, 2, N_HEADS, HEAD_DIM, LANES), lambda b: (b, 0, 0, 0, 0)),
                pl.BlockSpec((None, nu, 1), lambda b: (b, 0, 0))]
    if has_state:
        in_specs += st_specs
        args += list(state)
    else:
        out_specs += st_specs
        out_shape += [jax.ShapeDtypeStruct((n_seq, 2, N_HEADS, HEAD_DIM, LANES), F32),
                      jax.ShapeDtypeStruct((n_seq, nu, 1), F32)]
    return pl.pallas_call(
        functools.partial(_mlstm_kernel, n_chunk=t // CHUNK, has_state=has_state),
        grid=(n_seq,), in_specs=in_specs, out_specs=out_specs, out_shape=out_shape,
        compiler_params=_params(("arbitrary",)), name="mlstm",
    )(*args)


def _ret_kernel(*refs, n_chunk, has_state):
    if has_state:
        q_ref, k_ref, v_ref, g_ref, lg_ref, gain_ref, s0_ref, out_ref, dmat_ref, col_ref, row_ref = refs
    else:
        q_ref, k_ref, v_ref, g_ref, lg_ref, gain_ref, out_ref, sn_ref, dmat_ref, col_ref, row_ref = refs
    L = CHUNK

    @pl.when(pl.program_id(0) == 0)
    def _():
        row = lax.broadcasted_iota(jnp.int32, (L, L), 0)
        col = lax.broadcasted_iota(jnp.int32, (L, L), 1)
        diff = (row - col).astype(F32)
        log_g = _log_sigmoid(lg_ref[...])
        pos_c = lax.broadcasted_iota(jnp.int32, (L, 1), 0).astype(F32)
        pos_r = lax.broadcasted_iota(jnp.int32, (1, L), 1).astype(F32)
        for h in range(N_HEADS):
            lgf = log_g[0:1, h:h + 1]
            lgb = log_g[1:2, h:h + 1]
            dmat_ref[h] = (jnp.where(diff >= 0, jnp.exp(jnp.maximum(diff, 0.0) * lgf), 0.0)
                           + jnp.where(diff <= 0, jnp.exp(jnp.maximum(-diff, 0.0) * lgb), 0.0))
            col_ref[:, h:h + 1] = jnp.exp((pos_c + 1.0) * lgf)
            col_ref[:, N_HEADS + h:N_HEADS + h + 1] = jnp.exp((L - pos_c) * lgb)
            row_ref[h:h + 1, :] = jnp.exp((L - 1.0 - pos_r) * lgf)
            row_ref[N_HEADS + h:N_HEADS + h + 1, :] = jnp.exp(pos_r * lgb)
            row_ref[2 * N_HEADS + h:2 * N_HEADS + h + 1, :] = jnp.exp(L * lgf) + jnp.zeros((1, L), F32)
            row_ref[3 * N_HEADS + h:3 * N_HEADS + h + 1, :] = jnp.exp(L * lgb) + jnp.zeros((1, L), F32)

    k_t = [(k_ref[c * L:(c + 1) * L, :] * (HEAD_DIM ** -0.5)).T for c in range(n_chunk)]
    o_all = []
    for h in range(N_HEADS):
        hs = _hs(h)
        zeta_f = row_ref[h:h + 1, :]
        zeta_b = row_ref[N_HEADS + h:N_HEADS + h + 1, :]
        gl_f = row_ref[2 * N_HEADS + h:2 * N_HEADS + h + 1, 0:1]
        gl_b = row_ref[3 * N_HEADS + h:3 * N_HEADS + h + 1, 0:1]
        kt = [k_t[c][hs, :] for c in range(n_chunk)]
        vb = [v_ref[c * L:(c + 1) * L, hs].astype(BF16) for c in range(n_chunk)]
        sf = [None] * (n_chunk + 1)
        sb = [None] * (n_chunk + 1)
        if has_state:
            sf[0] = s0_ref[0, h]
            sb[n_chunk] = s0_ref[1, h]
        for c in range(n_chunk):
            if (not has_state) or c < n_chunk - 1:
                upd = _dot((kt[c] * zeta_f).astype(BF16), vb[c])
                sf[c + 1] = upd if sf[c] is None else upd + gl_f * sf[c]
        for c in range(n_chunk - 1, -1, -1):
            if (not has_state) or c > 0:
                upd = _dot((kt[c] * zeta_b).astype(BF16), vb[c])
                sb[c] = upd if sb[c + 1] is None else upd + gl_b * sb[c + 1]
        if not has_state:
            sn_ref[0, h] = sf[n_chunk]
            sn_ref[1, h] = sb[0]
        o_h = []
        for c in range(n_chunk):
            qb = q_ref[c * L:(c + 1) * L, hs].astype(BF16)
            att = _dot(qb, kt[c].astype(BF16)) * dmat_ref[h]
            o = _dot(att.astype(BF16), vb[c])
            if has_state:
                o = o + col_ref[:, h:h + 1] * _dot(qb, sf[c].astype(BF16))
                o = o + col_ref[:, N_HEADS + h:N_HEADS + h + 1] * _dot(qb, sb[c + 1].astype(BF16))
            o_h.append(o)
        o_all.append(o_h)
    jm = _head_mean_matrix()
    for c in range(n_chunk):
        rows = slice(c * L, (c + 1) * L)
        o = jnp.concatenate([o_all[h][c] for h in range(N_HEADS)], axis=1)
        out_ref[rows, :] = _silu(g_ref[rows, :]) * _head_groupnorm(o, gain_ref[...], jm)


def _retention(z, lg8, gain, n_seq, t, state):
    has_state = state is not None
    in_specs = [_zspec(t, 4), _zspec(t, 5), _zspec(t, 6), _zspec(t, 7),
                _const_spec((8, LANES)), _const_spec((1, GROUP_W))]
    args = [z, z, z, z, lg8, gain]
    out_specs = [pl.BlockSpec((t, GROUP_W), lambda b: (b, 0))]
    out_shape = [jax.ShapeDtypeStruct((n_seq * t, GROUP_W), F32)]
    st_spec = pl.BlockSpec((None, 2, N_HEADS, HEAD_DIM, HEAD_DIM), lambda b: (b, 0, 0, 0, 0))
    if has_state:
        in_specs.append(st_spec)
        args.append(state)
    else:
        out_specs.append(st_spec)
        out_shape.append(jax.ShapeDtypeStruct((n_seq, 2, N_HEADS, HEAD_DIM, HEAD_DIM), F32))
    return pl.pallas_call(
        functools.partial(_ret_kernel, n_chunk=t // CHUNK, has_state=has_state),
        grid=(n_seq,), in_specs=in_specs, out_specs=out_specs, out_shape=out_shape,
        scratch_shapes=[pltpu.VMEM((N_HEADS, CHUNK, CHUNK), F32), pltpu.VMEM((CHUNK, LANES), F32),
                        pltpu.VMEM((4 * N_HEADS, CHUNK), F32)],
        compiler_params=_params(("arbitrary",)), name="retention",
    )(*args)


def _lambda(lam_ref, lam_init):
    v = lam_ref[...]
    s1 = jnp.sum(v[0:1, :] * v[1:2, :], axis=-1, keepdims=True)
    s2 = jnp.sum(v[2:3, :] * v[3:4, :], axis=-1, keepdims=True)
    return jnp.exp(s1) - jnp.exp(s2) + lam_init


def _first_half(shape):
    lane = lax.broadcasted_iota(jnp.int32, shape, 1)
    return (lane % (A_SUB // 2)) < (A_SUB // 4)


def _rope(x, cos, sin_signed):
    n = x.shape[1]
    first_half = _first_half(x.shape)
    partner = jnp.where(first_half, pltpu.roll(x, n - A_SUB // 4, 1), pltpu.roll(x, A_SUB // 4, 1))
    return x * cos + partner * sin_signed


def _attn_kernel(*refs, t, has_cache, lam_init):
    if has_cache:
        q_ref, k_ref, v_ref, kc_ref, vc_ref, lam_ref, gain_ref, out_ref, qr_ref, kr_ref, cos_ref, sin_ref = refs
    else:
        q_ref, k_ref, v_ref, lam_ref, gain_ref, out_ref, qr_ref, kr_ref = refs
    lam = _lambda(lam_ref, lam_init)
    scale = A_SUB ** -0.5
    n_qb = t // CHUNK

    if has_cache:
        @pl.when(pl.program_id(0) == 0)
        def _():
            tok = lax.broadcasted_iota(jnp.int32, (t, GROUP_W), 0)
            lane = lax.broadcasted_iota(jnp.int32, (t, GROUP_W), 1)
            pos = jnp.where((lane % A_SUB) < (A_SUB // 2), tok // GRID_W, tok % GRID_W).astype(F32)
            n_freq = A_SUB // 4
            inv = jnp.exp((lane % n_freq).astype(F32) * (-math.log(ROPE_BASE) / n_freq))
            ang = pos * inv
            cos_ref[...] = jnp.cos(ang)
            sin_ref[...] = jnp.where(_first_half((t, GROUP_W)), -jnp.sin(ang), jnp.sin(ang))

        kr_ref[...] = _rope(k_ref[...], cos_ref[...], sin_ref[...]).astype(BF16)
        qr_ref[...] = (_rope(q_ref[...], cos_ref[...], sin_ref[...]) * scale).astype(BF16)
    else:
        kr_ref[...] = k_ref[...].astype(BF16)
        qr_ref[...] = (q_ref[...] * scale).astype(BF16)

    o_all = []
    for h in range(N_HEADS):
        hs = _hs(h)
        vx = _v_ext(v_ref, slice(0, t), hs)
        if has_cache:
            vcx = _v_ext(vc_ref, slice(0, vc_ref.shape[0]), hs)
        o_h = []
        for qi in range(n_qb):
            rows = slice(qi * CHUNK, (qi + 1) * CHUNK)
            o_maps = []
            for m in range(2):
                ms = slice(HEAD_DIM * h + A_SUB * m, HEAD_DIM * h + A_SUB * (m + 1))
                qm = qr_ref[rows, ms]
                s_l = _dot_nt(qm, kr_ref[:, ms])
                mx = jnp.max(s_l, axis=-1, keepdims=True)
                if has_cache:
                    s_c = _dot_nt(qm, kc_ref[:, ms].astype(BF16))
                    mx = jnp.maximum(mx, jnp.max(s_c, axis=-1, keepdims=True))
                tot = _dot(jnp.exp(s_l - mx).astype(BF16), vx)
                if has_cache:
                    tot = tot + _dot(jnp.exp(s_c - mx).astype(BF16), vcx)
                o_maps.append(tot[:, 0:HEAD_DIM] / tot[:, HEAD_DIM:HEAD_DIM + 1])
            o_h.append(o_maps[0] - lam * o_maps[1])
        o_all.append(o_h)
    jm = _head_mean_matrix()
    for qi in range(n_qb):
        rows = slice(qi * CHUNK, (qi + 1) * CHUNK)
        o = jnp.concatenate([o_all[h][qi] for h in range(N_HEADS)], axis=1)
        out_ref[rows, :] = _head_rms(o, gain_ref[...], jm) * (1.0 - lam_init)


def _attention(z, lam8, gain, n_seq, t, cache, lam_init):
    has_cache = cache is not None
    in_specs = [_zspec(t, 8), _zspec(t, 9), _zspec(t, 10)]
    args = [z, z, z]
    scratch = [pltpu.VMEM((t, GROUP_W), BF16), pltpu.VMEM((t, GROUP_W), BF16)]
    if has_cache:
        kc, vc = cache
        p = kc.shape[1]
        in_specs += [pl.BlockSpec((None, p, GROUP_W), lambda b: (b, 0, 0))] * 2
        args += [kc, vc]
        scratch += [pltpu.VMEM((t, GROUP_W), F32), pltpu.VMEM((t, GROUP_W), F32)]
    in_specs += [_const_spec((8, LANES)), _const_spec((1, GROUP_W))]
    args += [lam8, gain]
    return pl.pallas_call(
        functools.partial(_attn_kernel, t=t, has_cache=has_cache, lam_init=lam_init),
        grid=(n_seq,), in_specs=in_specs,
        out_specs=pl.BlockSpec((t, GROUP_W), lambda b: (b, 0)),
        out_shape=jax.ShapeDtypeStruct((n_seq * t, GROUP_W), F32),
        scratch_shapes=scratch,
        compiler_params=_params(("arbitrary",)), name="diff_attention",
    )(*args)


def _cmul(ar, ai, br, bi):
    return ar * br - ai * bi, ar * bi + ai * br


def _lam_bar(lre, lim, dt):
    mag = jnp.exp(lre * dt)
    return mag * jnp.cos(lim * dt), mag * jnp.sin(lim * dt)


def _cpow_int(br, bi, e, n_bits):
    pr = jnp.ones(e.shape, F32)
    pi = jnp.zeros(e.shape, F32)
    for bit in range(n_bits):
        on = ((e >> bit) & 1) == 1
        qr, qi = _cmul(pr, pi, br, bi)
        pr, pi = jnp.where(on, qr, pr), jnp.where(on, qi, pi)
        if bit + 1 < n_bits:
            br, bi = _cmul(br, bi, br, bi)
    return pr, pi


def _zoh_coef(lre, lim, br, bi):
    den = lre * lre + lim * lim
    return ((br - 1.0) * lre + bi * lim) / den, (bi * lre - (br - 1.0) * lim) / den


def _toeplitz_rows(kall):
    n = S5_L * S5_GC
    lane = lax.broadcasted_iota(jnp.int32, (S5_GC, n), 1)
    pieces = []
    for s in range(S5_L):
        shifted = kall if s == 0 else pltpu.roll(kall, S5_GC * s, 1)
        pieces.append(jnp.where(lane >= S5_GC * s, shifted, 0.0))
    return jnp.concatenate(pieces, axis=0)


def _s5_prep_kernel(lam_c_re, lam_c_im, lam_r_re, lam_r_im, ldt_ref, b_t_re, b_t_im, b_tile_re, b_tile_im,
                    c_re, c_im, c_tile_re, c_tile_im, d_ref, t_ref, bcf_ref, bcb_ref, ccf_ref, ccb_ref, lam_ref):
    n = S5_L * S5_GC
    lane_blk = lax.broadcasted_iota(jnp.int32, (S5_P, n), 1) // S5_GC
    row_blk = lax.broadcasted_iota(jnp.int32, (n, S5_P), 0) // S5_GC
    bits = S5_L.bit_length()

    def bases(d):
        dt = jnp.exp(ldt_ref[d])
        return (_lam_bar(lam_c_re[d][:, 0:1], lam_c_im[d][:, 0:1], dt), _lam_bar(lam_r_re[d], lam_r_im[d], dt))

    def tile_rows(x):
        return jnp.concatenate([x] * S5_L, axis=0)

    (cbr, cbi), (rbr, rbi) = bases(0)
    kr, ki = _zoh_coef(lam_r_re[0], lam_r_im[0], rbr, rbi)
    bbt_re, bbt_im = _cmul(kr, ki, b_t_re[0], b_t_im[0])
    pr, pi = _cpow_int(cbr, cbi, lane_blk, bits)
    cl_re, cl_im = _cmul(c_tile_re[0], c_tile_im[0], pr, pi)
    t_f = _toeplitz_rows(_dot(bbt_re, cl_re, HI) - _dot(bbt_im, cl_im, HI))
    pr, pi = _cpow_int(rbr, rbi, S5_L - 1 - row_blk, bits)
    re, im = _cmul(tile_rows(bbt_re), tile_rows(bbt_im), pr, pi)
    bcf_ref[...] = jnp.concatenate([re, im], axis=1).astype(BF16)
    re, im = _cmul(cl_re, cl_im, cbr, cbi)
    ccf_ref[...] = jnp.concatenate([re, -im], axis=0).astype(BF16)
    pr, pi = _cpow_int(rbr, rbi, jnp.full((1, S5_P), S5_L, jnp.int32), bits)
    lam_ref[0:1, :] = jnp.concatenate([pr, pr], axis=1)
    lam_ref[1:2, :] = jnp.concatenate([-pi, pi], axis=1)

    (cbr, cbi), (rbr, rbi) = bases(1)
    kr, ki = _zoh_coef(lam_c_re[1][:, 0:1], lam_c_im[1][:, 0:1], cbr, cbi)
    bb_re, bb_im = _cmul(kr, ki, b_tile_re[1], b_tile_im[1])
    pr, pi = _cpow_int(cbr, cbi, lane_blk, bits)
    bl_re, bl_im = _cmul(bb_re, bb_im, pr, pi)
    t_b = _toeplitz_rows(_dot(c_re[1], bl_re, HI) - _dot(c_im[1], bl_im, HI)).T
    kr, ki = _zoh_coef(lam_r_re[1], lam_r_im[1], rbr, rbi)
    bbt_re, bbt_im = _cmul(kr, ki, b_t_re[1], b_t_im[1])
    pr, pi = _cpow_int(rbr, rbi, row_blk, bits)
    re, im = _cmul(tile_rows(bbt_re), tile_rows(bbt_im), pr, pi)
    bcb_ref[...] = jnp.concatenate([re, im], axis=1).astype(BF16)
    pr, pi = _cpow_int(cbr, cbi, S5_L - lane_blk, bits)
    re, im = _cmul(c_tile_re[1], c_tile_im[1], pr, pi)
    ccb_ref[...] = jnp.concatenate([re, -im], axis=0).astype(BF16)
    pr, pi = _cpow_int(rbr, rbi, jnp.full((1, S5_P), S5_L, jnp.int32), bits)
    lam_ref[2:3, :] = jnp.concatenate([pr, pr], axis=1)
    lam_ref[3:4, :] = jnp.concatenate([-pi, pi], axis=1)
    lam_ref[4:8, :] = jnp.zeros((4, 2 * S5_P), F32)

    eye = (lax.broadcasted_iota(jnp.int32, (n, n), 0) == lax.broadcasted_iota(jnp.int32, (n, n), 1))
    t_ref[...] = (t_f + t_b + jnp.where(eye, d_ref[...], 0.0)).astype(BF16)


def _s5_prep(lam_re, lam_im, log_dt, b_re, b_im, c_re, c_im, s5_d):
    n = S5_L * S5_GC
    lg = (DEPTH, 2, S5_G)
    bc = lambda a, shape: jnp.broadcast_to(a, lg + shape)
    swap = lambda a: jnp.swapaxes(a, -1, -2)
    ins = [bc(lam_re[..., :, None], (S5_P, n)), bc(lam_im[..., :, None], (S5_P, n)),
           lam_re[..., None, :], lam_im[..., None, :], log_dt[..., None, None],
           swap(b_re), swap(b_im), jnp.tile(b_re, (1, 1, 1, 1, S5_L)), jnp.tile(b_im, (1, 1, 1, 1, S5_L)),
           c_re, c_im, jnp.tile(swap(c_re), (1, 1, 1, 1, S5_L)), jnp.tile(swap(c_im), (1, 1, 1, 1, S5_L))]
    d_tile = jnp.tile(s5_d.reshape(DEPTH, S5_G, 1, S5_GC), (1, 1, 1, S5_L))

    def dir_spec(a):
        return pl.BlockSpec((None, 2, None) + a.shape[3:], lambda l, g: (l, 0, g, 0, 0))

    def out(rows, cols, dtype):
        return (pl.BlockSpec((None, None, rows, cols), lambda l, g: (l, g, 0, 0)),
                jax.ShapeDtypeStruct((DEPTH, S5_G, rows, cols), dtype))

    outs = [out(n, n, BF16), out(n, 2 * S5_P, BF16), out(n, 2 * S5_P, BF16),
            out(2 * S5_P, n, BF16), out(2 * S5_P, n, BF16), out(8, 2 * S5_P, F32)]
    return pl.pallas_call(
        _s5_prep_kernel,
        grid=(DEPTH, S5_G),
        in_specs=[dir_spec(a) for a in ins] + [pl.BlockSpec((None, None, 1, n), lambda l, g: (l, g, 0, 0))],
        out_specs=[o[0] for o in outs], out_shape=[o[1] for o in outs],
        compiler_params=_params(("arbitrary", "arbitrary")), name="s5_prep",
    )(*ins, d_tile)


def _s5_kernel(*refs, n_seq, n_k, has_state):
    if has_state:
        (ua_ref, ub_ref, t_ref, bcf_ref, bcb_ref, ccf_ref, ccb_ref, lam_ref, x0_ref,
         ya_ref, yb_ref, ug_ref, yg_ref, inj_f, inj_b, inj_s, xin_f, xin_b) = refs
    else:
        (ua_ref, ub_ref, t_ref, bcf_ref, bcb_ref, ccf_ref, ccb_ref, lam_ref,
         ya_ref, yb_ref, xf_ref, xb_ref, ug_ref, yg_ref, inj_f, inj_b, inj_s, xin_f, xin_b) = refs
    half = S5_G // 2
    for u_ref, g0 in ((ua_ref, 0), (ub_ref, half)):
        u_i = [u_ref[:, i, :] for i in range(S5_L)]
        for gl in range(half):
            g = g0 + gl
            u = jnp.concatenate([u_i[i][:, S5_GC * gl:S5_GC * (gl + 1)] for i in range(S5_L)], axis=1).astype(BF16)
            ug_ref[g] = u
            inj_f[g] = _dot(u, bcf_ref[g])
            inj_b[g] = _dot(u, bcb_ref[g])

    for d, (inj, xin) in enumerate(((inj_f, xin_f), (inj_b, xin_b))):
        a = [lam_ref[g, 2 * d:2 * d + 1, :] for g in range(S5_G)]
        bs = [lam_ref[g, 2 * d + 1:2 * d + 2, :] for g in range(S5_G)]
        for g in range(S5_G):
            inj_s[g] = pltpu.roll(inj[g], S5_P, 1)
        x = [x0_ref[d, g] if has_state else jnp.zeros((n_seq, 2 * S5_P), F32) for g in range(S5_G)]
        xs = [pltpu.roll(v, S5_P, 1) for v in x]
        for k in (range(n_k) if d == 0 else range(n_k - 1, -1, -1)):
            rows = pl.ds(k, n_seq, stride=n_k)
            for g in range(S5_G):
                xin[g, rows, :] = x[g]
                x[g], xs[g] = (a[g] * x[g] + bs[g] * xs[g] + inj[g, rows, :],
                               a[g] * xs[g] - bs[g] * x[g] + inj_s[g, rows, :])
        if not has_state:
            for g in range(S5_G):
                (xf_ref if d == 0 else xb_ref)[g] = x[g]

    for g in range(S5_G):
        yg_ref[g] = (_dot(ug_ref[g], t_ref[g]) + _dot(xin_f[g].astype(BF16), ccf_ref[g])
                     + _dot(xin_b[g].astype(BF16), ccb_ref[g]))
    for y_ref, g0 in ((ya_ref, 0), (yb_ref, half)):
        for i in range(S5_L):
            y_ref[:, i, :] = jnp.concatenate(
                [yg_ref[g0 + gl][:, S5_GC * i:S5_GC * (i + 1)] for gl in range(half)], axis=1)


def _s5(z3, ops_l, n_seq, n_k, x0):
    has_state = x0 is not None
    n = S5_L * S5_GC
    r = n_seq * n_k
    w = 2 * S5_P
    full = lambda a: pl.BlockSpec(a.shape, lambda i: (0,) * a.ndim)
    args = [z3, z3] + list(ops_l) + ([x0] if has_state else [])
    in_specs = [pl.BlockSpec((r, S5_L, LANES), lambda i: (0, 0, N_MAIN // LANES - 2)),
                pl.BlockSpec((r, S5_L, LANES), lambda i: (0, 0, N_MAIN // LANES - 1))] + [full(a) for a in args[2:]]
    out_shape = [jax.ShapeDtypeStruct((r, S5_L, LANES), F32)] * 2
    out_specs = [pl.BlockSpec((r, S5_L, LANES), lambda i: (0, 0, 0))] * 2
    if not has_state:
        out_shape += [jax.ShapeDtypeStruct((S5_G, n_seq, w), F32)] * 2
        out_specs += [pl.BlockSpec((S5_G, n_seq, w), lambda i: (0, 0, 0))] * 2
    return pl.pallas_call(
        functools.partial(_s5_kernel, n_seq=n_seq, n_k=n_k, has_state=has_state),
        grid=(1,), in_specs=in_specs, out_specs=out_specs, out_shape=out_shape,
        scratch_shapes=[pltpu.VMEM((S5_G, r, n), BF16), pltpu.VMEM((S5_G, r, n), F32)]
        + [pltpu.VMEM((S5_G, r, w), F32)] * 5,
        compiler_params=_params(("arbitrary",)), name="s5_scan",
    )(*args)


def _out_ffn_kernel(x_ref, m_ref, r_ref, a_ref, ya_ref, yb_ref, mod_ref, wglu_ref, bglu_ref, wout_ref,
                    n_post_ref, n_pre_ref, n_fpost_ref, wg_ref, wu_ref, wd_ref, o_ref):
    def mod(i):
        return mod_ref[:, i * D_MODEL:(i + 1) * D_MODEL]

    y = jnp.concatenate([ya_ref[...], yb_ref[...]], axis=1)
    gs = 0.5 * y * (1.0 + jnp.tanh(math.sqrt(2.0 / math.pi) * (y + 0.044715 * (y * y * y))))
    s_out = gs * jax.nn.sigmoid(_dot(gs.astype(BF16), wglu_ref[...]) + bglu_ref[...])
    mixed = jnp.concatenate([m_ref[...], r_ref[...], a_ref[...], s_out], axis=1).astype(BF16)
    x1 = x_ref[...] + mod(2) * _rms(_dot(mixed, wout_ref[...]), n_post_ref[...])
    h = (_rms(x1, n_pre_ref[...]) * (1.0 + mod(4)) + mod(3)).astype(BF16)
    act = (_silu(_dot(h, wg_ref[...])) * _dot(h, wu_ref[...])).astype(BF16)
    o_ref[...] = x1 + mod(5) * _rms(_dot(act, wd_ref[...]), n_fpost_ref[...])


def _out_ffn(x, m_out, r_out, a_out, y_halves, mod_l, mod_row, wl):
    n = x.shape[0]
    row = lambda w: pl.BlockSpec((ROW_TILE, w), lambda i: (i, 0))

    def resident(shape):
        return pl.BlockSpec(shape, lambda i: (0,) * len(shape), pipeline_mode=pl.Buffered(1))

    return pl.pallas_call(
        _out_ffn_kernel,
        grid=(n // ROW_TILE,),
        in_specs=[row(D_MODEL), row(GROUP_W), row(GROUP_W), row(GROUP_W), row(LANES), row(LANES),
                  pl.BlockSpec((None, 1, 6 * D_MODEL), lambda i: (mod_row(i), 0, 0)),
                  resident((GROUP_W, GROUP_W)), resident((1, GROUP_W)), resident((D_MODEL, D_MODEL)),
                  resident((1, D_MODEL)), resident((1, D_MODEL)), resident((1, D_MODEL)),
                  resident((D_MODEL, D_FF)), resident((D_MODEL, D_FF)), resident((D_FF, D_MODEL))],
        out_specs=row(D_MODEL),
        out_shape=jax.ShapeDtypeStruct((n, D_MODEL), F32),
        compiler_params=_params(("arbitrary",)), name="out_ffn",
    )(x, m_out, r_out, a_out, *y_halves, mod_l, wl['w_glu'], wl['b_glu'], wl['w_out'],
      wl['n_mix_post'], wl['n_ffn_pre'], wl['n_ffn_post'], wl['w_gate'], wl['w_up'], wl['w_down'])


def _layer(x, n_seq, t, mod_l, mod_row, wl, s5_ops, lam_init, st):
    z, gates = _in_proj(x, mod_l, mod_row, wl['n_mix_pre'], wl['w_in'])
    m_res = _mlstm(z, gates, wl['m_bias'], wl['m_norm'], n_seq, t, None if st is None else st[0:2])
    r_res = _retention(z, wl['r_decay'], wl['r_norm'], n_seq, t, None if st is None else st[2])
    a_out = _attention(z, wl['a_lam'], wl['a_norm'], n_seq, t, None if st is None else st[3:5], lam_init)
    s_res = _s5(z.reshape(n_seq * t // S5_L, S5_L, N_MAIN), s5_ops, n_seq, t // S5_L, None if st is None else st[5])
    y_halves = [s_res[i].reshape(n_seq * t, LANES) for i in range(2)]
    x_new = _out_ffn(x, m_res[0], r_res[0], a_out, y_halves, mod_l, mod_row, wl)
    if st is not None:
        return x_new, None
    k_new = z[:, 9 * GROUP_W:10 * GROUP_W].reshape(n_seq, t, N_HEADS, HEAD_DIM)
    v_new = z[:, 10 * GROUP_W:11 * GROUP_W].reshape(n_seq, t, N_HEADS, HEAD_DIM)
    xs = jnp.stack([s_res[2], s_res[3]], axis=0).reshape(2, S5_G, n_seq, 2, S5_P).transpose(2, 0, 1, 3, 4)
    s_ext = m_res[1]
    return x_new, (s_ext[..., :HEAD_DIM], s_ext[..., HEAD_DIM], m_res[2].reshape(n_seq, 2, N_HEADS), r_res[1],
                   k_new, v_new, xs[:, :, :, 0], xs[:, :, :, 1])


def _pad_rows(a, rows=8, cols=LANES):
    return jnp.zeros((rows, cols), F32).at[:a.shape[0], :a.shape[1]].set(a)


def kernel(x_prompt, x_sample, state_mlstm_C, state_mlstm_n, state_mlstm_m, state_ret, cache_diff_k, cache_diff_v, state_s5_re, state_s5_im, c, c_ctx, w_ada, b_ada, n_mix_pre, n_mix_post, n_ffn_pre, n_ffn_post, w_in, w_out, m_gate_bias, m_norm, r_decay_logit, r_norm, a_lam_q1, a_lam_k1, a_lam_q2, a_lam_k2, a_norm, s5_lam_re, s5_lam_im, s5_log_dt, s5_b_re, s5_b_im, s5_c_re, s5_c_im, s5_d, s5_w_glu, s5_b_glu, w_ffn_gate, w_ffn_up, w_ffn_down):
    n_ctx, t_ctx, _ = x_prompt.shape
    n_lat, t_lat, _ = x_sample.shape
    past = cache_diff_k.shape[2]

    cond8 = jnp.concatenate([c, c_ctx[None, :], jnp.zeros((8 - n_lat - 1, D_MODEL), F32)], axis=0)
    mod = _ada(cond8, w_ada, b_ada).reshape(DEPTH, 8, 1, 6 * D_MODEL)
    s5_all = _s5_prep(s5_lam_re, s5_lam_im, s5_log_dt, s5_b_re, s5_b_im, s5_c_re, s5_c_im, s5_d)

    n_gate = 4 * N_HEADS
    layers = []
    for l in range(DEPTH):
        w = w_in[l]
        w_perm = jnp.concatenate([w[:, :4 * GROUP_W], w[:, 4 * GROUP_W + n_gate:],
                                  w[:, 4 * GROUP_W:4 * GROUP_W + n_gate],
                                  jnp.zeros((D_MODEL, LANES - n_gate), F32)], axis=1).astype(BF16)
        row = lambda a: a[l][None, :]
        layers.append(dict(
            w_in=w_perm, w_out=w_out[l].astype(BF16), w_glu=s5_w_glu[l].astype(BF16), b_glu=row(s5_b_glu),
            w_gate=w_ffn_gate[l].astype(BF16), w_up=w_ffn_up[l].astype(BF16), w_down=w_ffn_down[l].astype(BF16),
            n_mix_pre=row(n_mix_pre), n_mix_post=row(n_mix_post), n_ffn_pre=row(n_ffn_pre),
            n_ffn_post=row(n_ffn_post), m_norm=row(m_norm), r_norm=row(r_norm), a_norm=row(a_norm),
            m_bias=_pad_rows(m_gate_bias[l][None, :], 1), r_decay=_pad_rows(r_decay_logit[l]),
            a_lam=_pad_rows(jnp.stack([a_lam_q1[l], a_lam_k1[l], a_lam_q2[l], a_lam_k2[l]]))))
    lam_inits = [0.8 - 0.6 * math.exp(-0.3 * l) for l in range(DEPTH)]

    x = x_prompt.reshape(n_ctx * t_ctx, D_MODEL)
    new_states = []
    for l in range(DEPTH):
        x, st = _layer(x, n_ctx, t_ctx, mod[l], lambda i: n_lat, layers[l],
                       [a[l] for a in s5_all], lam_inits[l], None)
        new_states.append(st)
    y_prompt = x.reshape(n_ctx, t_ctx, D_MODEL)

    x = x_sample.reshape(n_lat * t_lat, D_MODEL)
    tiles_per_seq = t_lat // ROW_TILE
    for l in range(DEPTH):
        x0 = jnp.stack([state_s5_re[:, l], state_s5_im[:, l]], axis=3)
        x0 = x0.transpose(1, 2, 0, 3, 4).reshape(2, S5_G, n_lat, 2 * S5_P)
        s_ext0 = jnp.concatenate([state_mlstm_C[:, l], state_mlstm_n[:, l][..., None],
                                  jnp.zeros((n_lat, 2, N_HEADS, HEAD_DIM, LANES - HEAD_DIM - 1), F32)], axis=-1)
        st = (s_ext0, state_mlstm_m[:, l].reshape(n_lat, 2 * N_HEADS, 1), state_ret[:, l],
              cache_diff_k[:, l].reshape(n_lat, past, GROUP_W), cache_diff_v[:, l].reshape(n_lat, past, GROUP_W), x0)
        x, _ = _layer(x, n_lat, t_lat, mod[l], lambda i: i // tiles_per_seq, layers[l],
                      [a[l] for a in s5_all], lam_inits[l], st)
    y_sample = x.reshape(n_lat, t_lat, D_MODEL)

    outs = [jnp.stack([s[i] for s in new_states], axis=1) for i in range(8)]
    return (y_prompt, y_sample, *outs)
```

```python
import functools
import math

import jax
import jax.numpy as jnp
from jax import lax
from jax.experimental import pallas as pl
from jax.experimental.pallas import tpu as pltpu

F32 = jnp.float32
BF16 = jnp.bfloat16
HI = lax.Precision.HIGHEST

D_MODEL = 1024
DEPTH = 2
GRID_W = 64
HEAD_DIM = 64
GROUP_W = 256
N_HEADS = 4
A_SUB = 32
S5_GC = 16
S5_G = 16
S5_P = 64
D_FF = 2816
ROPE_BASE = 10000.0
EPS = 1e-6
N_MAIN = 12 * GROUP_W
N_MIX = 11 * GROUP_W
LANES = 128
CHUNK = 256
ROW_TILE = 256
S5_L = 16
NEG = -1e30
VMEM_LIMIT = 56 * 1024 * 1024


def _dot(a, b, precision=None):
    return jnp.dot(a, b, preferred_element_type=F32, precision=precision)


def _dot_nt(a, b):
    return lax.dot_general(a, b, (((1,), (1,)), ((), ())), preferred_element_type=F32)


def _log_sigmoid(x):
    return jnp.minimum(x, 0.0) - jnp.log(1.0 + jnp.exp(-jnp.abs(x)))


def _silu(x):
    return x * jax.nn.sigmoid(x)


def _rms(x, g):
    return x * lax.rsqrt(jnp.mean(x * x, axis=-1, keepdims=True) + EPS) * g


def _params(sem=None):
    return pltpu.CompilerParams(dimension_semantics=sem, vmem_limit_bytes=VMEM_LIMIT)


def _layer_spec(a, l, **kw):
    n = a.ndim - 1
    return pl.BlockSpec((None,) + a.shape[1:], lambda *_: (l,) + (0,) * n, **kw)


def _seq_layer_spec(a, l):
    n = a.ndim - 2
    return pl.BlockSpec((None, None) + a.shape[2:], lambda b: (b, l) + (0,) * n)


def _split_bf16(x):
    hi = x.astype(BF16)
    return hi, (x - hi.astype(F32)).astype(BF16)


def _ada_kernel(c_ref, w_ref, b_ref, o_ref):
    a_hi, a_lo = _split_bf16(_silu(c_ref[...]))
    w_hi, w_lo = _split_bf16(w_ref[...])
    o_ref[...] = _dot(a_hi, w_hi) + (_dot(a_lo, w_hi) + _dot(a_hi, w_lo)) + b_ref[...]


def _ada(cond8, w_ada, b_ada):
    tn = 1536
    return pl.pallas_call(
        _ada_kernel,
        grid=(DEPTH, 6 * D_MODEL // tn),
        in_specs=[pl.BlockSpec((8, D_MODEL), lambda l, j: (0, 0)),
                  pl.BlockSpec((None, D_MODEL, tn), lambda l, j: (l, 0, j)),
                  pl.BlockSpec((None, 1, tn), lambda l, j: (l, 0, j))],
        out_specs=pl.BlockSpec((None, 8, tn), lambda l, j: (l, 0, j)),
        out_shape=jax.ShapeDtypeStruct((DEPTH, 8, 6 * D_MODEL), F32),
        compiler_params=_params(("arbitrary", "arbitrary")),
        name="ada",
    )(cond8, w_ada, b_ada.reshape(DEPTH, 1, 6 * D_MODEL))


def _in_proj_kernel(x_ref, mod_ref, g_ref, w_ref, z_ref, u_ref, gate_ref, *kv_refs):
    h = _rms(x_ref[...], g_ref[...]) * (1.0 + mod_ref[:, D_MODEL:2 * D_MODEL]) + mod_ref[:, 0:D_MODEL]
    hb = h.astype(BF16)
    z = _dot(hb, w_ref[:, :N_MIX])
    z_ref[...] = z.astype(BF16)
    u = _dot(hb, w_ref[:, N_MIX:N_MAIN])
    u_ref[0] = u[:, :LANES]
    u_ref[1] = u[:, LANES:]
    gate_ref[...] = _dot(hb, w_ref[:, N_MAIN:])
    if kv_refs:
        kv_refs[0][...] = z[:, 9 * GROUP_W:10 * GROUP_W]
        kv_refs[1][...] = z[:, 10 * GROUP_W:11 * GROUP_W]


def _mod_spec(l, mod_row):
    return pl.BlockSpec((None, None, 1, 6 * D_MODEL), lambda i: (l, mod_row(i), 0, 0))


def _in_proj(x, mod, l, mod_row, gain, w, emit_kv):
    n = x.shape[0]
    row = lambda width: pl.BlockSpec((ROW_TILE, width), lambda i: (i, 0))
    out_specs = [row(N_MIX), pl.BlockSpec((2, ROW_TILE, LANES), lambda i: (0, i, 0)), row(LANES)]
    out_specs += [row(GROUP_W)] * (2 if emit_kv else 0)
    out_shape = [jax.ShapeDtypeStruct((n, N_MIX), BF16), jax.ShapeDtypeStruct((2, n, LANES), F32),
                 jax.ShapeDtypeStruct((n, LANES), F32)]
    out_shape += [jax.ShapeDtypeStruct((n, GROUP_W), F32)] * (2 if emit_kv else 0)
    return pl.pallas_call(
        _in_proj_kernel,
        grid=(n // ROW_TILE,),
        in_specs=[row(D_MODEL), _mod_spec(l, mod_row), _layer_spec(gain, l), _layer_spec(w, l)],
        out_specs=out_specs, out_shape=out_shape,
        compiler_params=_params(("arbitrary",)),
        name="in_proj",
    )(x, mod, gain, w)


def _tri_masks(n):
    row = lax.broadcasted_iota(jnp.int32, (n, n), 0)
    col = lax.broadcasted_iota(jnp.int32, (n, n), 1)
    return row >= col, row <= col


def _head_mean_matrix():
    r = lax.broadcasted_iota(jnp.int32, (GROUP_W, GROUP_W), 0) // HEAD_DIM
    c = lax.broadcasted_iota(jnp.int32, (GROUP_W, GROUP_W), 1) // HEAD_DIM
    return jnp.where(r == c, 1.0 / HEAD_DIM, 0.0).astype(BF16)


def _head_mean(x, j):
    hi, lo = _split_bf16(x)
    return _dot(hi, j) + _dot(lo, j)


def _head_groupnorm(x, g, j):
    xc = x - _head_mean(x, j)
    return xc * lax.rsqrt(_head_mean(xc * xc, j) + EPS) * g


def _head_rms(x, g, j):
    return x * lax.rsqrt(_head_mean(x * x, j) + EPS) * g


def _hs(h):
    return slice(HEAD_DIM * h, HEAD_DIM * (h + 1))


def _v_ext(v_ref, rows, hs):
    ones = jnp.ones((rows.stop - rows.start, HEAD_DIM), BF16)
    return jnp.concatenate([v_ref[rows, hs].astype(BF16), ones], axis=1)


def _zspec(t, colblk):
    return pl.BlockSpec((t, GROUP_W), lambda b: (b, colblk))


def _scan_max(x, reverse):
    n = x.shape[1]
    lane = lax.broadcasted_iota(jnp.int32, x.shape, 1)
    sh = 1
    while sh < n:
        if reverse:
            x = jnp.maximum(x, jnp.where(lane < n - sh, pltpu.roll(x, n - sh, 1), NEG))
        else:
            x = jnp.maximum(x, jnp.where(lane >= sh, pltpu.roll(x, sh, 1), NEG))
        sh *= 2
    return x


def _ends(x, is_fwd):
    return jnp.where(is_fwd, x[:, x.shape[1] - 1:], x[:, 0:1])


def _mlstm_kernel(*refs, n_chunk, has_state):
    if has_state:
        q_ref, k_ref, v_ref, o_ref, g_ref, bias_ref, gain_ref, s0_ref, m0_ref, out_ref = refs
    else:
        q_ref, k_ref, v_ref, o_ref, g_ref, bias_ref, gain_ref, out_ref, sn_ref, mn_ref = refs
    L = CHUNK
    nu = 2 * N_HEADS
    tril, triu = _tri_masks(L)
    is_fwd = lax.broadcasted_iota(jnp.int32, (nu, 1), 0) < N_HEADS

    b8, w8, mcum8 = [], [], []
    for c in range(n_chunk):
        p_t = (g_ref[c * L:(c + 1) * L, :] + bias_ref[...]).T
        lf = _log_sigmoid(p_t[nu:2 * nu, :])
        b = jnp.where(is_fwd, _dot(lf, triu.astype(F32), HI), _dot(lf, tril.astype(F32), HI))
        w = p_t[0:nu, :] - b
        b8.append(b)
        w8.append(w)
        mcum8.append(jnp.where(is_fwd, _scan_max(w, False), _scan_max(w, True)))

    m0 = m0_ref[...] if has_state else jnp.zeros((nu, 1), F32)
    m_in_f, m_in_b = [None] * n_chunk, [None] * n_chunk
    m = m0
    for c in range(n_chunk):
        m_in_f[c] = m
        m = (b8[c] + jnp.maximum(mcum8[c], m))[:, L - 1:]
    m_fin_f = m
    m = m0
    for c in range(n_chunk - 1, -1, -1):
        m_in_b[c] = m
        m = (b8[c] + jnp.maximum(mcum8[c], m))[:, 0:1]
    if not has_state:
        mn_ref[...] = jnp.where(is_fwd, m_fin_f, m)

    wk8, dec8, cols = [], [], []
    for c in range(n_chunk):
        m_in = jnp.where(is_fwd, m_in_f[c], m_in_b[c])
        g = jnp.maximum(mcum8[c], m_in)
        m_row = b8[c] + g
        m_new, b_last = _ends(m_row, is_fwd), _ends(b8[c], is_fwd)
        wk8.append(jnp.exp(b_last + w8[c] - m_new))
        dec8.append(jnp.exp(b_last + m_in - m_new))
        stats = jnp.concatenate([g, jnp.exp(m_in - g), jnp.exp(-m_row), jnp.zeros((LANES - 3 * nu, L), F32)], axis=0)
        cols.append(stats.T)

    k_t = [(k_ref[c * L:(c + 1) * L, :].astype(F32) * (HEAD_DIM ** -0.5)).T for c in range(n_chunk)]

    h_all = []
    for h in range(N_HEADS):
        hs = _hs(h)
        qb = [q_ref[c * L:(c + 1) * L, hs].astype(BF16) for c in range(n_chunk)]
        kt = [k_t[c][hs, :] for c in range(n_chunk)]
        vx = [_v_ext(v_ref, slice(c * L, (c + 1) * L), hs) for c in range(n_chunk)]
        qk = [_dot(qb[c], kt[c].astype(BF16)) for c in range(n_chunk)]
        h_sum = [None] * n_chunk
        for d in range(2):
            tri = tril if d == 0 else triu
            j = N_HEADS * d + h
            s_ext = s0_ref[d, h] if has_state else None
            order = range(n_chunk) if d == 0 else range(n_chunk - 1, -1, -1)
            for ci, c in enumerate(order):
                wgt = jnp.exp(jnp.where(tri, w8[c][j:j + 1, :] - cols[c][:, j:j + 1], NEG))
                tot = _dot((qk[c] * wgt).astype(BF16), vx[c])
                if s_ext is not None:
                    tot = tot + cols[c][:, nu + j:nu + j + 1] * _dot(qb[c], s_ext.astype(BF16))
                den = jnp.maximum(jnp.abs(tot[:, HEAD_DIM:HEAD_DIM + 1]), cols[c][:, 2 * nu + j:2 * nu + j + 1])
                hd = tot[:, 0:HEAD_DIM] / den
                h_sum[c] = hd if h_sum[c] is None else h_sum[c] + hd
                if (not has_state) or ci < n_chunk - 1:
                    upd = _dot((kt[c] * wk8[c][j:j + 1, :]).astype(BF16), vx[c])
                    s_ext = upd if s_ext is None else upd + dec8[c][j:j + 1, :] * s_ext
            if not has_state:
                sn_ref[d, h] = s_ext
        h_all.append(h_sum)
    jm = _head_mean_matrix()
    for c in range(n_chunk):
        rows = slice(c * L, (c + 1) * L)
        gated = jax.nn.sigmoid(o_ref[rows, :].astype(F32)) * jnp.concatenate([h_all[h][c] for h in range(N_HEADS)], axis=1)
        out_ref[rows, :] = _head_groupnorm(gated, gain_ref[...], jm)


def _mlstm(z, gates, bias128, gain, l, n_seq, t, state):
    has_state = state is not None
    nu = 2 * N_HEADS
    in_specs = [_zspec(t, 0), _zspec(t, 1), _zspec(t, 2), _zspec(t, 3),
                pl.BlockSpec((t, LANES), lambda b: (b, 0)),
                _layer_spec(bias128, l), _layer_spec(gain, l)]
    args = [z, z, z, z, gates, bias128, gain]
    out_specs = [pl.BlockSpec((t, GROUP_W), lambda b: (b, 0))]
    out_shape = [jax.ShapeDtypeStruct((n_seq * t, GROUP_W), F32)]
    if has_state:
        in_specs += [_seq_layer_spec(a, l) for a in state]
        args += list(state)
    else:
        out_specs += [pl.BlockSpec((None, 2, N_HEADS, HEAD_DIM, LANES), lambda b: (b, 0, 0, 0, 0)),
                      pl.BlockSpec((None, nu, 1), lambda b: (b, 0, 0))]
        out_shape += [jax.ShapeDtypeStruct((n_seq, 2, N_HEADS, HEAD_DIM, LANES), F32),
                      jax.ShapeDtypeStruct((n_seq, nu, 1), F32)]
    return pl.pallas_call(
        functools.partial(_mlstm_kernel, n_chunk=t // CHUNK, has_state=has_state),
        grid=(n_seq,), in_specs=in_specs, out_specs=out_specs, out_shape=out_shape,
        compiler_params=_params(("arbitrary",)), name="mlstm",
    )(*args)


def _ret_kernel(*refs, n_chunk, has_state):
    if has_state:
        q_ref, k_ref, v_ref, g_ref, lg_ref, gain_ref, s0_ref, out_ref, dmat_ref, col_ref, row_ref = refs
    else:
        q_ref, k_ref, v_ref, g_ref, lg_ref, gain_ref, out_ref, sn_ref, dmat_ref, col_ref, row_ref = refs
    L = CHUNK

    @pl.when(pl.program_id(0) == 0)
    def _():
        row = lax.broadcasted_iota(jnp.int32, (L, L), 0)
        col = lax.broadcasted_iota(jnp.int32, (L, L), 1)
        diff = (row - col).astype(F32)
        log_g = _log_sigmoid(lg_ref[...])
        pos_c = lax.broadcasted_iota(jnp.int32, (L, 1), 0).astype(F32)
        pos_r = lax.broadcasted_iota(jnp.int32, (1, L), 1).astype(F32)
        for h in range(N_HEADS):
            lgf = log_g[0:1, h:h + 1]
            lgb = log_g[1:2, h:h + 1]
            dmat_ref[h] = (jnp.where(diff >= 0, jnp.exp(jnp.maximum(diff, 0.0) * lgf), 0.0)
                           + jnp.where(diff <= 0, jnp.exp(jnp.maximum(-diff, 0.0) * lgb), 0.0))
            col_ref[:, h:h + 1] = jnp.exp((pos_c + 1.0) * lgf)
            col_ref[:, N_HEADS + h:N_HEADS + h + 1] = jnp.exp((L - pos_c) * lgb)
            row_ref[h:h + 1, :] = jnp.exp((L - 1.0 - pos_r) * lgf)
            row_ref[N_HEADS + h:N_HEADS + h + 1, :] = jnp.exp(pos_r * lgb)
            row_ref[2 * N_HEADS + h:2 * N_HEADS + h + 1, :] = jnp.exp(L * lgf) + jnp.zeros((1, L), F32)
            row_ref[3 * N_HEADS + h:3 * N_HEADS + h + 1, :] = jnp.exp(L * lgb) + jnp.zeros((1, L), F32)

    k_t = [(k_ref[c * L:(c + 1) * L, :].astype(F32) * (HEAD_DIM ** -0.5)).T for c in range(n_chunk)]
    o_all = []
    for h in range(N_HEADS):
        hs = _hs(h)
        zeta_f = row_ref[h:h + 1, :]
        zeta_b = row_ref[N_HEADS + h:N_HEADS + h + 1, :]
        gl_f = row_ref[2 * N_HEADS + h:2 * N_HEADS + h + 1, 0:1]
        gl_b = row_ref[3 * N_HEADS + h:3 * N_HEADS + h + 1, 0:1]
        kt = [k_t[c][hs, :] for c in range(n_chunk)]
        vb = [v_ref[c * L:(c + 1) * L, hs].astype(BF16) for c in range(n_chunk)]
        sf = [None] * (n_chunk + 1)
        sb = [None] * (n_chunk + 1)
        if has_state:
            sf[0] = s0_ref[0, h]
            sb[n_chunk] = s0_ref[1, h]
        for c in range(n_chunk):
            if (not has_state) or c < n_chunk - 1:
                upd = _dot((kt[c] * zeta_f).astype(BF16), vb[c])
                sf[c + 1] = upd if sf[c] is None else upd + gl_f * sf[c]
        for c in range(n_chunk - 1, -1, -1):
            if (not has_state) or c > 0:
                upd = _dot((kt[c] * zeta_b).astype(BF16), vb[c])
                sb[c] = upd if sb[c + 1] is None else upd + gl_b * sb[c + 1]
        if not has_state:
            sn_ref[0, h] = sf[n_chunk]
            sn_ref[1, h] = sb[0]
        o_h = []
        for c in range(n_chunk):
            qb = q_ref[c * L:(c + 1) * L, hs].astype(BF16)
            att = _dot(qb, kt[c].astype(BF16)) * dmat_ref[h]
            o = _dot(att.astype(BF16), vb[c])
            if has_state:
                o = o + col_ref[:, h:h + 1] * _dot(qb, sf[c].astype(BF16))
                o = o + col_ref[:, N_HEADS + h:N_HEADS + h + 1] * _dot(qb, sb[c + 1].astype(BF16))
            o_h.append(o)
        o_all.append(o_h)
    jm = _head_mean_matrix()
    for c in range(n_chunk):
        rows = slice(c * L, (c + 1) * L)
        o = jnp.concatenate([o_all[h][c] for h in range(N_HEADS)], axis=1)
        out_ref[rows, :] = _silu(g_ref[rows, :].astype(F32)) * _head_groupnorm(o, gain_ref[...], jm)


def _retention(z, lg8, gain, l, n_seq, t, state):
    has_state = state is not None
    in_specs = [_zspec(t, 4), _zspec(t, 5), _zspec(t, 6), _zspec(t, 7),
                _layer_spec(lg8, l), _layer_spec(gain, l)]
    args = [z, z, z, z, lg8, gain]
    out_specs = [pl.BlockSpec((t, GROUP_W), lambda b: (b, 0))]
    out_shape = [jax.ShapeDtypeStruct((n_seq * t, GROUP_W), F32)]
    if has_state:
        in_specs.append(_seq_layer_spec(state, l))
        args.append(state)
    else:
        out_specs.append(pl.BlockSpec((None, 2, N_HEADS, HEAD_DIM, HEAD_DIM), lambda b: (b, 0, 0, 0, 0)))
        out_shape.append(jax.ShapeDtypeStruct((n_seq, 2, N_HEADS, HEAD_DIM, HEAD_DIM), F32))
    return pl.pallas_call(
        functools.partial(_ret_kernel, n_chunk=t // CHUNK, has_state=has_state),
        grid=(n_seq,), in_specs=in_specs, out_specs=out_specs, out_shape=out_shape,
        scratch_shapes=[pltpu.VMEM((N_HEADS, CHUNK, CHUNK), F32), pltpu.VMEM((CHUNK, LANES), F32),
                        pltpu.VMEM((4 * N_HEADS, CHUNK), F32)],
        compiler_params=_params(("arbitrary",)), name="retention",
    )(*args)


def _lambda(lam_ref, lam_init):
    v = lam_ref[...]
    s1 = jnp.sum(v[0:1, :] * v[1:2, :], axis=-1, keepdims=True)
    s2 = jnp.sum(v[2:3, :] * v[3:4, :], axis=-1, keepdims=True)
    return jnp.exp(s1) - jnp.exp(s2) + lam_init


def _first_half(shape):
    lane = lax.broadcasted_iota(jnp.int32, shape, 1)
    return (lane % (A_SUB // 2)) < (A_SUB // 4)


def _rope(x, cos, sin_signed):
    n = x.shape[1]
    first_half = _first_half(x.shape)
    partner = jnp.where(first_half, pltpu.roll(x, n - A_SUB // 4, 1), pltpu.roll(x, A_SUB // 4, 1))
    return x * cos + partner * sin_signed


def _attn_kernel(*refs, t, has_cache, lam_init):
    if has_cache:
        q_ref, k_ref, v_ref, kc_ref, vc_ref, lam_ref, gain_ref, out_ref, qr_ref, kr_ref, cos_ref, sin_ref = refs
    else:
        q_ref, k_ref, v_ref, lam_ref, gain_ref, out_ref, qr_ref, kr_ref = refs
    lam = _lambda(lam_ref, lam_init)
    scale = A_SUB ** -0.5
    n_qb = t // CHUNK

    if has_cache:
        @pl.when(pl.program_id(0) == 0)
        def _():
            tok = lax.broadcasted_iota(jnp.int32, (t, GROUP_W), 0)
            lane = lax.broadcasted_iota(jnp.int32, (t, GROUP_W), 1)
            pos = jnp.where((lane % A_SUB) < (A_SUB // 2), tok // GRID_W, tok % GRID_W).astype(F32)
            n_freq = A_SUB // 4
            inv = jnp.exp((lane % n_freq).astype(F32) * (-math.log(ROPE_BASE) / n_freq))
            ang = pos * inv
            cos_ref[...] = jnp.cos(ang)
            sin_ref[...] = jnp.where(_first_half((t, GROUP_W)), -jnp.sin(ang), jnp.sin(ang))

        kr_ref[...] = _rope(k_ref[...].astype(F32), cos_ref[...], sin_ref[...]).astype(BF16)
        qr_ref[...] = (_rope(q_ref[...].astype(F32), cos_ref[...], sin_ref[...]) * scale).astype(BF16)
    else:
        kr_ref[...] = k_ref[...].astype(BF16)
        qr_ref[...] = (q_ref[...].astype(F32) * scale).astype(BF16)

    o_all = []
    for h in range(N_HEADS):
        hs = _hs(h)
        vx = _v_ext(v_ref, slice(0, t), hs)
        if has_cache:
            vcx = _v_ext(vc_ref, slice(0, vc_ref.shape[0]), hs)
        o_h = []
        for qi in range(n_qb):
            rows = slice(qi * CHUNK, (qi + 1) * CHUNK)
            o_maps = []
            for m in range(2):
                ms = slice(HEAD_DIM * h + A_SUB * m, HEAD_DIM * h + A_SUB * (m + 1))
                qm = qr_ref[rows, ms]
                s_l = _dot_nt(qm, kr_ref[:, ms])
                mx = jnp.max(s_l, axis=-1, keepdims=True)
                if has_cache:
                    s_c = _dot_nt(qm, kc_ref[:, ms].astype(BF16))
                    mx = jnp.maximum(mx, jnp.max(s_c, axis=-1, keepdims=True))
                tot = _dot(jnp.exp(s_l - mx).astype(BF16), vx)
                if has_cache:
                    tot = tot + _dot(jnp.exp(s_c - mx).astype(BF16), vcx)
                o_maps.append(tot[:, 0:HEAD_DIM] / tot[:, HEAD_DIM:HEAD_DIM + 1])
            o_h.append(o_maps[0] - lam * o_maps[1])
        o_all.append(o_h)
    jm = _head_mean_matrix()
    for qi in range(n_qb):
        rows = slice(qi * CHUNK, (qi + 1) * CHUNK)
        o = jnp.concatenate([o_all[h][qi] for h in range(N_HEADS)], axis=1)
        out_ref[rows, :] = _head_rms(o, gain_ref[...], jm) * (1.0 - lam_init)


def _attention(z, lam8, gain, l, n_seq, t, cache, lam_init):
    has_cache = cache is not None
    in_specs = [_zspec(t, 8), _zspec(t, 9), _zspec(t, 10)]
    args = [z, z, z]
    scratch = [pltpu.VMEM((t, GROUP_W), BF16), pltpu.VMEM((t, GROUP_W), BF16)]
    if has_cache:
        in_specs += [_seq_layer_spec(a, l) for a in cache]
        args += list(cache)
        scratch += [pltpu.VMEM((t, GROUP_W), F32), pltpu.VMEM((t, GROUP_W), F32)]
    in_specs += [_layer_spec(lam8, l), _layer_spec(gain, l)]
    args += [lam8, gain]
    return pl.pallas_call(
        functools.partial(_attn_kernel, t=t, has_cache=has_cache, lam_init=lam_init),
        grid=(n_seq,), in_specs=in_specs,
        out_specs=pl.BlockSpec((t, GROUP_W), lambda b: (b, 0)),
        out_shape=jax.ShapeDtypeStruct((n_seq * t, GROUP_W), F32),
        scratch_shapes=scratch,
        compiler_params=_params(("arbitrary",)), name="diff_attention",
    )(*args)


def _cmul(ar, ai, br, bi):
    return ar * br - ai * bi, ar * bi + ai * br


def _lam_bar(lre, lim, dt):
    mag = jnp.exp(lre * dt)
    return mag * jnp.cos(lim * dt), mag * jnp.sin(lim * dt)


def _cpow_int(br, bi, e, n_bits):
    pr = jnp.ones(e.shape, F32)
    pi = jnp.zeros(e.shape, F32)
    for bit in range(n_bits):
        on = ((e >> bit) & 1) == 1
        qr, qi = _cmul(pr, pi, br, bi)
        pr, pi = jnp.where(on, qr, pr), jnp.where(on, qi, pi)
        if bit + 1 < n_bits:
            br, bi = _cmul(br, bi, br, bi)
    return pr, pi


def _zoh_coef(lre, lim, br, bi):
    den = lre * lre + lim * lim
    return ((br - 1.0) * lre + bi * lim) / den, (bi * lre - (br - 1.0) * lim) / den


def _toeplitz_rows(kall):
    n = S5_L * S5_GC
    lane = lax.broadcasted_iota(jnp.int32, (S5_GC, n), 1)
    pieces = []
    for s in range(S5_L):
        shifted = kall if s == 0 else pltpu.roll(kall, S5_GC * s, 1)
        pieces.append(jnp.where(lane >= S5_GC * s, shifted, 0.0))
    return jnp.concatenate(pieces, axis=0)


def _s5_prep_kernel(lam_c_re, lam_c_im, lam_r_re, lam_r_im, ldt_ref, b_t_re, b_t_im, b_tile_re, b_tile_im,
                    c_re, c_im, c_tile_re, c_tile_im, d_ref, t_ref, bcf_ref, bcb_ref, ccf_ref, ccb_ref, lam_ref):
    n = S5_L * S5_GC
    lane_blk = lax.broadcasted_iota(jnp.int32, (S5_P, n), 1) // S5_GC
    row_blk = lax.broadcasted_iota(jnp.int32, (n, S5_P), 0) // S5_GC
    bits = S5_L.bit_length()

    def bases(d):
        dt = jnp.exp(ldt_ref[d])
        return (_lam_bar(lam_c_re[d][:, 0:1], lam_c_im[d][:, 0:1], dt), _lam_bar(lam_r_re[d], lam_r_im[d], dt))

    def tile_rows(x):
        return jnp.concatenate([x] * S5_L, axis=0)

    (cbr, cbi), (rbr, rbi) = bases(0)
    kr, ki = _zoh_coef(lam_r_re[0], lam_r_im[0], rbr, rbi)
    bbt_re, bbt_im = _cmul(kr, ki, b_t_re[0], b_t_im[0])
    pr, pi = _cpow_int(cbr, cbi, lane_blk, bits)
    cl_re, cl_im = _cmul(c_tile_re[0], c_tile_im[0], pr, pi)
    t_f = _toeplitz_rows(_dot(bbt_re, cl_re, HI) - _dot(bbt_im, cl_im, HI))
    pr, pi = _cpow_int(rbr, rbi, S5_L - 1 - row_blk, bits)
    re, im = _cmul(tile_rows(bbt_re), tile_rows(bbt_im), pr, pi)
    bcf_ref[...] = jnp.concatenate([re, im], axis=1).astype(BF16)
    re, im = _cmul(cl_re, cl_im, cbr, cbi)
    ccf_ref[...] = jnp.concatenate([re, -im], axis=0).astype(BF16)
    pr, pi = _cpow_int(rbr, rbi, jnp.full((1, S5_P), S5_L, jnp.int32), bits)
    lam_ref[0:1, :] = jnp.concatenate([pr, pr], axis=1)
    lam_ref[1:2, :] = jnp.concatenate([-pi, pi], axis=1)

    (cbr, cbi), (rbr, rbi) = bases(1)
    kr, ki = _zoh_coef(lam_c_re[1][:, 0:1], lam_c_im[1][:, 0:1], cbr, cbi)
    bb_re, bb_im = _cmul(kr, ki, b_tile_re[1], b_tile_im[1])
    pr, pi = _cpow_int(cbr, cbi, lane_blk, bits)
    bl_re, bl_im = _cmul(bb_re, bb_im, pr, pi)
    t_b = _toeplitz_rows(_dot(c_re[1], bl_re, HI) - _dot(c_im[1], bl_im, HI)).T
    kr, ki = _zoh_coef(lam_r_re[1], lam_r_im[1], rbr, rbi)
    bbt_re, bbt_im = _cmul(kr, ki, b_t_re[1], b_t_im[1])
    pr, pi = _cpow_int(rbr, rbi, row_blk, bits)
    re, im = _cmul(tile_rows(bbt_re), tile_rows(bbt_im), pr, pi)
    bcb_ref[...] = jnp.concatenate([re, im], axis=1).astype(BF16)
    pr, pi = _cpow_int(cbr, cbi, S5_L - lane_blk, bits)
    re, im = _cmul(c_tile_re[1], c_tile_im[1], pr, pi)
    ccb_ref[...] = jnp.concatenate([re, -im], axis=0).astype(BF16)
    pr, pi = _cpow_int(rbr, rbi, jnp.full((1, S5_P), S5_L, jnp.int32), bits)
    lam_ref[2:3, :] = jnp.concatenate([pr, pr], axis=1)
    lam_ref[3:4, :] = jnp.concatenate([-pi, pi], axis=1)
    lam_ref[4:8, :] = jnp.zeros((4, 2 * S5_P), F32)

    eye = (lax.broadcasted_iota(jnp.int32, (n, n), 0) == lax.broadcasted_iota(jnp.int32, (n, n), 1))
    t_ref[...] = (t_f + t_b + jnp.where(eye, d_ref[...], 0.0)).astype(BF16)


def _s5_prep(lam_re, lam_im, log_dt, b_re, b_im, c_re, c_im, s5_d):
    n = S5_L * S5_GC
    lg = (DEPTH, 2, S5_G)
    bc = lambda a, shape: jnp.broadcast_to(a, lg + shape)
    swap = lambda a: jnp.swapaxes(a, -1, -2)
    ins = [bc(lam_re[..., :, None], (S5_P, n)), bc(lam_im[..., :, None], (S5_P, n)),
           lam_re[..., None, :], lam_im[..., None, :], log_dt[..., None, None],
           swap(b_re), swap(b_im), jnp.tile(b_re, (1, 1, 1, 1, S5_L)), jnp.tile(b_im, (1, 1, 1, 1, S5_L)),
           c_re, c_im, jnp.tile(swap(c_re), (1, 1, 1, 1, S5_L)), jnp.tile(swap(c_im), (1, 1, 1, 1, S5_L))]
    d_tile = jnp.tile(s5_d.reshape(DEPTH, S5_G, 1, S5_GC), (1, 1, 1, S5_L))

    def dir_spec(a):
        return pl.BlockSpec((None, 2, None) + a.shape[3:], lambda l, g: (l, 0, g, 0, 0))

    def out(rows, cols, dtype):
        return (pl.BlockSpec((None, None, rows, cols), lambda l, g: (l, g, 0, 0)),
                jax.ShapeDtypeStruct((DEPTH, S5_G, rows, cols), dtype))

    outs = [out(n, n, BF16), out(n, 2 * S5_P, BF16), out(n, 2 * S5_P, BF16),
            out(2 * S5_P, n, BF16), out(2 * S5_P, n, BF16), out(8, 2 * S5_P, F32)]
    return pl.pallas_call(
        _s5_prep_kernel,
        grid=(DEPTH, S5_G),
        in_specs=[dir_spec(a) for a in ins] + [pl.BlockSpec((None, None, 1, n), lambda l, g: (l, g, 0, 0))],
        out_specs=[o[0] for o in outs], out_shape=[o[1] for o in outs],
        compiler_params=_params(("arbitrary", "arbitrary")), name="s5_prep",
    )(*ins, d_tile)


def _s5_kernel(*refs, n_seq, n_k, has_state):
    if has_state:
        (u_ref, t_ref, bcf_ref, bcb_ref, ccf_ref, ccb_ref, lam_ref, x0_ref,
         y_ref, ug_ref, yg_ref, inj_f, inj_b, inj_s, xin_f, xin_b) = refs
    else:
        (u_ref, t_ref, bcf_ref, bcb_ref, ccf_ref, ccb_ref, lam_ref,
         y_ref, xf_ref, xb_ref, ug_ref, yg_ref, inj_f, inj_b, inj_s, xin_f, xin_b) = refs
    ng = S5_G // 2
    u_i = [u_ref[:, i, :] for i in range(S5_L)]
    for g in range(ng):
        u = jnp.concatenate([u_i[i][:, S5_GC * g:S5_GC * (g + 1)] for i in range(S5_L)], axis=1).astype(BF16)
        ug_ref[g] = u
        inj_f[g] = _dot(u, bcf_ref[g])
        inj_b[g] = _dot(u, bcb_ref[g])

    for d, (inj, xin) in enumerate(((inj_f, xin_f), (inj_b, xin_b))):
        a = [lam_ref[g, 2 * d:2 * d + 1, :] for g in range(ng)]
        bs = [lam_ref[g, 2 * d + 1:2 * d + 2, :] for g in range(ng)]
        for g in range(ng):
            inj_s[g] = pltpu.roll(inj[g], S5_P, 1)
        x = [x0_ref[d, g] if has_state else jnp.zeros((n_seq, 2 * S5_P), F32) for g in range(ng)]
        xs = [pltpu.roll(v, S5_P, 1) for v in x]
        for k in (range(n_k) if d == 0 else range(n_k - 1, -1, -1)):
            rows = pl.ds(k, n_seq, stride=n_k)
            for g in range(ng):
                xin[g, rows, :] = x[g]
                x[g], xs[g] = (a[g] * x[g] + bs[g] * xs[g] + inj[g, rows, :],
                               a[g] * xs[g] - bs[g] * x[g] + inj_s[g, rows, :])
        if not has_state:
            for g in range(ng):
                (xf_ref if d == 0 else xb_ref)[g] = x[g]

    for g in range(ng):
        yg_ref[g] = (_dot(ug_ref[g], t_ref[g]) + _dot(xin_f[g].astype(BF16), ccf_ref[g])
                     + _dot(xin_b[g].astype(BF16), ccb_ref[g]))
    for i in range(S5_L):
        y_ref[:, i, :] = jnp.concatenate([yg_ref[g][:, S5_GC * i:S5_GC * (i + 1)] for g in range(ng)], axis=1)


def _s5(u_halves, ops, l, n_seq, n_k, x0):
    has_state = x0 is not None
    n = S5_L * S5_GC
    r = n_seq * n_k
    w = 2 * S5_P
    ng = S5_G // 2
    half_spec = pl.BlockSpec((None, r, S5_L, LANES), lambda j: (j, 0, 0, 0))
    op_spec = lambda a: pl.BlockSpec((None, ng) + a.shape[2:], lambda j: (l, j) + (0,) * (a.ndim - 2))
    args = [u_halves] + list(ops)
    in_specs = [half_spec] + [op_spec(a) for a in ops]
    if has_state:
        args.append(x0)
        in_specs.append(pl.BlockSpec((None, 2, ng, n_seq, w), lambda j: (l, 0, j, 0, 0)))
    out_shape = [jax.ShapeDtypeStruct((2, r, S5_L, LANES), F32)]
    out_specs = [half_spec]
    if not has_state:
        out_shape += [jax.ShapeDtypeStruct((S5_G, n_seq, w), F32)] * 2
        out_specs += [pl.BlockSpec((ng, n_seq, w), lambda j: (j, 0, 0))] * 2
    return pl.pallas_call(
        functools.partial(_s5_kernel, n_seq=n_seq, n_k=n_k, has_state=has_state),
        grid=(2,), in_specs=in_specs, out_specs=out_specs, out_shape=out_shape,
        scratch_shapes=[pltpu.VMEM((ng, r, n), BF16), pltpu.VMEM((ng, r, n), F32)]
        + [pltpu.VMEM((ng, r, w), F32)] * 5,
        compiler_params=_params(("arbitrary",)), name="s5_scan",
    )(*args)


def _out_ffn_kernel(x_ref, m_ref, r_ref, a_ref, ya_ref, yb_ref, mod_ref, wglu_ref, bglu_ref, wout_ref,
                    n_post_ref, n_pre_ref, n_fpost_ref, wg_ref, wu_ref, wd_ref, o_ref):
    def mod(i):
        return mod_ref[:, i * D_MODEL:(i + 1) * D_MODEL]

    y = jnp.concatenate([ya_ref[...], yb_ref[...]], axis=1)
    gs = 0.5 * y * (1.0 + jnp.tanh(math.sqrt(2.0 / math.pi) * (y + 0.044715 * (y * y * y))))
    s_out = gs * jax.nn.sigmoid(_dot(gs.astype(BF16), wglu_ref[...]) + bglu_ref[...])
    mixed = jnp.concatenate([m_ref[...], r_ref[...], a_ref[...], s_out], axis=1).astype(BF16)
    x1 = x_ref[...] + mod(2) * _rms(_dot(mixed, wout_ref[...]), n_post_ref[...])
    h = (_rms(x1, n_pre_ref[...]) * (1.0 + mod(4)) + mod(3)).astype(BF16)
    act = (_silu(_dot(h, wg_ref[...])) * _dot(h, wu_ref[...])).astype(BF16)
    o_ref[...] = x1 + mod(5) * _rms(_dot(act, wd_ref[...]), n_fpost_ref[...])


def _out_ffn(x, m_out, r_out, a_out, y_halves, mod, l, mod_row, wp):
    n = x.shape[0]
    row = lambda w: pl.BlockSpec((ROW_TILE, w), lambda i: (i, 0))
    half = lambda j: pl.BlockSpec((None, ROW_TILE, LANES), lambda i: (j, i, 0))
    params = [wp[k] for k in ('w_glu', 'b_glu', 'w_out', 'n_mix_post', 'n_ffn_pre', 'n_ffn_post',
                              'w_gate', 'w_up', 'w_down')]
    return pl.pallas_call(
        _out_ffn_kernel,
        grid=(n // ROW_TILE,),
        in_specs=[row(D_MODEL), row(GROUP_W), row(GROUP_W), row(GROUP_W), half(0), half(1),
                  _mod_spec(l, mod_row)] + [_layer_spec(a, l, pipeline_mode=pl.Buffered(1)) for a in params],
        out_specs=row(D_MODEL),
        out_shape=jax.ShapeDtypeStruct((n, D_MODEL), F32),
        compiler_params=_params(("arbitrary",)), name="out_ffn",
    )(x, m_out, r_out, a_out, y_halves, y_halves, mod, *params)


def _layer(x, l, n_seq, t, mod, mod_row, wp, s5_ops, lam_init, st):
    ctx = st is None
    res = _in_proj(x, mod, l, mod_row, wp['n_mix_pre'], wp['w_in'], emit_kv=ctx)
    z, gates = res[0], res[2]
    u_halves = res[1].reshape(2, n_seq * t // S5_L, S5_L, LANES)
    m_res = _mlstm(z, gates, wp['m_bias'], wp['m_norm'], l, n_seq, t, None if ctx else st[0:2])
    r_res = _retention(z, wp['r_decay'], wp['r_norm'], l, n_seq, t, None if ctx else st[2])
    a_out = _attention(z, wp['a_lam'], wp['a_norm'], l, n_seq, t, None if ctx else st[3:5], lam_init)
    s_res = _s5(u_halves, s5_ops, l, n_seq, t // S5_L, None if ctx else st[5])
    y_halves = s_res[0].reshape(2, n_seq * t, LANES)
    x_new = _out_ffn(x, m_res[0], r_res[0], a_out, y_halves, mod, l, mod_row, wp)
    if not ctx:
        return x_new, None
    return x_new, (m_res[1], m_res[2], r_res[1], res[3], res[4], s_res[1], s_res[2])


def _pad_to(a, rows, cols=LANES):
    pad = [(0, 0)] * (a.ndim - 2) + [(0, rows - a.shape[-2]), (0, cols - a.shape[-1])]
    return jnp.pad(a, pad)


def kernel(x_prompt, x_sample, state_mlstm_C, state_mlstm_n, state_mlstm_m, state_ret, cache_diff_k, cache_diff_v, state_s5_re, state_s5_im, c, c_ctx, w_ada, b_ada, n_mix_pre, n_mix_post, n_ffn_pre, n_ffn_post, w_in, w_out, m_gate_bias, m_norm, r_decay_logit, r_norm, a_lam_q1, a_lam_k1, a_lam_q2, a_lam_k2, a_norm, s5_lam_re, s5_lam_im, s5_log_dt, s5_b_re, s5_b_im, s5_c_re, s5_c_im, s5_d, s5_w_glu, s5_b_glu, w_ffn_gate, w_ffn_up, w_ffn_down):
    n_ctx, t_ctx, _ = x_prompt.shape
    n_lat, t_lat, _ = x_sample.shape
    past = cache_diff_k.shape[2]

    cond8 = jnp.concatenate([c, c_ctx[None, :], jnp.zeros((8 - n_lat - 1, D_MODEL), F32)], axis=0)
    mod = _ada(cond8, w_ada, b_ada).reshape(DEPTH, 8, 1, 6 * D_MODEL)
    s5_ops = _s5_prep(s5_lam_re, s5_lam_im, s5_log_dt, s5_b_re, s5_b_im, s5_c_re, s5_c_im, s5_d)

    n_gate = 4 * N_HEADS
    row = lambda a: a[:, None, :]
    wp = dict(
        w_in=jnp.concatenate([w_in[..., :4 * GROUP_W], w_in[..., 4 * GROUP_W + n_gate:],
                              w_in[..., 4 * GROUP_W:4 * GROUP_W + n_gate],
                              jnp.zeros((DEPTH, D_MODEL, LANES - n_gate), F32)], axis=-1).astype(BF16),
        w_out=w_out.astype(BF16), w_glu=s5_w_glu.astype(BF16), b_glu=row(s5_b_glu),
        w_gate=w_ffn_gate.astype(BF16), w_up=w_ffn_up.astype(BF16), w_down=w_ffn_down.astype(BF16),
        n_mix_pre=row(n_mix_pre), n_mix_post=row(n_mix_post), n_ffn_pre=row(n_ffn_pre),
        n_ffn_post=row(n_ffn_post), m_norm=row(m_norm), r_norm=row(r_norm), a_norm=row(a_norm),
        m_bias=_pad_to(row(m_gate_bias), 1), r_decay=_pad_to(r_decay_logit, 8),
        a_lam=_pad_to(jnp.stack([a_lam_q1, a_lam_k1, a_lam_q2, a_lam_k2], axis=1), 8))
    lam_inits = [0.8 - 0.6 * math.exp(-0.3 * l) for l in range(DEPTH)]

    x = x_prompt.reshape(n_ctx * t_ctx, D_MODEL)
    new_states = []
    for l in range(DEPTH):
        x, st = _layer(x, l, n_ctx, t_ctx, mod, lambda i: n_lat, wp, s5_ops, lam_inits[l], None)
        new_states.append(st)
    y_prompt = x.reshape(n_ctx, t_ctx, D_MODEL)

    x0 = jnp.stack([state_s5_re, state_s5_im], axis=-2)
    x0 = x0.transpose(1, 2, 3, 0, 4, 5).reshape(DEPTH, 2, S5_G, n_lat, 2 * S5_P)
    s_ext0 = jnp.concatenate([state_mlstm_C, state_mlstm_n[..., None],
                              jnp.zeros(state_mlstm_n.shape + (LANES - HEAD_DIM - 1,), F32)], axis=-1)
    st = (s_ext0, state_mlstm_m.reshape(n_lat, DEPTH, 2 * N_HEADS, 1), state_ret,
          cache_diff_k.reshape(n_lat, DEPTH, past, GROUP_W), cache_diff_v.reshape(n_lat, DEPTH, past, GROUP_W), x0)
    x = x_sample.reshape(n_lat * t_lat, D_MODEL)
    tiles_per_seq = t_lat // ROW_TILE
    for l in range(DEPTH):
        x, _ = _layer(x, l, n_lat, t_lat, mod, lambda i: i // tiles_per_seq, wp, s5_ops, lam_inits[l], st)
    y_sample = x.reshape(n_lat, t_lat, D_MODEL)

    stack = lambda i: jnp.stack([s[i] for s in new_states], axis=1)
    kv = lambda i: jnp.stack([s[i].reshape(n_ctx, t_ctx, N_HEADS, HEAD_DIM) for s in new_states], axis=1)
    s_ext = stack(0)
    xs = jnp.stack([stack(5), stack(6)], axis=2)
    xs = xs.reshape(S5_G, DEPTH, 2, n_ctx, 2, S5_P).transpose(3, 1, 2, 0, 4, 5)
    return (y_prompt, y_sample, s_ext[..., :HEAD_DIM], s_ext[..., HEAD_DIM],
            stack(1).reshape(n_ctx, DEPTH, 2, N_HEADS), stack(2), kv(3), kv(4), xs[..., 0, :], xs[..., 1, :])
```

```python
import functools
import math

import jax
import jax.numpy as jnp
from jax import lax
from jax.experimental import pallas as pl
from jax.experimental.pallas import tpu as pltpu

F32 = jnp.float32
BF16 = jnp.bfloat16
HI = lax.Precision.HIGHEST

D_MODEL = 1024
DEPTH = 2
GRID_W = 64
HEAD_DIM = 64
GROUP_W = 256
N_HEADS = 4
A_SUB = 32
S5_GC = 16
S5_G = 16
S5_P = 64
D_FF = 2816
ROPE_BASE = 10000.0
EPS = 1e-6
N_MAIN = 12 * GROUP_W
N_MIX = 11 * GROUP_W
LANES = 128
CHUNK = 256
ROW_TILE = 512
S5_L = 16
NEG = -1e30
VMEM_LIMIT = 56 * 1024 * 1024


def _dot(a, b, precision=None):
    return jnp.dot(a, b, preferred_element_type=F32, precision=precision)


def _dot_nt(a, b):
    return lax.dot_general(a, b, (((1,), (1,)), ((), ())), preferred_element_type=F32)


def _log_sigmoid(x):
    return jnp.minimum(x, 0.0) - jnp.log(1.0 + jnp.exp(-jnp.abs(x)))


def _silu(x):
    return x * jax.nn.sigmoid(x)


def _rms(x, g):
    return x * lax.rsqrt(jnp.mean(x * x, axis=-1, keepdims=True) + EPS) * g


def _params(sem=None):
    return pltpu.CompilerParams(dimension_semantics=sem, vmem_limit_bytes=VMEM_LIMIT)


def _layer_spec(a, l, **kw):
    n = a.ndim - 1
    return pl.BlockSpec((None,) + a.shape[1:], lambda *_: (l,) + (0,) * n, **kw)


def _seq_layer_spec(a, l):
    n = a.ndim - 2
    return pl.BlockSpec((None, None) + a.shape[2:], lambda b: (b, l) + (0,) * n)


def _split_bf16(x):
    hi = x.astype(BF16)
    return hi, (x - hi.astype(F32)).astype(BF16)


def _ada_kernel(c_ref, w_ref, b_ref, o_ref):
    a_hi, a_lo = _split_bf16(_silu(c_ref[...]))
    w_hi, w_lo = _split_bf16(w_ref[...])
    o_ref[...] = _dot(a_hi, w_hi) + (_dot(a_lo, w_hi) + _dot(a_hi, w_lo)) + b_ref[...]


def _ada(cond8, w_ada, b_ada):
    tn = 1536
    return pl.pallas_call(
        _ada_kernel,
        grid=(DEPTH, 6 * D_MODEL // tn),
        in_specs=[pl.BlockSpec((8, D_MODEL), lambda l, j: (0, 0)),
                  pl.BlockSpec((None, D_MODEL, tn), lambda l, j: (l, 0, j)),
                  pl.BlockSpec((None, 1, tn), lambda l, j: (l, 0, j))],
        out_specs=pl.BlockSpec((None, 8, tn), lambda l, j: (l, 0, j)),
        out_shape=jax.ShapeDtypeStruct((DEPTH, 8, 6 * D_MODEL), F32),
        compiler_params=_params(("arbitrary", "arbitrary")),
        name="ada",
    )(cond8, w_ada, b_ada.reshape(DEPTH, 1, 6 * D_MODEL))


def _in_proj_kernel(x_ref, mod_ref, g_ref, w_ref, z_ref, u_ref, gate_ref, *kv_refs):
    h = _rms(x_ref[...], g_ref[...]) * (1.0 + mod_ref[:, D_MODEL:2 * D_MODEL]) + mod_ref[:, 0:D_MODEL]
    hb = h.astype(BF16)
    z = _dot(hb, w_ref[:, :N_MIX])
    z_ref[...] = z.astype(BF16)
    u = _dot(hb, w_ref[:, N_MIX:N_MAIN])
    u_ref[0] = u[:, :LANES]
    u_ref[1] = u[:, LANES:]
    gate_ref[...] = _dot(hb, w_ref[:, N_MAIN:])
    if kv_refs:
        kv_refs[0][...] = z[:, 9 * GROUP_W:10 * GROUP_W]
        kv_refs[1][...] = z[:, 10 * GROUP_W:11 * GROUP_W]


def _mod_spec(l, mod_row):
    return pl.BlockSpec((None, None, 1, 6 * D_MODEL), lambda i: (l, mod_row(i), 0, 0))


def _in_proj(x, mod, l, mod_row, gain, w, emit_kv):
    n = x.shape[0]
    row = lambda width: pl.BlockSpec((ROW_TILE, width), lambda i: (i, 0))
    out_specs = [row(N_MIX), pl.BlockSpec((2, ROW_TILE, LANES), lambda i: (0, i, 0)), row(LANES)]
    out_specs += [row(GROUP_W)] * (2 if emit_kv else 0)
    out_shape = [jax.ShapeDtypeStruct((n, N_MIX), BF16), jax.ShapeDtypeStruct((2, n, LANES), F32),
                 jax.ShapeDtypeStruct((n, LANES), F32)]
    out_shape += [jax.ShapeDtypeStruct((n, GROUP_W), F32)] * (2 if emit_kv else 0)
    return pl.pallas_call(
        _in_proj_kernel,
        grid=(n // ROW_TILE,),
        in_specs=[row(D_MODEL), _mod_spec(l, mod_row), _layer_spec(gain, l), _layer_spec(w, l)],
        out_specs=out_specs, out_shape=out_shape,
        compiler_params=_params(("arbitrary",)),
        name="in_proj",
    )(x, mod, gain, w)


def _tri_masks(n):
    row = lax.broadcasted_iota(jnp.int32, (n, n), 0)
    col = lax.broadcasted_iota(jnp.int32, (n, n), 1)
    return row >= col, row <= col


def _head_mean_matrix():
    r = lax.broadcasted_iota(jnp.int32, (GROUP_W, GROUP_W), 0) // HEAD_DIM
    c = lax.broadcasted_iota(jnp.int32, (GROUP_W, GROUP_W), 1) // HEAD_DIM
    return jnp.where(r == c, 1.0 / HEAD_DIM, 0.0).astype(BF16)


def _head_mean(x, j):
    hi, lo = _split_bf16(x)
    return _dot(hi, j) + _dot(lo, j)


def _head_groupnorm(x, g, j):
    xc = x - _head_mean(x, j)
    return xc * lax.rsqrt(_head_mean(xc * xc, j) + EPS) * g


def _head_rms(x, g, j):
    return x * lax.rsqrt(_head_mean(x * x, j) + EPS) * g


def _hs(h):
    return slice(HEAD_DIM * h, HEAD_DIM * (h + 1))


def _v_ext(v_ref, rows, hs):
    ones = jnp.ones((rows.stop - rows.start, HEAD_DIM), BF16)
    return jnp.concatenate([v_ref[rows, hs].astype(BF16), ones], axis=1)


def _zspec(t, colblk):
    return pl.BlockSpec((t, GROUP_W), lambda b: (b, colblk))


def _scan_max(x, reverse):
    n = x.shape[1]
    lane = lax.broadcasted_iota(jnp.int32, x.shape, 1)
    sh = 1
    while sh < n:
        if reverse:
            x = jnp.maximum(x, jnp.where(lane < n - sh, pltpu.roll(x, n - sh, 1), NEG))
        else:
            x = jnp.maximum(x, jnp.where(lane >= sh, pltpu.roll(x, sh, 1), NEG))
        sh *= 2
    return x


def _ends(x, is_fwd):
    return jnp.where(is_fwd, x[:, x.shape[1] - 1:], x[:, 0:1])


def _mlstm_kernel(*refs, n_chunk, has_state):
    if has_state:
        q_ref, k_ref, v_ref, o_ref, g_ref, bias_ref, gain_ref, s0_ref, m0_ref, out_ref = refs
    else:
        q_ref, k_ref, v_ref, o_ref, g_ref, bias_ref, gain_ref, out_ref, cn_ref, nn_ref, mn_ref = refs
    L = CHUNK
    nu = 2 * N_HEADS
    tril, triu = _tri_masks(L)
    is_fwd = lax.broadcasted_iota(jnp.int32, (nu, 1), 0) < N_HEADS

    b8, w8, mcum8 = [], [], []
    for c in range(n_chunk):
        p_t = (g_ref[c * L:(c + 1) * L, :] + bias_ref[...]).T
        lf = _log_sigmoid(p_t[nu:2 * nu, :])
        b = jnp.where(is_fwd, _dot(lf, triu.astype(F32), HI), _dot(lf, tril.astype(F32), HI))
        w = p_t[0:nu, :] - b
        b8.append(b)
        w8.append(w)
        mcum8.append(jnp.where(is_fwd, _scan_max(w, False), _scan_max(w, True)))

    m0 = m0_ref[...] if has_state else jnp.zeros((nu, 1), F32)
    m_in_f, m_in_b = [None] * n_chunk, [None] * n_chunk
    m = m0
    for c in range(n_chunk):
        m_in_f[c] = m
        m = (b8[c] + jnp.maximum(mcum8[c], m))[:, L - 1:]
    m_fin_f = m
    m = m0
    for c in range(n_chunk - 1, -1, -1):
        m_in_b[c] = m
        m = (b8[c] + jnp.maximum(mcum8[c], m))[:, 0:1]
    if not has_state:
        mn_ref[...] = jnp.where(is_fwd, m_fin_f, m)

    wk8, dec8, cols = [], [], []
    for c in range(n_chunk):
        m_in = jnp.where(is_fwd, m_in_f[c], m_in_b[c])
        g = jnp.maximum(mcum8[c], m_in)
        m_row = b8[c] + g
        m_new, b_last = _ends(m_row, is_fwd), _ends(b8[c], is_fwd)
        wk8.append(jnp.exp(b_last + w8[c] - m_new))
        dec8.append(jnp.exp(b_last + m_in - m_new))
        stats = jnp.concatenate([g, jnp.exp(m_in - g), jnp.exp(-m_row), jnp.zeros((LANES - 3 * nu, L), F32)], axis=0)
        cols.append(stats.T)

    k_t = [(k_ref[c * L:(c + 1) * L, :].astype(F32) * (HEAD_DIM ** -0.5)).T for c in range(n_chunk)]

    h_all = []
    for h in range(N_HEADS):
        hs = _hs(h)
        qb = [q_ref[c * L:(c + 1) * L, hs].astype(BF16) for c in range(n_chunk)]
        kt = [k_t[c][hs, :] for c in range(n_chunk)]
        vx = [_v_ext(v_ref, slice(c * L, (c + 1) * L), hs) for c in range(n_chunk)]
        qk = [_dot(qb[c], kt[c].astype(BF16)) for c in range(n_chunk)]
        h_sum = [None] * n_chunk
        for d in range(2):
            tri = tril if d == 0 else triu
            j = N_HEADS * d + h
            s_ext = s0_ref[d, h] if has_state else None
            order = range(n_chunk) if d == 0 else range(n_chunk - 1, -1, -1)
            for ci, c in enumerate(order):
                wgt = jnp.exp(jnp.where(tri, w8[c][j:j + 1, :] - cols[c][:, j:j + 1], NEG))
                tot = _dot((qk[c] * wgt).astype(BF16), vx[c])
                if s_ext is not None:
                    tot = tot + cols[c][:, nu + j:nu + j + 1] * _dot(qb[c], s_ext.astype(BF16))
                den = jnp.maximum(jnp.abs(tot[:, HEAD_DIM:HEAD_DIM + 1]), cols[c][:, 2 * nu + j:2 * nu + j + 1])
                hd = tot[:, 0:HEAD_DIM] / den
                h_sum[c] = hd if h_sum[c] is None else h_sum[c] + hd
                if (not has_state) or ci < n_chunk - 1:
                    upd = _dot((kt[c] * wk8[c][j:j + 1, :]).astype(BF16), vx[c])
                    s_ext = upd if s_ext is None else upd + dec8[c][j:j + 1, :] * s_ext
            if not has_state:
                cn_ref[d, h] = s_ext[:, 0:HEAD_DIM]
                nn_ref[d, h] = s_ext[:, HEAD_DIM:HEAD_DIM + 1]
        h_all.append(h_sum)
    jm = _head_mean_matrix()
    for c in range(n_chunk):
        rows = slice(c * L, (c + 1) * L)
        gated = jax.nn.sigmoid(o_ref[rows, :].astype(F32)) * jnp.concatenate([h_all[h][c] for h in range(N_HEADS)], axis=1)
        out_ref[rows, :] = _head_groupnorm(gated, gain_ref[...], jm)


def _mlstm(z, gates, bias128, gain, l, n_seq, t, state):
    has_state = state is not None
    nu = 2 * N_HEADS
    in_specs = [_zspec(t, 0), _zspec(t, 1), _zspec(t, 2), _zspec(t, 3),
                pl.BlockSpec((t, LANES), lambda b: (b, 0)),
                _layer_spec(bias128, l), _layer_spec(gain, l)]
    args = [z, z, z, z, gates, bias128, gain]
    out_specs = [pl.BlockSpec((t, GROUP_W), lambda b: (b, 0))]
    out_shape = [jax.ShapeDtypeStruct((n_seq * t, GROUP_W), F32)]
    if has_state:
        in_specs += [_seq_layer_spec(a, l) for a in state]
        args += list(state)
    else:
        out_specs += [pl.BlockSpec((None, 2, N_HEADS, HEAD_DIM, HEAD_DIM), lambda b: (b, 0, 0, 0, 0)),
                      pl.BlockSpec((None, 2, N_HEADS, HEAD_DIM, 1), lambda b: (b, 0, 0, 0, 0)),
                      pl.BlockSpec((None, nu, 1), lambda b: (b, 0, 0))]
        out_shape += [jax.ShapeDtypeStruct((n_seq, 2, N_HEADS, HEAD_DIM, HEAD_DIM), F32),
                      jax.ShapeDtypeStruct((n_seq, 2, N_HEADS, HEAD_DIM, 1), F32),
                      jax.ShapeDtypeStruct((n_seq, nu, 1), F32)]
    return pl.pallas_call(
        functools.partial(_mlstm_kernel, n_chunk=t // CHUNK, has_state=has_state),
        grid=(n_seq,), in_specs=in_specs, out_specs=out_specs, out_shape=out_shape,
        compiler_params=_params(("arbitrary",)), name="mlstm",
    )(*args)


def _ret_kernel(*refs, n_chunk, has_state):
    if has_state:
        q_ref, k_ref, v_ref, g_ref, lg_ref, gain_ref, s0_ref, out_ref, dmat_ref, col_ref, row_ref = refs
    else:
        q_ref, k_ref, v_ref, g_ref, lg_ref, gain_ref, out_ref, sn_ref, dmat_ref, col_ref, row_ref = refs
    L = CHUNK

    @pl.when(pl.program_id(0) == 0)
    def _():
        row = lax.broadcasted_iota(jnp.int32, (L, L), 0)
        col = lax.broadcasted_iota(jnp.int32, (L, L), 1)
        diff = (row - col).astype(F32)
        log_g = _log_sigmoid(lg_ref[...])
        pos_c = lax.broadcasted_iota(jnp.int32, (L, 1), 0).astype(F32)
        pos_r = lax.broadcasted_iota(jnp.int32, (1, L), 1).astype(F32)
        for h in range(N_HEADS):
            lgf = log_g[0:1, h:h + 1]
            lgb = log_g[1:2, h:h + 1]
            dmat_ref[h] = (jnp.where(diff >= 0, jnp.exp(jnp.maximum(diff, 0.0) * lgf), 0.0)
                           + jnp.where(diff <= 0, jnp.exp(jnp.maximum(-diff, 0.0) * lgb), 0.0))
            col_ref[:, h:h + 1] = jnp.exp((pos_c + 1.0) * lgf)
            col_ref[:, N_HEADS + h:N_HEADS + h + 1] = jnp.exp((L - pos_c) * lgb)
            row_ref[h:h + 1, :] = jnp.exp((L - 1.0 - pos_r) * lgf)
            row_ref[N_HEADS + h:N_HEADS + h + 1, :] = jnp.exp(pos_r * lgb)
            row_ref[2 * N_HEADS + h:2 * N_HEADS + h + 1, :] = jnp.exp(L * lgf) + jnp.zeros((1, L), F32)
            row_ref[3 * N_HEADS + h:3 * N_HEADS + h + 1, :] = jnp.exp(L * lgb) + jnp.zeros((1, L), F32)

    k_t = [(k_ref[c * L:(c + 1) * L, :].astype(F32) * (HEAD_DIM ** -0.5)).T for c in range(n_chunk)]
    o_all = []
    for h in range(N_HEADS):
        hs = _hs(h)
        zeta_f = row_ref[h:h + 1, :]
        zeta_b = row_ref[N_HEADS + h:N_HEADS + h + 1, :]
        gl_f = row_ref[2 * N_HEADS + h:2 * N_HEADS + h + 1, 0:1]
        gl_b = row_ref[3 * N_HEADS + h:3 * N_HEADS + h + 1, 0:1]
        kt = [k_t[c][hs, :] for c in range(n_chunk)]
        vb = [v_ref[c * L:(c + 1) * L, hs].astype(BF16) for c in range(n_chunk)]
        sf = [None] * (n_chunk + 1)
        sb = [None] * (n_chunk + 1)
        if has_state:
            sf[0] = s0_ref[0, h]
            sb[n_chunk] = s0_ref[1, h]
        for c in range(n_chunk):
            if (not has_state) or c < n_chunk - 1:
                upd = _dot((kt[c] * zeta_f).astype(BF16), vb[c])
                sf[c + 1] = upd if sf[c] is None else upd + gl_f * sf[c]
        for c in range(n_chunk - 1, -1, -1):
            if (not has_state) or c > 0:
                upd = _dot((kt[c] * zeta_b).astype(BF16), vb[c])
                sb[c] = upd if sb[c + 1] is None else upd + gl_b * sb[c + 1]
        if not has_state:
            sn_ref[0, h] = sf[n_chunk]
            sn_ref[1, h] = sb[0]
        o_h = []
        for c in range(n_chunk):
            qb = q_ref[c * L:(c + 1) * L, hs].astype(BF16)
            att = _dot(qb, kt[c].astype(BF16)) * dmat_ref[h]
            o = _dot(att.astype(BF16), vb[c])
            if has_state:
                o = o + col_ref[:, h:h + 1] * _dot(qb, sf[c].astype(BF16))
                o = o + col_ref[:, N_HEADS + h:N_HEADS + h + 1] * _dot(qb, sb[c + 1].astype(BF16))
            o_h.append(o)
        o_all.append(o_h)
    jm = _head_mean_matrix()
    for c in range(n_chunk):
        rows = slice(c * L, (c + 1) * L)
        o = jnp.concatenate([o_all[h][c] for h in range(N_HEADS)], axis=1)
        out_ref[rows, :] = _silu(g_ref[rows, :].astype(F32)) * _head_groupnorm(o, gain_ref[...], jm)


def _retention(z, lg8, gain, l, n_seq, t, state):
    has_state = state is not None
    in_specs = [_zspec(t, 4), _zspec(t, 5), _zspec(t, 6), _zspec(t, 7),
                _layer_spec(lg8, l), _layer_spec(gain, l)]
    args = [z, z, z, z, lg8, gain]
    out_specs = [pl.BlockSpec((t, GROUP_W), lambda b: (b, 0))]
    out_shape = [jax.ShapeDtypeStruct((n_seq * t, GROUP_W), F32)]
    if has_state:
        in_specs.append(_seq_layer_spec(state, l))
        args.append(state)
    else:
        out_specs.append(pl.BlockSpec((None, 2, N_HEADS, HEAD_DIM, HEAD_DIM), lambda b: (b, 0, 0, 0, 0)))
        out_shape.append(jax.ShapeDtypeStruct((n_seq, 2, N_HEADS, HEAD_DIM, HEAD_DIM), F32))
    return pl.pallas_call(
        functools.partial(_ret_kernel, n_chunk=t // CHUNK, has_state=has_state),
        grid=(n_seq,), in_specs=in_specs, out_specs=out_specs, out_shape=out_shape,
        scratch_shapes=[pltpu.VMEM((N_HEADS, CHUNK, CHUNK), F32), pltpu.VMEM((CHUNK, LANES), F32),
                        pltpu.VMEM((4 * N_HEADS, CHUNK), F32)],
        compiler_params=_params(("arbitrary",)), name="retention",
    )(*args)


def _lambda(lam_ref, lam_init):
    v = lam_ref[...]
    s1 = jnp.sum(v[0:1, :] * v[1:2, :], axis=-1, keepdims=True)
    s2 = jnp.sum(v[2:3, :] * v[3:4, :], axis=-1, keepdims=True)
    return jnp.exp(s1) - jnp.exp(s2) + lam_init


def _first_half(shape):
    lane = lax.broadcasted_iota(jnp.int32, shape, 1)
    return (lane % (A_SUB // 2)) < (A_SUB // 4)


def _rope(x, cos, sin_signed):
    n = x.shape[1]
    first_half = _first_half(x.shape)
    partner = jnp.where(first_half, pltpu.roll(x, n - A_SUB // 4, 1), pltpu.roll(x, A_SUB // 4, 1))
    return x * cos + partner * sin_signed


def _attn_kernel(*refs, t, has_cache, lam_init):
    if has_cache:
        q_ref, k_ref, v_ref, kc_ref, vc_ref, lam_ref, gain_ref, out_ref, qr_ref, kr_ref, cos_ref, sin_ref = refs
    else:
        q_ref, k_ref, v_ref, lam_ref, gain_ref, out_ref, qr_ref, kr_ref = refs
    lam = _lambda(lam_ref, lam_init)
    scale = A_SUB ** -0.5
    n_qb = t // CHUNK

    if has_cache:
        @pl.when(pl.program_id(0) == 0)
        def _():
            lane = lax.broadcasted_iota(jnp.int32, (GRID_W, GROUP_W), 1)
            pos = lax.broadcasted_iota(jnp.int32, (GRID_W, GROUP_W), 0).astype(F32)
            n_freq = A_SUB // 4
            ang = pos * jnp.exp((lane % n_freq).astype(F32) * (-math.log(ROPE_BASE) / n_freq))
            cos_t = jnp.cos(ang)
            sin_t = jnp.where(_first_half((GRID_W, GROUP_W)), -jnp.sin(ang), jnp.sin(ang))
            row_axis = (lane % A_SUB) < (A_SUB // 2)
            for r in range(t // GRID_W):
                rows = slice(r * GRID_W, (r + 1) * GRID_W)
                cos_ref[rows, :] = jnp.where(row_axis, cos_t[r:r + 1, :], cos_t)
                sin_ref[rows, :] = jnp.where(row_axis, sin_t[r:r + 1, :], sin_t)

        kr_ref[...] = _rope(k_ref[...].astype(F32), cos_ref[...], sin_ref[...]).astype(BF16)
        qr_ref[...] = (_rope(q_ref[...].astype(F32), cos_ref[...], sin_ref[...]) * scale).astype(BF16)
    else:
        kr_ref[...] = k_ref[...].astype(BF16)
        qr_ref[...] = (q_ref[...].astype(F32) * scale).astype(BF16)

    o_all = []
    for h in range(N_HEADS):
        hs = _hs(h)
        vx = _v_ext(v_ref, slice(0, t), hs)
        if has_cache:
            vcx = _v_ext(vc_ref, slice(0, vc_ref.shape[0]), hs)
        o_h = []
        for qi in range(n_qb):
            rows = slice(qi * CHUNK, (qi + 1) * CHUNK)
            o_maps = []
            for m in range(2):
                ms = slice(HEAD_DIM * h + A_SUB * m, HEAD_DIM * h + A_SUB * (m + 1))
                qm = qr_ref[rows, ms]
                s_l = _dot_nt(qm, kr_ref[:, ms])
                mx = jnp.max(s_l, axis=-1, keepdims=True)
                if has_cache:
                    s_c = _dot_nt(qm, kc_ref[:, ms].astype(BF16))
                    mx = jnp.maximum(mx, jnp.max(s_c, axis=-1, keepdims=True))
                tot = _dot(jnp.exp(s_l - mx).astype(BF16), vx)
                if has_cache:
                    tot = tot + _dot(jnp.exp(s_c - mx).astype(BF16), vcx)
                o_maps.append(tot[:, 0:HEAD_DIM] / tot[:, HEAD_DIM:HEAD_DIM + 1])
            o_h.append(o_maps[0] - lam * o_maps[1])
        o_all.append(o_h)
    jm = _head_mean_matrix()
    for qi in range(n_qb):
        rows = slice(qi * CHUNK, (qi + 1) * CHUNK)
        o = jnp.concatenate([o_all[h][qi] for h in range(N_HEADS)], axis=1)
        out_ref[rows, :] = _head_rms(o, gain_ref[...], jm) * (1.0 - lam_init)


def _attention(z, lam8, gain, l, n_seq, t, cache, lam_init):
    has_cache = cache is not None
    in_specs = [_zspec(t, 8), _zspec(t, 9), _zspec(t, 10)]
    args = [z, z, z]
    scratch = [pltpu.VMEM((t, GROUP_W), BF16), pltpu.VMEM((t, GROUP_W), BF16)]
    if has_cache:
        in_specs += [_seq_layer_spec(a, l) for a in cache]
        args += list(cache)
        scratch += [pltpu.VMEM((t, GROUP_W), F32), pltpu.VMEM((t, GROUP_W), F32)]
    in_specs += [_layer_spec(lam8, l), _layer_spec(gain, l)]
    args += [lam8, gain]
    return pl.pallas_call(
        functools.partial(_attn_kernel, t=t, has_cache=has_cache, lam_init=lam_init),
        grid=(n_seq,), in_specs=in_specs,
        out_specs=pl.BlockSpec((t, GROUP_W), lambda b: (b, 0)),
        out_shape=jax.ShapeDtypeStruct((n_seq * t, GROUP_W), F32),
        scratch_shapes=scratch,
        compiler_params=_params(("arbitrary",)), name="diff_attention",
    )(*args)


def _cmul(ar, ai, br, bi):
    return ar * br - ai * bi, ar * bi + ai * br


def _lam_bar(lre, lim, dt):
    mag = jnp.exp(lre * dt)
    return mag * jnp.cos(lim * dt), mag * jnp.sin(lim * dt)


def _cpow_int(br, bi, e, n_bits):
    pr = jnp.ones(e.shape, F32)
    pi = jnp.zeros(e.shape, F32)
    for bit in range(n_bits):
        on = ((e >> bit) & 1) == 1
        qr, qi = _cmul(pr, pi, br, bi)
        pr, pi = jnp.where(on, qr, pr), jnp.where(on, qi, pi)
        if bit + 1 < n_bits:
            br, bi = _cmul(br, bi, br, bi)
    return pr, pi


def _zoh_coef(lre, lim, br, bi):
    den = lre * lre + lim * lim
    return ((br - 1.0) * lre + bi * lim) / den, (bi * lre - (br - 1.0) * lim) / den


def _toeplitz_rows(kall):
    n = S5_L * S5_GC
    lane = lax.broadcasted_iota(jnp.int32, (S5_GC, n), 1)
    pieces = []
    for s in range(S5_L):
        shifted = kall if s == 0 else pltpu.roll(kall, S5_GC * s, 1)
        pieces.append(jnp.where(lane >= S5_GC * s, shifted, 0.0))
    return jnp.concatenate(pieces, axis=0)


def _s5_prep_kernel(lam_c_re, lam_c_im, lam_r_re, lam_r_im, ldt_ref, b_t_re, b_t_im, b_re, b_im,
                    c_re, c_im, c_t_re, c_t_im, d_ref, t_ref, bcf_ref, bcb_ref, ccf_ref, ccb_ref, lam_ref):
    n = S5_L * S5_GC
    lane_blk = lax.broadcasted_iota(jnp.int32, (S5_P, n), 1) // S5_GC
    row_blk = lax.broadcasted_iota(jnp.int32, (n, S5_P), 0) // S5_GC
    bits = S5_L.bit_length()

    def bases(d):
        dt = jnp.exp(ldt_ref[d])
        return (_lam_bar(lam_c_re[d], lam_c_im[d], dt), _lam_bar(lam_r_re[d], lam_r_im[d], dt))

    def tile_rows(x):
        return jnp.concatenate([x] * S5_L, axis=0)

    def tile_lanes(ref, d):
        return jnp.concatenate([ref[d]] * S5_L, axis=1)

    ct_re, ct_im = [tile_lanes(c_t_re, d) for d in range(2)], [tile_lanes(c_t_im, d) for d in range(2)]

    (cbr, cbi), (rbr, rbi) = bases(0)
    kr, ki = _zoh_coef(lam_r_re[0], lam_r_im[0], rbr, rbi)
    bbt_re, bbt_im = _cmul(kr, ki, b_t_re[0], b_t_im[0])
    pr, pi = _cpow_int(cbr, cbi, lane_blk, bits)
    cl_re, cl_im = _cmul(ct_re[0], ct_im[0], pr, pi)
    t_f = _toeplitz_rows(_dot(bbt_re, cl_re, HI) - _dot(bbt_im, cl_im, HI))
    pr, pi = _cpow_int(rbr, rbi, S5_L - 1 - row_blk, bits)
    re, im = _cmul(tile_rows(bbt_re), tile_rows(bbt_im), pr, pi)
    bcf_ref[...] = jnp.concatenate([re, im], axis=1).astype(BF16)
    re, im = _cmul(cl_re, cl_im, cbr, cbi)
    ccf_ref[...] = jnp.concatenate([re, -im], axis=0).astype(BF16)
    pr, pi = _cpow_int(rbr, rbi, jnp.full((1, S5_P), S5_L, jnp.int32), bits)
    lam_ref[0:1, :] = jnp.concatenate([pr, pr], axis=1)
    lam_ref[1:2, :] = jnp.concatenate([-pi, pi], axis=1)

    (cbr, cbi), (rbr, rbi) = bases(1)
    kr, ki = _zoh_coef(lam_c_re[1], lam_c_im[1], cbr, cbi)
    bb_re, bb_im = _cmul(kr, ki, tile_lanes(b_re, 1), tile_lanes(b_im, 1))
    pr, pi = _cpow_int(cbr, cbi, lane_blk, bits)
    bl_re, bl_im = _cmul(bb_re, bb_im, pr, pi)
    t_b = _toeplitz_rows(_dot(c_re[1], bl_re, HI) - _dot(c_im[1], bl_im, HI)).T
    kr, ki = _zoh_coef(lam_r_re[1], lam_r_im[1], rbr, rbi)
    bbt_re, bbt_im = _cmul(kr, ki, b_t_re[1], b_t_im[1])
    pr, pi = _cpow_int(rbr, rbi, row_blk, bits)
    re, im = _cmul(tile_rows(bbt_re), tile_rows(bbt_im), pr, pi)
    bcb_ref[...] = jnp.concatenate([re, im], axis=1).astype(BF16)
    pr, pi = _cpow_int(cbr, cbi, S5_L - lane_blk, bits)
    re, im = _cmul(ct_re[1], ct_im[1], pr, pi)
    ccb_ref[...] = jnp.concatenate([re, -im], axis=0).astype(BF16)
    pr, pi = _cpow_int(rbr, rbi, jnp.full((1, S5_P), S5_L, jnp.int32), bits)
    lam_ref[2:3, :] = jnp.concatenate([pr, pr], axis=1)
    lam_ref[3:4, :] = jnp.concatenate([-pi, pi], axis=1)
    lam_ref[4:8, :] = jnp.zeros((4, 2 * S5_P), F32)

    eye = (lax.broadcasted_iota(jnp.int32, (n, n), 0) == lax.broadcasted_iota(jnp.int32, (n, n), 1))
    d_diag = jnp.concatenate([d_ref[...]] * S5_L, axis=1)
    t_ref[...] = (t_f + t_b + jnp.where(eye, d_diag, 0.0)).astype(BF16)


def _s5_prep(lam_re, lam_im, log_dt, b_re, b_im, c_re, c_im, s5_d):
    n = S5_L * S5_GC
    swap = lambda a: jnp.swapaxes(a, -1, -2)
    ins = [lam_re[..., :, None], lam_im[..., :, None], lam_re[..., None, :], lam_im[..., None, :],
           log_dt[..., None, None], swap(b_re), swap(b_im), b_re, b_im, c_re, c_im, swap(c_re), swap(c_im)]
    d_grp = s5_d.reshape(DEPTH, S5_G, 1, S5_GC)

    def dir_spec(a):
        return pl.BlockSpec((None, 2, None) + a.shape[3:], lambda l, g: (l, 0, g, 0, 0))

    def out(rows, cols, dtype):
        return (pl.BlockSpec((None, None, rows, cols), lambda l, g: (l, g, 0, 0)),
                jax.ShapeDtypeStruct((DEPTH, S5_G, rows, cols), dtype))

    outs = [out(n, n, BF16), out(n, 2 * S5_P, BF16), out(n, 2 * S5_P, BF16),
            out(2 * S5_P, n, BF16), out(2 * S5_P, n, BF16), out(8, 2 * S5_P, F32)]
    return pl.pallas_call(
        _s5_prep_kernel,
        grid=(DEPTH, S5_G),
        in_specs=[dir_spec(a) for a in ins] + [pl.BlockSpec((None, None, 1, S5_GC), lambda l, g: (l, g, 0, 0))],
        out_specs=[o[0] for o in outs], out_shape=[o[1] for o in outs],
        compiler_params=_params(("arbitrary", "arbitrary")), name="s5_prep",
    )(*ins, d_grp)


def _block_transpose(a):
    n = len(a)
    blk = lax.broadcasted_iota(jnp.int32, (1, a[0].shape[1]), 1) // S5_GC
    a = list(a)
    bit = 1
    while bit < n:
        upper = (blk & bit) != 0
        for i in range(n):
            if i & bit == 0:
                j = i | bit
                lo, hi = a[i], a[j]
                a[i] = jnp.where(upper, pltpu.roll(hi, bit * S5_GC, 1), lo)
                a[j] = jnp.where(upper, hi, pltpu.roll(lo, LANES - bit * S5_GC, 1))
        bit *= 2
    return a


def _s5_kernel(*refs, n_seq, n_k, has_state):
    if has_state:
        (u_ref, t_ref, bcf_ref, bcb_ref, ccf_ref, ccb_ref, lam_ref, x0_ref,
         y_ref, ug_ref, yg_ref, inj_f, inj_b, inj_s, xin_f, xin_b) = refs
    else:
        (u_ref, t_ref, bcf_ref, bcb_ref, ccf_ref, ccb_ref, lam_ref,
         y_ref, xf_ref, xb_ref, ug_ref, yg_ref, inj_f, inj_b, inj_s, xin_f, xin_b) = refs
    ng = S5_G // 2
    r = n_seq * n_k
    tok = lambda i: pl.ds(i, r, stride=S5_L)
    u_lo = _block_transpose([u_ref[tok(i), :] for i in range(ng)])
    u_hi = _block_transpose([u_ref[tok(i), :] for i in range(ng, S5_L)])
    for g in range(ng):
        ug_ref[g] = jnp.concatenate([u_lo[g], u_hi[g]], axis=1).astype(BF16)
        inj_f[g] = _dot(ug_ref[g], bcf_ref[g])
        inj_b[g] = _dot(ug_ref[g], bcb_ref[g])

    for d, (inj, xin) in enumerate(((inj_f, xin_f), (inj_b, xin_b))):
        a = [lam_ref[g, 2 * d:2 * d + 1, :] for g in range(ng)]
        bs = [lam_ref[g, 2 * d + 1:2 * d + 2, :] for g in range(ng)]
        for g in range(ng):
            inj_s[g] = pltpu.roll(inj[g], S5_P, 1)
        x = [x0_ref[d, g] if has_state else jnp.zeros((n_seq, 2 * S5_P), F32) for g in range(ng)]
        xs = [pltpu.roll(v, S5_P, 1) for v in x]
        for k in (range(n_k) if d == 0 else range(n_k - 1, -1, -1)):
            rows = pl.ds(k, n_seq, stride=n_k)
            for g in range(ng):
                xin[g, rows, :] = x[g]
                x[g], xs[g] = (a[g] * x[g] + bs[g] * xs[g] + inj[g, rows, :],
                               a[g] * xs[g] - bs[g] * x[g] + inj_s[g, rows, :])
        if not has_state:
            for g in range(ng):
                (xf_ref if d == 0 else xb_ref)[g] = x[g]

    for g in range(ng):
        yg_ref[g] = (_dot(ug_ref[g], t_ref[g]) + _dot(xin_f[g].astype(BF16), ccf_ref[g])
                     + _dot(xin_b[g].astype(BF16), ccb_ref[g]))
    for half in range(S5_L // ng):
        y_i = _block_transpose([yg_ref[g, :, LANES * half:LANES * (half + 1)] for g in range(ng)])
        for i in range(ng):
            y_ref[tok(ng * half + i), :] = y_i[i]


def _s5(u_halves, ops, l, n_seq, n_k, x0):
    has_state = x0 is not None
    n = S5_L * S5_GC
    r = n_seq * n_k
    w = 2 * S5_P
    ng = S5_G // 2
    half_spec = pl.BlockSpec((None, r * S5_L, LANES), lambda j: (j, 0, 0))
    op_spec = lambda a: pl.BlockSpec((None, ng) + a.shape[2:], lambda j: (l, j) + (0,) * (a.ndim - 2))
    args = [u_halves] + list(ops)
    in_specs = [half_spec] + [op_spec(a) for a in ops]
    if has_state:
        args.append(x0)
        in_specs.append(pl.BlockSpec((None, 2, ng, n_seq, w), lambda j: (l, 0, j, 0, 0)))
    out_shape = [jax.ShapeDtypeStruct((2, r * S5_L, LANES), F32)]
    out_specs = [half_spec]
    if not has_state:
        out_shape += [jax.ShapeDtypeStruct((S5_G, n_seq, w), F32)] * 2
        out_specs += [pl.BlockSpec((ng, n_seq, w), lambda j: (j, 0, 0))] * 2
    return pl.pallas_call(
        functools.partial(_s5_kernel, n_seq=n_seq, n_k=n_k, has_state=has_state),
        grid=(2,), in_specs=in_specs, out_specs=out_specs, out_shape=out_shape,
        scratch_shapes=[pltpu.VMEM((ng, r, n), BF16), pltpu.VMEM((ng, r, n), F32)]
        + [pltpu.VMEM((ng, r, w), F32)] * 5,
        compiler_params=_params(("arbitrary",)), name="s5_scan",
    )(*args)


def _out_ffn_kernel(x_ref, m_ref, r_ref, a_ref, ya_ref, yb_ref, mod_ref, wglu_ref, bglu_ref, wout_ref,
                    n_post_ref, n_pre_ref, n_fpost_ref, wg_ref, wu_ref, wd_ref, o_ref):
    def mod(i):
        return mod_ref[:, i * D_MODEL:(i + 1) * D_MODEL]

    y = jnp.concatenate([ya_ref[...], yb_ref[...]], axis=1)
    gs = 0.5 * y * (1.0 + jnp.tanh(math.sqrt(2.0 / math.pi) * (y + 0.044715 * (y * y * y))))
    s_out = gs * jax.nn.sigmoid(_dot(gs.astype(BF16), wglu_ref[...]) + bglu_ref[...])
    mixed = jnp.concatenate([m_ref[...], r_ref[...], a_ref[...], s_out], axis=1).astype(BF16)
    x1 = x_ref[...] + mod(2) * _rms(_dot(mixed, wout_ref[...]), n_post_ref[...])
    h = (_rms(x1, n_pre_ref[...]) * (1.0 + mod(4)) + mod(3)).astype(BF16)
    act = (_silu(_dot(h, wg_ref[...])) * _dot(h, wu_ref[...])).astype(BF16)
    o_ref[...] = x1 + mod(5) * _rms(_dot(act, wd_ref[...]), n_fpost_ref[...])


def _out_ffn(x, m_out, r_out, a_out, y_halves, mod, l, mod_row, wp):
    n = x.shape[0]
    row = lambda w: pl.BlockSpec((ROW_TILE, w), lambda i: (i, 0))
    half = lambda j: pl.BlockSpec((None, ROW_TILE, LANES), lambda i: (j, i, 0))
    params = [wp[k] for k in ('w_glu', 'b_glu', 'w_out', 'n_mix_post', 'n_ffn_pre', 'n_ffn_post',
                              'w_gate', 'w_up', 'w_down')]
    return pl.pallas_call(
        _out_ffn_kernel,
        grid=(n // ROW_TILE,),
        in_specs=[row(D_MODEL), row(GROUP_W), row(GROUP_W), row(GROUP_W), half(0), half(1),
                  _mod_spec(l, mod_row)] + [_layer_spec(a, l, pipeline_mode=pl.Buffered(1)) for a in params],
        out_specs=row(D_MODEL),
        out_shape=jax.ShapeDtypeStruct((n, D_MODEL), F32),
        compiler_params=_params(("arbitrary",)), name="out_ffn",
    )(x, m_out, r_out, a_out, y_halves, y_halves, mod, *params)


def _layer(x, l, n_seq, t, mod, mod_row, wp, s5_ops, lam_init, st):
    ctx = st is None
    res = _in_proj(x, mod, l, mod_row, wp['n_mix_pre'], wp['w_in'], emit_kv=ctx)
    z, gates = res[0], res[2]
    m_res = _mlstm(z, gates, wp['m_bias'], wp['m_norm'], l, n_seq, t, None if ctx else st[0:2])
    r_res = _retention(z, wp['r_decay'], wp['r_norm'], l, n_seq, t, None if ctx else st[2])
    a_out = _attention(z, wp['a_lam'], wp['a_norm'], l, n_seq, t, None if ctx else st[3:5], lam_init)
    s_res = _s5(res[1], s5_ops, l, n_seq, t // S5_L, None if ctx else st[5])
    x_new = _out_ffn(x, m_res[0], r_res[0], a_out, s_res[0], mod, l, mod_row, wp)
    if not ctx:
        return x_new, None
    return x_new, (m_res[1], m_res[2], m_res[3], r_res[1], res[3], res[4], s_res[1], s_res[2])


def _pad_to(a, rows, cols=LANES):
    pad = [(0, 0)] * (a.ndim - 2) + [(0, rows - a.shape[-2]), (0, cols - a.shape[-1])]
    return jnp.pad(a, pad)


def kernel(x_prompt, x_sample, state_mlstm_C, state_mlstm_n, state_mlstm_m, state_ret, cache_diff_k, cache_diff_v, state_s5_re, state_s5_im, c, c_ctx, w_ada, b_ada, n_mix_pre, n_mix_post, n_ffn_pre, n_ffn_post, w_in, w_out, m_gate_bias, m_norm, r_decay_logit, r_norm, a_lam_q1, a_lam_k1, a_lam_q2, a_lam_k2, a_norm, s5_lam_re, s5_lam_im, s5_log_dt, s5_b_re, s5_b_im, s5_c_re, s5_c_im, s5_d, s5_w_glu, s5_b_glu, w_ffn_gate, w_ffn_up, w_ffn_down):
    n_ctx, t_ctx, _ = x_prompt.shape
    n_lat, t_lat, _ = x_sample.shape
    past = cache_diff_k.shape[2]

    cond8 = jnp.concatenate([c, c_ctx[None, :], jnp.zeros((8 - n_lat - 1, D_MODEL), F32)], axis=0)
    mod = _ada(cond8, w_ada, b_ada).reshape(DEPTH, 8, 1, 6 * D_MODEL)
    s5_ops = _s5_prep(s5_lam_re, s5_lam_im, s5_log_dt, s5_b_re, s5_b_im, s5_c_re, s5_c_im, s5_d)

    n_gate = 4 * N_HEADS
    row = lambda a: a[:, None, :]
    wp = dict(
        w_in=jnp.concatenate([w_in[..., :4 * GROUP_W], w_in[..., 4 * GROUP_W + n_gate:],
                              w_in[..., 4 * GROUP_W:4 * GROUP_W + n_gate],
                              jnp.zeros((DEPTH, D_MODEL, LANES - n_gate), F32)], axis=-1).astype(BF16),
        w_out=w_out.astype(BF16), w_glu=s5_w_glu.astype(BF16), b_glu=row(s5_b_glu),
        w_gate=w_ffn_gate.astype(BF16), w_up=w_ffn_up.astype(BF16), w_down=w_ffn_down.astype(BF16),
        n_mix_pre=row(n_mix_pre), n_mix_post=row(n_mix_post), n_ffn_pre=row(n_ffn_pre),
        n_ffn_post=row(n_ffn_post), m_norm=row(m_norm), r_norm=row(r_norm), a_norm=row(a_norm),
        m_bias=_pad_to(row(m_gate_bias), 1), r_decay=_pad_to(r_decay_logit, 8),
        a_lam=_pad_to(jnp.stack([a_lam_q1, a_lam_k1, a_lam_q2, a_lam_k2], axis=1), 8))
    lam_inits = [0.8 - 0.6 * math.exp(-0.3 * l) for l in range(DEPTH)]

    x = x_prompt.reshape(n_ctx * t_ctx, D_MODEL)
    new_states = []
    for l in range(DEPTH):
        x, st = _layer(x, l, n_ctx, t_ctx, mod, lambda i: n_lat, wp, s5_ops, lam_inits[l], None)
        new_states.append(st)
    y_prompt = x.reshape(n_ctx, t_ctx, D_MODEL)

    x0 = jnp.stack([state_s5_re, state_s5_im], axis=-2)
    x0 = x0.transpose(1, 2, 3, 0, 4, 5).reshape(DEPTH, 2, S5_G, n_lat, 2 * S5_P)
    s_ext0 = jnp.concatenate([state_mlstm_C, state_mlstm_n[..., None],
                              jnp.zeros(state_mlstm_n.shape + (LANES - HEAD_DIM - 1,), F32)], axis=-1)
    st = (s_ext0, state_mlstm_m.reshape(n_lat, DEPTH, 2 * N_HEADS, 1), state_ret,
          cache_diff_k.reshape(n_lat, DEPTH, past, GROUP_W), cache_diff_v.reshape(n_lat, DEPTH, past, GROUP_W), x0)
    x = x_sample.reshape(n_lat * t_lat, D_MODEL)
    tiles_per_seq = t_lat // ROW_TILE
    for l in range(DEPTH):
        x, _ = _layer(x, l, n_lat, t_lat, mod, lambda i: i // tiles_per_seq, wp, s5_ops, lam_inits[l], st)
    y_sample = x.reshape(n_lat, t_lat, D_MODEL)

    stack = lambda i: jnp.stack([s[i] for s in new_states], axis=1)
    kv = lambda i: jnp.stack([s[i].reshape(n_ctx, t_ctx, N_HEADS, HEAD_DIM) for s in new_states], axis=1)
    xs = jnp.stack([stack(6), stack(7)], axis=2)
    xs = xs.reshape(S5_G, DEPTH, 2, n_ctx, 2, S5_P).transpose(3, 1, 2, 0, 4, 5)
    return (y_prompt, y_sample, stack(0), stack(1)[..., 0], stack(2).reshape(n_ctx, DEPTH, 2, N_HEADS),
            stack(3), kv(4), kv(5), xs[..., 0, :], xs[..., 1, :])
```

```python
import functools
import math

import jax
import jax.numpy as jnp
from jax import lax
from jax.experimental import pallas as pl
from jax.experimental.pallas import tpu as pltpu

F32 = jnp.float32
BF16 = jnp.bfloat16
HI = lax.Precision.HIGHEST

D_MODEL = 1024
DEPTH = 2
GRID_W = 64
HEAD_DIM = 64
GROUP_W = 256
N_HEADS = 4
A_SUB = 32
S5_GC = 16
S5_G = 16
S5_P = 64
D_FF = 2816
ROPE_BASE = 10000.0
EPS = 1e-6
N_MAIN = 12 * GROUP_W
N_MIX = 11 * GROUP_W
LANES = 128
CHUNK = 256
ROW_TILE = 512
S5_L = 16
NEG = -1e30
VMEM_LIMIT = 56 * 1024 * 1024


def _dot(a, b, precision=None):
    return jnp.dot(a, b, preferred_element_type=F32, precision=precision)


def _dot_nt(a, b):
    return lax.dot_general(a, b, (((1,), (1,)), ((), ())), preferred_element_type=F32)


def _log_sigmoid(x):
    return jnp.minimum(x, 0.0) - jnp.log(1.0 + jnp.exp(-jnp.abs(x)))


def _silu(x):
    return x * jax.nn.sigmoid(x)


def _rms(x, g):
    return x * lax.rsqrt(jnp.mean(x * x, axis=-1, keepdims=True) + EPS) * g


def _params(sem=None):
    return pltpu.CompilerParams(dimension_semantics=sem, vmem_limit_bytes=VMEM_LIMIT)


def _layer_spec(a, l, **kw):
    n = a.ndim - 1
    return pl.BlockSpec((None,) + a.shape[1:], lambda *_: (l,) + (0,) * n, **kw)


def _seq_layer_spec(a, l):
    n = a.ndim - 2
    return pl.BlockSpec((None, None) + a.shape[2:], lambda b: (b, l) + (0,) * n)


def _split_bf16(x):
    hi = x.astype(BF16)
    return hi, (x - hi.astype(F32)).astype(BF16)


def _ada_kernel(c_ref, w_ref, b_ref, o_ref):
    a_hi, a_lo = _split_bf16(_silu(c_ref[...]))
    w_hi, w_lo = _split_bf16(w_ref[...])
    o_ref[...] = _dot(a_hi, w_hi) + (_dot(a_lo, w_hi) + _dot(a_hi, w_lo)) + b_ref[...]


def _ada(cond8, w_ada, b_ada):
    tn = 1536
    return pl.pallas_call(
        _ada_kernel,
        grid=(DEPTH, 6 * D_MODEL // tn),
        in_specs=[pl.BlockSpec((8, D_MODEL), lambda l, j: (0, 0)),
                  pl.BlockSpec((None, D_MODEL, tn), lambda l, j: (l, 0, j)),
                  pl.BlockSpec((None, 1, tn), lambda l, j: (l, 0, j))],
        out_specs=pl.BlockSpec((None, 8, tn), lambda l, j: (l, 0, j)),
        out_shape=jax.ShapeDtypeStruct((DEPTH, 8, 6 * D_MODEL), F32),
        compiler_params=_params(("arbitrary", "arbitrary")),
        name="ada",
    )(cond8, w_ada, b_ada.reshape(DEPTH, 1, 6 * D_MODEL))


def _in_proj_kernel(x_ref, mod_ref, g_ref, w_ref, z_ref, u_ref, gate_ref, *rest):
    kv_refs, wb_ref = rest[:-1], rest[-1]

    @pl.when(pl.program_id(0) == 0)
    def _():
        n_gate = 4 * N_HEADS
        rows_per = 128
        for r0 in range(0, D_MODEL, rows_per):
            rows = slice(r0, r0 + rows_per)
            wb_ref[rows, 0:4 * GROUP_W] = w_ref[rows, 0:4 * GROUP_W].astype(BF16)
            tail = w_ref[rows, 4 * GROUP_W:]
            wb_ref[rows, 4 * GROUP_W:N_MAIN] = tail[:, n_gate:].astype(BF16)
            wb_ref[rows, N_MAIN:] = jnp.concatenate(
                [tail[:, :n_gate], jnp.zeros((rows_per, LANES - n_gate), F32)], axis=1).astype(BF16)

    h = _rms(x_ref[...], g_ref[...]) * (1.0 + mod_ref[:, D_MODEL:2 * D_MODEL]) + mod_ref[:, 0:D_MODEL]
    hb = h.astype(BF16)
    z = _dot(hb, wb_ref[:, :N_MIX])
    z_ref[...] = z.astype(BF16)
    u = _dot(hb, wb_ref[:, N_MIX:N_MAIN])
    u_ref[0] = u[:, :LANES]
    u_ref[1] = u[:, LANES:]
    gate_ref[...] = _dot(hb, wb_ref[:, N_MAIN:])
    if kv_refs:
        kv_refs[0][...] = z[:, 9 * GROUP_W:10 * GROUP_W]
        kv_refs[1][...] = z[:, 10 * GROUP_W:11 * GROUP_W]


def _mod_spec(l, mod_row):
    return pl.BlockSpec((None, None, 1, 6 * D_MODEL), lambda i: (l, mod_row(i), 0, 0))


def _in_proj(x, mod, l, mod_row, gain, w, emit_kv):
    n = x.shape[0]
    row = lambda width: pl.BlockSpec((ROW_TILE, width), lambda i: (i, 0))
    out_specs = [row(N_MIX), pl.BlockSpec((2, ROW_TILE, LANES), lambda i: (0, i, 0)), row(LANES)]
    out_specs += [row(GROUP_W)] * (2 if emit_kv else 0)
    out_shape = [jax.ShapeDtypeStruct((n, N_MIX), BF16), jax.ShapeDtypeStruct((2, n, LANES), F32),
                 jax.ShapeDtypeStruct((n, LANES), F32)]
    out_shape += [jax.ShapeDtypeStruct((n, GROUP_W), F32)] * (2 if emit_kv else 0)
    return pl.pallas_call(
        _in_proj_kernel,
        grid=(n // ROW_TILE,),
        in_specs=[row(D_MODEL), _mod_spec(l, mod_row), _layer_spec(gain, l),
                  _layer_spec(w, l, pipeline_mode=pl.Buffered(1))],
        out_specs=out_specs, out_shape=out_shape,
        scratch_shapes=[pltpu.VMEM((D_MODEL, N_MAIN + LANES), BF16)],
        compiler_params=_params(("arbitrary",)),
        name="in_proj",
    )(x, mod, gain, w)


def _tri_masks(n):
    row = lax.broadcasted_iota(jnp.int32, (n, n), 0)
    col = lax.broadcasted_iota(jnp.int32, (n, n), 1)
    return row >= col, row <= col


def _head_mean_matrix():
    r = lax.broadcasted_iota(jnp.int32, (GROUP_W, GROUP_W), 0) // HEAD_DIM
    c = lax.broadcasted_iota(jnp.int32, (GROUP_W, GROUP_W), 1) // HEAD_DIM
    return jnp.where(r == c, 1.0 / HEAD_DIM, 0.0).astype(BF16)


def _head_mean(x, j):
    hi, lo = _split_bf16(x)
    return _dot(hi, j) + _dot(lo, j)


def _head_groupnorm(x, g, j):
    xc = x - _head_mean(x, j)
    return xc * lax.rsqrt(_head_mean(xc * xc, j) + EPS) * g


def _head_rms(x, g, j):
    return x * lax.rsqrt(_head_mean(x * x, j) + EPS) * g


def _hs(h):
    return slice(HEAD_DIM * h, HEAD_DIM * (h + 1))


def _v_ext(v_ref, rows, hs):
    ones = jnp.ones((rows.stop - rows.start, HEAD_DIM), BF16)
    return jnp.concatenate([v_ref[rows, hs].astype(BF16), ones], axis=1)


def _zspec(t, colblk):
    return pl.BlockSpec((t, GROUP_W), lambda b: (b, colblk))


def _scan_max(x, reverse):
    n = x.shape[1]
    lane = lax.broadcasted_iota(jnp.int32, x.shape, 1)
    sh = 1
    while sh < n:
        if reverse:
            x = jnp.maximum(x, jnp.where(lane < n - sh, pltpu.roll(x, n - sh, 1), NEG))
        else:
            x = jnp.maximum(x, jnp.where(lane >= sh, pltpu.roll(x, sh, 1), NEG))
        sh *= 2
    return x


def _ends(x, is_fwd):
    return jnp.where(is_fwd, x[:, x.shape[1] - 1:], x[:, 0:1])


def _mlstm_kernel(*refs, n_chunk, has_state):
    if has_state:
        q_ref, k_ref, v_ref, o_ref, g_ref, bias_ref, gain_ref, s0_ref, m0_ref, out_ref = refs
    else:
        q_ref, k_ref, v_ref, o_ref, g_ref, bias_ref, gain_ref, out_ref, cn_ref, nn_ref, mn_ref = refs
    L = CHUNK
    nu = 2 * N_HEADS
    tril, triu = _tri_masks(L)
    is_fwd = lax.broadcasted_iota(jnp.int32, (nu, 1), 0) < N_HEADS

    b8, w8, mcum8 = [], [], []
    for c in range(n_chunk):
        p_t = (g_ref[c * L:(c + 1) * L, :] + bias_ref[...]).T
        lf = _log_sigmoid(p_t[nu:2 * nu, :])
        b = jnp.where(is_fwd, _dot(lf, triu.astype(F32), HI), _dot(lf, tril.astype(F32), HI))
        w = p_t[0:nu, :] - b
        b8.append(b)
        w8.append(w)
        mcum8.append(jnp.where(is_fwd, _scan_max(w, False), _scan_max(w, True)))

    m0 = m0_ref[...] if has_state else jnp.zeros((nu, 1), F32)
    m_in_f, m_in_b = [None] * n_chunk, [None] * n_chunk
    m = m0
    for c in range(n_chunk):
        m_in_f[c] = m
        m = (b8[c] + jnp.maximum(mcum8[c], m))[:, L - 1:]
    m_fin_f = m
    m = m0
    for c in range(n_chunk - 1, -1, -1):
        m_in_b[c] = m
        m = (b8[c] + jnp.maximum(mcum8[c], m))[:, 0:1]
    if not has_state:
        mn_ref[...] = jnp.where(is_fwd, m_fin_f, m)

    wk8, dec8, cols = [], [], []
    for c in range(n_chunk):
        m_in = jnp.where(is_fwd, m_in_f[c], m_in_b[c])
        g = jnp.maximum(mcum8[c], m_in)
        m_row = b8[c] + g
        m_new, b_last = _ends(m_row, is_fwd), _ends(b8[c], is_fwd)
        wk8.append(jnp.exp(b_last + w8[c] - m_new))
        dec8.append(jnp.exp(b_last + m_in - m_new))
        stats = jnp.concatenate([g, jnp.exp(m_in - g), jnp.exp(-m_row), jnp.zeros((LANES - 3 * nu, L), F32)], axis=0)
        cols.append(stats.T)

    k_t = [(k_ref[c * L:(c + 1) * L, :].astype(F32) * (HEAD_DIM ** -0.5)).T for c in range(n_chunk)]

    h_all = []
    for h in range(N_HEADS):
        hs = _hs(h)
        qb = [q_ref[c * L:(c + 1) * L, hs].astype(BF16) for c in range(n_chunk)]
        kt = [k_t[c][hs, :] for c in range(n_chunk)]
        vx = [_v_ext(v_ref, slice(c * L, (c + 1) * L), hs) for c in range(n_chunk)]
        qk = [_dot(qb[c], kt[c].astype(BF16)) for c in range(n_chunk)]
        h_sum = [None] * n_chunk
        for d in range(2):
            tri = tril if d == 0 else triu
            j = N_HEADS * d + h
            s_ext = s0_ref[d, h] if has_state else None
            order = range(n_chunk) if d == 0 else range(n_chunk - 1, -1, -1)
            for ci, c in enumerate(order):
                wgt = jnp.exp(jnp.where(tri, w8[c][j:j + 1, :] - cols[c][:, j:j + 1], NEG))
                tot = _dot((qk[c] * wgt).astype(BF16), vx[c])
                if s_ext is not None:
                    tot = tot + cols[c][:, nu + j:nu + j + 1] * _dot(qb[c], s_ext.astype(BF16))
                den = jnp.maximum(jnp.abs(tot[:, HEAD_DIM:HEAD_DIM + 1]), cols[c][:, 2 * nu + j:2 * nu + j + 1])
                hd = tot[:, 0:HEAD_DIM] / den
                h_sum[c] = hd if h_sum[c] is None else h_sum[c] + hd
                if (not has_state) or ci < n_chunk - 1:
                    upd = _dot((kt[c] * wk8[c][j:j + 1, :]).astype(BF16), vx[c])
                    s_ext = upd if s_ext is None else upd + dec8[c][j:j + 1, :] * s_ext
            if not has_state:
                cn_ref[d, h] = s_ext[:, 0:HEAD_DIM]
                nn_ref[d, h] = s_ext[:, HEAD_DIM:HEAD_DIM + 1]
        h_all.append(h_sum)
    jm = _head_mean_matrix()
    for c in range(n_chunk):
        rows = slice(c * L, (c + 1) * L)
        gated = jax.nn.sigmoid(o_ref[rows, :].astype(F32)) * jnp.concatenate([h_all[h][c] for h in range(N_HEADS)], axis=1)
        out_ref[rows, :] = _head_groupnorm(gated, gain_ref[...], jm)


def _ret_kernel(*refs, n_chunk, has_state):
    if has_state:
        q_ref, k_ref, v_ref, g_ref, lg_ref, gain_ref, s0_ref, out_ref, dmat_ref, col_ref, row_ref = refs
    else:
        q_ref, k_ref, v_ref, g_ref, lg_ref, gain_ref, out_ref, sn_ref, dmat_ref, col_ref, row_ref = refs
    L = CHUNK

    @pl.when(pl.program_id(0) == 0)
    def _():
        row = lax.broadcasted_iota(jnp.int32, (L, L), 0)
        col = lax.broadcasted_iota(jnp.int32, (L, L), 1)
        diff = (row - col).astype(F32)
        log_g = _log_sigmoid(lg_ref[...])
        pos_c = lax.broadcasted_iota(jnp.int32, (L, 1), 0).astype(F32)
        pos_r = lax.broadcasted_iota(jnp.int32, (1, L), 1).astype(F32)
        for h in range(N_HEADS):
            lgf = log_g[0:1, h:h + 1]
            lgb = log_g[1:2, h:h + 1]
            dmat_ref[h] = (jnp.where(diff >= 0, jnp.exp(jnp.maximum(diff, 0.0) * lgf), 0.0)
                           + jnp.where(diff <= 0, jnp.exp(jnp.maximum(-diff, 0.0) * lgb), 0.0))
            col_ref[:, h:h + 1] = jnp.exp((pos_c + 1.0) * lgf)
            col_ref[:, N_HEADS + h:N_HEADS + h + 1] = jnp.exp((L - pos_c) * lgb)
            row_ref[h:h + 1, :] = jnp.exp((L - 1.0 - pos_r) * lgf)
            row_ref[N_HEADS + h:N_HEADS + h + 1, :] = jnp.exp(pos_r * lgb)
            row_ref[2 * N_HEADS + h:2 * N_HEADS + h + 1, :] = jnp.exp(L * lgf) + jnp.zeros((1, L), F32)
            row_ref[3 * N_HEADS + h:3 * N_HEADS + h + 1, :] = jnp.exp(L * lgb) + jnp.zeros((1, L), F32)

    k_t = [(k_ref[c * L:(c + 1) * L, :].astype(F32) * (HEAD_DIM ** -0.5)).T for c in range(n_chunk)]
    o_all = []
    for h in range(N_HEADS):
        hs = _hs(h)
        zeta_f = row_ref[h:h + 1, :]
        zeta_b = row_ref[N_HEADS + h:N_HEADS + h + 1, :]
        gl_f = row_ref[2 * N_HEADS + h:2 * N_HEADS + h + 1, 0:1]
        gl_b = row_ref[3 * N_HEADS + h:3 * N_HEADS + h + 1, 0:1]
        kt = [k_t[c][hs, :] for c in range(n_chunk)]
        vb = [v_ref[c * L:(c + 1) * L, hs].astype(BF16) for c in range(n_chunk)]
        sf = [None] * (n_chunk + 1)
        sb = [None] * (n_chunk + 1)
        if has_state:
            sf[0] = s0_ref[0, h]
            sb[n_chunk] = s0_ref[1, h]
        for c in range(n_chunk):
            if (not has_state) or c < n_chunk - 1:
                upd = _dot((kt[c] * zeta_f).astype(BF16), vb[c])
                sf[c + 1] = upd if sf[c] is None else upd + gl_f * sf[c]
        for c in range(n_chunk - 1, -1, -1):
            if (not has_state) or c > 0:
                upd = _dot((kt[c] * zeta_b).astype(BF16), vb[c])
                sb[c] = upd if sb[c + 1] is None else upd + gl_b * sb[c + 1]
        if not has_state:
            sn_ref[0, h] = sf[n_chunk]
            sn_ref[1, h] = sb[0]
        o_h = []
        for c in range(n_chunk):
            qb = q_ref[c * L:(c + 1) * L, hs].astype(BF16)
            att = _dot(qb, kt[c].astype(BF16)) * dmat_ref[h]
            o = _dot(att.astype(BF16), vb[c])
            if has_state:
                o = o + col_ref[:, h:h + 1] * _dot(qb, sf[c].astype(BF16))
                o = o + col_ref[:, N_HEADS + h:N_HEADS + h + 1] * _dot(qb, sb[c + 1].astype(BF16))
            o_h.append(o)
        o_all.append(o_h)
    jm = _head_mean_matrix()
    for c in range(n_chunk):
        rows = slice(c * L, (c + 1) * L)
        o = jnp.concatenate([o_all[h][c] for h in range(N_HEADS)], axis=1)
        out_ref[rows, :] = _silu(g_ref[rows, :].astype(F32)) * _head_groupnorm(o, gain_ref[...], jm)


def _lambda(lam_ref, lam_init):
    v = lam_ref[...]
    s1 = jnp.sum(v[0:1, :] * v[1:2, :], axis=-1, keepdims=True)
    s2 = jnp.sum(v[2:3, :] * v[3:4, :], axis=-1, keepdims=True)
    return jnp.exp(s1) - jnp.exp(s2) + lam_init


def _first_half(shape):
    lane = lax.broadcasted_iota(jnp.int32, shape, 1)
    return (lane % (A_SUB // 2)) < (A_SUB // 4)


def _rope(x, cos, sin_signed):
    n = x.shape[1]
    first_half = _first_half(x.shape)
    partner = jnp.where(first_half, pltpu.roll(x, n - A_SUB // 4, 1), pltpu.roll(x, A_SUB // 4, 1))
    return x * cos + partner * sin_signed


def _attn_kernel(*refs, t, has_cache, lam_init):
    if has_cache:
        q_ref, k_ref, v_ref, kc_ref, vc_ref, lam_ref, gain_ref, out_ref, qr_ref, kr_ref, cos_ref, sin_ref = refs
    else:
        q_ref, k_ref, v_ref, lam_ref, gain_ref, out_ref, qr_ref, kr_ref = refs
    lam = _lambda(lam_ref, lam_init)
    scale = A_SUB ** -0.5
    n_qb = t // CHUNK

    if has_cache:
        @pl.when(pl.program_id(0) == 0)
        def _():
            lane = lax.broadcasted_iota(jnp.int32, (GRID_W, GROUP_W), 1)
            pos = lax.broadcasted_iota(jnp.int32, (GRID_W, GROUP_W), 0).astype(F32)
            n_freq = A_SUB // 4
            ang = pos * jnp.exp((lane % n_freq).astype(F32) * (-math.log(ROPE_BASE) / n_freq))
            cos_t = jnp.cos(ang)
            sin_t = jnp.where(_first_half((GRID_W, GROUP_W)), -jnp.sin(ang), jnp.sin(ang))
            row_axis = (lane % A_SUB) < (A_SUB // 2)
            for r in range(t // GRID_W):
                rows = slice(r * GRID_W, (r + 1) * GRID_W)
                cos_ref[rows, :] = jnp.where(row_axis, cos_t[r:r + 1, :], cos_t)
                sin_ref[rows, :] = jnp.where(row_axis, sin_t[r:r + 1, :], sin_t)

        kr_ref[...] = _rope(k_ref[...].astype(F32), cos_ref[...], sin_ref[...]).astype(BF16)
        qr_ref[...] = (_rope(q_ref[...].astype(F32), cos_ref[...], sin_ref[...]) * scale).astype(BF16)
    else:
        kr_ref[...] = k_ref[...].astype(BF16)
        qr_ref[...] = (q_ref[...].astype(F32) * scale).astype(BF16)

    o_all = []
    for h in range(N_HEADS):
        hs = _hs(h)
        vx = _v_ext(v_ref, slice(0, t), hs)
        if has_cache:
            vcx = _v_ext(vc_ref, slice(0, vc_ref.shape[0]), hs)
        o_h = []
        for qi in range(n_qb):
            rows = slice(qi * CHUNK, (qi + 1) * CHUNK)
            o_maps = []
            for m in range(2):
                ms = slice(HEAD_DIM * h + A_SUB * m, HEAD_DIM * h + A_SUB * (m + 1))
                qm = qr_ref[rows, ms]
                s_l = _dot_nt(qm, kr_ref[:, ms])
                mx = jnp.max(s_l, axis=-1, keepdims=True)
                if has_cache:
                    s_c = _dot_nt(qm, kc_ref[:, ms].astype(BF16))
                    mx = jnp.maximum(mx, jnp.max(s_c, axis=-1, keepdims=True))
                tot = _dot(jnp.exp(s_l - mx).astype(BF16), vx)
                if has_cache:
                    tot = tot + _dot(jnp.exp(s_c - mx).astype(BF16), vcx)
                o_maps.append(tot[:, 0:HEAD_DIM] / tot[:, HEAD_DIM:HEAD_DIM + 1])
            o_h.append(o_maps[0] - lam * o_maps[1])
        o_all.append(o_h)
    jm = _head_mean_matrix()
    for qi in range(n_qb):
        rows = slice(qi * CHUNK, (qi + 1) * CHUNK)
        o = jnp.concatenate([o_all[h][qi] for h in range(N_HEADS)], axis=1)
        out_ref[rows, :] = _head_rms(o, gain_ref[...], jm) * (1.0 - lam_init)


def _mixers_kernel(*refs, n_chunk, t, has_state, lam_init, n_in, n_out):
    ins, outs, scratch = refs[:sum(n_in)], refs[sum(n_in):sum(n_in) + sum(n_out)], refs[sum(n_in) + sum(n_out):]
    i0, o0 = 0, 0
    parts = []
    for k in range(3):
        parts.append((ins[i0:i0 + n_in[k]], outs[o0:o0 + n_out[k]]))
        i0 += n_in[k]
        o0 += n_out[k]
    _mlstm_kernel(*parts[0][0], *parts[0][1], n_chunk=n_chunk, has_state=has_state)
    _ret_kernel(*parts[1][0], *parts[1][1], *scratch[0:3], n_chunk=n_chunk, has_state=has_state)
    _attn_kernel(*parts[2][0], *parts[2][1], *scratch[3:], t=t, has_cache=has_state, lam_init=lam_init)


def _mixers(z, gates, wp, l, n_seq, t, st, lam_init):
    has_state = st is not None
    nu = 2 * N_HEADS
    seq = lambda width: pl.BlockSpec((t, width), lambda b: (b, 0))
    per_seq = lambda *shape: pl.BlockSpec((None,) + shape, lambda b: (b,) + (0,) * len(shape))
    out_rows = jax.ShapeDtypeStruct((n_seq * t, GROUP_W), F32)
    head_state = (2, N_HEADS, HEAD_DIM, HEAD_DIM)

    m_in = [z, z, z, z, gates, wp['m_bias'], wp['m_norm']]
    m_specs = [_zspec(t, 0), _zspec(t, 1), _zspec(t, 2), _zspec(t, 3), seq(LANES),
               _layer_spec(wp['m_bias'], l), _layer_spec(wp['m_norm'], l)]
    r_in = [z, z, z, z, wp['r_decay'], wp['r_norm']]
    r_specs = [_zspec(t, 4), _zspec(t, 5), _zspec(t, 6), _zspec(t, 7),
               _layer_spec(wp['r_decay'], l), _layer_spec(wp['r_norm'], l)]
    a_in = [z, z, z]
    a_specs = [_zspec(t, 8), _zspec(t, 9), _zspec(t, 10)]
    m_out, r_out, a_out = [out_rows], [out_rows], [out_rows]
    m_ospecs, r_ospecs, a_ospecs = [seq(GROUP_W)], [seq(GROUP_W)], [seq(GROUP_W)]
    scratch = [pltpu.VMEM((N_HEADS, CHUNK, CHUNK), F32), pltpu.VMEM((CHUNK, LANES), F32),
               pltpu.VMEM((4 * N_HEADS, CHUNK), F32),
               pltpu.VMEM((t, GROUP_W), BF16), pltpu.VMEM((t, GROUP_W), BF16)]
    if has_state:
        m_in += list(st[0:2])
        m_specs += [_seq_layer_spec(a, l) for a in st[0:2]]
        r_in.append(st[2])
        r_specs.append(_seq_layer_spec(st[2], l))
        a_in += list(st[3:5])
        a_specs += [_seq_layer_spec(a, l) for a in st[3:5]]
        scratch += [pltpu.VMEM((t, GROUP_W), F32), pltpu.VMEM((t, GROUP_W), F32)]
    else:
        m_out += [jax.ShapeDtypeStruct((n_seq,) + head_state, F32),
                  jax.ShapeDtypeStruct((n_seq, 2, N_HEADS, HEAD_DIM, 1), F32),
                  jax.ShapeDtypeStruct((n_seq, nu, 1), F32)]
        m_ospecs += [per_seq(*head_state), per_seq(2, N_HEADS, HEAD_DIM, 1), per_seq(nu, 1)]
        r_out.append(jax.ShapeDtypeStruct((n_seq,) + head_state, F32))
        r_ospecs.append(per_seq(*head_state))
    a_in += [wp['a_lam'], wp['a_norm']]
    a_specs += [_layer_spec(wp['a_lam'], l), _layer_spec(wp['a_norm'], l)]
    res = pl.pallas_call(
        functools.partial(_mixers_kernel, n_chunk=t // CHUNK, t=t, has_state=has_state, lam_init=lam_init,
                          n_in=(len(m_in), len(r_in), len(a_in)), n_out=(len(m_out), len(r_out), len(a_out))),
        grid=(n_seq,), in_specs=m_specs + r_specs + a_specs,
        out_specs=m_ospecs + r_ospecs + a_ospecs, out_shape=m_out + r_out + a_out,
        scratch_shapes=scratch, compiler_params=_params(("arbitrary",)), name="mixers",
    )(*m_in, *r_in, *a_in)
    return res[:len(m_out)], res[len(m_out):len(m_out) + len(r_out)], res[-1]


def _cmul(ar, ai, br, bi):
    return ar * br - ai * bi, ar * bi + ai * br


def _lam_bar(lre, lim, dt):
    mag = jnp.exp(lre * dt)
    return mag * jnp.cos(lim * dt), mag * jnp.sin(lim * dt)


def _cpow_int(br, bi, e, n_bits):
    pr = jnp.ones(e.shape, F32)
    pi = jnp.zeros(e.shape, F32)
    for bit in range(n_bits):
        on = ((e >> bit) & 1) == 1
        qr, qi = _cmul(pr, pi, br, bi)
        pr, pi = jnp.where(on, qr, pr), jnp.where(on, qi, pi)
        if bit + 1 < n_bits:
            br, bi = _cmul(br, bi, br, bi)
    return pr, pi


def _zoh_coef(lre, lim, br, bi):
    den = lre * lre + lim * lim
    return ((br - 1.0) * lre + bi * lim) / den, (bi * lre - (br - 1.0) * lim) / den


def _toeplitz_rows(kall):
    n = S5_L * S5_GC
    lane = lax.broadcasted_iota(jnp.int32, (S5_GC, n), 1)
    pieces = []
    for s in range(S5_L):
        shifted = kall if s == 0 else pltpu.roll(kall, S5_GC * s, 1)
        pieces.append(jnp.where(lane >= S5_GC * s, shifted, 0.0))
    return jnp.concatenate(pieces, axis=0)


def _s5_prep_kernel(lam_c_re, lam_c_im, lam_r_re, lam_r_im, ldt_ref, b_t_re, b_t_im, b_re, b_im,
                    c_re, c_im, c_t_re, c_t_im, d_ref, t_ref, bcf_ref, bcb_ref, ccf_ref, ccb_ref, lam_ref):
    n = S5_L * S5_GC
    lane_blk = lax.broadcasted_iota(jnp.int32, (S5_P, n), 1) // S5_GC
    row_blk = lax.broadcasted_iota(jnp.int32, (n, S5_P), 0) // S5_GC
    bits = S5_L.bit_length()

    def bases(d):
        dt = jnp.exp(ldt_ref[d])
        return (_lam_bar(lam_c_re[d], lam_c_im[d], dt), _lam_bar(lam_r_re[d], lam_r_im[d], dt))

    def tile_rows(x):
        return jnp.concatenate([x] * S5_L, axis=0)

    def tile_lanes(ref, d):
        return jnp.concatenate([ref[d]] * S5_L, axis=1)

    ct_re, ct_im = [tile_lanes(c_t_re, d) for d in range(2)], [tile_lanes(c_t_im, d) for d in range(2)]

    (cbr, cbi), (rbr, rbi) = bases(0)
    kr, ki = _zoh_coef(lam_r_re[0], lam_r_im[0], rbr, rbi)
    bbt_re, bbt_im = _cmul(kr, ki, b_t_re[0], b_t_im[0])
    pr, pi = _cpow_int(cbr, cbi, lane_blk, bits)
    cl_re, cl_im = _cmul(ct_re[0], ct_im[0], pr, pi)
    t_f = _toeplitz_rows(_dot(bbt_re, cl_re, HI) - _dot(bbt_im, cl_im, HI))
    pr, pi = _cpow_int(rbr, rbi, S5_L - 1 - row_blk, bits)
    re, im = _cmul(tile_rows(bbt_re), tile_rows(bbt_im), pr, pi)
    bcf_ref[...] = jnp.concatenate([re, im], axis=1).astype(BF16)
    re, im = _cmul(cl_re, cl_im, cbr, cbi)
    ccf_ref[...] = jnp.concatenate([re, -im], axis=0).astype(BF16)
    pr, pi = _cpow_int(rbr, rbi, jnp.full((1, S5_P), S5_L, jnp.int32), bits)
    lam_ref[0:1, :] = jnp.concatenate([pr, pr], axis=1)
    lam_ref[1:2, :] = jnp.concatenate([-pi, pi], axis=1)

    (cbr, cbi), (rbr, rbi) = bases(1)
    kr, ki = _zoh_coef(lam_c_re[1], lam_c_im[1], cbr, cbi)
    bb_re, bb_im = _cmul(kr, ki, tile_lanes(b_re, 1), tile_lanes(b_im, 1))
    pr, pi = _cpow_int(cbr, cbi, lane_blk, bits)
    bl_re, bl_im = _cmul(bb_re, bb_im, pr, pi)
    t_b = _toeplitz_rows(_dot(c_re[1], bl_re, HI) - _dot(c_im[1], bl_im, HI)).T
    kr, ki = _zoh_coef(lam_r_re[1], lam_r_im[1], rbr, rbi)
    bbt_re, bbt_im = _cmul(kr, ki, b_t_re[1], b_t_im[1])
    pr, pi = _cpow_int(rbr, rbi, row_blk, bits)
    re, im = _cmul(tile_rows(bbt_re), tile_rows(bbt_im), pr, pi)
    bcb_ref[...] = jnp.concatenate([re, im], axis=1).astype(BF16)
    pr, pi = _cpow_int(cbr, cbi, S5_L - lane_blk, bits)
    re, im = _cmul(ct_re[1], ct_im[1], pr, pi)
    ccb_ref[...] = jnp.concatenate([re, -im], axis=0).astype(BF16)
    pr, pi = _cpow_int(rbr, rbi, jnp.full((1, S5_P), S5_L, jnp.int32), bits)
    lam_ref[2:3, :] = jnp.concatenate([pr, pr], axis=1)
    lam_ref[3:4, :] = jnp.concatenate([-pi, pi], axis=1)
    lam_ref[4:8, :] = jnp.zeros((4, 2 * S5_P), F32)

    eye = (lax.broadcasted_iota(jnp.int32, (n, n), 0) == lax.broadcasted_iota(jnp.int32, (n, n), 1))
    d_diag = jnp.concatenate([d_ref[...]] * S5_L, axis=1)
    t_ref[...] = (t_f + t_b + jnp.where(eye, d_diag, 0.0)).astype(BF16)


def _s5_prep(lam_re, lam_im, log_dt, b_re, b_im, c_re, c_im, s5_d):
    n = S5_L * S5_GC
    swap = lambda a: jnp.swapaxes(a, -1, -2)
    ins = [lam_re[..., :, None], lam_im[..., :, None], lam_re[..., None, :], lam_im[..., None, :],
           log_dt[..., None, None], swap(b_re), swap(b_im), b_re, b_im, c_re, c_im, swap(c_re), swap(c_im)]
    d_grp = s5_d.reshape(DEPTH, S5_G, 1, S5_GC)

    def dir_spec(a):
        return pl.BlockSpec((None, 2, None) + a.shape[3:], lambda l, g: (l, 0, g, 0, 0))

    def out(rows, cols, dtype):
        return (pl.BlockSpec((None, None, rows, cols), lambda l, g: (l, g, 0, 0)),
                jax.ShapeDtypeStruct((DEPTH, S5_G, rows, cols), dtype))

    outs = [out(n, n, BF16), out(n, 2 * S5_P, BF16), out(n, 2 * S5_P, BF16),
            out(2 * S5_P, n, BF16), out(2 * S5_P, n, BF16), out(8, 2 * S5_P, F32)]
    return pl.pallas_call(
        _s5_prep_kernel,
        grid=(DEPTH, S5_G),
        in_specs=[dir_spec(a) for a in ins] + [pl.BlockSpec((None, None, 1, S5_GC), lambda l, g: (l, g, 0, 0))],
        out_specs=[o[0] for o in outs], out_shape=[o[1] for o in outs],
        compiler_params=_params(("arbitrary", "arbitrary")), name="s5_prep",
    )(*ins, d_grp)


def _block_transpose(a):
    n = len(a)
    blk = lax.broadcasted_iota(jnp.int32, (1, a[0].shape[1]), 1) // S5_GC
    a = list(a)
    bit = 1
    while bit < n:
        upper = (blk & bit) != 0
        for i in range(n):
            if i & bit == 0:
                j = i | bit
                lo, hi = a[i], a[j]
                a[i] = jnp.where(upper, pltpu.roll(hi, bit * S5_GC, 1), lo)
                a[j] = jnp.where(upper, hi, pltpu.roll(lo, LANES - bit * S5_GC, 1))
        bit *= 2
    return a


def _s5_kernel(*refs, n_seq, n_k, has_state):
    if has_state:
        (u_ref, t_ref, bcf_ref, bcb_ref, ccf_ref, ccb_ref, lam_ref, x0_ref,
         y_ref, ug_ref, yg_ref, inj_f, inj_b, inj_s, xin_f, xin_b) = refs
    else:
        (u_ref, t_ref, bcf_ref, bcb_ref, ccf_ref, ccb_ref, lam_ref,
         y_ref, xf_ref, xb_ref, ug_ref, yg_ref, inj_f, inj_b, inj_s, xin_f, xin_b) = refs
    ng = S5_G // 2
    r = n_seq * n_k
    tok = lambda i: pl.ds(i, r, stride=S5_L)
    u_lo = _block_transpose([u_ref[tok(i), :] for i in range(ng)])
    u_hi = _block_transpose([u_ref[tok(i), :] for i in range(ng, S5_L)])
    for g in range(ng):
        ug_ref[g] = jnp.concatenate([u_lo[g], u_hi[g]], axis=1).astype(BF16)
        inj_f[g] = _dot(ug_ref[g], bcf_ref[g])
        inj_b[g] = _dot(ug_ref[g], bcb_ref[g])

    for d, (inj, xin) in enumerate(((inj_f, xin_f), (inj_b, xin_b))):
        a = [lam_ref[g, 2 * d:2 * d + 1, :] for g in range(ng)]
        bs = [lam_ref[g, 2 * d + 1:2 * d + 2, :] for g in range(ng)]
        for g in range(ng):
            inj_s[g] = pltpu.roll(inj[g], S5_P, 1)
        x = [x0_ref[d, g] if has_state else jnp.zeros((n_seq, 2 * S5_P), F32) for g in range(ng)]
        xs = [pltpu.roll(v, S5_P, 1) for v in x]
        for k in (range(n_k) if d == 0 else range(n_k - 1, -1, -1)):
            rows = pl.ds(k, n_seq, stride=n_k)
            for g in range(ng):
                xin[g, rows, :] = x[g]
                x[g], xs[g] = (a[g] * x[g] + bs[g] * xs[g] + inj[g, rows, :],
                               a[g] * xs[g] - bs[g] * x[g] + inj_s[g, rows, :])
        if not has_state:
            for g in range(ng):
                (xf_ref if d == 0 else xb_ref)[g] = x[g]

    for g in range(ng):
        yg_ref[g] = (_dot(ug_ref[g], t_ref[g]) + _dot(xin_f[g].astype(BF16), ccf_ref[g])
                     + _dot(xin_b[g].astype(BF16), ccb_ref[g]))
    for half in range(S5_L // ng):
        y_i = _block_transpose([yg_ref[g, :, LANES * half:LANES * (half + 1)] for g in range(ng)])
        for i in range(ng):
            y_ref[tok(ng * half + i), :] = y_i[i]


def _s5(u_halves, ops, l, n_seq, n_k, x0):
    has_state = x0 is not None
    n = S5_L * S5_GC
    r = n_seq * n_k
    w = 2 * S5_P
    ng = S5_G // 2
    half_spec = pl.BlockSpec((None, r * S5_L, LANES), lambda j: (j, 0, 0))
    op_spec = lambda a: pl.BlockSpec((None, ng) + a.shape[2:], lambda j: (l, j) + (0,) * (a.ndim - 2))
    args = [u_halves] + list(ops)
    in_specs = [half_spec] + [op_spec(a) for a in ops]
    if has_state:
        args.append(x0)
        in_specs.append(pl.BlockSpec((None, 2, ng, n_seq, w), lambda j: (l, 0, j, 0, 0)))
    out_shape = [jax.ShapeDtypeStruct((2, r * S5_L, LANES), F32)]
    out_specs = [half_spec]
    if not has_state:
        out_shape += [jax.ShapeDtypeStruct((S5_G, n_seq, w), F32)] * 2
        out_specs += [pl.BlockSpec((ng, n_seq, w), lambda j: (j, 0, 0))] * 2
    return pl.pallas_call(
        functools.partial(_s5_kernel, n_seq=n_seq, n_k=n_k, has_state=has_state),
        grid=(2,), in_specs=in_specs, out_specs=out_specs, out_shape=out_shape,
        scratch_shapes=[pltpu.VMEM((ng, r, n), BF16), pltpu.VMEM((ng, r, n), F32)]
        + [pltpu.VMEM((ng, r, w), F32)] * 5,
        compiler_params=_params(("arbitrary",)), name="s5_scan",
    )(*args)


def _out_ffn_kernel(x_ref, m_ref, r_ref, a_ref, ya_ref, yb_ref, mod_ref, wglu_ref, bglu_ref, wout_ref,
                    n_post_ref, n_pre_ref, n_fpost_ref, wg_ref, wu_ref, wd_ref, o_ref):
    def mod(i):
        return mod_ref[:, i * D_MODEL:(i + 1) * D_MODEL]

    y = jnp.concatenate([ya_ref[...], yb_ref[...]], axis=1)
    gs = 0.5 * y * (1.0 + jnp.tanh(math.sqrt(2.0 / math.pi) * (y + 0.044715 * (y * y * y))))
    s_out = gs * jax.nn.sigmoid(_dot(gs.astype(BF16), wglu_ref[...]) + bglu_ref[...])
    mixed = jnp.concatenate([m_ref[...], r_ref[...], a_ref[...], s_out], axis=1).astype(BF16)
    x1 = x_ref[...] + mod(2) * _rms(_dot(mixed, wout_ref[...]), n_post_ref[...])
    h = (_rms(x1, n_pre_ref[...]) * (1.0 + mod(4)) + mod(3)).astype(BF16)
    act = (_silu(_dot(h, wg_ref[...])) * _dot(h, wu_ref[...])).astype(BF16)
    o_ref[...] = x1 + mod(5) * _rms(_dot(act, wd_ref[...]), n_fpost_ref[...])


def _out_ffn(x, m_out, r_out, a_out, y_halves, mod, l, mod_row, wp):
    n = x.shape[0]
    row = lambda w: pl.BlockSpec((ROW_TILE, w), lambda i: (i, 0))
    half = lambda j: pl.BlockSpec((None, ROW_TILE, LANES), lambda i: (j, i, 0))
    params = [wp[k] for k in ('w_glu', 'b_glu', 'w_out', 'n_mix_post', 'n_ffn_pre', 'n_ffn_post',
                              'w_gate', 'w_up', 'w_down')]
    return pl.pallas_call(
        _out_ffn_kernel,
        grid=(n // ROW_TILE,),
        in_specs=[row(D_MODEL), row(GROUP_W), row(GROUP_W), row(GROUP_W), half(0), half(1),
                  _mod_spec(l, mod_row)] + [_layer_spec(a, l, pipeline_mode=pl.Buffered(1)) for a in params],
        out_specs=row(D_MODEL),
        out_shape=jax.ShapeDtypeStruct((n, D_MODEL), F32),
        compiler_params=_params(("arbitrary",)), name="out_ffn",
    )(x, m_out, r_out, a_out, y_halves, y_halves, mod, *params)


def _layer(x, l, n_seq, t, mod, mod_row, wp, s5_ops, lam_init, st):
    ctx = st is None
    res = _in_proj(x, mod, l, mod_row, wp['n_mix_pre'], wp['w_in'], emit_kv=ctx)
    z, gates = res[0], res[2]
    m_res, r_res, a_out = _mixers(z, gates, wp, l, n_seq, t, None if ctx else st[0:5], lam_init)
    s_res = _s5(res[1], s5_ops, l, n_seq, t // S5_L, None if ctx else st[5])
    x_new = _out_ffn(x, m_res[0], r_res[0], a_out, s_res[0], mod, l, mod_row, wp)
    if not ctx:
        return x_new, None
    return x_new, (m_res[1], m_res[2], m_res[3], r_res[1], res[3], res[4], s_res[1], s_res[2])


def _pad_to(a, rows, cols=LANES):
    pad = [(0, 0)] * (a.ndim - 2) + [(0, rows - a.shape[-2]), (0, cols - a.shape[-1])]
    return jnp.pad(a, pad)


def kernel(x_prompt, x_sample, state_mlstm_C, state_mlstm_n, state_mlstm_m, state_ret, cache_diff_k, cache_diff_v, state_s5_re, state_s5_im, c, c_ctx, w_ada, b_ada, n_mix_pre, n_mix_post, n_ffn_pre, n_ffn_post, w_in, w_out, m_gate_bias, m_norm, r_decay_logit, r_norm, a_lam_q1, a_lam_k1, a_lam_q2, a_lam_k2, a_norm, s5_lam_re, s5_lam_im, s5_log_dt, s5_b_re, s5_b_im, s5_c_re, s5_c_im, s5_d, s5_w_glu, s5_b_glu, w_ffn_gate, w_ffn_up, w_ffn_down):
    n_ctx, t_ctx, _ = x_prompt.shape
    n_lat, t_lat, _ = x_sample.shape
    past = cache_diff_k.shape[2]

    cond8 = jnp.concatenate([c, c_ctx[None, :], jnp.zeros((8 - n_lat - 1, D_MODEL), F32)], axis=0)
    mod = _ada(cond8, w_ada, b_ada).reshape(DEPTH, 8, 1, 6 * D_MODEL)
    s5_ops = _s5_prep(s5_lam_re, s5_lam_im, s5_log_dt, s5_b_re, s5_b_im, s5_c_re, s5_c_im, s5_d)

    row = lambda a: a[:, None, :]
    wp = dict(
        w_in=w_in, w_out=w_out.astype(BF16), w_glu=s5_w_glu.astype(BF16), b_glu=row(s5_b_glu),
        w_gate=w_ffn_gate.astype(BF16), w_up=w_ffn_up.astype(BF16), w_down=w_ffn_down.astype(BF16),
        n_mix_pre=row(n_mix_pre), n_mix_post=row(n_mix_post), n_ffn_pre=row(n_ffn_pre),
        n_ffn_post=row(n_ffn_post), m_norm=row(m_norm), r_norm=row(r_norm), a_norm=row(a_norm),
        m_bias=_pad_to(row(m_gate_bias), 1), r_decay=_pad_to(r_decay_logit, 8),
        a_lam=_pad_to(jnp.stack([a_lam_q1, a_lam_k1, a_lam_q2, a_lam_k2], axis=1), 8))
    lam_inits = [0.8 - 0.6 * math.exp(-0.3 * l) for l in range(DEPTH)]

    x = x_prompt.reshape(n_ctx * t_ctx, D_MODEL)
    new_states = []
    for l in range(DEPTH):
        x, st = _layer(x, l, n_ctx, t_ctx, mod, lambda i: n_lat, wp, s5_ops, lam_inits[l], None)
        new_states.append(st)
    y_prompt = x.reshape(n_ctx, t_ctx, D_MODEL)

    x0 = jnp.stack([state_s5_re, state_s5_im], axis=-2)
    x0 = x0.transpose(1, 2, 3, 0, 4, 5).reshape(DEPTH, 2, S5_G, n_lat, 2 * S5_P)
    s_ext0 = jnp.concatenate([state_mlstm_C, state_mlstm_n[..., None],
                              jnp.zeros(state_mlstm_n.shape + (LANES - HEAD_DIM - 1,), F32)], axis=-1)
    st = (s_ext0, state_mlstm_m.reshape(n_lat, DEPTH, 2 * N_HEADS, 1), state_ret,
          cache_diff_k.reshape(n_lat, DEPTH, past, GROUP_W), cache_diff_v.reshape(n_lat, DEPTH, past, GROUP_W), x0)
    x = x_sample.reshape(n_lat * t_lat, D_MODEL)
    tiles_per_seq = t_lat // ROW_TILE
    for l in range(DEPTH):
        x, _ = _layer(x, l, n_lat, t_lat, mod, lambda i: i // tiles_per_seq, wp, s5_ops, lam_inits[l], st)
    y_sample = x.reshape(n_lat, t_lat, D_MODEL)

    stack = lambda i: jnp.stack([s[i] for s in new_states], axis=1)
    kv = lambda i: jnp.stack([s[i].reshape(n_ctx, t_ctx, N_HEADS, HEAD_DIM) for s in new_states], axis=1)
    xs = jnp.stack([stack(6), stack(7)], axis=2)
    xs = xs.reshape(S5_G, DEPTH, 2, n_ctx, 2, S5_P).transpose(3, 1, 2, 0, 4, 5)
    return (y_prompt, y_sample, stack(0), stack(1)[..., 0], stack(2).reshape(n_ctx, DEPTH, 2, N_HEADS),
            stack(3), kv(4), kv(5), xs[..., 0, :], xs[..., 1, :])
```

```python
import functools
import math

import jax
import jax.numpy as jnp
from jax import lax
from jax.experimental import pallas as pl
from jax.experimental.pallas import tpu as pltpu

F32 = jnp.float32
BF16 = jnp.bfloat16
HI = lax.Precision.HIGHEST

D_MODEL = 1024
DEPTH = 2
GRID_W = 64
HEAD_DIM = 64
GROUP_W = 256
N_HEADS = 4
A_SUB = 32
S5_GC = 16
S5_G = 16
S5_P = 64
D_FF = 2816
ROPE_BASE = 10000.0
EPS = 1e-6
N_MAIN = 12 * GROUP_W
N_MIX = 11 * GROUP_W
LANES = 128
CHUNK = 256
ROW_TILE = 512
S5_L = 16
NEG = -1e30
VMEM_LIMIT = 56 * 1024 * 1024


def _dot(a, b, precision=None):
    return jnp.dot(a, b, preferred_element_type=F32, precision=precision)


def _dot_nt(a, b):
    return lax.dot_general(a, b, (((1,), (1,)), ((), ())), preferred_element_type=F32)


def _log_sigmoid(x):
    return jnp.minimum(x, 0.0) - jnp.log(1.0 + jnp.exp(-jnp.abs(x)))


def _silu(x):
    return x * jax.nn.sigmoid(x)


def _rms(x, g):
    return x * lax.rsqrt(jnp.mean(x * x, axis=-1, keepdims=True) + EPS) * g


def _params(sem=None):
    return pltpu.CompilerParams(dimension_semantics=sem, vmem_limit_bytes=VMEM_LIMIT)


def _layer_spec(a, l, **kw):
    n = a.ndim - 1
    return pl.BlockSpec((None,) + a.shape[1:], lambda *_: (l,) + (0,) * n, **kw)


def _seq_layer_spec(a, l):
    n = a.ndim - 2
    return pl.BlockSpec((None, None) + a.shape[2:], lambda b: (b, l) + (0,) * n)


def _split_bf16(x):
    hi = x.astype(BF16)
    return hi, (x - hi.astype(F32)).astype(BF16)


def _ada_kernel(c_ref, w_ref, b_ref, o_ref):
    a_hi, a_lo = _split_bf16(_silu(c_ref[...]))
    w_hi, w_lo = _split_bf16(w_ref[...])
    o_ref[...] = _dot(a_hi, w_hi) + (_dot(a_lo, w_hi) + _dot(a_hi, w_lo)) + b_ref[...]


def _ada(cond8, w_ada, b_ada):
    tn = 1536
    return pl.pallas_call(
        _ada_kernel,
        grid=(DEPTH, 6 * D_MODEL // tn),
        in_specs=[pl.BlockSpec((8, D_MODEL), lambda l, j: (0, 0)),
                  pl.BlockSpec((None, D_MODEL, tn), lambda l, j: (l, 0, j)),
                  pl.BlockSpec((None, 1, tn), lambda l, j: (l, 0, j))],
        out_specs=pl.BlockSpec((None, 8, tn), lambda l, j: (l, 0, j)),
        out_shape=jax.ShapeDtypeStruct((DEPTH, 8, 6 * D_MODEL), F32),
        compiler_params=_params(("arbitrary", "arbitrary")),
        name="ada",
    )(cond8, w_ada, b_ada.reshape(DEPTH, 1, 6 * D_MODEL))


def _in_proj_kernel(x_ref, mod_ref, g_ref, w_ref, z_ref, u_ref, gate_ref, *kv_refs):
    h = _rms(x_ref[...], g_ref[...]) * (1.0 + mod_ref[:, D_MODEL:2 * D_MODEL]) + mod_ref[:, 0:D_MODEL]
    hb = h.astype(BF16)
    n_gate = 4 * N_HEADS
    n_head = 4 * GROUP_W
    z_ref[:, 0:n_head] = _dot(hb, w_ref[:, 0:n_head]).astype(BF16)
    tail = _dot(hb, w_ref[:, n_head:])
    gate_ref[...] = tail[:, 0:LANES]
    rest = tail[:, n_gate:n_gate + N_MAIN - n_head]
    z_ref[:, n_head:] = rest[:, 0:N_MIX - n_head].astype(BF16)
    u_ref[0] = rest[:, N_MIX - n_head:N_MIX - n_head + LANES]
    u_ref[1] = rest[:, N_MIX - n_head + LANES:]
    if kv_refs:
        kv_refs[0][...] = rest[:, 9 * GROUP_W - n_head:10 * GROUP_W - n_head]
        kv_refs[1][...] = rest[:, 10 * GROUP_W - n_head:11 * GROUP_W - n_head]


def _mod_spec(l, mod_row):
    return pl.BlockSpec((None, None, 1, 6 * D_MODEL), lambda i: (l, mod_row(i), 0, 0))


def _in_proj(x, mod, l, mod_row, gain, w, emit_kv):
    n = x.shape[0]
    row = lambda width: pl.BlockSpec((ROW_TILE, width), lambda i: (i, 0))
    out_specs = [row(N_MIX), pl.BlockSpec((2, ROW_TILE, LANES), lambda i: (0, i, 0)), row(LANES)]
    out_specs += [row(GROUP_W)] * (2 if emit_kv else 0)
    out_shape = [jax.ShapeDtypeStruct((n, N_MIX), BF16), jax.ShapeDtypeStruct((2, n, LANES), F32),
                 jax.ShapeDtypeStruct((n, LANES), F32)]
    out_shape += [jax.ShapeDtypeStruct((n, GROUP_W), F32)] * (2 if emit_kv else 0)
    return pl.pallas_call(
        _in_proj_kernel,
        grid=(n // ROW_TILE,),
        in_specs=[row(D_MODEL), _mod_spec(l, mod_row), _layer_spec(gain, l),
                  _layer_spec(w, l, pipeline_mode=pl.Buffered(1))],
        out_specs=out_specs, out_shape=out_shape,
        compiler_params=_params(("arbitrary",)),
        name="in_proj",
    )(x, mod, gain, w)


def _tri_masks(n):
    row = lax.broadcasted_iota(jnp.int32, (n, n), 0)
    col = lax.broadcasted_iota(jnp.int32, (n, n), 1)
    return row >= col, row <= col


def _head_mean_matrix():
    r = lax.broadcasted_iota(jnp.int32, (GROUP_W, GROUP_W), 0) // HEAD_DIM
    c = lax.broadcasted_iota(jnp.int32, (GROUP_W, GROUP_W), 1) // HEAD_DIM
    return jnp.where(r == c, 1.0 / HEAD_DIM, 0.0).astype(BF16)


def _head_mean(x, j):
    hi, lo = _split_bf16(x)
    return _dot(hi, j) + _dot(lo, j)


def _head_groupnorm(x, g, j):
    xc = x - _head_mean(x, j)
    return xc * lax.rsqrt(_head_mean(xc * xc, j) + EPS) * g


def _head_rms(x, g, j):
    return x * lax.rsqrt(_head_mean(x * x, j) + EPS) * g


def _hs(h):
    return slice(HEAD_DIM * h, HEAD_DIM * (h + 1))


def _v_ext(v_ref, rows, hs):
    ones = jnp.ones((rows.stop - rows.start, HEAD_DIM), BF16)
    return jnp.concatenate([v_ref[rows, hs].astype(BF16), ones], axis=1)


def _zspec(t, colblk):
    return pl.BlockSpec((t, GROUP_W), lambda b: (b, colblk))


def _scan_max(x, reverse):
    n = x.shape[1]
    lane = lax.broadcasted_iota(jnp.int32, x.shape, 1)
    sh = 1
    while sh < n:
        if reverse:
            x = jnp.maximum(x, jnp.where(lane < n - sh, pltpu.roll(x, n - sh, 1), NEG))
        else:
            x = jnp.maximum(x, jnp.where(lane >= sh, pltpu.roll(x, sh, 1), NEG))
        sh *= 2
    return x


def _ends(x, is_fwd):
    return jnp.where(is_fwd, x[:, x.shape[1] - 1:], x[:, 0:1])


def _mlstm_kernel(*refs, n_chunk, has_state):
    if has_state:
        q_ref, k_ref, v_ref, o_ref, g_ref, bias_ref, gain_ref, s0_ref, m0_ref, out_ref = refs
    else:
        q_ref, k_ref, v_ref, o_ref, g_ref, bias_ref, gain_ref, out_ref, cn_ref, nn_ref, mn_ref = refs
    L = CHUNK
    nu = 2 * N_HEADS
    tril, triu = _tri_masks(L)
    is_fwd = lax.broadcasted_iota(jnp.int32, (nu, 1), 0) < N_HEADS

    b8, w8, mcum8 = [], [], []
    for c in range(n_chunk):
        p_t = (g_ref[c * L:(c + 1) * L, :] + bias_ref[...]).T
        lf = _log_sigmoid(p_t[nu:2 * nu, :])
        b = jnp.where(is_fwd, _dot(lf, triu.astype(F32), HI), _dot(lf, tril.astype(F32), HI))
        w = p_t[0:nu, :] - b
        b8.append(b)
        w8.append(w)
        mcum8.append(jnp.where(is_fwd, _scan_max(w, False), _scan_max(w, True)))

    m0 = m0_ref[...] if has_state else jnp.zeros((nu, 1), F32)
    m_in_f, m_in_b = [None] * n_chunk, [None] * n_chunk
    m = m0
    for c in range(n_chunk):
        m_in_f[c] = m
        m = (b8[c] + jnp.maximum(mcum8[c], m))[:, L - 1:]
    m_fin_f = m
    m = m0
    for c in range(n_chunk - 1, -1, -1):
        m_in_b[c] = m
        m = (b8[c] + jnp.maximum(mcum8[c], m))[:, 0:1]
    if not has_state:
        mn_ref[...] = jnp.where(is_fwd, m_fin_f, m)

    wk8, dec8, cols = [], [], []
    for c in range(n_chunk):
        m_in = jnp.where(is_fwd, m_in_f[c], m_in_b[c])
        g = jnp.maximum(mcum8[c], m_in)
        m_row = b8[c] + g
        m_new, b_last = _ends(m_row, is_fwd), _ends(b8[c], is_fwd)
        wk8.append(jnp.exp(b_last + w8[c] - m_new))
        dec8.append(jnp.exp(b_last + m_in - m_new))
        stats = jnp.concatenate([g, jnp.exp(m_in - g), jnp.exp(-m_row), jnp.zeros((LANES - 3 * nu, L), F32)], axis=0)
        cols.append(stats.T)

    k_t = [(k_ref[c * L:(c + 1) * L, :].astype(F32) * (HEAD_DIM ** -0.5)).T for c in range(n_chunk)]

    h_all = []
    for h in range(N_HEADS):
        hs = _hs(h)
        qb = [q_ref[c * L:(c + 1) * L, hs].astype(BF16) for c in range(n_chunk)]
        kt = [k_t[c][hs, :] for c in range(n_chunk)]
        vx = [_v_ext(v_ref, slice(c * L, (c + 1) * L), hs) for c in range(n_chunk)]
        qk = [_dot(qb[c], kt[c].astype(BF16)) for c in range(n_chunk)]
        h_sum = [None] * n_chunk
        for d in range(2):
            tri = tril if d == 0 else triu
            j = N_HEADS * d + h
            s_ext = s0_ref[d, h] if has_state else None
            order = range(n_chunk) if d == 0 else range(n_chunk - 1, -1, -1)
            for ci, c in enumerate(order):
                wgt = jnp.exp(jnp.where(tri, w8[c][j:j + 1, :] - cols[c][:, j:j + 1], NEG))
                tot = _dot((qk[c] * wgt).astype(BF16), vx[c])
                if s_ext is not None:
                    tot = tot + cols[c][:, nu + j:nu + j + 1] * _dot(qb[c], s_ext.astype(BF16))
                den = jnp.maximum(jnp.abs(tot[:, HEAD_DIM:HEAD_DIM + 1]), cols[c][:, 2 * nu + j:2 * nu + j + 1])
                hd = tot[:, 0:HEAD_DIM] / den
                h_sum[c] = hd if h_sum[c] is None else h_sum[c] + hd
                if (not has_state) or ci < n_chunk - 1:
                    upd = _dot((kt[c] * wk8[c][j:j + 1, :]).astype(BF16), vx[c])
                    s_ext = upd if s_ext is None else upd + dec8[c][j:j + 1, :] * s_ext
            if not has_state:
                cn_ref[d, h] = s_ext[:, 0:HEAD_DIM]
                nn_ref[d, h] = s_ext[:, HEAD_DIM:HEAD_DIM + 1]
        h_all.append(h_sum)
    jm = _head_mean_matrix()
    for c in range(n_chunk):
        rows = slice(c * L, (c + 1) * L)
        gated = jax.nn.sigmoid(o_ref[rows, :].astype(F32)) * jnp.concatenate([h_all[h][c] for h in range(N_HEADS)], axis=1)
        out_ref[rows, :] = _head_groupnorm(gated, gain_ref[...], jm)


def _ret_kernel(*refs, n_chunk, has_state):
    if has_state:
        q_ref, k_ref, v_ref, g_ref, lg_ref, gain_ref, s0_ref, out_ref, dmat_ref, col_ref, row_ref = refs
    else:
        q_ref, k_ref, v_ref, g_ref, lg_ref, gain_ref, out_ref, sn_ref, dmat_ref, col_ref, row_ref = refs
    L = CHUNK

    @pl.when(pl.program_id(0) == 0)
    def _():
        row = lax.broadcasted_iota(jnp.int32, (L, L), 0)
        col = lax.broadcasted_iota(jnp.int32, (L, L), 1)
        diff = (row - col).astype(F32)
        log_g = _log_sigmoid(lg_ref[...])
        pos_c = lax.broadcasted_iota(jnp.int32, (L, 1), 0).astype(F32)
        pos_r = lax.broadcasted_iota(jnp.int32, (1, L), 1).astype(F32)
        for h in range(N_HEADS):
            lgf = log_g[0:1, h:h + 1]
            lgb = log_g[1:2, h:h + 1]
            dmat_ref[h] = (jnp.where(diff >= 0, jnp.exp(jnp.maximum(diff, 0.0) * lgf), 0.0)
                           + jnp.where(diff <= 0, jnp.exp(jnp.maximum(-diff, 0.0) * lgb), 0.0))
            col_ref[:, h:h + 1] = jnp.exp((pos_c + 1.0) * lgf)
            col_ref[:, N_HEADS + h:N_HEADS + h + 1] = jnp.exp((L - pos_c) * lgb)
            row_ref[h:h + 1, :] = jnp.exp((L - 1.0 - pos_r) * lgf)
            row_ref[N_HEADS + h:N_HEADS + h + 1, :] = jnp.exp(pos_r * lgb)
            row_ref[2 * N_HEADS + h:2 * N_HEADS + h + 1, :] = jnp.exp(L * lgf) + jnp.zeros((1, L), F32)
            row_ref[3 * N_HEADS + h:3 * N_HEADS + h + 1, :] = jnp.exp(L * lgb) + jnp.zeros((1, L), F32)

    k_t = [(k_ref[c * L:(c + 1) * L, :].astype(F32) * (HEAD_DIM ** -0.5)).T for c in range(n_chunk)]
    o_all = []
    for h in range(N_HEADS):
        hs = _hs(h)
        zeta_f = row_ref[h:h + 1, :]
        zeta_b = row_ref[N_HEADS + h:N_HEADS + h + 1, :]
        gl_f = row_ref[2 * N_HEADS + h:2 * N_HEADS + h + 1, 0:1]
        gl_b = row_ref[3 * N_HEADS + h:3 * N_HEADS + h + 1, 0:1]
        kt = [k_t[c][hs, :] for c in range(n_chunk)]
        vb = [v_ref[c * L:(c + 1) * L, hs].astype(BF16) for c in range(n_chunk)]
        sf = [None] * (n_chunk + 1)
        sb = [None] * (n_chunk + 1)
        if has_state:
            sf[0] = s0_ref[0, h]
            sb[n_chunk] = s0_ref[1, h]
        for c in range(n_chunk):
            if (not has_state) or c < n_chunk - 1:
                upd = _dot((kt[c] * zeta_f).astype(BF16), vb[c])
                sf[c + 1] = upd if sf[c] is None else upd + gl_f * sf[c]
        for c in range(n_chunk - 1, -1, -1):
            if (not has_state) or c > 0:
                upd = _dot((kt[c] * zeta_b).astype(BF16), vb[c])
                sb[c] = upd if sb[c + 1] is None else upd + gl_b * sb[c + 1]
        if not has_state:
            sn_ref[0, h] = sf[n_chunk]
            sn_ref[1, h] = sb[0]
        o_h = []
        for c in range(n_chunk):
            qb = q_ref[c * L:(c + 1) * L, hs].astype(BF16)
            att = _dot(qb, kt[c].astype(BF16)) * dmat_ref[h]
            o = _dot(att.astype(BF16), vb[c])
            if has_state:
                o = o + col_ref[:, h:h + 1] * _dot(qb, sf[c].astype(BF16))
                o = o + col_ref[:, N_HEADS + h:N_HEADS + h + 1] * _dot(qb, sb[c + 1].astype(BF16))
            o_h.append(o)
        o_all.append(o_h)
    jm = _head_mean_matrix()
    for c in range(n_chunk):
        rows = slice(c * L, (c + 1) * L)
        o = jnp.concatenate([o_all[h][c] for h in range(N_HEADS)], axis=1)
        out_ref[rows, :] = _silu(g_ref[rows, :].astype(F32)) * _head_groupnorm(o, gain_ref[...], jm)


def _lambda(lam_ref, lam_init):
    v = lam_ref[...]
    s1 = jnp.sum(v[0:1, :] * v[1:2, :], axis=-1, keepdims=True)
    s2 = jnp.sum(v[2:3, :] * v[3:4, :], axis=-1, keepdims=True)
    return jnp.exp(s1) - jnp.exp(s2) + lam_init


def _first_half(shape):
    lane = lax.broadcasted_iota(jnp.int32, shape, 1)
    return (lane % (A_SUB // 2)) < (A_SUB // 4)


def _rope(x, cos, sin_signed):
    n = x.shape[1]
    first_half = _first_half(x.shape)
    partner = jnp.where(first_half, pltpu.roll(x, n - A_SUB // 4, 1), pltpu.roll(x, A_SUB // 4, 1))
    return x * cos + partner * sin_signed


def _attn_kernel(*refs, t, has_cache, lam_init):
    if has_cache:
        q_ref, k_ref, v_ref, kc_ref, vc_ref, lam_ref, gain_ref, out_ref, qr_ref, kr_ref, cos_ref, sin_ref = refs
    else:
        q_ref, k_ref, v_ref, lam_ref, gain_ref, out_ref, qr_ref, kr_ref = refs
    lam = _lambda(lam_ref, lam_init)
    scale = A_SUB ** -0.5
    n_qb = t // CHUNK

    if has_cache:
        @pl.when(pl.program_id(0) == 0)
        def _():
            lane = lax.broadcasted_iota(jnp.int32, (GRID_W, GROUP_W), 1)
            pos = lax.broadcasted_iota(jnp.int32, (GRID_W, GROUP_W), 0).astype(F32)
            n_freq = A_SUB // 4
            ang = pos * jnp.exp((lane % n_freq).astype(F32) * (-math.log(ROPE_BASE) / n_freq))
            cos_t = jnp.cos(ang)
            sin_t = jnp.where(_first_half((GRID_W, GROUP_W)), -jnp.sin(ang), jnp.sin(ang))
            row_axis = (lane % A_SUB) < (A_SUB // 2)
            for r in range(t // GRID_W):
                rows = slice(r * GRID_W, (r + 1) * GRID_W)
                cos_ref[rows, :] = jnp.where(row_axis, cos_t[r:r + 1, :], cos_t)
                sin_ref[rows, :] = jnp.where(row_axis, sin_t[r:r + 1, :], sin_t)

        kr_ref[...] = _rope(k_ref[...].astype(F32), cos_ref[...], sin_ref[...]).astype(BF16)
        qr_ref[...] = (_rope(q_ref[...].astype(F32), cos_ref[...], sin_ref[...]) * scale).astype(BF16)
    else:
        kr_ref[...] = k_ref[...].astype(BF16)
        qr_ref[...] = (q_ref[...].astype(F32) * scale).astype(BF16)

    o_all = []
    for h in range(N_HEADS):
        hs = _hs(h)
        vx = _v_ext(v_ref, slice(0, t), hs)
        if has_cache:
            vcx = _v_ext(vc_ref, slice(0, vc_ref.shape[0]), hs)
        o_h = []
        for qi in range(n_qb):
            rows = slice(qi * CHUNK, (qi + 1) * CHUNK)
            o_maps = []
            for m in range(2):
                ms = slice(HEAD_DIM * h + A_SUB * m, HEAD_DIM * h + A_SUB * (m + 1))
                qm = qr_ref[rows, ms]
                s_l = _dot_nt(qm, kr_ref[:, ms])
                mx = jnp.max(s_l, axis=-1, keepdims=True)
                if has_cache:
                    s_c = _dot_nt(qm, kc_ref[:, ms].astype(BF16))
                    mx = jnp.maximum(mx, jnp.max(s_c, axis=-1, keepdims=True))
                tot = _dot(jnp.exp(s_l - mx).astype(BF16), vx)
                if has_cache:
                    tot = tot + _dot(jnp.exp(s_c - mx).astype(BF16), vcx)
                o_maps.append(tot[:, 0:HEAD_DIM] / tot[:, HEAD_DIM:HEAD_DIM + 1])
            o_h.append(o_maps[0] - lam * o_maps[1])
        o_all.append(o_h)
    jm = _head_mean_matrix()
    for qi in range(n_qb):
        rows = slice(qi * CHUNK, (qi + 1) * CHUNK)
        o = jnp.concatenate([o_all[h][qi] for h in range(N_HEADS)], axis=1)
        out_ref[rows, :] = _head_rms(o, gain_ref[...], jm) * (1.0 - lam_init)


def _mixers_kernel(*refs, n_chunk, t, has_state, lam_init, n_in, n_out):
    ins, outs, scratch = refs[:sum(n_in)], refs[sum(n_in):sum(n_in) + sum(n_out)], refs[sum(n_in) + sum(n_out):]
    i0, o0 = 0, 0
    parts = []
    for k in range(3):
        parts.append((ins[i0:i0 + n_in[k]], outs[o0:o0 + n_out[k]]))
        i0 += n_in[k]
        o0 += n_out[k]
    _mlstm_kernel(*parts[0][0], *parts[0][1], n_chunk=n_chunk, has_state=has_state)
    _ret_kernel(*parts[1][0], *parts[1][1], *scratch[0:3], n_chunk=n_chunk, has_state=has_state)
    _attn_kernel(*parts[2][0], *parts[2][1], *scratch[3:], t=t, has_cache=has_state, lam_init=lam_init)


def _mixers(z, gates, wp, l, n_seq, t, st, lam_init, fuse):
    has_state = st is not None
    nu = 2 * N_HEADS
    seq = lambda width: pl.BlockSpec((t, width), lambda b: (b, 0))
    per_seq = lambda *shape: pl.BlockSpec((None,) + shape, lambda b: (b,) + (0,) * len(shape))
    out_rows = jax.ShapeDtypeStruct((n_seq * t, GROUP_W), F32)
    head_state = (2, N_HEADS, HEAD_DIM, HEAD_DIM)

    m_in = [z, z, z, z, gates, wp['m_bias'], wp['m_norm']]
    m_specs = [_zspec(t, 0), _zspec(t, 1), _zspec(t, 2), _zspec(t, 3), seq(LANES),
               _layer_spec(wp['m_bias'], l), _layer_spec(wp['m_norm'], l)]
    r_in = [z, z, z, z, wp['r_decay'], wp['r_norm']]
    r_specs = [_zspec(t, 4), _zspec(t, 5), _zspec(t, 6), _zspec(t, 7),
               _layer_spec(wp['r_decay'], l), _layer_spec(wp['r_norm'], l)]
    a_in = [z, z, z]
    a_specs = [_zspec(t, 8), _zspec(t, 9), _zspec(t, 10)]
    m_out, r_out, a_out = [out_rows], [out_rows], [out_rows]
    m_ospecs, r_ospecs, a_ospecs = [seq(GROUP_W)], [seq(GROUP_W)], [seq(GROUP_W)]
    scratch = [pltpu.VMEM((N_HEADS, CHUNK, CHUNK), F32), pltpu.VMEM((CHUNK, LANES), F32),
               pltpu.VMEM((4 * N_HEADS, CHUNK), F32),
               pltpu.VMEM((t, GROUP_W), BF16), pltpu.VMEM((t, GROUP_W), BF16)]
    if has_state:
        m_in += list(st[0:2])
        m_specs += [_seq_layer_spec(a, l) for a in st[0:2]]
        r_in.append(st[2])
        r_specs.append(_seq_layer_spec(st[2], l))
        a_in += list(st[3:5])
        a_specs += [_seq_layer_spec(a, l) for a in st[3:5]]
        scratch += [pltpu.VMEM((t, GROUP_W), F32), pltpu.VMEM((t, GROUP_W), F32)]
    else:
        m_out += [jax.ShapeDtypeStruct((n_seq,) + head_state, F32),
                  jax.ShapeDtypeStruct((n_seq, 2, N_HEADS, HEAD_DIM, 1), F32),
                  jax.ShapeDtypeStruct((n_seq, nu, 1), F32)]
        m_ospecs += [per_seq(*head_state), per_seq(2, N_HEADS, HEAD_DIM, 1), per_seq(nu, 1)]
        r_out.append(jax.ShapeDtypeStruct((n_seq,) + head_state, F32))
        r_ospecs.append(per_seq(*head_state))
    a_in += [wp['a_lam'], wp['a_norm']]
    a_specs += [_layer_spec(wp['a_lam'], l), _layer_spec(wp['a_norm'], l)]
    n_chunk = t // CHUNK
    call = functools.partial(pl.pallas_call, grid=(n_seq,), compiler_params=_params(("arbitrary",)))
    if fuse:
        res = call(
            functools.partial(_mixers_kernel, n_chunk=n_chunk, t=t, has_state=has_state, lam_init=lam_init,
                              n_in=(len(m_in), len(r_in), len(a_in)), n_out=(len(m_out), len(r_out), len(a_out))),
            in_specs=m_specs + r_specs + a_specs, out_specs=m_ospecs + r_ospecs + a_ospecs,
            out_shape=m_out + r_out + a_out, scratch_shapes=scratch, name="mixers",
        )(*m_in, *r_in, *a_in)
        return res[:len(m_out)], res[len(m_out):len(m_out) + len(r_out)], res[-1]
    m_res = call(functools.partial(_mlstm_kernel, n_chunk=n_chunk, has_state=has_state),
                 in_specs=m_specs, out_specs=m_ospecs, out_shape=m_out, name="mlstm")(*m_in)
    r_res = call(functools.partial(_ret_kernel, n_chunk=n_chunk, has_state=has_state),
                 in_specs=r_specs, out_specs=r_ospecs, out_shape=r_out, scratch_shapes=scratch[0:3],
                 name="retention")(*r_in)
    a_res = call(functools.partial(_attn_kernel, t=t, has_cache=has_state, lam_init=lam_init),
                 in_specs=a_specs, out_specs=a_ospecs, out_shape=a_out, scratch_shapes=scratch[3:],
                 name="diff_attention")(*a_in)
    return m_res, r_res, a_res[0]


def _cmul(ar, ai, br, bi):
    return ar * br - ai * bi, ar * bi + ai * br


def _lam_bar(lre, lim, dt):
    mag = jnp.exp(lre * dt)
    return mag * jnp.cos(lim * dt), mag * jnp.sin(lim * dt)


def _cpow_int(br, bi, e, n_bits):
    pr = jnp.ones(e.shape, F32)
    pi = jnp.zeros(e.shape, F32)
    for bit in range(n_bits):
        on = ((e >> bit) & 1) == 1
        qr, qi = _cmul(pr, pi, br, bi)
        pr, pi = jnp.where(on, qr, pr), jnp.where(on, qi, pi)
        if bit + 1 < n_bits:
            br, bi = _cmul(br, bi, br, bi)
    return pr, pi


def _zoh_coef(lre, lim, br, bi):
    den = lre * lre + lim * lim
    return ((br - 1.0) * lre + bi * lim) / den, (bi * lre - (br - 1.0) * lim) / den


def _toeplitz_rows(kall):
    n = S5_L * S5_GC
    lane = lax.broadcasted_iota(jnp.int32, (S5_GC, n), 1)
    pieces = []
    for s in range(S5_L):
        shifted = kall if s == 0 else pltpu.roll(kall, S5_GC * s, 1)
        pieces.append(jnp.where(lane >= S5_GC * s, shifted, 0.0))
    return jnp.concatenate(pieces, axis=0)


def _s5_prep_kernel(lam_c_re, lam_c_im, lam_r_re, lam_r_im, ldt_ref, b_t_re, b_t_im, b_re, b_im,
                    c_re, c_im, c_t_re, c_t_im, d_ref, t_ref, bcf_ref, bcb_ref, ccf_ref, ccb_ref, lam_ref):
    n = S5_L * S5_GC
    lane_blk = lax.broadcasted_iota(jnp.int32, (S5_P, n), 1) // S5_GC
    row_blk = lax.broadcasted_iota(jnp.int32, (n, S5_P), 0) // S5_GC
    bits = S5_L.bit_length()

    def bases(d):
        dt = jnp.exp(ldt_ref[d])
        return (_lam_bar(lam_c_re[d], lam_c_im[d], dt), _lam_bar(lam_r_re[d], lam_r_im[d], dt))

    def tile_rows(x):
        return jnp.concatenate([x] * S5_L, axis=0)

    def tile_lanes(ref, d):
        return jnp.concatenate([ref[d]] * S5_L, axis=1)

    ct_re, ct_im = [tile_lanes(c_t_re, d) for d in range(2)], [tile_lanes(c_t_im, d) for d in range(2)]

    (cbr, cbi), (rbr, rbi) = bases(0)
    kr, ki = _zoh_coef(lam_r_re[0], lam_r_im[0], rbr, rbi)
    bbt_re, bbt_im = _cmul(kr, ki, b_t_re[0], b_t_im[0])
    pr, pi = _cpow_int(cbr, cbi, lane_blk, bits)
    cl_re, cl_im = _cmul(ct_re[0], ct_im[0], pr, pi)
    t_f = _toeplitz_rows(_dot(bbt_re, cl_re, HI) - _dot(bbt_im, cl_im, HI))
    pr, pi = _cpow_int(rbr, rbi, S5_L - 1 - row_blk, bits)
    re, im = _cmul(tile_rows(bbt_re), tile_rows(bbt_im), pr, pi)
    bcf_ref[...] = jnp.concatenate([re, im], axis=1).astype(BF16)
    re, im = _cmul(cl_re, cl_im, cbr, cbi)
    ccf_ref[...] = jnp.concatenate([re, -im], axis=0).astype(BF16)
    pr, pi = _cpow_int(rbr, rbi, jnp.full((1, S5_P), S5_L, jnp.int32), bits)
    lam_ref[0:1, :] = jnp.concatenate([pr, pr], axis=1)
    lam_ref[1:2, :] = jnp.concatenate([-pi, pi], axis=1)

    (cbr, cbi), (rbr, rbi) = bases(1)
    kr, ki = _zoh_coef(lam_c_re[1], lam_c_im[1], cbr, cbi)
    bb_re, bb_im = _cmul(kr, ki, tile_lanes(b_re, 1), tile_lanes(b_im, 1))
    pr, pi = _cpow_int(cbr, cbi, lane_blk, bits)
    bl_re, bl_im = _cmul(bb_re, bb_im, pr, pi)
    t_b = _toeplitz_rows(_dot(c_re[1], bl_re, HI) - _dot(c_im[1], bl_im, HI)).T
    kr, ki = _zoh_coef(lam_r_re[1], lam_r_im[1], rbr, rbi)
    bbt_re, bbt_im = _cmul(kr, ki, b_t_re[1], b_t_im[1])
    pr, pi = _cpow_int(rbr, rbi, row_blk, bits)
    re, im = _cmul(tile_rows(bbt_re), tile_rows(bbt_im), pr, pi)
    bcb_ref[...] = jnp.concatenate([re, im], axis=1).astype(BF16)
    pr, pi = _cpow_int(cbr, cbi, S5_L - lane_blk, bits)
    re, im = _cmul(ct_re[1], ct_im[1], pr, pi)
    ccb_ref[...] = jnp.concatenate([re, -im], axis=0).astype(BF16)
    pr, pi = _cpow_int(rbr, rbi, jnp.full((1, S5_P), S5_L, jnp.int32), bits)
    lam_ref[2:3, :] = jnp.concatenate([pr, pr], axis=1)
    lam_ref[3:4, :] = jnp.concatenate([-pi, pi], axis=1)
    lam_ref[4:8, :] = jnp.zeros((4, 2 * S5_P), F32)

    eye = (lax.broadcasted_iota(jnp.int32, (n, n), 0) == lax.broadcasted_iota(jnp.int32, (n, n), 1))
    d_diag = jnp.concatenate([d_ref[...]] * S5_L, axis=1)
    t_ref[...] = (t_f + t_b + jnp.where(eye, d_diag, 0.0)).astype(BF16)


def _s5_prep(lam_re, lam_im, log_dt, b_re, b_im, c_re, c_im, s5_d):
    n = S5_L * S5_GC
    swap = lambda a: jnp.swapaxes(a, -1, -2)
    ins = [lam_re[..., :, None], lam_im[..., :, None], lam_re[..., None, :], lam_im[..., None, :],
           log_dt[..., None, None], swap(b_re), swap(b_im), b_re, b_im, c_re, c_im, swap(c_re), swap(c_im)]
    d_grp = s5_d.reshape(DEPTH, S5_G, 1, S5_GC)

    def dir_spec(a):
        return pl.BlockSpec((None, 2, None) + a.shape[3:], lambda l, g: (l, 0, g, 0, 0))

    def out(rows, cols, dtype):
        return (pl.BlockSpec((None, None, rows, cols), lambda l, g: (l, g, 0, 0)),
                jax.ShapeDtypeStruct((DEPTH, S5_G, rows, cols), dtype))

    outs = [out(n, n, BF16), out(n, 2 * S5_P, BF16), out(n, 2 * S5_P, BF16),
            out(2 * S5_P, n, BF16), out(2 * S5_P, n, BF16), out(8, 2 * S5_P, F32)]
    return pl.pallas_call(
        _s5_prep_kernel,
        grid=(DEPTH, S5_G),
        in_specs=[dir_spec(a) for a in ins] + [pl.BlockSpec((None, None, 1, S5_GC), lambda l, g: (l, g, 0, 0))],
        out_specs=[o[0] for o in outs], out_shape=[o[1] for o in outs],
        compiler_params=_params(("arbitrary", "arbitrary")), name="s5_prep",
    )(*ins, d_grp)


def _block_transpose(a):
    n = len(a)
    blk = lax.broadcasted_iota(jnp.int32, (1, a[0].shape[1]), 1) // S5_GC
    a = list(a)
    bit = 1
    while bit < n:
        upper = (blk & bit) != 0
        for i in range(n):
            if i & bit == 0:
                j = i | bit
                lo, hi = a[i], a[j]
                a[i] = jnp.where(upper, pltpu.roll(hi, bit * S5_GC, 1), lo)
                a[j] = jnp.where(upper, hi, pltpu.roll(lo, LANES - bit * S5_GC, 1))
        bit *= 2
    return a


def _s5_kernel(*refs, n_seq, n_k, has_state):
    if has_state:
        (u_ref, t_ref, bcf_ref, bcb_ref, ccf_ref, ccb_ref, lam_ref, x0_ref,
         y_ref, ug_ref, yg_ref, inj_f, inj_b, inj_s, xin_f, xin_b) = refs
    else:
        (u_ref, t_ref, bcf_ref, bcb_ref, ccf_ref, ccb_ref, lam_ref,
         y_ref, xf_ref, xb_ref, ug_ref, yg_ref, inj_f, inj_b, inj_s, xin_f, xin_b) = refs
    ng = S5_G // 2
    r = n_seq * n_k
    tok = lambda i: pl.ds(i, r, stride=S5_L)
    u_lo = _block_transpose([u_ref[tok(i), :] for i in range(ng)])
    u_hi = _block_transpose([u_ref[tok(i), :] for i in range(ng, S5_L)])
    for g in range(ng):
        ug_ref[g] = jnp.concatenate([u_lo[g], u_hi[g]], axis=1).astype(BF16)
        inj_f[g] = _dot(ug_ref[g], bcf_ref[g])
        inj_b[g] = _dot(ug_ref[g], bcb_ref[g])

    for d, (inj, xin) in enumerate(((inj_f, xin_f), (inj_b, xin_b))):
        a = [lam_ref[g, 2 * d:2 * d + 1, :] for g in range(ng)]
        bs = [lam_ref[g, 2 * d + 1:2 * d + 2, :] for g in range(ng)]
        for g in range(ng):
            inj_s[g] = pltpu.roll(inj[g], S5_P, 1)
        x = [x0_ref[d, g] if has_state else jnp.zeros((n_seq, 2 * S5_P), F32) for g in range(ng)]
        xs = [pltpu.roll(v, S5_P, 1) for v in x]
        for k in (range(n_k) if d == 0 else range(n_k - 1, -1, -1)):
            rows = pl.ds(k, n_seq, stride=n_k)
            for g in range(ng):
                xin[g, rows, :] = x[g]
                x[g], xs[g] = (a[g] * x[g] + bs[g] * xs[g] + inj[g, rows, :],
                               a[g] * xs[g] - bs[g] * x[g] + inj_s[g, rows, :])
        if not has_state:
            for g in range(ng):
                (xf_ref if d == 0 else xb_ref)[g] = x[g]

    for g in range(ng):
        yg_ref[g] = (_dot(ug_ref[g], t_ref[g]) + _dot(xin_f[g].astype(BF16), ccf_ref[g])
                     + _dot(xin_b[g].astype(BF16), ccb_ref[g]))
    for half in range(S5_L // ng):
        y_i = _block_transpose([yg_ref[g, :, LANES * half:LANES * (half + 1)] for g in range(ng)])
        for i in range(ng):
            y_ref[tok(ng * half + i), :] = y_i[i]


def _s5(u_halves, ops, l, n_seq, n_k, x0):
    has_state = x0 is not None
    n = S5_L * S5_GC
    r = n_seq * n_k
    w = 2 * S5_P
    ng = S5_G // 2
    half_spec = pl.BlockSpec((None, r * S5_L, LANES), lambda j: (j, 0, 0))
    op_spec = lambda a: pl.BlockSpec((None, ng) + a.shape[2:], lambda j: (l, j) + (0,) * (a.ndim - 2))
    args = [u_halves] + list(ops)
    in_specs = [half_spec] + [op_spec(a) for a in ops]
    if has_state:
        args.append(x0)
        in_specs.append(pl.BlockSpec((None, 2, ng, n_seq, w), lambda j: (l, 0, j, 0, 0)))
    out_shape = [jax.ShapeDtypeStruct((2, r * S5_L, LANES), F32)]
    out_specs = [half_spec]
    if not has_state:
        out_shape += [jax.ShapeDtypeStruct((S5_G, n_seq, w), F32)] * 2
        out_specs += [pl.BlockSpec((ng, n_seq, w), lambda j: (j, 0, 0))] * 2
    return pl.pallas_call(
        functools.partial(_s5_kernel, n_seq=n_seq, n_k=n_k, has_state=has_state),
        grid=(2,), in_specs=in_specs, out_specs=out_specs, out_shape=out_shape,
        scratch_shapes=[pltpu.VMEM((ng, r, n), BF16), pltpu.VMEM((ng, r, n), F32)]
        + [pltpu.VMEM((ng, r, w), F32)] * 5,
        compiler_params=_params(("arbitrary",)), name="s5_scan",
    )(*args)


def _out_ffn_kernel(x_ref, m_ref, r_ref, a_ref, ya_ref, yb_ref, mod_ref, wglu_ref, bglu_ref, wout_ref,
                    n_post_ref, n_pre_ref, n_fpost_ref, wg_ref, wu_ref, wd_ref, o_ref):
    def mod(i):
        return mod_ref[:, i * D_MODEL:(i + 1) * D_MODEL]

    y = jnp.concatenate([ya_ref[...], yb_ref[...]], axis=1)
    gs = 0.5 * y * (1.0 + jnp.tanh(math.sqrt(2.0 / math.pi) * (y + 0.044715 * (y * y * y))))
    s_out = gs * jax.nn.sigmoid(_dot(gs.astype(BF16), wglu_ref[...]) + bglu_ref[...])
    mixed = jnp.concatenate([m_ref[...], r_ref[...], a_ref[...], s_out], axis=1).astype(BF16)
    x1 = x_ref[...] + mod(2) * _rms(_dot(mixed, wout_ref[...]), n_post_ref[...])
    h = (_rms(x1, n_pre_ref[...]) * (1.0 + mod(4)) + mod(3)).astype(BF16)
    act = (_silu(_dot(h, wg_ref[...])) * _dot(h, wu_ref[...])).astype(BF16)
    o_ref[...] = x1 + mod(5) * _rms(_dot(act, wd_ref[...]), n_fpost_ref[...])


def _out_ffn(x, m_out, r_out, a_out, y_halves, mod, l, mod_row, wp):
    n = x.shape[0]
    row = lambda w: pl.BlockSpec((ROW_TILE, w), lambda i: (i, 0))
    half = lambda j: pl.BlockSpec((None, ROW_TILE, LANES), lambda i: (j, i, 0))
    params = [wp[k] for k in ('w_glu', 'b_glu', 'w_out', 'n_mix_post', 'n_ffn_pre', 'n_ffn_post',
                              'w_gate', 'w_up', 'w_down')]
    return pl.pallas_call(
        _out_ffn_kernel,
        grid=(n // ROW_TILE,),
        in_specs=[row(D_MODEL), row(GROUP_W), row(GROUP_W), row(GROUP_W), half(0), half(1),
                  _mod_spec(l, mod_row)] + [_layer_spec(a, l, pipeline_mode=pl.Buffered(1)) for a in params],
        out_specs=row(D_MODEL),
        out_shape=jax.ShapeDtypeStruct((n, D_MODEL), F32),
        compiler_params=_params(("arbitrary",)), name="out_ffn",
    )(x, m_out, r_out, a_out, y_halves, y_halves, mod, *params)


def _layer(x, l, n_seq, t, mod, mod_row, wp, s5_ops, lam_init, st):
    ctx = st is None
    res = _in_proj(x, mod, l, mod_row, wp['n_mix_pre'], wp['w_in'], emit_kv=ctx)
    z, gates = res[0], res[2]
    m_res, r_res, a_out = _mixers(z, gates, wp, l, n_seq, t, None if ctx else st[0:5], lam_init, fuse=ctx)
    s_res = _s5(res[1], s5_ops, l, n_seq, t // S5_L, None if ctx else st[5])
    x_new = _out_ffn(x, m_res[0], r_res[0], a_out, s_res[0], mod, l, mod_row, wp)
    if not ctx:
        return x_new, None
    return x_new, (m_res[1], m_res[2], m_res[3], r_res[1], res[3], res[4], s_res[1], s_res[2])


def _pad_to(a, rows, cols=LANES):
    pad = [(0, 0)] * (a.ndim - 2) + [(0, rows - a.shape[-2]), (0, cols - a.shape[-1])]
    return jnp.pad(a, pad)


def kernel(x_prompt, x_sample, state_mlstm_C, state_mlstm_n, state_mlstm_m, state_ret, cache_diff_k, cache_diff_v, state_s5_re, state_s5_im, c, c_ctx, w_ada, b_ada, n_mix_pre, n_mix_post, n_ffn_pre, n_ffn_post, w_in, w_out, m_gate_bias, m_norm, r_decay_logit, r_norm, a_lam_q1, a_lam_k1, a_lam_q2, a_lam_k2, a_norm, s5_lam_re, s5_lam_im, s5_log_dt, s5_b_re, s5_b_im, s5_c_re, s5_c_im, s5_d, s5_w_glu, s5_b_glu, w_ffn_gate, w_ffn_up, w_ffn_down):
    n_ctx, t_ctx, _ = x_prompt.shape
    n_lat, t_lat, _ = x_sample.shape
    past = cache_diff_k.shape[2]

    cond8 = jnp.concatenate([c, c_ctx[None, :], jnp.zeros((8 - n_lat - 1, D_MODEL), F32)], axis=0)
    mod = _ada(cond8, w_ada, b_ada).reshape(DEPTH, 8, 1, 6 * D_MODEL)
    s5_ops = _s5_prep(s5_lam_re, s5_lam_im, s5_log_dt, s5_b_re, s5_b_im, s5_c_re, s5_c_im, s5_d)

    row = lambda a: a[:, None, :]
    wp = dict(
        w_in=jnp.pad(w_in, ((0, 0), (0, 0), (0, N_MAIN + LANES - w_in.shape[-1]))).astype(BF16), w_out=w_out.astype(BF16), w_glu=s5_w_glu.astype(BF16), b_glu=row(s5_b_glu),
        w_gate=w_ffn_gate.astype(BF16), w_up=w_ffn_up.astype(BF16), w_down=w_ffn_down.astype(BF16),
        n_mix_pre=row(n_mix_pre), n_mix_post=row(n_mix_post), n_ffn_pre=row(n_ffn_pre),
        n_ffn_post=row(n_ffn_post), m_norm=row(m_norm), r_norm=row(r_norm), a_norm=row(a_norm),
        m_bias=_pad_to(row(m_gate_bias), 1), r_decay=_pad_to(r_decay_logit, 8),
        a_lam=_pad_to(jnp.stack([a_lam_q1, a_lam_k1, a_lam_q2, a_lam_k2], axis=1), 8))
    lam_inits = [0.8 - 0.6 * math.exp(-0.3 * l) for l in range(DEPTH)]

    x = x_prompt.reshape(n_ctx * t_ctx, D_MODEL)
    new_states = []
    for l in range(DEPTH):
        x, st = _layer(x, l, n_ctx, t_ctx, mod, lambda i: n_lat, wp, s5_ops, lam_inits[l], None)
        new_states.append(st)
    y_prompt = x.reshape(n_ctx, t_ctx, D_MODEL)

    x0 = jnp.stack([state_s5_re, state_s5_im], axis=-2)
    x0 = x0.transpose(1, 2, 3, 0, 4, 5).reshape(DEPTH, 2, S5_G, n_lat, 2 * S5_P)
    s_ext0 = jnp.concatenate([state_mlstm_C, state_mlstm_n[..., None],
                              jnp.zeros(state_mlstm_n.shape + (LANES - HEAD_DIM - 1,), F32)], axis=-1)
    st = (s_ext0, state_mlstm_m.reshape(n_lat, DEPTH, 2 * N_HEADS, 1), state_ret,
          cache_diff_k.reshape(n_lat, DEPTH, past, GROUP_W), cache_diff_v.reshape(n_lat, DEPTH, past, GROUP_W), x0)
    x = x_sample.reshape(n_lat * t_lat, D_MODEL)
    tiles_per_seq = t_lat // ROW_TILE
    for l in range(DEPTH):
        x, _ = _layer(x, l, n_lat, t_lat, mod, lambda i: i // tiles_per_seq, wp, s5_ops, lam_inits[l], st)
    y_sample = x.reshape(n_lat, t_lat, D_MODEL)

    stack = lambda i: jnp.stack([s[i] for s in new_states], axis=1)
    kv = lambda i: jnp.stack([s[i].reshape(n_ctx, t_ctx, N_HEADS, HEAD_DIM) for s in new_states], axis=1)
    xs = jnp.stack([stack(6), stack(7)], axis=2)
    xs = xs.reshape(S5_G, DEPTH, 2, n_ctx, 2, S5_P).transpose(3, 1, 2, 0, 4, 5)
    return (y_prompt, y_sample, stack(0), stack(1)[..., 0], stack(2).reshape(n_ctx, DEPTH, 2, N_HEADS),
            stack(3), kv(4), kv(5), xs[..., 0, :], xs[..., 1, :])
```

```python
import functools
import math

import jax
import jax.numpy as jnp
from jax import lax
from jax.experimental import pallas as pl
from jax.experimental.pallas import tpu as pltpu

F32 = jnp.float32
BF16 = jnp.bfloat16
HI = lax.Precision.HIGHEST

D_MODEL = 1024
DEPTH = 2
GRID_W = 64
HEAD_DIM = 64
GROUP_W = 256
N_HEADS = 4
A_SUB = 32
S5_GC = 16
S5_G = 16
S5_P = 64
D_FF = 2816
ROPE_BASE = 10000.0
EPS = 1e-6
N_MAIN = 12 * GROUP_W
N_MIX = 11 * GROUP_W
LANES = 128
CHUNK = 256
ROW_TILE = 512
S5_L = 16
NEG = -1e30
VMEM_LIMIT = 56 * 1024 * 1024


def _dot(a, b, precision=None):
    return jnp.dot(a, b, preferred_element_type=F32, precision=precision)


def _dot_nt(a, b):
    return lax.dot_general(a, b, (((1,), (1,)), ((), ())), preferred_element_type=F32)


def _log_sigmoid(x):
    return jnp.minimum(x, 0.0) - jnp.log(1.0 + jnp.exp(-jnp.abs(x)))


def _silu(x):
    return x * jax.nn.sigmoid(x)


def _rms(x, g):
    return x * lax.rsqrt(jnp.mean(x * x, axis=-1, keepdims=True) + EPS) * g


def _params(sem=None):
    return pltpu.CompilerParams(dimension_semantics=sem, vmem_limit_bytes=VMEM_LIMIT)


def _layer_spec(a, l, **kw):
    n = a.ndim - 1
    return pl.BlockSpec((None,) + a.shape[1:], lambda *_: (l,) + (0,) * n, **kw)


def _seq_layer_spec(a, l):
    n = a.ndim - 2
    return pl.BlockSpec((None, None) + a.shape[2:], lambda b: (b, l) + (0,) * n)


def _split_bf16(x):
    hi = x.astype(BF16)
    return hi, (x - hi.astype(F32)).astype(BF16)


def _ada_kernel(c_ref, w_ref, b_ref, o_ref):
    a_hi, a_lo = _split_bf16(_silu(c_ref[...]))
    w_hi, w_lo = _split_bf16(w_ref[...])
    o_ref[...] = _dot(a_hi, w_hi) + (_dot(a_lo, w_hi) + _dot(a_hi, w_lo)) + b_ref[...]


def _ada(cond8, w_ada, b_ada):
    tn = 1536
    return pl.pallas_call(
        _ada_kernel,
        grid=(DEPTH, 6 * D_MODEL // tn),
        in_specs=[pl.BlockSpec((8, D_MODEL), lambda l, j: (0, 0)),
                  pl.BlockSpec((None, D_MODEL, tn), lambda l, j: (l, 0, j)),
                  pl.BlockSpec((None, 1, tn), lambda l, j: (l, 0, j))],
        out_specs=pl.BlockSpec((None, 8, tn), lambda l, j: (l, 0, j)),
        out_shape=jax.ShapeDtypeStruct((DEPTH, 8, 6 * D_MODEL), F32),
        compiler_params=_params(("arbitrary", "arbitrary")),
        name="ada",
    )(cond8, w_ada, b_ada.reshape(DEPTH, 1, 6 * D_MODEL))


def _in_proj_kernel(x_ref, mod_ref, g_ref, w_ref, z_ref, u_ref, gate_ref, *kv_refs):
    h = _rms(x_ref[...], g_ref[...]) * (1.0 + mod_ref[:, D_MODEL:2 * D_MODEL]) + mod_ref[:, 0:D_MODEL]
    hb = h.astype(BF16)
    n_gate = 4 * N_HEADS
    n_head = 4 * GROUP_W
    z_ref[:, 0:n_head] = _dot(hb, w_ref[:, 0:n_head]).astype(BF16)
    tail = _dot(hb, w_ref[:, n_head:])
    gate_ref[...] = tail[:, 0:LANES]
    rest = tail[:, n_gate:n_gate + N_MAIN - n_head]
    z_ref[:, n_head:] = rest[:, 0:N_MIX - n_head].astype(BF16)
    u_ref[0] = rest[:, N_MIX - n_head:N_MIX - n_head + LANES]
    u_ref[1] = rest[:, N_MIX - n_head + LANES:]
    if kv_refs:
        kv_refs[0][...] = rest[:, 9 * GROUP_W - n_head:10 * GROUP_W - n_head]
        kv_refs[1][...] = rest[:, 10 * GROUP_W - n_head:11 * GROUP_W - n_head]


def _mod_spec(l, mod_row):
    return pl.BlockSpec((None, None, 1, 6 * D_MODEL), lambda i: (l, mod_row(i), 0, 0))


def _in_proj(x, mod, l, mod_row, gain, w, emit_kv):
    n = x.shape[0]
    row = lambda width: pl.BlockSpec((ROW_TILE, width), lambda i: (i, 0))
    out_specs = [row(N_MIX), pl.BlockSpec((2, ROW_TILE, LANES), lambda i: (0, i, 0)), row(LANES)]
    out_specs += [row(GROUP_W)] * (2 if emit_kv else 0)
    out_shape = [jax.ShapeDtypeStruct((n, N_MIX), BF16), jax.ShapeDtypeStruct((2, n, LANES), F32),
                 jax.ShapeDtypeStruct((n, LANES), F32)]
    out_shape += [jax.ShapeDtypeStruct((n, GROUP_W), F32)] * (2 if emit_kv else 0)
    return pl.pallas_call(
        _in_proj_kernel,
        grid=(n // ROW_TILE,),
        in_specs=[row(D_MODEL), _mod_spec(l, mod_row), _layer_spec(gain, l),
                  _layer_spec(w, l, pipeline_mode=pl.Buffered(1))],
        out_specs=out_specs, out_shape=out_shape,
        compiler_params=_params(("arbitrary",)),
        name="in_proj",
    )(x, mod, gain, w)


def _tri_masks(n):
    row = lax.broadcasted_iota(jnp.int32, (n, n), 0)
    col = lax.broadcasted_iota(jnp.int32, (n, n), 1)
    return row >= col, row <= col


def _head_mean_matrix():
    r = lax.broadcasted_iota(jnp.int32, (GROUP_W, GROUP_W), 0) // HEAD_DIM
    c = lax.broadcasted_iota(jnp.int32, (GROUP_W, GROUP_W), 1) // HEAD_DIM
    return jnp.where(r == c, 1.0 / HEAD_DIM, 0.0).astype(BF16)


def _head_mean(x, j):
    hi, lo = _split_bf16(x)
    return _dot(hi, j) + _dot(lo, j)


def _head_groupnorm(x, g, j):
    xc = x - _head_mean(x, j)
    return xc * lax.rsqrt(_head_mean(xc * xc, j) + EPS) * g


def _head_rms(x, g, j):
    return x * lax.rsqrt(_head_mean(x * x, j) + EPS) * g


def _hs(h):
    return slice(HEAD_DIM * h, HEAD_DIM * (h + 1))


def _v_ext(v_ref, rows, hs):
    ones = jnp.ones((rows.stop - rows.start, HEAD_DIM), BF16)
    return jnp.concatenate([v_ref[rows, hs].astype(BF16), ones], axis=1)


def _scan_max(x, reverse):
    n = x.shape[1]
    lane = lax.broadcasted_iota(jnp.int32, x.shape, 1)
    sh = 1
    while sh < n:
        if reverse:
            x = jnp.maximum(x, jnp.where(lane < n - sh, pltpu.roll(x, n - sh, 1), NEG))
        else:
            x = jnp.maximum(x, jnp.where(lane >= sh, pltpu.roll(x, sh, 1), NEG))
        sh *= 2
    return x


def _ends(x, is_fwd):
    return jnp.where(is_fwd, x[:, x.shape[1] - 1:], x[:, 0:1])


def _mlstm_kernel(*refs, n_chunk, has_state):
    if has_state:
        q_ref, k_ref, v_ref, o_ref, g_ref, bias_ref, gain_ref, s0_ref, m0_ref, out_ref = refs
    else:
        q_ref, k_ref, v_ref, o_ref, g_ref, bias_ref, gain_ref, out_ref, cn_ref, nn_ref, mn_ref = refs
    L = CHUNK
    nu = 2 * N_HEADS
    tril, triu = _tri_masks(L)
    is_fwd = lax.broadcasted_iota(jnp.int32, (nu, 1), 0) < N_HEADS

    b8, w8, mcum8 = [], [], []
    for c in range(n_chunk):
        p_t = (g_ref[c * L:(c + 1) * L, :] + bias_ref[...]).T
        lf = _log_sigmoid(p_t[nu:2 * nu, :])
        b = jnp.where(is_fwd, _dot(lf, triu.astype(F32), HI), _dot(lf, tril.astype(F32), HI))
        w = p_t[0:nu, :] - b
        b8.append(b)
        w8.append(w)
        mcum8.append(jnp.where(is_fwd, _scan_max(w, False), _scan_max(w, True)))

    m0 = m0_ref[...] if has_state else jnp.zeros((nu, 1), F32)
    m_in_f, m_in_b = [None] * n_chunk, [None] * n_chunk
    m = m0
    for c in range(n_chunk):
        m_in_f[c] = m
        m = (b8[c] + jnp.maximum(mcum8[c], m))[:, L - 1:]
    m_fin_f = m
    m = m0
    for c in range(n_chunk - 1, -1, -1):
        m_in_b[c] = m
        m = (b8[c] + jnp.maximum(mcum8[c], m))[:, 0:1]
    if not has_state:
        mn_ref[...] = jnp.where(is_fwd, m_fin_f, m)

    wk8, dec8, cols = [], [], []
    for c in range(n_chunk):
        m_in = jnp.where(is_fwd, m_in_f[c], m_in_b[c])
        g = jnp.maximum(mcum8[c], m_in)
        m_row = b8[c] + g
        m_new, b_last = _ends(m_row, is_fwd), _ends(b8[c], is_fwd)
        wk8.append(jnp.exp(b_last + w8[c] - m_new))
        dec8.append(jnp.exp(b_last + m_in - m_new))
        stats = jnp.concatenate([g, jnp.exp(m_in - g), jnp.exp(-m_row), jnp.zeros((LANES - 3 * nu, L), F32)], axis=0)
        cols.append(stats.T)

    k_t = [(k_ref[c * L:(c + 1) * L, :].astype(F32) * (HEAD_DIM ** -0.5)).T for c in range(n_chunk)]

    h_all = []
    for h in range(N_HEADS):
        hs = _hs(h)
        qb = [q_ref[c * L:(c + 1) * L, hs].astype(BF16) for c in range(n_chunk)]
        kt = [k_t[c][hs, :] for c in range(n_chunk)]
        vx = [_v_ext(v_ref, slice(c * L, (c + 1) * L), hs) for c in range(n_chunk)]
        qk = [_dot(qb[c], kt[c].astype(BF16)) for c in range(n_chunk)]
        h_sum = [None] * n_chunk
        for d in range(2):
            tri = tril if d == 0 else triu
            j = N_HEADS * d + h
            s_ext = s0_ref[d, h] if has_state else None
            order = range(n_chunk) if d == 0 else range(n_chunk - 1, -1, -1)
            for ci, c in enumerate(order):
                wgt = jnp.exp(jnp.where(tri, w8[c][j:j + 1, :] - cols[c][:, j:j + 1], NEG))
                tot = _dot((qk[c] * wgt).astype(BF16), vx[c])
                if s_ext is not None:
                    tot = tot + cols[c][:, nu + j:nu + j + 1] * _dot(qb[c], s_ext.astype(BF16))
                den = jnp.maximum(jnp.abs(tot[:, HEAD_DIM:HEAD_DIM + 1]), cols[c][:, 2 * nu + j:2 * nu + j + 1])
                hd = tot[:, 0:HEAD_DIM] / den
                h_sum[c] = hd if h_sum[c] is None else h_sum[c] + hd
                if (not has_state) or ci < n_chunk - 1:
                    upd = _dot((kt[c] * wk8[c][j:j + 1, :]).astype(BF16), vx[c])
                    s_ext = upd if s_ext is None else upd + dec8[c][j:j + 1, :] * s_ext
            if not has_state:
                cn_ref[d, h] = s_ext[:, 0:HEAD_DIM]
                nn_ref[d, h] = s_ext[:, HEAD_DIM:HEAD_DIM + 1]
        h_all.append(h_sum)
    jm = _head_mean_matrix()
    for c in range(n_chunk):
        rows = slice(c * L, (c + 1) * L)
        gated = jax.nn.sigmoid(o_ref[rows, :].astype(F32)) * jnp.concatenate([h_all[h][c] for h in range(N_HEADS)], axis=1)
        out_ref[rows, :] = _head_groupnorm(gated, gain_ref[...], jm)


def _ret_kernel(*refs, n_chunk, has_state):
    if has_state:
        q_ref, k_ref, v_ref, g_ref, lg_ref, gain_ref, s0_ref, out_ref, dmat_ref, col_ref, row_ref = refs
    else:
        q_ref, k_ref, v_ref, g_ref, lg_ref, gain_ref, out_ref, sn_ref, dmat_ref, col_ref, row_ref = refs
    L = CHUNK

    @pl.when(pl.program_id(0) == 0)
    def _():
        row = lax.broadcasted_iota(jnp.int32, (L, L), 0)
        col = lax.broadcasted_iota(jnp.int32, (L, L), 1)
        diff = (row - col).astype(F32)
        log_g = _log_sigmoid(lg_ref[...])
        pos_c = lax.broadcasted_iota(jnp.int32, (L, 1), 0).astype(F32)
        pos_r = lax.broadcasted_iota(jnp.int32, (1, L), 1).astype(F32)
        for h in range(N_HEADS):
            lgf = log_g[0:1, h:h + 1]
            lgb = log_g[1:2, h:h + 1]
            dmat_ref[h] = (jnp.where(diff >= 0, jnp.exp(jnp.maximum(diff, 0.0) * lgf), 0.0)
                           + jnp.where(diff <= 0, jnp.exp(jnp.maximum(-diff, 0.0) * lgb), 0.0))
            col_ref[:, h:h + 1] = jnp.exp((pos_c + 1.0) * lgf)
            col_ref[:, N_HEADS + h:N_HEADS + h + 1] = jnp.exp((L - pos_c) * lgb)
            row_ref[h:h + 1, :] = jnp.exp((L - 1.0 - pos_r) * lgf)
            row_ref[N_HEADS + h:N_HEADS + h + 1, :] = jnp.exp(pos_r * lgb)
            row_ref[2 * N_HEADS + h:2 * N_HEADS + h + 1, :] = jnp.exp(L * lgf) + jnp.zeros((1, L), F32)
            row_ref[3 * N_HEADS + h:3 * N_HEADS + h + 1, :] = jnp.exp(L * lgb) + jnp.zeros((1, L), F32)

    k_t = [(k_ref[c * L:(c + 1) * L, :].astype(F32) * (HEAD_DIM ** -0.5)).T for c in range(n_chunk)]
    o_all = []
    for h in range(N_HEADS):
        hs = _hs(h)
        zeta_f = row_ref[h:h + 1, :]
        zeta_b = row_ref[N_HEADS + h:N_HEADS + h + 1, :]
        gl_f = row_ref[2 * N_HEADS + h:2 * N_HEADS + h + 1, 0:1]
        gl_b = row_ref[3 * N_HEADS + h:3 * N_HEADS + h + 1, 0:1]
        kt = [k_t[c][hs, :] for c in range(n_chunk)]
        vb = [v_ref[c * L:(c + 1) * L, hs].astype(BF16) for c in range(n_chunk)]
        sf = [None] * (n_chunk + 1)
        sb = [None] * (n_chunk + 1)
        if has_state:
            sf[0] = s0_ref[0, h]
            sb[n_chunk] = s0_ref[1, h]
        for c in range(n_chunk):
            if (not has_state) or c < n_chunk - 1:
                upd = _dot((kt[c] * zeta_f).astype(BF16), vb[c])
                sf[c + 1] = upd if sf[c] is None else upd + gl_f * sf[c]
        for c in range(n_chunk - 1, -1, -1):
            if (not has_state) or c > 0:
                upd = _dot((kt[c] * zeta_b).astype(BF16), vb[c])
                sb[c] = upd if sb[c + 1] is None else upd + gl_b * sb[c + 1]
        if not has_state:
            sn_ref[0, h] = sf[n_chunk]
            sn_ref[1, h] = sb[0]
        o_h = []
        for c in range(n_chunk):
            qb = q_ref[c * L:(c + 1) * L, hs].astype(BF16)
            att = _dot(qb, kt[c].astype(BF16)) * dmat_ref[h]
            o = _dot(att.astype(BF16), vb[c])
            if has_state:
                o = o + col_ref[:, h:h + 1] * _dot(qb, sf[c].astype(BF16))
                o = o + col_ref[:, N_HEADS + h:N_HEADS + h + 1] * _dot(qb, sb[c + 1].astype(BF16))
            o_h.append(o)
        o_all.append(o_h)
    jm = _head_mean_matrix()
    for c in range(n_chunk):
        rows = slice(c * L, (c + 1) * L)
        o = jnp.concatenate([o_all[h][c] for h in range(N_HEADS)], axis=1)
        out_ref[rows, :] = _silu(g_ref[rows, :].astype(F32)) * _head_groupnorm(o, gain_ref[...], jm)


def _lambda(lam_ref, lam_init):
    v = lam_ref[...]
    s1 = jnp.sum(v[0:1, :] * v[1:2, :], axis=-1, keepdims=True)
    s2 = jnp.sum(v[2:3, :] * v[3:4, :], axis=-1, keepdims=True)
    return jnp.exp(s1) - jnp.exp(s2) + lam_init


def _first_half(shape):
    lane = lax.broadcasted_iota(jnp.int32, shape, 1)
    return (lane % (A_SUB // 2)) < (A_SUB // 4)


def _rope(x, cos, sin_signed):
    n = x.shape[1]
    first_half = _first_half(x.shape)
    partner = jnp.where(first_half, pltpu.roll(x, n - A_SUB // 4, 1), pltpu.roll(x, A_SUB // 4, 1))
    return x * cos + partner * sin_signed


def _attn_kernel(*refs, t, has_cache, lam_init):
    if has_cache:
        q_ref, k_ref, v_ref, kc_ref, vc_ref, lam_ref, gain_ref, out_ref, qr_ref, kr_ref, cos_ref, sin_ref = refs
    else:
        q_ref, k_ref, v_ref, lam_ref, gain_ref, out_ref, qr_ref, kr_ref = refs
    lam = _lambda(lam_ref, lam_init)
    scale = A_SUB ** -0.5
    n_qb = t // CHUNK

    if has_cache:
        @pl.when(pl.program_id(0) == 0)
        def _():
            lane = lax.broadcasted_iota(jnp.int32, (GRID_W, GROUP_W), 1)
            pos = lax.broadcasted_iota(jnp.int32, (GRID_W, GROUP_W), 0).astype(F32)
            n_freq = A_SUB // 4
            ang = pos * jnp.exp((lane % n_freq).astype(F32) * (-math.log(ROPE_BASE) / n_freq))
            cos_t = jnp.cos(ang)
            sin_t = jnp.where(_first_half((GRID_W, GROUP_W)), -jnp.sin(ang), jnp.sin(ang))
            row_axis = (lane % A_SUB) < (A_SUB // 2)
            for r in range(t // GRID_W):
                rows = slice(r * GRID_W, (r + 1) * GRID_W)
                cos_ref[rows, :] = jnp.where(row_axis, cos_t[r:r + 1, :], cos_t)
                sin_ref[rows, :] = jnp.where(row_axis, sin_t[r:r + 1, :], sin_t)

        kr_ref[...] = _rope(k_ref[...].astype(F32), cos_ref[...], sin_ref[...]).astype(BF16)
        qr_ref[...] = (_rope(q_ref[...].astype(F32), cos_ref[...], sin_ref[...]) * scale).astype(BF16)
    else:
        kr_ref[...] = k_ref[...].astype(BF16)
        qr_ref[...] = (q_ref[...].astype(F32) * scale).astype(BF16)

    o_all = []
    for h in range(N_HEADS):
        hs = _hs(h)
        vx = _v_ext(v_ref, slice(0, t), hs)
        if has_cache:
            vcx = _v_ext(vc_ref, slice(0, vc_ref.shape[0]), hs)
        o_h = []
        for qi in range(n_qb):
            rows = slice(qi * CHUNK, (qi + 1) * CHUNK)
            o_maps = []
            for m in range(2):
                ms = slice(HEAD_DIM * h + A_SUB * m, HEAD_DIM * h + A_SUB * (m + 1))
                qm = qr_ref[rows, ms]
                s_l = _dot_nt(qm, kr_ref[:, ms])
                mx = jnp.max(s_l, axis=-1, keepdims=True)
                if has_cache:
                    s_c = _dot_nt(qm, kc_ref[:, ms].astype(BF16))
                    mx = jnp.maximum(mx, jnp.max(s_c, axis=-1, keepdims=True))
                tot = _dot(jnp.exp(s_l - mx).astype(BF16), vx)
                if has_cache:
                    tot = tot + _dot(jnp.exp(s_c - mx).astype(BF16), vcx)
                o_maps.append(tot[:, 0:HEAD_DIM] / tot[:, HEAD_DIM:HEAD_DIM + 1])
            o_h.append(o_maps[0] - lam * o_maps[1])
        o_all.append(o_h)
    jm = _head_mean_matrix()
    for qi in range(n_qb):
        rows = slice(qi * CHUNK, (qi + 1) * CHUNK)
        o = jnp.concatenate([o_all[h][qi] for h in range(N_HEADS)], axis=1)
        out_ref[rows, :] = _head_rms(o, gain_ref[...], jm) * (1.0 - lam_init)


def _mixers_kernel(*refs, n_chunk, t, has_state, lam_init, n_in, n_out, kinds, n_sub):
    n_io = sum(n_in) + sum(n_out)
    io, scratch = refs[:n_io], refs[n_io:]
    for sub in range(n_sub):
        def view(r, kind):
            if kind == 'rows':
                return r.at[pl.ds(sub * t, t)]
            return r.at[sub] if kind == 'seq' else r

        v = [view(r, k) for r, k in zip(io, kinds)]
        ins, outs = v[:sum(n_in)], v[sum(n_in):]
        i0, o0 = 0, 0
        parts = []
        for k in range(3):
            parts.append((ins[i0:i0 + n_in[k]], outs[o0:o0 + n_out[k]]))
            i0 += n_in[k]
            o0 += n_out[k]
        a_scratch = [r.at[sub] for r in scratch[3:]]
        _mlstm_kernel(*parts[0][0], *parts[0][1], n_chunk=n_chunk, has_state=has_state)
        _ret_kernel(*parts[1][0], *parts[1][1], *scratch[0:3], n_chunk=n_chunk, has_state=has_state)
        _attn_kernel(*parts[2][0], *parts[2][1], *a_scratch, t=t, has_cache=has_state, lam_init=lam_init)


def _mixers(z, gates, wp, l, n_seq, t, st, lam_init, fuse):
    has_state = st is not None
    nu = 2 * N_HEADS
    n_sub = 2 if fuse else 1
    tt = n_sub * t
    seq = lambda width: pl.BlockSpec((tt, width), lambda b: (b, 0))
    zspec = lambda colblk: pl.BlockSpec((tt, GROUP_W), lambda b: (b, colblk))
    out_rows = jax.ShapeDtypeStruct((n_seq * t, GROUP_W), F32)
    head_state = (2, N_HEADS, HEAD_DIM, HEAD_DIM)
    lead = (n_sub,) if fuse else (None,)
    per_seq = lambda *shape: pl.BlockSpec(lead + shape, lambda b: (b,) + (0,) * len(shape))

    m_in = [z, z, z, z, gates, wp['m_bias'], wp['m_norm']]
    m_specs = [zspec(0), zspec(1), zspec(2), zspec(3), seq(LANES),
               _layer_spec(wp['m_bias'], l), _layer_spec(wp['m_norm'], l)]
    m_kinds = ['rows'] * 5 + ['shared'] * 2
    r_in = [z, z, z, z, wp['r_decay'], wp['r_norm']]
    r_specs = [zspec(4), zspec(5), zspec(6), zspec(7),
               _layer_spec(wp['r_decay'], l), _layer_spec(wp['r_norm'], l)]
    r_kinds = ['rows'] * 4 + ['shared'] * 2
    a_in = [z, z, z]
    a_specs = [zspec(8), zspec(9), zspec(10)]
    a_kinds = ['rows'] * 3
    m_out, r_out, a_out = [out_rows], [out_rows], [out_rows]
    m_ospecs, r_ospecs, a_ospecs = [seq(GROUP_W)], [seq(GROUP_W)], [seq(GROUP_W)]
    mo_kinds, ro_kinds, ao_kinds = ['rows'], ['rows'], ['rows']
    a_lead = (n_sub,) if fuse else ()
    scratch = [pltpu.VMEM((N_HEADS, CHUNK, CHUNK), F32), pltpu.VMEM((CHUNK, LANES), F32),
               pltpu.VMEM((4 * N_HEADS, CHUNK), F32),
               pltpu.VMEM(a_lead + (t, GROUP_W), BF16), pltpu.VMEM(a_lead + (t, GROUP_W), BF16)]
    if has_state:
        assert not fuse
        m_in += list(st[0:2])
        m_specs += [_seq_layer_spec(a, l) for a in st[0:2]]
        r_in.append(st[2])
        r_specs.append(_seq_layer_spec(st[2], l))
        a_in += list(st[3:5])
        a_specs += [_seq_layer_spec(a, l) for a in st[3:5]]
        scratch += [pltpu.VMEM((t, GROUP_W), F32), pltpu.VMEM((t, GROUP_W), F32)]
    else:
        m_out += [jax.ShapeDtypeStruct((n_seq,) + head_state, F32),
                  jax.ShapeDtypeStruct((n_seq, 2, N_HEADS, HEAD_DIM, 1), F32),
                  jax.ShapeDtypeStruct((n_seq, nu, 1), F32)]
        m_ospecs += [per_seq(*head_state), per_seq(2, N_HEADS, HEAD_DIM, 1), per_seq(nu, 1)]
        mo_kinds += ['seq'] * 3
        r_out.append(jax.ShapeDtypeStruct((n_seq,) + head_state, F32))
        r_ospecs.append(per_seq(*head_state))
        ro_kinds.append('seq')
    a_in += [wp['a_lam'], wp['a_norm']]
    a_specs += [_layer_spec(wp['a_lam'], l), _layer_spec(wp['a_norm'], l)]
    a_kinds += ['shared'] * 2
    n_chunk = t // CHUNK
    call = functools.partial(pl.pallas_call, grid=(n_seq // n_sub,), compiler_params=_params(("arbitrary",)))
    if fuse:
        res = call(
            functools.partial(_mixers_kernel, n_chunk=n_chunk, t=t, has_state=has_state, lam_init=lam_init,
                              n_in=(len(m_in), len(r_in), len(a_in)), n_out=(len(m_out), len(r_out), len(a_out)),
                              kinds=tuple(m_kinds + r_kinds + a_kinds + mo_kinds + ro_kinds + ao_kinds),
                              n_sub=n_sub),
            in_specs=m_specs + r_specs + a_specs, out_specs=m_ospecs + r_ospecs + a_ospecs,
            out_shape=m_out + r_out + a_out, scratch_shapes=scratch, name="mixers",
        )(*m_in, *r_in, *a_in)
        return res[:len(m_out)], res[len(m_out):len(m_out) + len(r_out)], res[-1]
    m_res = call(functools.partial(_mlstm_kernel, n_chunk=n_chunk, has_state=has_state),
                 in_specs=m_specs, out_specs=m_ospecs, out_shape=m_out, name="mlstm")(*m_in)
    r_res = call(functools.partial(_ret_kernel, n_chunk=n_chunk, has_state=has_state),
                 in_specs=r_specs, out_specs=r_ospecs, out_shape=r_out, scratch_shapes=scratch[0:3],
                 name="retention")(*r_in)
    a_res = call(functools.partial(_attn_kernel, t=t, has_cache=has_state, lam_init=lam_init),
                 in_specs=a_specs, out_specs=a_ospecs, out_shape=a_out, scratch_shapes=scratch[3:],
                 name="diff_attention")(*a_in)
    return m_res, r_res, a_res[0]


def _cmul(ar, ai, br, bi):
    return ar * br - ai * bi, ar * bi + ai * br


def _lam_bar(lre, lim, dt):
    mag = jnp.exp(lre * dt)
    return mag * jnp.cos(lim * dt), mag * jnp.sin(lim * dt)


def _cpow_int(br, bi, e, n_bits):
    pr = jnp.ones(e.shape, F32)
    pi = jnp.zeros(e.shape, F32)
    for bit in range(n_bits):
        on = ((e >> bit) & 1) == 1
        qr, qi = _cmul(pr, pi, br, bi)
        pr, pi = jnp.where(on, qr, pr), jnp.where(on, qi, pi)
        if bit + 1 < n_bits:
            br, bi = _cmul(br, bi, br, bi)
    return pr, pi


def _zoh_coef(lre, lim, br, bi):
    den = lre * lre + lim * lim
    return ((br - 1.0) * lre + bi * lim) / den, (bi * lre - (br - 1.0) * lim) / den


def _toeplitz_rows(kall):
    n = S5_L * S5_GC
    lane = lax.broadcasted_iota(jnp.int32, (S5_GC, n), 1)
    pieces = []
    for s in range(S5_L):
        shifted = kall if s == 0 else pltpu.roll(kall, S5_GC * s, 1)
        pieces.append(jnp.where(lane >= S5_GC * s, shifted, 0.0))
    return jnp.concatenate(pieces, axis=0)


def _s5_prep_group(lam_c_re, lam_c_im, lam_r_re, lam_r_im, ldt_ref, b_t_re, b_t_im, b_re, b_im,
                   c_re, c_im, c_t_re, c_t_im, d_ref, t_ref, bcf_ref, bcb_ref, ccf_ref, ccb_ref, lam_ref):
    n = S5_L * S5_GC
    lane_blk = lax.broadcasted_iota(jnp.int32, (S5_P, n), 1) // S5_GC
    bits = S5_L.bit_length()

    def bases(d):
        dt = jnp.exp(ldt_ref[d])
        return (_lam_bar(lam_c_re[d], lam_c_im[d], dt), _lam_bar(lam_r_re[d], lam_r_im[d], dt))

    def tile_rows(x):
        return jnp.concatenate([x] * S5_L, axis=0)

    def rows_pow(br, bi, reverse):
        s_idx = lax.broadcasted_iota(jnp.int32, (S5_L, S5_P), 0)
        pr, pi = _cpow_int(br, bi, S5_L - 1 - s_idx if reverse else s_idx, bits)
        rep = lambda p: jnp.concatenate(
            [jnp.broadcast_to(p[s:s + 1, :], (S5_GC, S5_P)) for s in range(S5_L)], axis=0)
        return rep(pr), rep(pi)

    def tile_lanes(ref, d):
        return jnp.concatenate([ref[d]] * S5_L, axis=1)

    ct_re, ct_im = [tile_lanes(c_t_re, d) for d in range(2)], [tile_lanes(c_t_im, d) for d in range(2)]

    (cbr, cbi), (rbr, rbi) = bases(0)
    kr, ki = _zoh_coef(lam_r_re[0], lam_r_im[0], rbr, rbi)
    bbt_re, bbt_im = _cmul(kr, ki, b_t_re[0], b_t_im[0])
    pr, pi = _cpow_int(cbr, cbi, lane_blk, bits)
    cl_re, cl_im = _cmul(ct_re[0], ct_im[0], pr, pi)
    t_f = _toeplitz_rows(_dot(bbt_re, cl_re, HI) - _dot(bbt_im, cl_im, HI))
    pr, pi = rows_pow(rbr, rbi, True)
    re, im = _cmul(tile_rows(bbt_re), tile_rows(bbt_im), pr, pi)
    bcf_ref[...] = jnp.concatenate([re, im], axis=1).astype(BF16)
    re, im = _cmul(cl_re, cl_im, cbr, cbi)
    ccf_ref[...] = jnp.concatenate([re, -im], axis=0).astype(BF16)
    pr, pi = _cpow_int(rbr, rbi, jnp.full((1, S5_P), S5_L, jnp.int32), bits)
    lam_ref[0:1, :] = jnp.concatenate([pr, pr], axis=1)
    lam_ref[1:2, :] = jnp.concatenate([-pi, pi], axis=1)

    (cbr, cbi), (rbr, rbi) = bases(1)
    kr, ki = _zoh_coef(lam_c_re[1], lam_c_im[1], cbr, cbi)
    bb_re, bb_im = _cmul(kr, ki, tile_lanes(b_re, 1), tile_lanes(b_im, 1))
    pr, pi = _cpow_int(cbr, cbi, lane_blk, bits)
    bl_re, bl_im = _cmul(bb_re, bb_im, pr, pi)
    t_b = _toeplitz_rows(_dot(c_re[1], bl_re, HI) - _dot(c_im[1], bl_im, HI)).T
    kr, ki = _zoh_coef(lam_r_re[1], lam_r_im[1], rbr, rbi)
    bbt_re, bbt_im = _cmul(kr, ki, b_t_re[1], b_t_im[1])
    pr, pi = rows_pow(rbr, rbi, False)
    re, im = _cmul(tile_rows(bbt_re), tile_rows(bbt_im), pr, pi)
    bcb_ref[...] = jnp.concatenate([re, im], axis=1).astype(BF16)
    pr, pi = _cpow_int(cbr, cbi, S5_L - lane_blk, bits)
    re, im = _cmul(ct_re[1], ct_im[1], pr, pi)
    ccb_ref[...] = jnp.concatenate([re, -im], axis=0).astype(BF16)
    pr, pi = _cpow_int(rbr, rbi, jnp.full((1, S5_P), S5_L, jnp.int32), bits)
    lam_ref[2:3, :] = jnp.concatenate([pr, pr], axis=1)
    lam_ref[3:4, :] = jnp.concatenate([-pi, pi], axis=1)
    lam_ref[4:8, :] = jnp.zeros((4, 2 * S5_P), F32)

    eye = (lax.broadcasted_iota(jnp.int32, (n, n), 0) == lax.broadcasted_iota(jnp.int32, (n, n), 1))
    d_diag = jnp.concatenate([d_ref[...]] * S5_L, axis=1)
    t_ref[...] = (t_f + t_b + jnp.where(eye, d_diag, 0.0)).astype(BF16)


def _s5_prep_kernel(*refs, n_grp):
    ins, outs = refs[:14], refs[14:]
    for gi in range(n_grp):
        _s5_prep_group(*[r.at[:, gi] for r in ins[:13]], ins[13].at[gi], *[r.at[gi] for r in outs])


def _s5_prep(lam_re, lam_im, log_dt, b_re, b_im, c_re, c_im, s5_d):
    n = S5_L * S5_GC
    swap = lambda a: jnp.swapaxes(a, -1, -2)
    ins = [lam_re[..., :, None], lam_im[..., :, None], lam_re[..., None, :], lam_im[..., None, :],
           log_dt[..., None, None], swap(b_re), swap(b_im), b_re, b_im, c_re, c_im, swap(c_re), swap(c_im)]
    d_grp = s5_d.reshape(DEPTH, S5_G, 1, S5_GC)

    n_grp = 4

    def dir_spec(a):
        return pl.BlockSpec((None, 2, n_grp) + a.shape[3:], lambda l, g: (l, 0, g, 0, 0))

    def out(rows, cols, dtype):
        return (pl.BlockSpec((None, n_grp, rows, cols), lambda l, g: (l, g, 0, 0)),
                jax.ShapeDtypeStruct((DEPTH, S5_G, rows, cols), dtype))

    outs = [out(n, n, BF16), out(n, 2 * S5_P, BF16), out(n, 2 * S5_P, BF16),
            out(2 * S5_P, n, BF16), out(2 * S5_P, n, BF16), out(8, 2 * S5_P, F32)]
    return pl.pallas_call(
        functools.partial(_s5_prep_kernel, n_grp=n_grp),
        grid=(DEPTH, S5_G // n_grp),
        in_specs=[dir_spec(a) for a in ins] + [pl.BlockSpec((None, n_grp, 1, S5_GC), lambda l, g: (l, g, 0, 0))],
        out_specs=[o[0] for o in outs], out_shape=[o[1] for o in outs],
        compiler_params=_params(("arbitrary", "arbitrary")), name="s5_prep",
    )(*ins, d_grp)


def _block_transpose(a):
    n = len(a)
    blk = lax.broadcasted_iota(jnp.int32, (1, a[0].shape[1]), 1) // S5_GC
    a = list(a)
    bit = 1
    while bit < n:
        upper = (blk & bit) != 0
        for i in range(n):
            if i & bit == 0:
                j = i | bit
                lo, hi = a[i], a[j]
                a[i] = jnp.where(upper, pltpu.roll(hi, bit * S5_GC, 1), lo)
                a[j] = jnp.where(upper, hi, pltpu.roll(lo, LANES - bit * S5_GC, 1))
        bit *= 2
    return a


def _s5_kernel(*refs, n_seq, n_k, has_state):
    if has_state:
        (u_ref, t_ref, bcf_ref, bcb_ref, ccf_ref, ccb_ref, lam_ref, x0_ref,
         y_ref, ug_ref, yg_ref, inj_f, inj_b, inj_s, xin_f, xin_b) = refs
    else:
        (u_ref, t_ref, bcf_ref, bcb_ref, ccf_ref, ccb_ref, lam_ref,
         y_ref, xf_ref, xb_ref, ug_ref, yg_ref, inj_f, inj_b, inj_s, xin_f, xin_b) = refs
    ng = S5_G // 2
    r = n_seq * n_k
    tok = lambda i: pl.ds(i, r, stride=S5_L)
    u_lo = _block_transpose([u_ref[tok(i), :] for i in range(ng)])
    u_hi = _block_transpose([u_ref[tok(i), :] for i in range(ng, S5_L)])
    for g in range(ng):
        ug_ref[g] = jnp.concatenate([u_lo[g], u_hi[g]], axis=1).astype(BF16)
        inj_f[g] = _dot(ug_ref[g], bcf_ref[g])
        inj_b[g] = _dot(ug_ref[g], bcb_ref[g])

    for d, (inj, xin) in enumerate(((inj_f, xin_f), (inj_b, xin_b))):
        a = [lam_ref[g, 2 * d:2 * d + 1, :] for g in range(ng)]
        bs = [lam_ref[g, 2 * d + 1:2 * d + 2, :] for g in range(ng)]
        for g in range(ng):
            inj_s[g] = pltpu.roll(inj[g], S5_P, 1)
        x = [x0_ref[d, g] if has_state else jnp.zeros((n_seq, 2 * S5_P), F32) for g in range(ng)]
        xs = [pltpu.roll(v, S5_P, 1) for v in x]
        for k in (range(n_k) if d == 0 else range(n_k - 1, -1, -1)):
            rows = pl.ds(k, n_seq, stride=n_k)
            for g in range(ng):
                xin[g, rows, :] = x[g]
                x[g], xs[g] = (a[g] * x[g] + bs[g] * xs[g] + inj[g, rows, :],
                               a[g] * xs[g] - bs[g] * x[g] + inj_s[g, rows, :])
        if not has_state:
            for g in range(ng):
                (xf_ref if d == 0 else xb_ref)[g] = x[g]

    for g in range(ng):
        yg_ref[g] = (_dot(ug_ref[g], t_ref[g]) + _dot(xin_f[g].astype(BF16), ccf_ref[g])
                     + _dot(xin_b[g].astype(BF16), ccb_ref[g]))
    for half in range(S5_L // ng):
        y_i = _block_transpose([yg_ref[g, :, LANES * half:LANES * (half + 1)] for g in range(ng)])
        for i in range(ng):
            y_ref[tok(ng * half + i), :] = y_i[i]


def _s5(u_halves, ops, l, n_seq, n_k, x0):
    has_state = x0 is not None
    n = S5_L * S5_GC
    r = n_seq * n_k
    w = 2 * S5_P
    ng = S5_G // 2
    half_spec = pl.BlockSpec((None, r * S5_L, LANES), lambda j: (j, 0, 0))
    op_spec = lambda a: pl.BlockSpec((None, ng) + a.shape[2:], lambda j: (l, j) + (0,) * (a.ndim - 2))
    args = [u_halves] + list(ops)
    in_specs = [half_spec] + [op_spec(a) for a in ops]
    if has_state:
        args.append(x0)
        in_specs.append(pl.BlockSpec((None, 2, ng, n_seq, w), lambda j: (l, 0, j, 0, 0)))
    out_shape = [jax.ShapeDtypeStruct((2, r * S5_L, LANES), F32)]
    out_specs = [half_spec]
    if not has_state:
        out_shape += [jax.ShapeDtypeStruct((S5_G, n_seq, w), F32)] * 2
        out_specs += [pl.BlockSpec((ng, n_seq, w), lambda j: (j, 0, 0))] * 2
    return pl.pallas_call(
        functools.partial(_s5_kernel, n_seq=n_seq, n_k=n_k, has_state=has_state),
        grid=(2,), in_specs=in_specs, out_specs=out_specs, out_shape=out_shape,
        scratch_shapes=[pltpu.VMEM((ng, r, n), BF16), pltpu.VMEM((ng, r, n), F32)]
        + [pltpu.VMEM((ng, r, w), F32)] * 5,
        compiler_params=_params(("arbitrary",)), name="s5_scan",
    )(*args)


def _out_ffn_kernel(x_ref, m_ref, r_ref, a_ref, ya_ref, yb_ref, mod_ref, wglu_ref, bglu_ref, wout_ref,
                    n_post_ref, n_pre_ref, n_fpost_ref, wg_ref, wu_ref, wd_ref, o_ref):
    def mod(i):
        return mod_ref[:, i * D_MODEL:(i + 1) * D_MODEL]

    y = jnp.concatenate([ya_ref[...], yb_ref[...]], axis=1)
    gs = 0.5 * y * (1.0 + jnp.tanh(math.sqrt(2.0 / math.pi) * (y + 0.044715 * (y * y * y))))
    s_out = gs * jax.nn.sigmoid(_dot(gs.astype(BF16), wglu_ref[...]) + bglu_ref[...])
    mixed = jnp.concatenate([m_ref[...], r_ref[...], a_ref[...], s_out], axis=1).astype(BF16)
    x1 = x_ref[...] + mod(2) * _rms(_dot(mixed, wout_ref[...]), n_post_ref[...])
    h = (_rms(x1, n_pre_ref[...]) * (1.0 + mod(4)) + mod(3)).astype(BF16)
    act = (_silu(_dot(h, wg_ref[...])) * _dot(h, wu_ref[...])).astype(BF16)
    o_ref[...] = x1 + mod(5) * _rms(_dot(act, wd_ref[...]), n_fpost_ref[...])


def _out_ffn(x, m_out, r_out, a_out, y_halves, mod, l, mod_row, wp):
    n = x.shape[0]
    row = lambda w: pl.BlockSpec((ROW_TILE, w), lambda i: (i, 0))
    half = lambda j: pl.BlockSpec((None, ROW_TILE, LANES), lambda i: (j, i, 0))
    params = [wp[k] for k in ('w_glu', 'b_glu', 'w_out', 'n_mix_post', 'n_ffn_pre', 'n_ffn_post',
                              'w_gate', 'w_up', 'w_down')]
    return pl.pallas_call(
        _out_ffn_kernel,
        grid=(n // ROW_TILE,),
        in_specs=[row(D_MODEL), row(GROUP_W), row(GROUP_W), row(GROUP_W), half(0), half(1),
                  _mod_spec(l, mod_row)] + [_layer_spec(a, l, pipeline_mode=pl.Buffered(1)) for a in params],
        out_specs=row(D_MODEL),
        out_shape=jax.ShapeDtypeStruct((n, D_MODEL), F32),
        compiler_params=_params(("arbitrary",)), name="out_ffn",
    )(x, m_out, r_out, a_out, y_halves, y_halves, mod, *params)


def _layer(x, l, n_seq, t, mod, mod_row, wp, s5_ops, lam_init, st):
    ctx = st is None
    res = _in_proj(x, mod, l, mod_row, wp['n_mix_pre'], wp['w_in'], emit_kv=ctx)
    z, gates = res[0], res[2]
    m_res, r_res, a_out = _mixers(z, gates, wp, l, n_seq, t, None if ctx else st[0:5], lam_init, fuse=ctx)
    s_res = _s5(res[1], s5_ops, l, n_seq, t // S5_L, None if ctx else st[5])
    x_new = _out_ffn(x, m_res[0], r_res[0], a_out, s_res[0], mod, l, mod_row, wp)
    if not ctx:
        return x_new, None
    return x_new, (m_res[1], m_res[2], m_res[3], r_res[1], res[3], res[4], s_res[1], s_res[2])


def _pad_to(a, rows, cols=LANES):
    pad = [(0, 0)] * (a.ndim - 2) + [(0, rows - a.shape[-2]), (0, cols - a.shape[-1])]
    return jnp.pad(a, pad)


def kernel(x_prompt, x_sample, state_mlstm_C, state_mlstm_n, state_mlstm_m, state_ret, cache_diff_k, cache_diff_v, state_s5_re, state_s5_im, c, c_ctx, w_ada, b_ada, n_mix_pre, n_mix_post, n_ffn_pre, n_ffn_post, w_in, w_out, m_gate_bias, m_norm, r_decay_logit, r_norm, a_lam_q1, a_lam_k1, a_lam_q2, a_lam_k2, a_norm, s5_lam_re, s5_lam_im, s5_log_dt, s5_b_re, s5_b_im, s5_c_re, s5_c_im, s5_d, s5_w_glu, s5_b_glu, w_ffn_gate, w_ffn_up, w_ffn_down):
    n_ctx, t_ctx, _ = x_prompt.shape
    n_lat, t_lat, _ = x_sample.shape
    past = cache_diff_k.shape[2]

    cond8 = jnp.concatenate([c, c_ctx[None, :], jnp.zeros((8 - n_lat - 1, D_MODEL), F32)], axis=0)
    mod = _ada(cond8, w_ada, b_ada).reshape(DEPTH, 8, 1, 6 * D_MODEL)
    s5_ops = _s5_prep(s5_lam_re, s5_lam_im, s5_log_dt, s5_b_re, s5_b_im, s5_c_re, s5_c_im, s5_d)

    row = lambda a: a[:, None, :]
    wp = dict(
        w_in=jnp.pad(w_in.astype(BF16), ((0, 0), (0, 0), (0, N_MAIN + LANES - w_in.shape[-1]))), w_out=w_out.astype(BF16), w_glu=s5_w_glu.astype(BF16), b_glu=row(s5_b_glu),
        w_gate=w_ffn_gate.astype(BF16), w_up=w_ffn_up.astype(BF16), w_down=w_ffn_down.astype(BF16),
        n_mix_pre=row(n_mix_pre), n_mix_post=row(n_mix_post), n_ffn_pre=row(n_ffn_pre),
        n_ffn_post=row(n_ffn_post), m_norm=row(m_norm), r_norm=row(r_norm), a_norm=row(a_norm),
        m_bias=_pad_to(row(m_gate_bias), 1), r_decay=_pad_to(r_decay_logit, 8),
        a_lam=_pad_to(jnp.stack([a_lam_q1, a_lam_k1, a_lam_q2, a_lam_k2], axis=1), 8))
    lam_inits = [0.8 - 0.6 * math.exp(-0.3 * l) for l in range(DEPTH)]

    x = x_prompt.reshape(n_ctx * t_ctx, D_MODEL)
    new_states = []
    for l in range(DEPTH):
        x, st = _layer(x, l, n_ctx, t_ctx, mod, lambda i: n_lat, wp, s5_ops, lam_inits[l], None)
        new_states.append(st)
    y_prompt = x.reshape(n_ctx, t_ctx, D_MODEL)

    x0 = jnp.stack([state_s5_re, state_s5_im], axis=-2)
    x0 = x0.transpose(1, 2, 3, 0, 4, 5).reshape(DEPTH, 2, S5_G, n_lat, 2 * S5_P)
    s_ext0 = jnp.concatenate([state_mlstm_C, state_mlstm_n[..., None],
                              jnp.zeros(state_mlstm_n.shape + (LANES - HEAD_DIM - 1,), F32)], axis=-1)
    st = (s_ext0, state_mlstm_m.reshape(n_lat, DEPTH, 2 * N_HEADS, 1), state_ret,
          cache_diff_k.reshape(n_lat, DEPTH, past, GROUP_W), cache_diff_v.reshape(n_lat, DEPTH, past, GROUP_W), x0)
    x = x_sample.reshape(n_lat * t_lat, D_MODEL)
    tiles_per_seq = t_lat // ROW_TILE
    for l in range(DEPTH):
        x, _ = _layer(x, l, n_lat, t_lat, mod, lambda i: i // tiles_per_seq, wp, s5_ops, lam_inits[l], st)
    y_sample = x.reshape(n_lat, t_lat, D_MODEL)

    stack = lambda i: jnp.stack([s[i] for s in new_states], axis=1)
    kv = lambda i: jnp.stack([s[i].reshape(n_ctx, t_ctx, N_HEADS, HEAD_DIM) for s in new_states], axis=1)
    xs = jnp.stack([stack(6), stack(7)], axis=2)
    xs = xs.reshape(S5_G, DEPTH, 2, n_ctx, 2, S5_P).transpose(3, 1, 2, 0, 4, 5)
    return (y_prompt, y_sample, stack(0), stack(1)[..., 0], stack(2).reshape(n_ctx, DEPTH, 2, N_HEADS),
            stack(3), kv(4), kv(5), xs[..., 0, :], xs[..., 1, :])
```

```python
import functools
import math

import jax
import jax.numpy as jnp
from jax import lax
from jax.experimental import pallas as pl
from jax.experimental.pallas import tpu as pltpu

F32 = jnp.float32
BF16 = jnp.bfloat16
HI = lax.Precision.HIGHEST

D_MODEL = 1024
DEPTH = 2
GRID_W = 64
HEAD_DIM = 64
GROUP_W = 256
N_HEADS = 4
A_SUB = 32
S5_GC = 16
S5_G = 16
S5_P = 64
D_FF = 2816
ROPE_BASE = 10000.0
EPS = 1e-6
N_MAIN = 12 * GROUP_W
N_MIX = 11 * GROUP_W
LANES = 128
CHUNK = 256
IN_TILE = 1024
ROW_TILE = 512
S5_L = 16
NEG = -1e30
VMEM_LIMIT = 56 * 1024 * 1024


def _dot(a, b, precision=None):
    return jnp.dot(a, b, preferred_element_type=F32, precision=precision)


def _dot_nt(a, b):
    return lax.dot_general(a, b, (((1,), (1,)), ((), ())), preferred_element_type=F32)


def _log_sigmoid(x):
    return jnp.minimum(x, 0.0) - jnp.log(1.0 + jnp.exp(-jnp.abs(x)))


def _silu(x):
    return x * jax.nn.sigmoid(x)


def _rms(x, g):
    return x * lax.rsqrt(jnp.mean(x * x, axis=-1, keepdims=True) + EPS) * g


def _params(sem=None):
    return pltpu.CompilerParams(dimension_semantics=sem, vmem_limit_bytes=VMEM_LIMIT)


def _layer_spec(a, l, **kw):
    n = a.ndim - 1
    return pl.BlockSpec((None,) + a.shape[1:], lambda *_: (l,) + (0,) * n, **kw)


def _seq_layer_spec(a, l):
    n = a.ndim - 2
    return pl.BlockSpec((None, None) + a.shape[2:], lambda b: (b, l) + (0,) * n)


def _split_bf16(x):
    hi = x.astype(BF16)
    return hi, (x - hi.astype(F32)).astype(BF16)


def _ada_kernel(c_ref, w_ref, b_ref, o_ref):
    a_hi, a_lo = _split_bf16(_silu(c_ref[...]))
    w_hi, w_lo = _split_bf16(w_ref[...])
    o_ref[...] = _dot(a_hi, w_hi) + (_dot(a_lo, w_hi) + _dot(a_hi, w_lo)) + b_ref[...]


def _ada(cond8, w_ada, b_ada):
    tn = 1536
    return pl.pallas_call(
        _ada_kernel,
        grid=(DEPTH, 6 * D_MODEL // tn),
        in_specs=[pl.BlockSpec((8, D_MODEL), lambda l, j: (0, 0)),
                  pl.BlockSpec((None, D_MODEL, tn), lambda l, j: (l, 0, j)),
                  pl.BlockSpec((None, 1, tn), lambda l, j: (l, 0, j))],
        out_specs=pl.BlockSpec((None, 8, tn), lambda l, j: (l, 0, j)),
        out_shape=jax.ShapeDtypeStruct((DEPTH, 8, 6 * D_MODEL), F32),
        compiler_params=_params(("arbitrary", "arbitrary")),
        name="ada",
    )(cond8, w_ada, b_ada.reshape(DEPTH, 1, 6 * D_MODEL))


def _in_proj_kernel(x_ref, mod_ref, g_ref, w0_ref, w1_ref, w2_ref, w3_ref, z_ref, u_ref, gate_ref, *kv_refs):
    h = _rms(x_ref[...], g_ref[...]) * (1.0 + mod_ref[:, D_MODEL:2 * D_MODEL]) + mod_ref[:, 0:D_MODEL]
    hb = h.astype(BF16)
    n_gate = 4 * N_HEADS
    n_head = 4 * GROUP_W
    z_ref[:, 0:n_head] = _dot(hb, w0_ref[...]).astype(BF16)
    tail = jnp.concatenate([_dot(hb, w1_ref[...]), _dot(hb, w2_ref[...]), _dot(hb, w3_ref[:, 0:LANES])], axis=1)
    gate_ref[...] = tail[:, 0:LANES]
    rest = tail[:, n_gate:n_gate + N_MAIN - n_head]
    z_ref[:, n_head:] = rest[:, 0:N_MIX - n_head].astype(BF16)
    u_ref[0] = rest[:, N_MIX - n_head:N_MIX - n_head + LANES]
    u_ref[1] = rest[:, N_MIX - n_head + LANES:]
    if kv_refs:
        kv_refs[0][...] = rest[:, 9 * GROUP_W - n_head:10 * GROUP_W - n_head]
        kv_refs[1][...] = rest[:, 10 * GROUP_W - n_head:11 * GROUP_W - n_head]


def _mod_spec(l, mod_row, tile):
    return pl.BlockSpec((None, None, 1, 6 * D_MODEL), lambda i: (l, mod_row(i * tile), 0, 0))


def _in_proj(x, mod, l, mod_row, gain, w, emit_kv):
    n = x.shape[0]
    row = lambda width: pl.BlockSpec((IN_TILE, width), lambda i: (i, 0))
    w_blk = lambda j: pl.BlockSpec((None, D_MODEL, D_MODEL), lambda i: (l, 0, j), pipeline_mode=pl.Buffered(1))
    out_specs = [row(N_MIX), pl.BlockSpec((2, IN_TILE, LANES), lambda i: (0, i, 0)), row(LANES)]
    out_specs += [row(GROUP_W)] * (2 if emit_kv else 0)
    out_shape = [jax.ShapeDtypeStruct((n, N_MIX), BF16), jax.ShapeDtypeStruct((2, n, LANES), F32),
                 jax.ShapeDtypeStruct((n, LANES), F32)]
    out_shape += [jax.ShapeDtypeStruct((n, GROUP_W), F32)] * (2 if emit_kv else 0)
    return pl.pallas_call(
        _in_proj_kernel,
        grid=(n // IN_TILE,),
        in_specs=[row(D_MODEL), _mod_spec(l, mod_row, IN_TILE), _layer_spec(gain, l)] + [w_blk(j) for j in range(4)],
        out_specs=out_specs, out_shape=out_shape,
        compiler_params=_params(("arbitrary",)),
        name="in_proj",
    )(x, mod, gain, w, w, w, w)


def _tri_masks(n):
    row = lax.broadcasted_iota(jnp.int32, (n, n), 0)
    col = lax.broadcasted_iota(jnp.int32, (n, n), 1)
    return row >= col, row <= col


def _head_mean_matrix():
    r = lax.broadcasted_iota(jnp.int32, (GROUP_W, GROUP_W), 0) // HEAD_DIM
    c = lax.broadcasted_iota(jnp.int32, (GROUP_W, GROUP_W), 1) // HEAD_DIM
    return jnp.where(r == c, 1.0 / HEAD_DIM, 0.0).astype(BF16)


def _head_mean(x, j):
    hi, lo = _split_bf16(x)
    return _dot(hi, j) + _dot(lo, j)


def _head_groupnorm(x, g, j):
    xc = x - _head_mean(x, j)
    return xc * lax.rsqrt(_head_mean(xc * xc, j) + EPS) * g


def _head_rms(x, g, j):
    return x * lax.rsqrt(_head_mean(x * x, j) + EPS) * g


def _hs(h):
    return slice(HEAD_DIM * h, HEAD_DIM * (h + 1))


def _v_ext(v_ref, rows, hs):
    ones = jnp.ones((rows.stop - rows.start, HEAD_DIM), BF16)
    return jnp.concatenate([v_ref[rows, hs].astype(BF16), ones], axis=1)


def _scan_max(x, reverse):
    n = x.shape[1]
    lane = lax.broadcasted_iota(jnp.int32, x.shape, 1)
    sh = 1
    while sh < n:
        if reverse:
            x = jnp.maximum(x, jnp.where(lane < n - sh, pltpu.roll(x, n - sh, 1), NEG))
        else:
            x = jnp.maximum(x, jnp.where(lane >= sh, pltpu.roll(x, sh, 1), NEG))
        sh *= 2
    return x


def _ends(x, is_fwd):
    return jnp.where(is_fwd, x[:, x.shape[1] - 1:], x[:, 0:1])


def _mlstm_kernel(*refs, n_chunk, has_state):
    if has_state:
        q_ref, k_ref, v_ref, o_ref, g_ref, bias_ref, gain_ref, s0_ref, m0_ref, out_ref = refs
    else:
        q_ref, k_ref, v_ref, o_ref, g_ref, bias_ref, gain_ref, out_ref, cn_ref, nn_ref, mn_ref = refs
    L = CHUNK
    nu = 2 * N_HEADS
    tril, triu = _tri_masks(L)
    is_fwd = lax.broadcasted_iota(jnp.int32, (nu, 1), 0) < N_HEADS

    b8, w8, mcum8 = [], [], []
    for c in range(n_chunk):
        p_t = (g_ref[c * L:(c + 1) * L, :] + bias_ref[...]).T
        lf = _log_sigmoid(p_t[nu:2 * nu, :])
        b = jnp.where(is_fwd, _dot(lf, triu.astype(F32), HI), _dot(lf, tril.astype(F32), HI))
        w = p_t[0:nu, :] - b
        b8.append(b)
        w8.append(w)
        mcum8.append(jnp.where(is_fwd, _scan_max(w, False), _scan_max(w, True)))

    m0 = m0_ref[...] if has_state else jnp.zeros((nu, 1), F32)
    m_in_f, m_in_b = [None] * n_chunk, [None] * n_chunk
    m = m0
    for c in range(n_chunk):
        m_in_f[c] = m
        m = (b8[c] + jnp.maximum(mcum8[c], m))[:, L - 1:]
    m_fin_f = m
    m = m0
    for c in range(n_chunk - 1, -1, -1):
        m_in_b[c] = m
        m = (b8[c] + jnp.maximum(mcum8[c], m))[:, 0:1]
    if not has_state:
        mn_ref[...] = jnp.where(is_fwd, m_fin_f, m)

    wk8, dec8, cols = [], [], []
    for c in range(n_chunk):
        m_in = jnp.where(is_fwd, m_in_f[c], m_in_b[c])
        g = jnp.maximum(mcum8[c], m_in)
        m_row = b8[c] + g
        m_new, b_last = _ends(m_row, is_fwd), _ends(b8[c], is_fwd)
        wk8.append(jnp.exp(b_last + w8[c] - m_new))
        dec8.append(jnp.exp(b_last + m_in - m_new))
        stats = jnp.concatenate([g, jnp.exp(m_in - g), jnp.exp(-m_row), jnp.zeros((LANES - 3 * nu, L), F32)], axis=0)
        cols.append(stats.T)

    k_t = [(k_ref[c * L:(c + 1) * L, :].astype(F32) * (HEAD_DIM ** -0.5)).T for c in range(n_chunk)]

    h_all = []
    for h in range(N_HEADS):
        hs = _hs(h)
        qb = [q_ref[c * L:(c + 1) * L, hs].astype(BF16) for c in range(n_chunk)]
        kt = [k_t[c][hs, :] for c in range(n_chunk)]
        vx = [_v_ext(v_ref, slice(c * L, (c + 1) * L), hs) for c in range(n_chunk)]
        qk = [_dot(qb[c], kt[c].astype(BF16)) for c in range(n_chunk)]
        h_sum = [None] * n_chunk
        for d in range(2):
            tri = tril if d == 0 else triu
            j = N_HEADS * d + h
            s_ext = s0_ref[d, h] if has_state else None
            order = range(n_chunk) if d == 0 else range(n_chunk - 1, -1, -1)
            for ci, c in enumerate(order):
                wgt = jnp.exp(jnp.where(tri, w8[c][j:j + 1, :] - cols[c][:, j:j + 1], NEG))
                tot = _dot((qk[c] * wgt).astype(BF16), vx[c])
                if s_ext is not None:
                    tot = tot + cols[c][:, nu + j:nu + j + 1] * _dot(qb[c], s_ext.astype(BF16))
                den = jnp.maximum(jnp.abs(tot[:, HEAD_DIM:HEAD_DIM + 1]), cols[c][:, 2 * nu + j:2 * nu + j + 1])
                hd = tot[:, 0:HEAD_DIM] / den
                h_sum[c] = hd if h_sum[c] is None else h_sum[c] + hd
                if (not has_state) or ci < n_chunk - 1:
                    upd = _dot((kt[c] * wk8[c][j:j + 1, :]).astype(BF16), vx[c])
                    s_ext = upd if s_ext is None else upd + dec8[c][j:j + 1, :] * s_ext
            if not has_state:
                cn_ref[d, h] = s_ext[:, 0:HEAD_DIM]
                nn_ref[d, h] = s_ext[:, HEAD_DIM:HEAD_DIM + 1]
        h_all.append(h_sum)
    jm = _head_mean_matrix()
    for c in range(n_chunk):
        rows = slice(c * L, (c + 1) * L)
        gated = jax.nn.sigmoid(o_ref[rows, :].astype(F32)) * jnp.concatenate([h_all[h][c] for h in range(N_HEADS)], axis=1)
        out_ref[rows, :] = _head_groupnorm(gated, gain_ref[...], jm)


def _ret_kernel(*refs, n_chunk, has_state):
    if has_state:
        q_ref, k_ref, v_ref, g_ref, lg_ref, gain_ref, s0_ref, out_ref, dmat_ref, col_ref, row_ref = refs
    else:
        q_ref, k_ref, v_ref, g_ref, lg_ref, gain_ref, out_ref, sn_ref, dmat_ref, col_ref, row_ref = refs
    L = CHUNK

    @pl.when(pl.program_id(0) == 0)
    def _():
        row = lax.broadcasted_iota(jnp.int32, (L, L), 0)
        col = lax.broadcasted_iota(jnp.int32, (L, L), 1)
        diff = (row - col).astype(F32)
        log_g = _log_sigmoid(lg_ref[...])
        pos_c = lax.broadcasted_iota(jnp.int32, (L, 1), 0).astype(F32)
        pos_r = lax.broadcasted_iota(jnp.int32, (1, L), 1).astype(F32)
        for h in range(N_HEADS):
            lgf = log_g[0:1, h:h + 1]
            lgb = log_g[1:2, h:h + 1]
            dmat_ref[h] = (jnp.where(diff >= 0, jnp.exp(jnp.maximum(diff, 0.0) * lgf), 0.0)
                           + jnp.where(diff <= 0, jnp.exp(jnp.maximum(-diff, 0.0) * lgb), 0.0))
            col_ref[:, h:h + 1] = jnp.exp((pos_c + 1.0) * lgf)
            col_ref[:, N_HEADS + h:N_HEADS + h + 1] = jnp.exp((L - pos_c) * lgb)
            row_ref[h:h + 1, :] = jnp.exp((L - 1.0 - pos_r) * lgf)
            row_ref[N_HEADS + h:N_HEADS + h + 1, :] = jnp.exp(pos_r * lgb)
            row_ref[2 * N_HEADS + h:2 * N_HEADS + h + 1, :] = jnp.exp(L * lgf) + jnp.zeros((1, L), F32)
            row_ref[3 * N_HEADS + h:3 * N_HEADS + h + 1, :] = jnp.exp(L * lgb) + jnp.zeros((1, L), F32)

    k_t = [(k_ref[c * L:(c + 1) * L, :].astype(F32) * (HEAD_DIM ** -0.5)).T for c in range(n_chunk)]
    o_all = []
    for h in range(N_HEADS):
        hs = _hs(h)
        zeta_f = row_ref[h:h + 1, :]
        zeta_b = row_ref[N_HEADS + h:N_HEADS + h + 1, :]
        gl_f = row_ref[2 * N_HEADS + h:2 * N_HEADS + h + 1, 0:1]
        gl_b = row_ref[3 * N_HEADS + h:3 * N_HEADS + h + 1, 0:1]
        kt = [k_t[c][hs, :] for c in range(n_chunk)]
        vb = [v_ref[c * L:(c + 1) * L, hs].astype(BF16) for c in range(n_chunk)]
        sf = [None] * (n_chunk + 1)
        sb = [None] * (n_chunk + 1)
        if has_state:
            sf[0] = s0_ref[0, h]
            sb[n_chunk] = s0_ref[1, h]
        for c in range(n_chunk):
            if (not has_state) or c < n_chunk - 1:
                upd = _dot((kt[c] * zeta_f).astype(BF16), vb[c])
                sf[c + 1] = upd if sf[c] is None else upd + gl_f * sf[c]
        for c in range(n_chunk - 1, -1, -1):
            if (not has_state) or c > 0:
                upd = _dot((kt[c] * zeta_b).astype(BF16), vb[c])
                sb[c] = upd if sb[c + 1] is None else upd + gl_b * sb[c + 1]
        if not has_state:
            sn_ref[0, h] = sf[n_chunk]
            sn_ref[1, h] = sb[0]
        o_h = []
        for c in range(n_chunk):
            qb = q_ref[c * L:(c + 1) * L, hs].astype(BF16)
            att = _dot(qb, kt[c].astype(BF16)) * dmat_ref[h]
            o = _dot(att.astype(BF16), vb[c])
            if has_state:
                o = o + col_ref[:, h:h + 1] * _dot(qb, sf[c].astype(BF16))
                o = o + col_ref[:, N_HEADS + h:N_HEADS + h + 1] * _dot(qb, sb[c + 1].astype(BF16))
            o_h.append(o)
        o_all.append(o_h)
    jm = _head_mean_matrix()
    for c in range(n_chunk):
        rows = slice(c * L, (c + 1) * L)
        o = jnp.concatenate([o_all[h][c] for h in range(N_HEADS)], axis=1)
        out_ref[rows, :] = _silu(g_ref[rows, :].astype(F32)) * _head_groupnorm(o, gain_ref[...], jm)


def _lambda(lam_ref, lam_init):
    v = lam_ref[...]
    s1 = jnp.sum(v[0:1, :] * v[1:2, :], axis=-1, keepdims=True)
    s2 = jnp.sum(v[2:3, :] * v[3:4, :], axis=-1, keepdims=True)
    return jnp.exp(s1) - jnp.exp(s2) + lam_init


def _first_half(shape):
    lane = lax.broadcasted_iota(jnp.int32, shape, 1)
    return (lane % (A_SUB // 2)) < (A_SUB // 4)


def _rope(x, cos, sin_signed):
    n = x.shape[1]
    first_half = _first_half(x.shape)
    partner = jnp.where(first_half, pltpu.roll(x, n - A_SUB // 4, 1), pltpu.roll(x, A_SUB // 4, 1))
    return x * cos + partner * sin_signed


def _attn_kernel(*refs, t, has_cache, lam_init):
    if has_cache:
        q_ref, k_ref, v_ref, kc_ref, vc_ref, lam_ref, gain_ref, out_ref, qr_ref, kr_ref, cos_ref, sin_ref = refs
    else:
        q_ref, k_ref, v_ref, lam_ref, gain_ref, out_ref, qr_ref, kr_ref = refs
    lam = _lambda(lam_ref, lam_init)
    scale = A_SUB ** -0.5
    n_qb = t // CHUNK

    if has_cache:
        @pl.when(pl.program_id(0) == 0)
        def _():
            lane = lax.broadcasted_iota(jnp.int32, (GRID_W, GROUP_W), 1)
            pos = lax.broadcasted_iota(jnp.int32, (GRID_W, GROUP_W), 0).astype(F32)
            n_freq = A_SUB // 4
            ang = pos * jnp.exp((lane % n_freq).astype(F32) * (-math.log(ROPE_BASE) / n_freq))
            cos_t = jnp.cos(ang)
            sin_t = jnp.where(_first_half((GRID_W, GROUP_W)), -jnp.sin(ang), jnp.sin(ang))
            row_axis = (lane % A_SUB) < (A_SUB // 2)
            for r in range(t // GRID_W):
                rows = slice(r * GRID_W, (r + 1) * GRID_W)
                cos_ref[rows, :] = jnp.where(row_axis, cos_t[r:r + 1, :], cos_t)
                sin_ref[rows, :] = jnp.where(row_axis, sin_t[r:r + 1, :], sin_t)

        kr_ref[...] = _rope(k_ref[...].astype(F32), cos_ref[...], sin_ref[...]).astype(BF16)
        qr_ref[...] = (_rope(q_ref[...].astype(F32), cos_ref[...], sin_ref[...]) * scale).astype(BF16)
    else:
        kr_ref[...] = k_ref[...].astype(BF16)
        qr_ref[...] = (q_ref[...].astype(F32) * scale).astype(BF16)

    o_all = []
    for h in range(N_HEADS):
        hs = _hs(h)
        vx = _v_ext(v_ref, slice(0, t), hs)
        if has_cache:
            vcx = _v_ext(vc_ref, slice(0, vc_ref.shape[0]), hs)
        o_h = []
        for qi in range(n_qb):
            rows = slice(qi * CHUNK, (qi + 1) * CHUNK)
            o_maps = []
            for m in range(2):
                ms = slice(HEAD_DIM * h + A_SUB * m, HEAD_DIM * h + A_SUB * (m + 1))
                qm = qr_ref[rows, ms]
                s_l = _dot_nt(qm, kr_ref[:, ms])
                mx = jnp.max(s_l, axis=-1, keepdims=True)
                if has_cache:
                    s_c = _dot_nt(qm, kc_ref[:, ms].astype(BF16))
                    mx = jnp.maximum(mx, jnp.max(s_c, axis=-1, keepdims=True))
                tot = _dot(jnp.exp(s_l - mx).astype(BF16), vx)
                if has_cache:
                    tot = tot + _dot(jnp.exp(s_c - mx).astype(BF16), vcx)
                o_maps.append(tot[:, 0:HEAD_DIM] / tot[:, HEAD_DIM:HEAD_DIM + 1])
            o_h.append(o_maps[0] - lam * o_maps[1])
        o_all.append(o_h)
    jm = _head_mean_matrix()
    for qi in range(n_qb):
        rows = slice(qi * CHUNK, (qi + 1) * CHUNK)
        o = jnp.concatenate([o_all[h][qi] for h in range(N_HEADS)], axis=1)
        out_ref[rows, :] = _head_rms(o, gain_ref[...], jm) * (1.0 - lam_init)


def _mixers_kernel(*refs, n_chunk, t, has_state, lam_init, n_in, n_out, kinds, n_sub):
    n_io = sum(n_in) + sum(n_out)
    io, scratch = refs[:n_io], refs[n_io:]
    for sub in range(n_sub):
        def view(r, kind):
            if kind == 'rows':
                return r.at[pl.ds(sub * t, t)]
            return r.at[sub] if kind == 'seq' else r

        v = [view(r, k) for r, k in zip(io, kinds)]
        ins, outs = v[:sum(n_in)], v[sum(n_in):]
        i0, o0 = 0, 0
        parts = []
        for k in range(3):
            parts.append((ins[i0:i0 + n_in[k]], outs[o0:o0 + n_out[k]]))
            i0 += n_in[k]
            o0 += n_out[k]
        a_scratch = [r.at[sub] for r in scratch[3:]]
        _mlstm_kernel(*parts[0][0], *parts[0][1], n_chunk=n_chunk, has_state=has_state)
        _ret_kernel(*parts[1][0], *parts[1][1], *scratch[0:3], n_chunk=n_chunk, has_state=has_state)
        _attn_kernel(*parts[2][0], *parts[2][1], *a_scratch, t=t, has_cache=has_state, lam_init=lam_init)


def _mixers(z, gates, wp, l, n_seq, t, st, lam_init, fuse):
    has_state = st is not None
    nu = 2 * N_HEADS
    n_sub = 2 if fuse else 1
    tt = n_sub * t
    seq = lambda width: pl.BlockSpec((tt, width), lambda b: (b, 0))
    zspec = lambda colblk: pl.BlockSpec((tt, GROUP_W), lambda b: (b, colblk))
    out_rows = jax.ShapeDtypeStruct((n_seq * t, GROUP_W), F32)
    head_state = (2, N_HEADS, HEAD_DIM, HEAD_DIM)
    lead = (n_sub,) if fuse else (None,)
    per_seq = lambda *shape: pl.BlockSpec(lead + shape, lambda b: (b,) + (0,) * len(shape))

    m_in = [z, z, z, z, gates, wp['m_bias'], wp['m_norm']]
    m_specs = [zspec(0), zspec(1), zspec(2), zspec(3), seq(LANES),
               _layer_spec(wp['m_bias'], l), _layer_spec(wp['m_norm'], l)]
    m_kinds = ['rows'] * 5 + ['shared'] * 2
    r_in = [z, z, z, z, wp['r_decay'], wp['r_norm']]
    r_specs = [zspec(4), zspec(5), zspec(6), zspec(7),
               _layer_spec(wp['r_decay'], l), _layer_spec(wp['r_norm'], l)]
    r_kinds = ['rows'] * 4 + ['shared'] * 2
    a_in = [z, z, z]
    a_specs = [zspec(8), zspec(9), zspec(10)]
    a_kinds = ['rows'] * 3
    m_out, r_out, a_out = [out_rows], [out_rows], [out_rows]
    m_ospecs, r_ospecs, a_ospecs = [seq(GROUP_W)], [seq(GROUP_W)], [seq(GROUP_W)]
    mo_kinds, ro_kinds, ao_kinds = ['rows'], ['rows'], ['rows']
    a_lead = (n_sub,) if fuse else ()
    scratch = [pltpu.VMEM((N_HEADS, CHUNK, CHUNK), F32), pltpu.VMEM((CHUNK, LANES), F32),
               pltpu.VMEM((4 * N_HEADS, CHUNK), F32),
               pltpu.VMEM(a_lead + (t, GROUP_W), BF16), pltpu.VMEM(a_lead + (t, GROUP_W), BF16)]
    if has_state:
        assert not fuse
        m_in += list(st[0:2])
        m_specs += [_seq_layer_spec(a, l) for a in st[0:2]]
        r_in.append(st[2])
        r_specs.append(_seq_layer_spec(st[2], l))
        a_in += list(st[3:5])
        a_specs += [_seq_layer_spec(a, l) for a in st[3:5]]
        scratch += [pltpu.VMEM((t, GROUP_W), F32), pltpu.VMEM((t, GROUP_W), F32)]
    else:
        m_out += [jax.ShapeDtypeStruct((n_seq,) + head_state, F32),
                  jax.ShapeDtypeStruct((n_seq, 2, N_HEADS, HEAD_DIM, 1), F32),
                  jax.ShapeDtypeStruct((n_seq, nu, 1), F32)]
        m_ospecs += [per_seq(*head_state), per_seq(2, N_HEADS, HEAD_DIM, 1), per_seq(nu, 1)]
        mo_kinds += ['seq'] * 3
        r_out.append(jax.ShapeDtypeStruct((n_seq,) + head_state, F32))
        r_ospecs.append(per_seq(*head_state))
        ro_kinds.append('seq')
    a_in += [wp['a_lam'], wp['a_norm']]
    a_specs += [_layer_spec(wp['a_lam'], l), _layer_spec(wp['a_norm'], l)]
    a_kinds += ['shared'] * 2
    n_chunk = t // CHUNK
    call = functools.partial(pl.pallas_call, grid=(n_seq // n_sub,), compiler_params=_params(("arbitrary",)))
    if fuse:
        res = call(
            functools.partial(_mixers_kernel, n_chunk=n_chunk, t=t, has_state=has_state, lam_init=lam_init,
                              n_in=(len(m_in), len(r_in), len(a_in)), n_out=(len(m_out), len(r_out), len(a_out)),
                              kinds=tuple(m_kinds + r_kinds + a_kinds + mo_kinds + ro_kinds + ao_kinds),
                              n_sub=n_sub),
            in_specs=m_specs + r_specs + a_specs, out_specs=m_ospecs + r_ospecs + a_ospecs,
            out_shape=m_out + r_out + a_out, scratch_shapes=scratch, name="mixers",
        )(*m_in, *r_in, *a_in)
        return res[:len(m_out)], res[len(m_out):len(m_out) + len(r_out)], res[-1]
    m_res = call(functools.partial(_mlstm_kernel, n_chunk=n_chunk, has_state=has_state),
                 in_specs=m_specs, out_specs=m_ospecs, out_shape=m_out, name="mlstm")(*m_in)
    r_res = call(functools.partial(_ret_kernel, n_chunk=n_chunk, has_state=has_state),
                 in_specs=r_specs, out_specs=r_ospecs, out_shape=r_out, scratch_shapes=scratch[0:3],
                 name="retention")(*r_in)
    a_res = call(functools.partial(_attn_kernel, t=t, has_cache=has_state, lam_init=lam_init),
                 in_specs=a_specs, out_specs=a_ospecs, out_shape=a_out, scratch_shapes=scratch[3:],
                 name="diff_attention")(*a_in)
    return m_res, r_res, a_res[0]


def _cmul(ar, ai, br, bi):
    return ar * br - ai * bi, ar * bi + ai * br


def _lam_bar(lre, lim, dt):
    mag = jnp.exp(lre * dt)
    return mag * jnp.cos(lim * dt), mag * jnp.sin(lim * dt)


def _cpow_int(br, bi, e, n_bits):
    pr = jnp.ones(e.shape, F32)
    pi = jnp.zeros(e.shape, F32)
    for bit in range(n_bits):
        on = ((e >> bit) & 1) == 1
        qr, qi = _cmul(pr, pi, br, bi)
        pr, pi = jnp.where(on, qr, pr), jnp.where(on, qi, pi)
        if bit + 1 < n_bits:
            br, bi = _cmul(br, bi, br, bi)
    return pr, pi


def _zoh_coef(lre, lim, br, bi):
    den = lre * lre + lim * lim
    return ((br - 1.0) * lre + bi * lim) / den, (bi * lre - (br - 1.0) * lim) / den


def _toeplitz_rows(kall):
    n = S5_L * S5_GC
    lane = lax.broadcasted_iota(jnp.int32, (S5_GC, n), 1)
    pieces = []
    for s in range(S5_L):
        shifted = kall if s == 0 else pltpu.roll(kall, S5_GC * s, 1)
        pieces.append(jnp.where(lane >= S5_GC * s, shifted, 0.0))
    return jnp.concatenate(pieces, axis=0)


def _s5_prep_group(lam_c_re, lam_c_im, lam_r_re, lam_r_im, ldt_ref, b_t_re, b_t_im, b_re, b_im,
                   c_re, c_im, c_t_re, c_t_im, d_ref, t_ref, bcf_ref, bcb_ref, ccf_ref, ccb_ref, lam_ref):
    n = S5_L * S5_GC
    lane_blk = lax.broadcasted_iota(jnp.int32, (S5_P, n), 1) // S5_GC
    bits = S5_L.bit_length()

    def bases(d):
        dt = jnp.exp(ldt_ref[d])
        return (_lam_bar(lam_c_re[d], lam_c_im[d], dt), _lam_bar(lam_r_re[d], lam_r_im[d], dt))

    def tile_rows(x):
        return jnp.concatenate([x] * S5_L, axis=0)

    def rows_pow(br, bi, reverse):
        s_idx = lax.broadcasted_iota(jnp.int32, (S5_L, S5_P), 0)
        pr, pi = _cpow_int(br, bi, S5_L - 1 - s_idx if reverse else s_idx, bits)
        rep = lambda p: jnp.concatenate(
            [jnp.broadcast_to(p[s:s + 1, :], (S5_GC, S5_P)) for s in range(S5_L)], axis=0)
        return rep(pr), rep(pi)

    def tile_lanes(ref, d):
        return jnp.concatenate([ref[d]] * S5_L, axis=1)

    ct_re, ct_im = [tile_lanes(c_t_re, d) for d in range(2)], [tile_lanes(c_t_im, d) for d in range(2)]

    (cbr, cbi), (rbr, rbi) = bases(0)
    kr, ki = _zoh_coef(lam_r_re[0], lam_r_im[0], rbr, rbi)
    bbt_re, bbt_im = _cmul(kr, ki, b_t_re[0], b_t_im[0])
    pr, pi = _cpow_int(cbr, cbi, lane_blk, bits)
    cl_re, cl_im = _cmul(ct_re[0], ct_im[0], pr, pi)
    t_f = _toeplitz_rows(_dot(bbt_re, cl_re, HI) - _dot(bbt_im, cl_im, HI))
    pr, pi = rows_pow(rbr, rbi, True)
    re, im = _cmul(tile_rows(bbt_re), tile_rows(bbt_im), pr, pi)
    bcf_ref[...] = jnp.concatenate([re, im], axis=1).astype(BF16)
    re, im = _cmul(cl_re, cl_im, cbr, cbi)
    ccf_ref[...] = jnp.concatenate([re, -im], axis=0).astype(BF16)
    pr, pi = _cpow_int(rbr, rbi, jnp.full((1, S5_P), S5_L, jnp.int32), bits)
    lam_ref[0:1, :] = jnp.concatenate([pr, pr], axis=1)
    lam_ref[1:2, :] = jnp.concatenate([-pi, pi], axis=1)

    (cbr, cbi), (rbr, rbi) = bases(1)
    kr, ki = _zoh_coef(lam_c_re[1], lam_c_im[1], cbr, cbi)
    bb_re, bb_im = _cmul(kr, ki, tile_lanes(b_re, 1), tile_lanes(b_im, 1))
    pr, pi = _cpow_int(cbr, cbi, lane_blk, bits)
    bl_re, bl_im = _cmul(bb_re, bb_im, pr, pi)
    t_b = _toeplitz_rows(_dot(c_re[1], bl_re, HI) - _dot(c_im[1], bl_im, HI)).T
    kr, ki = _zoh_coef(lam_r_re[1], lam_r_im[1], rbr, rbi)
    bbt_re, bbt_im = _cmul(kr, ki, b_t_re[1], b_t_im[1])
    pr, pi = rows_pow(rbr, rbi, False)
    re, im = _cmul(tile_rows(bbt_re), tile_rows(bbt_im), pr, pi)
    bcb_ref[...] = jnp.concatenate([re, im], axis=1).astype(BF16)
    pr, pi = _cpow_int(cbr, cbi, S5_L - lane_blk, bits)
    re, im = _cmul(ct_re[1], ct_im[1], pr, pi)
    ccb_ref[...] = jnp.concatenate([re, -im], axis=0).astype(BF16)
    pr, pi = _cpow_int(rbr, rbi, jnp.full((1, S5_P), S5_L, jnp.int32), bits)
    lam_ref[2:3, :] = jnp.concatenate([pr, pr], axis=1)
    lam_ref[3:4, :] = jnp.concatenate([-pi, pi], axis=1)
    lam_ref[4:8, :] = jnp.zeros((4, 2 * S5_P), F32)

    eye = (lax.broadcasted_iota(jnp.int32, (n, n), 0) == lax.broadcasted_iota(jnp.int32, (n, n), 1))
    d_diag = jnp.concatenate([d_ref[...]] * S5_L, axis=1)
    t_ref[...] = (t_f + t_b + jnp.where(eye, d_diag, 0.0)).astype(BF16)


def _s5_prep_kernel(*refs, n_grp):
    ins, outs = refs[:14], refs[14:]
    for gi in range(n_grp):
        _s5_prep_group(*[r.at[:, gi] for r in ins[:13]], ins[13].at[gi], *[r.at[gi] for r in outs])


def _s5_prep(lam_re, lam_im, log_dt, b_re, b_im, c_re, c_im, s5_d):
    n = S5_L * S5_GC
    swap = lambda a: jnp.swapaxes(a, -1, -2)
    ins = [lam_re[..., :, None], lam_im[..., :, None], lam_re[..., None, :], lam_im[..., None, :],
           log_dt[..., None, None], swap(b_re), swap(b_im), b_re, b_im, c_re, c_im, swap(c_re), swap(c_im)]
    d_grp = s5_d.reshape(DEPTH, S5_G, 1, S5_GC)

    n_grp = 4

    def dir_spec(a):
        return pl.BlockSpec((None, 2, n_grp) + a.shape[3:], lambda l, g: (l, 0, g, 0, 0))

    def out(rows, cols, dtype):
        return (pl.BlockSpec((None, n_grp, rows, cols), lambda l, g: (l, g, 0, 0)),
                jax.ShapeDtypeStruct((DEPTH, S5_G, rows, cols), dtype))

    outs = [out(n, n, BF16), out(n, 2 * S5_P, BF16), out(n, 2 * S5_P, BF16),
            out(2 * S5_P, n, BF16), out(2 * S5_P, n, BF16), out(8, 2 * S5_P, F32)]
    return pl.pallas_call(
        functools.partial(_s5_prep_kernel, n_grp=n_grp),
        grid=(DEPTH, S5_G // n_grp),
        in_specs=[dir_spec(a) for a in ins] + [pl.BlockSpec((None, n_grp, 1, S5_GC), lambda l, g: (l, g, 0, 0))],
        out_specs=[o[0] for o in outs], out_shape=[o[1] for o in outs],
        compiler_params=_params(("arbitrary", "arbitrary")), name="s5_prep",
    )(*ins, d_grp)


def _block_transpose(a):
    n = len(a)
    blk = lax.broadcasted_iota(jnp.int32, (1, a[0].shape[1]), 1) // S5_GC
    a = list(a)
    bit = 1
    while bit < n:
        upper = (blk & bit) != 0
        for i in range(n):
            if i & bit == 0:
                j = i | bit
                lo, hi = a[i], a[j]
                a[i] = jnp.where(upper, pltpu.roll(hi, bit * S5_GC, 1), lo)
                a[j] = jnp.where(upper, hi, pltpu.roll(lo, LANES - bit * S5_GC, 1))
        bit *= 2
    return a


def _s5_kernel(*refs, n_seq, n_k, has_state):
    if has_state:
        (u_ref, t_ref, bcf_ref, bcb_ref, ccf_ref, ccb_ref, lam_ref, x0_ref,
         y_ref, ug_ref, yg_ref, inj_f, inj_b, inj_s, xin_f, xin_b) = refs
    else:
        (u_ref, t_ref, bcf_ref, bcb_ref, ccf_ref, ccb_ref, lam_ref,
         y_ref, xf_ref, xb_ref, ug_ref, yg_ref, inj_f, inj_b, inj_s, xin_f, xin_b) = refs
    ng = S5_G // 2
    r = n_seq * n_k
    tok = lambda i: pl.ds(i, r, stride=S5_L)
    u_lo = _block_transpose([u_ref[tok(i), :] for i in range(ng)])
    u_hi = _block_transpose([u_ref[tok(i), :] for i in range(ng, S5_L)])
    for g in range(ng):
        ug_ref[g] = jnp.concatenate([u_lo[g], u_hi[g]], axis=1).astype(BF16)
        inj_f[g] = _dot(ug_ref[g], bcf_ref[g])
        inj_b[g] = _dot(ug_ref[g], bcb_ref[g])

    for d, (inj, xin) in enumerate(((inj_f, xin_f), (inj_b, xin_b))):
        a = [lam_ref[g, 2 * d:2 * d + 1, :] for g in range(ng)]
        bs = [lam_ref[g, 2 * d + 1:2 * d + 2, :] for g in range(ng)]
        for g in range(ng):
            inj_s[g] = pltpu.roll(inj[g], S5_P, 1)
        x = [x0_ref[d, g] if has_state else jnp.zeros((n_seq, 2 * S5_P), F32) for g in range(ng)]
        xs = [pltpu.roll(v, S5_P, 1) for v in x]
        for k in (range(n_k) if d == 0 else range(n_k - 1, -1, -1)):
            rows = pl.ds(k, n_seq, stride=n_k)
            for g in range(ng):
                xin[g, rows, :] = x[g]
                x[g], xs[g] = (a[g] * x[g] + bs[g] * xs[g] + inj[g, rows, :],
                               a[g] * xs[g] - bs[g] * x[g] + inj_s[g, rows, :])
        if not has_state:
            for g in range(ng):
                (xf_ref if d == 0 else xb_ref)[g] = x[g]

    for g in range(ng):
        yg_ref[g] = (_dot(ug_ref[g], t_ref[g]) + _dot(xin_f[g].astype(BF16), ccf_ref[g])
                     + _dot(xin_b[g].astype(BF16), ccb_ref[g]))
    for half in range(S5_L // ng):
        y_i = _block_transpose([yg_ref[g, :, LANES * half:LANES * (half + 1)] for g in range(ng)])
        for i in range(ng):
            y_ref[tok(ng * half + i), :] = y_i[i]


def _s5(u_halves, ops, l, n_seq, n_k, x0):
    has_state = x0 is not None
    n = S5_L * S5_GC
    r = n_seq * n_k
    w = 2 * S5_P
    ng = S5_G // 2
    half_spec = pl.BlockSpec((None, r * S5_L, LANES), lambda j: (j, 0, 0))
    op_spec = lambda a: pl.BlockSpec((None, ng) + a.shape[2:], lambda j: (l, j) + (0,) * (a.ndim - 2))
    args = [u_halves] + list(ops)
    in_specs = [half_spec] + [op_spec(a) for a in ops]
    if has_state:
        args.append(x0)
        in_specs.append(pl.BlockSpec((None, 2, ng, n_seq, w), lambda j: (l, 0, j, 0, 0)))
    out_shape = [jax.ShapeDtypeStruct((2, r * S5_L, LANES), F32)]
    out_specs = [half_spec]
    if not has_state:
        out_shape += [jax.ShapeDtypeStruct((S5_G, n_seq, w), F32)] * 2
        out_specs += [pl.BlockSpec((ng, n_seq, w), lambda j: (j, 0, 0))] * 2
    return pl.pallas_call(
        functools.partial(_s5_kernel, n_seq=n_seq, n_k=n_k, has_state=has_state),
        grid=(2,), in_specs=in_specs, out_specs=out_specs, out_shape=out_shape,
        scratch_shapes=[pltpu.VMEM((ng, r, n), BF16), pltpu.VMEM((ng, r, n), F32)]
        + [pltpu.VMEM((ng, r, w), F32)] * 5,
        compiler_params=_params(("arbitrary",)), name="s5_scan",
    )(*args)


def _out_ffn_kernel(x_ref, m_ref, r_ref, a_ref, ya_ref, yb_ref, mod_ref, wglu_ref, bglu_ref, wout_ref,
                    n_post_ref, n_pre_ref, n_fpost_ref, wg_ref, wu_ref, wd_ref, o_ref):
    def mod(i):
        return mod_ref[:, i * D_MODEL:(i + 1) * D_MODEL]

    y = jnp.concatenate([ya_ref[...], yb_ref[...]], axis=1)
    gs = 0.5 * y * (1.0 + jnp.tanh(math.sqrt(2.0 / math.pi) * (y + 0.044715 * (y * y * y))))
    s_out = gs * jax.nn.sigmoid(_dot(gs.astype(BF16), wglu_ref[...]) + bglu_ref[...])
    mixed = jnp.concatenate([m_ref[...], r_ref[...], a_ref[...], s_out], axis=1).astype(BF16)
    x1 = x_ref[...] + mod(2) * _rms(_dot(mixed, wout_ref[...]), n_post_ref[...])
    h = (_rms(x1, n_pre_ref[...]) * (1.0 + mod(4)) + mod(3)).astype(BF16)
    act = (_silu(_dot(h, wg_ref[...])) * _dot(h, wu_ref[...])).astype(BF16)
    o_ref[...] = x1 + mod(5) * _rms(_dot(act, wd_ref[...]), n_fpost_ref[...])


def _out_ffn(x, m_out, r_out, a_out, y_halves, mod, l, mod_row, wp):
    n = x.shape[0]
    row = lambda w: pl.BlockSpec((ROW_TILE, w), lambda i: (i, 0))
    half = lambda j: pl.BlockSpec((None, ROW_TILE, LANES), lambda i: (j, i, 0))
    params = [wp[k] for k in ('w_glu', 'b_glu', 'w_out', 'n_mix_post', 'n_ffn_pre', 'n_ffn_post',
                              'w_gate', 'w_up', 'w_down')]
    return pl.pallas_call(
        _out_ffn_kernel,
        grid=(n // ROW_TILE,),
        in_specs=[row(D_MODEL), row(GROUP_W), row(GROUP_W), row(GROUP_W), half(0), half(1),
                  _mod_spec(l, mod_row, ROW_TILE)] + [_layer_spec(a, l, pipeline_mode=pl.Buffered(1)) for a in params],
        out_specs=row(D_MODEL),
        out_shape=jax.ShapeDtypeStruct((n, D_MODEL), F32),
        compiler_params=_params(("arbitrary",)), name="out_ffn",
    )(x, m_out, r_out, a_out, y_halves, y_halves, mod, *params)


def _layer(x, l, n_seq, t, mod, mod_row, wp, s5_ops, lam_init, st):
    ctx = st is None
    res = _in_proj(x, mod, l, mod_row, wp['n_mix_pre'], wp['w_in'], emit_kv=ctx)
    z, gates = res[0], res[2]
    m_res, r_res, a_out = _mixers(z, gates, wp, l, n_seq, t, None if ctx else st[0:5], lam_init, fuse=ctx)
    s_res = _s5(res[1], s5_ops, l, n_seq, t // S5_L, None if ctx else st[5])
    x_new = _out_ffn(x, m_res[0], r_res[0], a_out, s_res[0], mod, l, mod_row, wp)
    if not ctx:
        return x_new, None
    return x_new, (m_res[1], m_res[2], m_res[3], r_res[1], res[3], res[4], s_res[1], s_res[2])


def _pad_to(a, rows, cols=LANES):
    pad = [(0, 0)] * (a.ndim - 2) + [(0, rows - a.shape[-2]), (0, cols - a.shape[-1])]
    return jnp.pad(a, pad)


def kernel(x_prompt, x_sample, state_mlstm_C, state_mlstm_n, state_mlstm_m, state_ret, cache_diff_k, cache_diff_v, state_s5_re, state_s5_im, c, c_ctx, w_ada, b_ada, n_mix_pre, n_mix_post, n_ffn_pre, n_ffn_post, w_in, w_out, m_gate_bias, m_norm, r_decay_logit, r_norm, a_lam_q1, a_lam_k1, a_lam_q2, a_lam_k2, a_norm, s5_lam_re, s5_lam_im, s5_log_dt, s5_b_re, s5_b_im, s5_c_re, s5_c_im, s5_d, s5_w_glu, s5_b_glu, w_ffn_gate, w_ffn_up, w_ffn_down):
    n_ctx, t_ctx, _ = x_prompt.shape
    n_lat, t_lat, _ = x_sample.shape
    past = cache_diff_k.shape[2]

    cond8 = jnp.concatenate([c, c_ctx[None, :], jnp.zeros((8 - n_lat - 1, D_MODEL), F32)], axis=0)
    mod = _ada(cond8, w_ada, b_ada).reshape(DEPTH, 8, 1, 6 * D_MODEL)
    s5_ops = _s5_prep(s5_lam_re, s5_lam_im, s5_log_dt, s5_b_re, s5_b_im, s5_c_re, s5_c_im, s5_d)

    row = lambda a: a[:, None, :]
    wp = dict(
        w_in=w_in.astype(BF16), w_out=w_out.astype(BF16), w_glu=s5_w_glu.astype(BF16), b_glu=row(s5_b_glu),
        w_gate=w_ffn_gate.astype(BF16), w_up=w_ffn_up.astype(BF16), w_down=w_ffn_down.astype(BF16),
        n_mix_pre=row(n_mix_pre), n_mix_post=row(n_mix_post), n_ffn_pre=row(n_ffn_pre),
        n_ffn_post=row(n_ffn_post), m_norm=row(m_norm), r_norm=row(r_norm), a_norm=row(a_norm),
        m_bias=_pad_to(row(m_gate_bias), 1), r_decay=_pad_to(r_decay_logit, 8),
        a_lam=_pad_to(jnp.stack([a_lam_q1, a_lam_k1, a_lam_q2, a_lam_k2], axis=1), 8))
    lam_inits = [0.8 - 0.6 * math.exp(-0.3 * l) for l in range(DEPTH)]

    x = x_prompt.reshape(n_ctx * t_ctx, D_MODEL)
    new_states = []
    for l in range(DEPTH):
        x, st = _layer(x, l, n_ctx, t_ctx, mod, lambda tok: n_lat, wp, s5_ops, lam_inits[l], None)
        new_states.append(st)
    y_prompt = x.reshape(n_ctx, t_ctx, D_MODEL)

    x0 = jnp.stack([state_s5_re, state_s5_im], axis=-2)
    x0 = x0.transpose(1, 2, 3, 0, 4, 5).reshape(DEPTH, 2, S5_G, n_lat, 2 * S5_P)
    s_ext0 = jnp.concatenate([state_mlstm_C, state_mlstm_n[..., None],
                              jnp.zeros(state_mlstm_n.shape + (LANES - HEAD_DIM - 1,), F32)], axis=-1)
    st = (s_ext0, state_mlstm_m.reshape(n_lat, DEPTH, 2 * N_HEADS, 1), state_ret,
          cache_diff_k.reshape(n_lat, DEPTH, past, GROUP_W), cache_diff_v.reshape(n_lat, DEPTH, past, GROUP_W), x0)
    x = x_sample.reshape(n_lat * t_lat, D_MODEL)
    for l in range(DEPTH):
        x, _ = _layer(x, l, n_lat, t_lat, mod, lambda tok: tok // t_lat, wp, s5_ops, lam_inits[l], st)
    y_sample = x.reshape(n_lat, t_lat, D_MODEL)

    stack = lambda i: jnp.stack([s[i] for s in new_states], axis=1)
    kv = lambda i: jnp.stack([s[i].reshape(n_ctx, t_ctx, N_HEADS, HEAD_DIM) for s in new_states], axis=1)
    xs = jnp.stack([stack(6), stack(7)], axis=2)
    xs = xs.reshape(S5_G, DEPTH, 2, n_ctx, 2, S5_P).transpose(3, 1, 2, 0, 4, 5)
    return (y_prompt, y_sample, stack(0), stack(1)[..., 0], stack(2).reshape(n_ctx, DEPTH, 2, N_HEADS),
            stack(3), kv(4), kv(5), xs[..., 0, :], xs[..., 1, :])
```

```python
import functools
import math

import jax
import jax.numpy as jnp
from jax import lax
from jax.experimental import pallas as pl
from jax.experimental.pallas import tpu as pltpu

F32 = jnp.float32
BF16 = jnp.bfloat16
HI = lax.Precision.HIGHEST

D_MODEL = 1024
DEPTH = 2
GRID_W = 64
HEAD_DIM = 64
GROUP_W = 256
N_HEADS = 4
A_SUB = 32
S5_GC = 16
S5_G = 16
S5_P = 64
D_FF = 2816
ROPE_BASE = 10000.0
EPS = 1e-6
N_MAIN = 12 * GROUP_W
N_MIX = 11 * GROUP_W
LANES = 128
CHUNK = 256
ROW_TILE = 512
S5_L = 16
NEG = -1e30
VMEM_LIMIT = 56 * 1024 * 1024


def _dot(a, b, precision=None):
    return jnp.dot(a, b, preferred_element_type=F32, precision=precision)


def _dot_nt(a, b):
    return lax.dot_general(a, b, (((1,), (1,)), ((), ())), preferred_element_type=F32)


def _log_sigmoid(x):
    return jnp.minimum(x, 0.0) - jnp.log(1.0 + jnp.exp(-jnp.abs(x)))


def _silu(x):
    return x * jax.nn.sigmoid(x)


def _rms(x, g):
    return x * lax.rsqrt(jnp.mean(x * x, axis=-1, keepdims=True) + EPS) * g


def _params(sem=None):
    return pltpu.CompilerParams(dimension_semantics=sem, vmem_limit_bytes=VMEM_LIMIT)


def _layer_spec(a, l, **kw):
    n = a.ndim - 1
    return pl.BlockSpec((None,) + a.shape[1:], lambda *_: (l,) + (0,) * n, **kw)


def _seq_layer_spec(a, l):
    n = a.ndim - 2
    return pl.BlockSpec((None, None) + a.shape[2:], lambda b: (b, l) + (0,) * n)


def _split_bf16(x):
    hi = x.astype(BF16)
    return hi, (x - hi.astype(F32)).astype(BF16)


def _ada_kernel(c_ref, w_ref, b_ref, o_ref):
    a_hi, a_lo = _split_bf16(_silu(c_ref[...]))
    w_hi, w_lo = _split_bf16(w_ref[...])
    o_ref[...] = _dot(a_hi, w_hi) + (_dot(a_lo, w_hi) + _dot(a_hi, w_lo)) + b_ref[...]


def _ada(cond8, w_ada, b_ada):
    tn = 1536
    return pl.pallas_call(
        _ada_kernel,
        grid=(DEPTH, 6 * D_MODEL // tn),
        in_specs=[pl.BlockSpec((8, D_MODEL), lambda l, j: (0, 0)),
                  pl.BlockSpec((None, D_MODEL, tn), lambda l, j: (l, 0, j)),
                  pl.BlockSpec((None, 1, tn), lambda l, j: (l, 0, j))],
        out_specs=pl.BlockSpec((None, 8, tn), lambda l, j: (l, 0, j)),
        out_shape=jax.ShapeDtypeStruct((DEPTH, 8, 6 * D_MODEL), F32),
        compiler_params=_params(("arbitrary", "arbitrary")),
        name="ada",
    )(cond8, w_ada, b_ada.reshape(DEPTH, 1, 6 * D_MODEL))


def _in_proj_kernel(x_ref, mod_ref, g_ref, w_ref, z_ref, u_ref, gate_ref, *kv_refs):
    h = _rms(x_ref[...], g_ref[...]) * (1.0 + mod_ref[:, D_MODEL:2 * D_MODEL]) + mod_ref[:, 0:D_MODEL]
    hb = h.astype(BF16)
    n_gate = 4 * N_HEADS
    n_head = 4 * GROUP_W
    z_ref[:, 0:n_head] = _dot(hb, w_ref[:, 0:n_head]).astype(BF16)
    tail = _dot(hb, w_ref[:, n_head:])
    gate_ref[...] = tail[:, 0:LANES]
    rest = tail[:, n_gate:n_gate + N_MAIN - n_head]
    z_ref[:, n_head:] = rest[:, 0:N_MIX - n_head].astype(BF16)
    u_ref[0] = rest[:, N_MIX - n_head:N_MIX - n_head + LANES]
    u_ref[1] = rest[:, N_MIX - n_head + LANES:]
    if kv_refs:
        kv_refs[0][...] = rest[:, 9 * GROUP_W - n_head:10 * GROUP_W - n_head]
        kv_refs[1][...] = rest[:, 10 * GROUP_W - n_head:11 * GROUP_W - n_head]


def _mod_spec(l, mod_row):
    return pl.BlockSpec((None, None, 1, 6 * D_MODEL), lambda i: (l, mod_row(i), 0, 0))


def _in_proj(x, mod, l, mod_row, gain, w, emit_kv):
    n = x.shape[0]
    row = lambda width: pl.BlockSpec((ROW_TILE, width), lambda i: (i, 0))
    out_specs = [row(N_MIX), pl.BlockSpec((2, ROW_TILE, LANES), lambda i: (0, i, 0)), row(LANES)]
    out_specs += [row(GROUP_W)] * (2 if emit_kv else 0)
    out_shape = [jax.ShapeDtypeStruct((n, N_MIX), BF16), jax.ShapeDtypeStruct((2, n, LANES), F32),
                 jax.ShapeDtypeStruct((n, LANES), F32)]
    out_shape += [jax.ShapeDtypeStruct((n, GROUP_W), F32)] * (2 if emit_kv else 0)
    return pl.pallas_call(
        _in_proj_kernel,
        grid=(n // ROW_TILE,),
        in_specs=[row(D_MODEL), _mod_spec(l, mod_row), _layer_spec(gain, l),
                  _layer_spec(w, l, pipeline_mode=pl.Buffered(1))],
        out_specs=out_specs, out_shape=out_shape,
        compiler_params=_params(("arbitrary",)),
        name="in_proj",
    )(x, mod, gain, w)


def _tri_masks(n):
    row = lax.broadcasted_iota(jnp.int32, (n, n), 0)
    col = lax.broadcasted_iota(jnp.int32, (n, n), 1)
    return row >= col, row <= col


def _head_mean_matrix():
    r = lax.broadcasted_iota(jnp.int32, (GROUP_W, GROUP_W), 0) // HEAD_DIM
    c = lax.broadcasted_iota(jnp.int32, (GROUP_W, GROUP_W), 1) // HEAD_DIM
    return jnp.where(r == c, 1.0 / HEAD_DIM, 0.0).astype(BF16)


def _head_mean(x, j):
    hi, lo = _split_bf16(x)
    return _dot(hi, j) + _dot(lo, j)


def _head_groupnorm(x, g, j):
    xc = x - _head_mean(x, j)
    return xc * lax.rsqrt(_head_mean(xc * xc, j) + EPS) * g


def _head_rms(x, g, j):
    return x * lax.rsqrt(_head_mean(x * x, j) + EPS) * g


def _hs(h):
    return slice(HEAD_DIM * h, HEAD_DIM * (h + 1))


def _v_ext(v_ref, rows, hs):
    ones = jnp.ones((rows.stop - rows.start, HEAD_DIM), BF16)
    return jnp.concatenate([v_ref[rows, hs].astype(BF16), ones], axis=1)


def _scan_max(x, reverse):
    n = x.shape[1]
    lane = lax.broadcasted_iota(jnp.int32, x.shape, 1)
    sh = 1
    while sh < n:
        if reverse:
            x = jnp.maximum(x, jnp.where(lane < n - sh, pltpu.roll(x, n - sh, 1), NEG))
        else:
            x = jnp.maximum(x, jnp.where(lane >= sh, pltpu.roll(x, sh, 1), NEG))
        sh *= 2
    return x


def _ends(x, is_fwd):
    return jnp.where(is_fwd, x[:, x.shape[1] - 1:], x[:, 0:1])


def _mlstm_kernel(*refs, n_chunk, has_state):
    if has_state:
        q_ref, k_ref, v_ref, o_ref, g_ref, bias_ref, gain_ref, s0_ref, m0_ref, out_ref = refs
    else:
        q_ref, k_ref, v_ref, o_ref, g_ref, bias_ref, gain_ref, out_ref, cn_ref, nn_ref, mn_ref = refs
    L = CHUNK
    nu = 2 * N_HEADS
    tril, triu = _tri_masks(L)
    is_fwd = lax.broadcasted_iota(jnp.int32, (nu, 1), 0) < N_HEADS

    b8, w8, mcum8 = [], [], []
    for c in range(n_chunk):
        p_t = (g_ref[c * L:(c + 1) * L, :] + bias_ref[...]).T
        lf = _log_sigmoid(p_t[nu:2 * nu, :])
        b = jnp.where(is_fwd, _dot(lf, triu.astype(F32), HI), _dot(lf, tril.astype(F32), HI))
        w = p_t[0:nu, :] - b
        b8.append(b)
        w8.append(w)
        mcum8.append(jnp.where(is_fwd, _scan_max(w, False), _scan_max(w, True)))

    m0 = m0_ref[...] if has_state else jnp.zeros((nu, 1), F32)
    m_in_f, m_in_b = [None] * n_chunk, [None] * n_chunk
    m = m0
    for c in range(n_chunk):
        m_in_f[c] = m
        m = (b8[c] + jnp.maximum(mcum8[c], m))[:, L - 1:]
    m_fin_f = m
    m = m0
    for c in range(n_chunk - 1, -1, -1):
        m_in_b[c] = m
        m = (b8[c] + jnp.maximum(mcum8[c], m))[:, 0:1]
    if not has_state:
        mn_ref[...] = jnp.where(is_fwd, m_fin_f, m)

    wk8, dec8, cols = [], [], []
    for c in range(n_chunk):
        m_in = jnp.where(is_fwd, m_in_f[c], m_in_b[c])
        g = jnp.maximum(mcum8[c], m_in)
        m_row = b8[c] + g
        m_new, b_last = _ends(m_row, is_fwd), _ends(b8[c], is_fwd)
        wk8.append(jnp.exp(b_last + w8[c] - m_new))
        dec8.append(jnp.exp(b_last + m_in - m_new))
        stats = jnp.concatenate([g, jnp.exp(m_in - g), jnp.exp(-m_row), jnp.zeros((LANES - 3 * nu, L), F32)], axis=0)
        cols.append(stats.T)

    k_t = [(k_ref[c * L:(c + 1) * L, :].astype(F32) * (HEAD_DIM ** -0.5)).T for c in range(n_chunk)]

    h_all = []
    for h in range(N_HEADS):
        hs = _hs(h)
        qb = [q_ref[c * L:(c + 1) * L, hs].astype(BF16) for c in range(n_chunk)]
        kt = [k_t[c][hs, :] for c in range(n_chunk)]
        vx = [_v_ext(v_ref, slice(c * L, (c + 1) * L), hs) for c in range(n_chunk)]
        qk = [_dot(qb[c], kt[c].astype(BF16)) for c in range(n_chunk)]
        h_sum = [None] * n_chunk
        for d in range(2):
            tri = tril if d == 0 else triu
            j = N_HEADS * d + h
            s_ext = s0_ref[d, h] if has_state else None
            order = range(n_chunk) if d == 0 else range(n_chunk - 1, -1, -1)
            for ci, c in enumerate(order):
                wgt = jnp.exp(jnp.where(tri, w8[c][j:j + 1, :] - cols[c][:, j:j + 1], NEG))
                tot = _dot((qk[c] * wgt).astype(BF16), vx[c])
                if s_ext is not None:
                    tot = tot + cols[c][:, nu + j:nu + j + 1] * _dot(qb[c], s_ext.astype(BF16))
                den = jnp.maximum(jnp.abs(tot[:, HEAD_DIM:HEAD_DIM + 1]), cols[c][:, 2 * nu + j:2 * nu + j + 1])
                hd = tot[:, 0:HEAD_DIM] / den
                h_sum[c] = hd if h_sum[c] is None else h_sum[c] + hd
                if (not has_state) or ci < n_chunk - 1:
                    upd = _dot((kt[c] * wk8[c][j:j + 1, :]).astype(BF16), vx[c])
                    s_ext = upd if s_ext is None else upd + dec8[c][j:j + 1, :] * s_ext
            if not has_state:
                cn_ref[d, h] = s_ext[:, 0:HEAD_DIM]
                nn_ref[d, h] = s_ext[:, HEAD_DIM:HEAD_DIM + 1]
        h_all.append(h_sum)
    jm = _head_mean_matrix()
    for c in range(n_chunk):
        rows = slice(c * L, (c + 1) * L)
        gated = jax.nn.sigmoid(o_ref[rows, :].astype(F32)) * jnp.concatenate([h_all[h][c] for h in range(N_HEADS)], axis=1)
        out_ref[rows, :] = _head_groupnorm(gated, gain_ref[...], jm)


def _ret_kernel(*refs, n_chunk, has_state):
    if has_state:
        q_ref, k_ref, v_ref, g_ref, lg_ref, gain_ref, s0_ref, out_ref, dmat_ref, col_ref, row_ref = refs
    else:
        q_ref, k_ref, v_ref, g_ref, lg_ref, gain_ref, out_ref, sn_ref, dmat_ref, col_ref, row_ref = refs
    L = CHUNK

    @pl.when(pl.program_id(0) == 0)
    def _():
        row = lax.broadcasted_iota(jnp.int32, (L, L), 0)
        col = lax.broadcasted_iota(jnp.int32, (L, L), 1)
        diff = (row - col).astype(F32)
        log_g = _log_sigmoid(lg_ref[...])
        pos_c = lax.broadcasted_iota(jnp.int32, (L, 1), 0).astype(F32)
        pos_r = lax.broadcasted_iota(jnp.int32, (1, L), 1).astype(F32)
        for h in range(N_HEADS):
            lgf = log_g[0:1, h:h + 1]
            lgb = log_g[1:2, h:h + 1]
            dmat_ref[h] = (jnp.where(diff >= 0, jnp.exp(jnp.maximum(diff, 0.0) * lgf), 0.0)
                           + jnp.where(diff <= 0, jnp.exp(jnp.maximum(-diff, 0.0) * lgb), 0.0))
            col_ref[:, h:h + 1] = jnp.exp((pos_c + 1.0) * lgf)
            col_ref[:, N_HEADS + h:N_HEADS + h + 1] = jnp.exp((L - pos_c) * lgb)
            row_ref[h:h + 1, :] = jnp.exp((L - 1.0 - pos_r) * lgf)
            row_ref[N_HEADS + h:N_HEADS + h + 1, :] = jnp.exp(pos_r * lgb)
            row_ref[2 * N_HEADS + h:2 * N_HEADS + h + 1, :] = jnp.exp(L * lgf) + jnp.zeros((1, L), F32)
            row_ref[3 * N_HEADS + h:3 * N_HEADS + h + 1, :] = jnp.exp(L * lgb) + jnp.zeros((1, L), F32)

    k_t = [(k_ref[c * L:(c + 1) * L, :].astype(F32) * (HEAD_DIM ** -0.5)).T for c in range(n_chunk)]
    o_all = []
    for h in range(N_HEADS):
        hs = _hs(h)
        zeta_f = row_ref[h:h + 1, :]
        zeta_b = row_ref[N_HEADS + h:N_HEADS + h + 1, :]
        gl_f = row_ref[2 * N_HEADS + h:2 * N_HEADS + h + 1, 0:1]
        gl_b = row_ref[3 * N_HEADS + h:3 * N_HEADS + h + 1, 0:1]
        kt = [k_t[c][hs, :] for c in range(n_chunk)]
        vb = [v_ref[c * L:(c + 1) * L, hs].astype(BF16) for c in range(n_chunk)]
        sf = [None] * (n_chunk + 1)
        sb = [None] * (n_chunk + 1)
        if has_state:
            sf[0] = s0_ref[0, h]
            sb[n_chunk] = s0_ref[1, h]
        for c in range(n_chunk):
            if (not has_state) or c < n_chunk - 1:
                upd = _dot((kt[c] * zeta_f).astype(BF16), vb[c])
                sf[c + 1] = upd if sf[c] is None else upd + gl_f * sf[c]
        for c in range(n_chunk - 1, -1, -1):
            if (not has_state) or c > 0:
                upd = _dot((kt[c] * zeta_b).astype(BF16), vb[c])
                sb[c] = upd if sb[c + 1] is None else upd + gl_b * sb[c + 1]
        if not has_state:
            sn_ref[0, h] = sf[n_chunk]
            sn_ref[1, h] = sb[0]
        o_h = []
        for c in range(n_chunk):
            qb = q_ref[c * L:(c + 1) * L, hs].astype(BF16)
            att = _dot(qb, kt[c].astype(BF16)) * dmat_ref[h]
            o = _dot(att.astype(BF16), vb[c])
            if has_state:
                o = o + col_ref[:, h:h + 1] * _dot(qb, sf[c].astype(BF16))
                o = o + col_ref[:, N_HEADS + h:N_HEADS + h + 1] * _dot(qb, sb[c + 1].astype(BF16))
            o_h.append(o)
        o_all.append(o_h)
    jm = _head_mean_matrix()
    for c in range(n_chunk):
        rows = slice(c * L, (c + 1) * L)
        o = jnp.concatenate([o_all[h][c] for h in range(N_HEADS)], axis=1)
        out_ref[rows, :] = _silu(g_ref[rows, :].astype(F32)) * _head_groupnorm(o, gain_ref[...], jm)


def _lambda(lam_ref, lam_init):
    v = lam_ref[...]
    s1 = jnp.sum(v[0:1, :] * v[1:2, :], axis=-1, keepdims=True)
    s2 = jnp.sum(v[2:3, :] * v[3:4, :], axis=-1, keepdims=True)
    return jnp.exp(s1) - jnp.exp(s2) + lam_init


def _first_half(shape):
    lane = lax.broadcasted_iota(jnp.int32, shape, 1)
    return (lane % (A_SUB // 2)) < (A_SUB // 4)


def _rope(x, cos, sin_signed):
    n = x.shape[1]
    first_half = _first_half(x.shape)
    partner = jnp.where(first_half, pltpu.roll(x, n - A_SUB // 4, 1), pltpu.roll(x, A_SUB // 4, 1))
    return x * cos + partner * sin_signed


def _attn_kernel(*refs, t, has_cache, lam_init):
    if has_cache:
        q_ref, k_ref, v_ref, kc_ref, vc_ref, lam_ref, gain_ref, out_ref, qr_ref, kr_ref, cos_ref, sin_ref = refs
    else:
        q_ref, k_ref, v_ref, lam_ref, gain_ref, out_ref, qr_ref, kr_ref = refs
    lam = _lambda(lam_ref, lam_init)
    scale = A_SUB ** -0.5
    n_qb = t // CHUNK

    if has_cache:
        @pl.when(pl.program_id(0) == 0)
        def _():
            lane = lax.broadcasted_iota(jnp.int32, (GRID_W, GROUP_W), 1)
            pos = lax.broadcasted_iota(jnp.int32, (GRID_W, GROUP_W), 0).astype(F32)
            n_freq = A_SUB // 4
            ang = pos * jnp.exp((lane % n_freq).astype(F32) * (-math.log(ROPE_BASE) / n_freq))
            cos_t = jnp.cos(ang)
            sin_t = jnp.where(_first_half((GRID_W, GROUP_W)), -jnp.sin(ang), jnp.sin(ang))
            row_axis = (lane % A_SUB) < (A_SUB // 2)
            for r in range(t // GRID_W):
                rows = slice(r * GRID_W, (r + 1) * GRID_W)
                cos_ref[rows, :] = jnp.where(row_axis, cos_t[r:r + 1, :], cos_t)
                sin_ref[rows, :] = jnp.where(row_axis, sin_t[r:r + 1, :], sin_t)

        kr_ref[...] = _rope(k_ref[...].astype(F32), cos_ref[...], sin_ref[...]).astype(BF16)
        qr_ref[...] = (_rope(q_ref[...].astype(F32), cos_ref[...], sin_ref[...]) * scale).astype(BF16)
    else:
        kr_ref[...] = k_ref[...].astype(BF16)
        qr_ref[...] = (q_ref[...].astype(F32) * scale).astype(BF16)

    vxs = [_v_ext(v_ref, slice(0, t), _hs(h)) for h in range(N_HEADS)]
    if has_cache:
        vcxs = [_v_ext(vc_ref, slice(0, vc_ref.shape[0]), _hs(h)) for h in range(N_HEADS)]

    def softmax_v(rows, units):
        ms = [slice(HEAD_DIM * h + A_SUB * m, HEAD_DIM * h + A_SUB * (m + 1)) for h, m in units]
        qm = [qr_ref[rows, c] for c in ms]
        s_l = [_dot_nt(q, kr_ref[:, c]) for q, c in zip(qm, ms)]
        mx = [jnp.max(s, axis=-1, keepdims=True) for s in s_l]
        if has_cache:
            s_c = [_dot_nt(q, kc_ref[:, c].astype(BF16)) for q, c in zip(qm, ms)]
            mx = [jnp.maximum(a, jnp.max(s, axis=-1, keepdims=True)) for a, s in zip(mx, s_c)]
        tot = [_dot(jnp.exp(s - a).astype(BF16), vxs[h]) for s, a, (h, m) in zip(s_l, mx, units)]
        if has_cache:
            tot = [tt + _dot(jnp.exp(s - a).astype(BF16), vcxs[h]) for tt, s, a, (h, m) in zip(tot, s_c, mx, units)]
        return [tt[:, 0:HEAD_DIM] / tt[:, HEAD_DIM:HEAD_DIM + 1] for tt in tot]

    heads_per_group = 2 if has_cache else 1
    o_all = [[None] * n_qb for _ in range(N_HEADS)]
    for h0 in range(0, N_HEADS, heads_per_group):
        for qi in range(n_qb):
            rows = slice(qi * CHUNK, (qi + 1) * CHUNK)
            units = [(h, m) for h in range(h0, h0 + heads_per_group) for m in range(2)]
            o = softmax_v(rows, units) if has_cache else [softmax_v(rows, [u])[0] for u in units]
            for k in range(heads_per_group):
                o_all[h0 + k][qi] = o[2 * k] - lam * o[2 * k + 1]
    jm = _head_mean_matrix()
    for qi in range(n_qb):
        rows = slice(qi * CHUNK, (qi + 1) * CHUNK)
        o = jnp.concatenate([o_all[h][qi] for h in range(N_HEADS)], axis=1)
        out_ref[rows, :] = _head_rms(o, gain_ref[...], jm) * (1.0 - lam_init)


def _mixers_kernel(*refs, n_chunk, t, has_state, lam_init, n_in, n_out, kinds, n_sub):
    n_io = sum(n_in) + sum(n_out)
    io, scratch = refs[:n_io], refs[n_io:]
    for sub in range(n_sub):
        def view(r, kind):
            if kind == 'rows':
                return r.at[pl.ds(sub * t, t)]
            return r.at[sub] if kind == 'seq' else r

        v = [view(r, k) for r, k in zip(io, kinds)]
        ins, outs = v[:sum(n_in)], v[sum(n_in):]
        i0, o0 = 0, 0
        parts = []
        for k in range(3):
            parts.append((ins[i0:i0 + n_in[k]], outs[o0:o0 + n_out[k]]))
            i0 += n_in[k]
            o0 += n_out[k]
        a_scratch = [r.at[sub] for r in scratch[3:]]
        _mlstm_kernel(*parts[0][0], *parts[0][1], n_chunk=n_chunk, has_state=has_state)
        _ret_kernel(*parts[1][0], *parts[1][1], *scratch[0:3], n_chunk=n_chunk, has_state=has_state)
        _attn_kernel(*parts[2][0], *parts[2][1], *a_scratch, t=t, has_cache=has_state, lam_init=lam_init)


def _mixers(z, gates, wp, l, n_seq, t, st, lam_init, fuse):
    has_state = st is not None
    nu = 2 * N_HEADS
    n_sub = 2 if fuse else 1
    tt = n_sub * t
    seq = lambda width: pl.BlockSpec((tt, width), lambda b: (b, 0))
    zspec = lambda colblk: pl.BlockSpec((tt, GROUP_W), lambda b: (b, colblk))
    out_rows = jax.ShapeDtypeStruct((n_seq * t, GROUP_W), F32)
    head_state = (2, N_HEADS, HEAD_DIM, HEAD_DIM)
    lead = (n_sub,) if fuse else (None,)
    per_seq = lambda *shape: pl.BlockSpec(lead + shape, lambda b: (b,) + (0,) * len(shape))

    m_in = [z, z, z, z, gates, wp['m_bias'], wp['m_norm']]
    m_specs = [zspec(0), zspec(1), zspec(2), zspec(3), seq(LANES),
               _layer_spec(wp['m_bias'], l), _layer_spec(wp['m_norm'], l)]
    m_kinds = ['rows'] * 5 + ['shared'] * 2
    r_in = [z, z, z, z, wp['r_decay'], wp['r_norm']]
    r_specs = [zspec(4), zspec(5), zspec(6), zspec(7),
               _layer_spec(wp['r_decay'], l), _layer_spec(wp['r_norm'], l)]
    r_kinds = ['rows'] * 4 + ['shared'] * 2
    a_in = [z, z, z]
    a_specs = [zspec(8), zspec(9), zspec(10)]
    a_kinds = ['rows'] * 3
    m_out, r_out, a_out = [out_rows], [out_rows], [out_rows]
    m_ospecs, r_ospecs, a_ospecs = [seq(GROUP_W)], [seq(GROUP_W)], [seq(GROUP_W)]
    mo_kinds, ro_kinds, ao_kinds = ['rows'], ['rows'], ['rows']
    a_lead = (n_sub,) if fuse else ()
    scratch = [pltpu.VMEM((N_HEADS, CHUNK, CHUNK), F32), pltpu.VMEM((CHUNK, LANES), F32),
               pltpu.VMEM((4 * N_HEADS, CHUNK), F32),
               pltpu.VMEM(a_lead + (t, GROUP_W), BF16), pltpu.VMEM(a_lead + (t, GROUP_W), BF16)]
    if has_state:
        assert not fuse
        m_in += list(st[0:2])
        m_specs += [_seq_layer_spec(a, l) for a in st[0:2]]
        r_in.append(st[2])
        r_specs.append(_seq_layer_spec(st[2], l))
        a_in += list(st[3:5])
        a_specs += [_seq_layer_spec(a, l) for a in st[3:5]]
        scratch += [pltpu.VMEM((t, GROUP_W), F32), pltpu.VMEM((t, GROUP_W), F32)]
    else:
        m_out += [jax.ShapeDtypeStruct((n_seq,) + head_state, F32),
                  jax.ShapeDtypeStruct((n_seq, 2, N_HEADS, HEAD_DIM, 1), F32),
                  jax.ShapeDtypeStruct((n_seq, nu, 1), F32)]
        m_ospecs += [per_seq(*head_state), per_seq(2, N_HEADS, HEAD_DIM, 1), per_seq(nu, 1)]
        mo_kinds += ['seq'] * 3
        r_out.append(jax.ShapeDtypeStruct((n_seq,) + head_state, F32))
        r_ospecs.append(per_seq(*head_state))
        ro_kinds.append('seq')
    a_in += [wp['a_lam'], wp['a_norm']]
    a_specs += [_layer_spec(wp['a_lam'], l), _layer_spec(wp['a_norm'], l)]
    a_kinds += ['shared'] * 2
    n_chunk = t // CHUNK
    call = functools.partial(pl.pallas_call, grid=(n_seq // n_sub,), compiler_params=_params(("arbitrary",)))
    if fuse:
        res = call(
            functools.partial(_mixers_kernel, n_chunk=n_chunk, t=t, has_state=has_state, lam_init=lam_init,
                              n_in=(len(m_in), len(r_in), len(a_in)), n_out=(len(m_out), len(r_out), len(a_out)),
                              kinds=tuple(m_kinds + r_kinds + a_kinds + mo_kinds + ro_kinds + ao_kinds),
                              n_sub=n_sub),
            in_specs=m_specs + r_specs + a_specs, out_specs=m_ospecs + r_ospecs + a_ospecs,
            out_shape=m_out + r_out + a_out, scratch_shapes=scratch, name="mixers",
        )(*m_in, *r_in, *a_in)
        return res[:len(m_out)], res[len(m_out):len(m_out) + len(r_out)], res[-1]
    m_res = call(functools.partial(_mlstm_kernel, n_chunk=n_chunk, has_state=has_state),
                 in_specs=m_specs, out_specs=m_ospecs, out_shape=m_out, name="mlstm")(*m_in)
    r_res = call(functools.partial(_ret_kernel, n_chunk=n_chunk, has_state=has_state),
                 in_specs=r_specs, out_specs=r_ospecs, out_shape=r_out, scratch_shapes=scratch[0:3],
                 name="retention")(*r_in)
    a_res = call(functools.partial(_attn_kernel, t=t, has_cache=has_state, lam_init=lam_init),
                 in_specs=a_specs, out_specs=a_ospecs, out_shape=a_out, scratch_shapes=scratch[3:],
                 name="diff_attention")(*a_in)
    return m_res, r_res, a_res[0]


def _cmul(ar, ai, br, bi):
    return ar * br - ai * bi, ar * bi + ai * br


def _lam_bar(lre, lim, dt):
    mag = jnp.exp(lre * dt)
    return mag * jnp.cos(lim * dt), mag * jnp.sin(lim * dt)


def _cpow_int(br, bi, e, n_bits):
    pr = jnp.ones(e.shape, F32)
    pi = jnp.zeros(e.shape, F32)
    for bit in range(n_bits):
        on = ((e >> bit) & 1) == 1
        qr, qi = _cmul(pr, pi, br, bi)
        pr, pi = jnp.where(on, qr, pr), jnp.where(on, qi, pi)
        if bit + 1 < n_bits:
            br, bi = _cmul(br, bi, br, bi)
    return pr, pi


def _zoh_coef(lre, lim, br, bi):
    den = lre * lre + lim * lim
    return ((br - 1.0) * lre + bi * lim) / den, (bi * lre - (br - 1.0) * lim) / den


def _toeplitz_rows(kall):
    n = S5_L * S5_GC
    lane = lax.broadcasted_iota(jnp.int32, (S5_GC, n), 1)
    pieces = []
    for s in range(S5_L):
        shifted = kall if s == 0 else pltpu.roll(kall, S5_GC * s, 1)
        pieces.append(jnp.where(lane >= S5_GC * s, shifted, 0.0))
    return jnp.concatenate(pieces, axis=0)


def _s5_prep_group(lam_c_re, lam_c_im, lam_r_re, lam_r_im, ldt_ref, b_t_re, b_t_im, b_re, b_im,
                   c_re, c_im, c_t_re, c_t_im, d_ref, t_ref, bcf_ref, bcb_ref, ccf_ref, ccb_ref, lam_ref):
    n = S5_L * S5_GC
    lane_blk = lax.broadcasted_iota(jnp.int32, (S5_P, n), 1) // S5_GC
    bits = S5_L.bit_length()

    def bases(d):
        dt = jnp.exp(ldt_ref[d])
        return (_lam_bar(lam_c_re[d], lam_c_im[d], dt), _lam_bar(lam_r_re[d], lam_r_im[d], dt))

    def tile_rows(x):
        return jnp.concatenate([x] * S5_L, axis=0)

    def rows_pow(br, bi, reverse):
        s_idx = lax.broadcasted_iota(jnp.int32, (S5_L, S5_P), 0)
        pr, pi = _cpow_int(br, bi, S5_L - 1 - s_idx if reverse else s_idx, bits)
        rep = lambda p: jnp.concatenate(
            [jnp.broadcast_to(p[s:s + 1, :], (S5_GC, S5_P)) for s in range(S5_L)], axis=0)
        return rep(pr), rep(pi)

    def tile_lanes(ref, d):
        return jnp.concatenate([ref[d]] * S5_L, axis=1)

    ct_re, ct_im = [tile_lanes(c_t_re, d) for d in range(2)], [tile_lanes(c_t_im, d) for d in range(2)]

    (cbr, cbi), (rbr, rbi) = bases(0)
    yield
    kr, ki = _zoh_coef(lam_r_re[0], lam_r_im[0], rbr, rbi)
    bbt_re, bbt_im = _cmul(kr, ki, b_t_re[0], b_t_im[0])
    pr, pi = _cpow_int(cbr, cbi, lane_blk, bits)
    yield
    cl_re, cl_im = _cmul(ct_re[0], ct_im[0], pr, pi)
    yield
    t_f = _toeplitz_rows(_dot(bbt_re, cl_re, HI) - _dot(bbt_im, cl_im, HI))
    yield
    pr, pi = rows_pow(rbr, rbi, True)
    yield
    re, im = _cmul(tile_rows(bbt_re), tile_rows(bbt_im), pr, pi)
    bcf_ref[...] = jnp.concatenate([re, im], axis=1).astype(BF16)
    re, im = _cmul(cl_re, cl_im, cbr, cbi)
    ccf_ref[...] = jnp.concatenate([re, -im], axis=0).astype(BF16)
    pr, pi = _cpow_int(rbr, rbi, jnp.full((1, S5_P), S5_L, jnp.int32), bits)
    yield
    lam_ref[0:1, :] = jnp.concatenate([pr, pr], axis=1)
    lam_ref[1:2, :] = jnp.concatenate([-pi, pi], axis=1)

    (cbr, cbi), (rbr, rbi) = bases(1)
    yield
    kr, ki = _zoh_coef(lam_c_re[1], lam_c_im[1], cbr, cbi)
    bb_re, bb_im = _cmul(kr, ki, tile_lanes(b_re, 1), tile_lanes(b_im, 1))
    pr, pi = _cpow_int(cbr, cbi, lane_blk, bits)
    yield
    bl_re, bl_im = _cmul(bb_re, bb_im, pr, pi)
    yield
    t_b = _toeplitz_rows(_dot(c_re[1], bl_re, HI) - _dot(c_im[1], bl_im, HI)).T
    yield
    kr, ki = _zoh_coef(lam_r_re[1], lam_r_im[1], rbr, rbi)
    bbt_re, bbt_im = _cmul(kr, ki, b_t_re[1], b_t_im[1])
    pr, pi = rows_pow(rbr, rbi, False)
    yield
    re, im = _cmul(tile_rows(bbt_re), tile_rows(bbt_im), pr, pi)
    bcb_ref[...] = jnp.concatenate([re, im], axis=1).astype(BF16)
    pr, pi = _cpow_int(cbr, cbi, S5_L - lane_blk, bits)
    yield
    re, im = _cmul(ct_re[1], ct_im[1], pr, pi)
    ccb_ref[...] = jnp.concatenate([re, -im], axis=0).astype(BF16)
    pr, pi = _cpow_int(rbr, rbi, jnp.full((1, S5_P), S5_L, jnp.int32), bits)
    yield
    lam_ref[2:3, :] = jnp.concatenate([pr, pr], axis=1)
    lam_ref[3:4, :] = jnp.concatenate([-pi, pi], axis=1)
    lam_ref[4:8, :] = jnp.zeros((4, 2 * S5_P), F32)

    eye = (lax.broadcasted_iota(jnp.int32, (n, n), 0) == lax.broadcasted_iota(jnp.int32, (n, n), 1))
    d_diag = jnp.concatenate([d_ref[...]] * S5_L, axis=1)
    t_ref[...] = (t_f + t_b + jnp.where(eye, d_diag, 0.0)).astype(BF16)


def _s5_prep_kernel(*refs, n_grp):
    ins, outs = refs[:14], refs[14:]
    gens = [_s5_prep_group(*[r.at[:, gi] for r in ins[:13]], ins[13].at[gi], *[r.at[gi] for r in outs])
            for gi in range(n_grp)]
    while gens:
        for g in list(gens):
            if next(g, True):
                gens.remove(g)


def _s5_prep(lam_re, lam_im, log_dt, b_re, b_im, c_re, c_im, s5_d):
    n = S5_L * S5_GC
    swap = lambda a: jnp.swapaxes(a, -1, -2)
    ins = [lam_re[..., :, None], lam_im[..., :, None], lam_re[..., None, :], lam_im[..., None, :],
           log_dt[..., None, None], swap(b_re), swap(b_im), b_re, b_im, c_re, c_im, swap(c_re), swap(c_im)]
    d_grp = s5_d.reshape(DEPTH, S5_G, 1, S5_GC)

    n_grp = 4

    def dir_spec(a):
        return pl.BlockSpec((None, 2, n_grp) + a.shape[3:], lambda l, g: (l, 0, g, 0, 0))

    def out(rows, cols, dtype):
        return (pl.BlockSpec((None, n_grp, rows, cols), lambda l, g: (l, g, 0, 0)),
                jax.ShapeDtypeStruct((DEPTH, S5_G, rows, cols), dtype))

    outs = [out(n, n, BF16), out(n, 2 * S5_P, BF16), out(n, 2 * S5_P, BF16),
            out(2 * S5_P, n, BF16), out(2 * S5_P, n, BF16), out(8, 2 * S5_P, F32)]
    return pl.pallas_call(
        functools.partial(_s5_prep_kernel, n_grp=n_grp),
        grid=(DEPTH, S5_G // n_grp),
        in_specs=[dir_spec(a) for a in ins] + [pl.BlockSpec((None, n_grp, 1, S5_GC), lambda l, g: (l, g, 0, 0))],
        out_specs=[o[0] for o in outs], out_shape=[o[1] for o in outs],
        compiler_params=_params(("arbitrary", "arbitrary")), name="s5_prep",
    )(*ins, d_grp)


def _block_transpose(a):
    n = len(a)
    blk = lax.broadcasted_iota(jnp.int32, (1, a[0].shape[1]), 1) // S5_GC
    a = list(a)
    bit = 1
    while bit < n:
        upper = (blk & bit) != 0
        for i in range(n):
            if i & bit == 0:
                j = i | bit
                lo, hi = a[i], a[j]
                a[i] = jnp.where(upper, pltpu.roll(hi, bit * S5_GC, 1), lo)
                a[j] = jnp.where(upper, hi, pltpu.roll(lo, LANES - bit * S5_GC, 1))
        bit *= 2
    return a


def _s5_kernel(*refs, n_seq, n_k, has_state):
    if has_state:
        (u_ref, t_ref, bcf_ref, bcb_ref, ccf_ref, ccb_ref, lam_ref, x0_ref,
         y_ref, ug_ref, yg_ref, inj_f, inj_b, inj_s, xin_f, xin_b) = refs
    else:
        (u_ref, t_ref, bcf_ref, bcb_ref, ccf_ref, ccb_ref, lam_ref,
         y_ref, xf_ref, xb_ref, ug_ref, yg_ref, inj_f, inj_b, inj_s, xin_f, xin_b) = refs
    ng = S5_G // 2
    r = n_seq * n_k
    tok = lambda i: pl.ds(i, r, stride=S5_L)
    u_lo = _block_transpose([u_ref[tok(i), :] for i in range(ng)])
    u_hi = _block_transpose([u_ref[tok(i), :] for i in range(ng, S5_L)])
    for g in range(ng):
        ug_ref[g] = jnp.concatenate([u_lo[g], u_hi[g]], axis=1).astype(BF16)
        inj_f[g] = _dot(ug_ref[g], bcf_ref[g])
        inj_b[g] = _dot(ug_ref[g], bcb_ref[g])

    for d, (inj, xin) in enumerate(((inj_f, xin_f), (inj_b, xin_b))):
        a = [lam_ref[g, 2 * d:2 * d + 1, :] for g in range(ng)]
        bs = [lam_ref[g, 2 * d + 1:2 * d + 2, :] for g in range(ng)]
        for g in range(ng):
            inj_s[g] = pltpu.roll(inj[g], S5_P, 1)
        x = [x0_ref[d, g] if has_state else jnp.zeros((n_seq, 2 * S5_P), F32) for g in range(ng)]
        xs = [pltpu.roll(v, S5_P, 1) for v in x]
        for k in (range(n_k) if d == 0 else range(n_k - 1, -1, -1)):
            rows = pl.ds(k, n_seq, stride=n_k)
            for g in range(ng):
                xin[g, rows, :] = x[g]
                x[g], xs[g] = (a[g] * x[g] + bs[g] * xs[g] + inj[g, rows, :],
                               a[g] * xs[g] - bs[g] * x[g] + inj_s[g, rows, :])
        if not has_state:
            for g in range(ng):
                (xf_ref if d == 0 else xb_ref)[g] = x[g]

    for g in range(ng):
        yg_ref[g] = (_dot(ug_ref[g], t_ref[g]) + _dot(xin_f[g].astype(BF16), ccf_ref[g])
                     + _dot(xin_b[g].astype(BF16), ccb_ref[g]))
    for half in range(S5_L // ng):
        y_i = _block_transpose([yg_ref[g, :, LANES * half:LANES * (half + 1)] for g in range(ng)])
        for i in range(ng):
            y_ref[tok(ng * half + i), :] = y_i[i]


def _s5(u_halves, ops, l, n_seq, n_k, x0):
    has_state = x0 is not None
    n = S5_L * S5_GC
    r = n_seq * n_k
    w = 2 * S5_P
    ng = S5_G // 2
    half_spec = pl.BlockSpec((None, r * S5_L, LANES), lambda j: (j, 0, 0))
    op_spec = lambda a: pl.BlockSpec((None, ng) + a.shape[2:], lambda j: (l, j) + (0,) * (a.ndim - 2))
    args = [u_halves] + list(ops)
    in_specs = [half_spec] + [op_spec(a) for a in ops]
    if has_state:
        args.append(x0)
        in_specs.append(pl.BlockSpec((None, 2, ng, n_seq, w), lambda j: (l, 0, j, 0, 0)))
    out_shape = [jax.ShapeDtypeStruct((2, r * S5_L, LANES), F32)]
    out_specs = [half_spec]
    if not has_state:
        out_shape += [jax.ShapeDtypeStruct((S5_G, n_seq, w), F32)] * 2
        out_specs += [pl.BlockSpec((ng, n_seq, w), lambda j: (j, 0, 0))] * 2
    return pl.pallas_call(
        functools.partial(_s5_kernel, n_seq=n_seq, n_k=n_k, has_state=has_state),
        grid=(2,), in_specs=in_specs, out_specs=out_specs, out_shape=out_shape,
        scratch_shapes=[pltpu.VMEM((ng, r, n), BF16), pltpu.VMEM((ng, r, n), F32)]
        + [pltpu.VMEM((ng, r, w), F32)] * 5,
        compiler_params=_params(("arbitrary",)), name="s5_scan",
    )(*args)


def _out_ffn_kernel(x_ref, m_ref, r_ref, a_ref, ya_ref, yb_ref, mod_ref, wglu_ref, bglu_ref, wout_ref,
                    n_post_ref, n_pre_ref, n_fpost_ref, wg_ref, wu_ref, wd_ref, o_ref):
    def mod(i):
        return mod_ref[:, i * D_MODEL:(i + 1) * D_MODEL]

    y = jnp.concatenate([ya_ref[...], yb_ref[...]], axis=1)
    gs = 0.5 * y * (1.0 + jnp.tanh(math.sqrt(2.0 / math.pi) * (y + 0.044715 * (y * y * y))))
    s_out = gs * jax.nn.sigmoid(_dot(gs.astype(BF16), wglu_ref[...]) + bglu_ref[...])
    mixed = jnp.concatenate([m_ref[...], r_ref[...], a_ref[...], s_out], axis=1).astype(BF16)
    x1 = x_ref[...] + mod(2) * _rms(_dot(mixed, wout_ref[...]), n_post_ref[...])
    h = (_rms(x1, n_pre_ref[...]) * (1.0 + mod(4)) + mod(3)).astype(BF16)
    act = (_silu(_dot(h, wg_ref[...])) * _dot(h, wu_ref[...])).astype(BF16)
    o_ref[...] = x1 + mod(5) * _rms(_dot(act, wd_ref[...]), n_fpost_ref[...])


def _out_ffn(x, m_out, r_out, a_out, y_halves, mod, l, mod_row, wp):
    n = x.shape[0]
    row = lambda w: pl.BlockSpec((ROW_TILE, w), lambda i: (i, 0))
    half = lambda j: pl.BlockSpec((None, ROW_TILE, LANES), lambda i: (j, i, 0))
    params = [wp[k] for k in ('w_glu', 'b_glu', 'w_out', 'n_mix_post', 'n_ffn_pre', 'n_ffn_post',
                              'w_gate', 'w_up', 'w_down')]
    return pl.pallas_call(
        _out_ffn_kernel,
        grid=(n // ROW_TILE,),
        in_specs=[row(D_MODEL), row(GROUP_W), row(GROUP_W), row(GROUP_W), half(0), half(1),
                  _mod_spec(l, mod_row)] + [_layer_spec(a, l, pipeline_mode=pl.Buffered(1)) for a in params],
        out_specs=row(D_MODEL),
        out_shape=jax.ShapeDtypeStruct((n, D_MODEL), F32),
        compiler_params=_params(("arbitrary",)), name="out_ffn",
    )(x, m_out, r_out, a_out, y_halves, y_halves, mod, *params)


def _layer(x, l, n_seq, t, mod, mod_row, wp, s5_ops, lam_init, st):
    ctx = st is None
    res = _in_proj(x, mod, l, mod_row, wp['n_mix_pre'], wp['w_in'], emit_kv=ctx)
    z, gates = res[0], res[2]
    m_res, r_res, a_out = _mixers(z, gates, wp, l, n_seq, t, None if ctx else st[0:5], lam_init, fuse=ctx)
    s_res = _s5(res[1], s5_ops, l, n_seq, t // S5_L, None if ctx else st[5])
    x_new = _out_ffn(x, m_res[0], r_res[0], a_out, s_res[0], mod, l, mod_row, wp)
    if not ctx:
        return x_new, None
    return x_new, (m_res[1], m_res[2], m_res[3], r_res[1], res[3], res[4], s_res[1], s_res[2])


def _pad_to(a, rows, cols=LANES):
    pad = [(0, 0)] * (a.ndim - 2) + [(0, rows - a.shape[-2]), (0, cols - a.shape[-1])]
    return jnp.pad(a, pad)


def kernel(x_prompt, x_sample, state_mlstm_C, state_mlstm_n, state_mlstm_m, state_ret, cache_diff_k, cache_diff_v, state_s5_re, state_s5_im, c, c_ctx, w_ada, b_ada, n_mix_pre, n_mix_post, n_ffn_pre, n_ffn_post, w_in, w_out, m_gate_bias, m_norm, r_decay_logit, r_norm, a_lam_q1, a_lam_k1, a_lam_q2, a_lam_k2, a_norm, s5_lam_re, s5_lam_im, s5_log_dt, s5_b_re, s5_b_im, s5_c_re, s5_c_im, s5_d, s5_w_glu, s5_b_glu, w_ffn_gate, w_ffn_up, w_ffn_down):
    n_ctx, t_ctx, _ = x_prompt.shape
    n_lat, t_lat, _ = x_sample.shape
    past = cache_diff_k.shape[2]

    cond8 = jnp.concatenate([c, c_ctx[None, :], jnp.zeros((8 - n_lat - 1, D_MODEL), F32)], axis=0)
    mod = _ada(cond8, w_ada, b_ada).reshape(DEPTH, 8, 1, 6 * D_MODEL)
    s5_ops = _s5_prep(s5_lam_re, s5_lam_im, s5_log_dt, s5_b_re, s5_b_im, s5_c_re, s5_c_im, s5_d)

    row = lambda a: a[:, None, :]
    wp = dict(
        w_in=jnp.pad(w_in.astype(BF16), ((0, 0), (0, 0), (0, N_MAIN + LANES - w_in.shape[-1]))), w_out=w_out.astype(BF16), w_glu=s5_w_glu.astype(BF16), b_glu=row(s5_b_glu),
        w_gate=w_ffn_gate.astype(BF16), w_up=w_ffn_up.astype(BF16), w_down=w_ffn_down.astype(BF16),
        n_mix_pre=row(n_mix_pre), n_mix_post=row(n_mix_post), n_ffn_pre=row(n_ffn_pre),
        n_ffn_post=row(n_ffn_post), m_norm=row(m_norm), r_norm=row(r_norm), a_norm=row(a_norm),
        m_bias=_pad_to(row(m_gate_bias), 1), r_decay=_pad_to(r_decay_logit, 8),
        a_lam=_pad_to(jnp.stack([a_lam_q1, a_lam_k1, a_lam_q2, a_lam_k2], axis=1), 8))
    lam_inits = [0.8 - 0.6 * math.exp(-0.3 * l) for l in range(DEPTH)]

    x = x_prompt.reshape(n_ctx * t_ctx, D_MODEL)
    new_states = []
    for l in range(DEPTH):
        x, st = _layer(x, l, n_ctx, t_ctx, mod, lambda i: n_lat, wp, s5_ops, lam_inits[l], None)
        new_states.append(st)
    y_prompt = x.reshape(n_ctx, t_ctx, D_MODEL)

    x0 = jnp.stack([state_s5_re, state_s5_im], axis=-2)
    x0 = x0.transpose(1, 2, 3, 0, 4, 5).reshape(DEPTH, 2, S5_G, n_lat, 2 * S5_P)
    s_ext0 = jnp.concatenate([state_mlstm_C, state_mlstm_n[..., None],
                              jnp.zeros(state_mlstm_n.shape + (LANES - HEAD_DIM - 1,), F32)], axis=-1)
    st = (s_ext0, state_mlstm_m.reshape(n_lat, DEPTH, 2 * N_HEADS, 1), state_ret,
          cache_diff_k.reshape(n_lat, DEPTH, past, GROUP_W), cache_diff_v.reshape(n_lat, DEPTH, past, GROUP_W), x0)
    x = x_sample.reshape(n_lat * t_lat, D_MODEL)
    tiles_per_seq = t_lat // ROW_TILE
    for l in range(DEPTH):
        x, _ = _layer(x, l, n_lat, t_lat, mod, lambda i: i // tiles_per_seq, wp, s5_ops, lam_inits[l], st)
    y_sample = x.reshape(n_lat, t_lat, D_MODEL)

    stack = lambda i: jnp.stack([s[i] for s in new_states], axis=1)
    kv = lambda i: jnp.stack([s[i].reshape(n_ctx, t_ctx, N_HEADS, HEAD_DIM) for s in new_states], axis=1)
    xs = jnp.stack([stack(6), stack(7)], axis=2)
    xs = xs.reshape(S5_G, DEPTH, 2, n_ctx, 2, S5_P).transpose(3, 1, 2, 0, 4, 5)
    return (y_prompt, y_sample, stack(0), stack(1)[..., 0], stack(2).reshape(n_ctx, DEPTH, 2, N_HEADS),
            stack(3), kv(4), kv(5), xs[..., 0, :], xs[..., 1, :])
```

```python
import functools
import math

import jax
import jax.numpy as jnp
from jax import lax
from jax.experimental import pallas as pl
from jax.experimental.pallas import tpu as pltpu

F32 = jnp.float32
BF16 = jnp.bfloat16
HI = lax.Precision.HIGHEST

D_MODEL = 1024
DEPTH = 2
GRID_W = 64
HEAD_DIM = 64
GROUP_W = 256
N_HEADS = 4
A_SUB = 32
S5_GC = 16
S5_G = 16
S5_P = 64
D_FF = 2816
ROPE_BASE = 10000.0
EPS = 1e-6
N_MAIN = 12 * GROUP_W
N_MIX = 11 * GROUP_W
LANES = 128
CHUNK = 256
IN_TILE = 1024
ROW_TILE = 512
S5_L = 16
NEG = -1e30
VMEM_LIMIT = 56 * 1024 * 1024


def _dot(a, b, precision=None):
    return jnp.dot(a, b, preferred_element_type=F32, precision=precision)


def _dot_nt(a, b):
    return lax.dot_general(a, b, (((1,), (1,)), ((), ())), preferred_element_type=F32)


def _log_sigmoid(x):
    return jnp.minimum(x, 0.0) - jnp.log(1.0 + jnp.exp(-jnp.abs(x)))


def _silu(x):
    return x * jax.nn.sigmoid(x)


def _rms(x, g):
    return x * lax.rsqrt(jnp.mean(x * x, axis=-1, keepdims=True) + EPS) * g


def _params(sem=None):
    return pltpu.CompilerParams(dimension_semantics=sem, vmem_limit_bytes=VMEM_LIMIT)


def _layer_spec(a, l, **kw):
    n = a.ndim - 1
    return pl.BlockSpec((None,) + a.shape[1:], lambda *_: (l,) + (0,) * n, **kw)


def _seq_layer_spec(a, l):
    n = a.ndim - 2
    return pl.BlockSpec((None, None) + a.shape[2:], lambda b: (b, l) + (0,) * n)


def _split_bf16(x):
    hi = x.astype(BF16)
    return hi, (x - hi.astype(F32)).astype(BF16)


def _ada_kernel(c_ref, w_ref, b_ref, o_ref):
    a_hi, a_lo = _split_bf16(_silu(c_ref[...]))
    w_hi, w_lo = _split_bf16(w_ref[...])
    o_ref[...] = _dot(a_hi, w_hi) + (_dot(a_lo, w_hi) + _dot(a_hi, w_lo)) + b_ref[...]


def _ada(cond8, w_ada, b_ada):
    tn = 1536
    return pl.pallas_call(
        _ada_kernel,
        grid=(DEPTH, 6 * D_MODEL // tn),
        in_specs=[pl.BlockSpec((8, D_MODEL), lambda l, j: (0, 0)),
                  pl.BlockSpec((None, D_MODEL, tn), lambda l, j: (l, 0, j)),
                  pl.BlockSpec((None, 1, tn), lambda l, j: (l, 0, j))],
        out_specs=pl.BlockSpec((None, 8, tn), lambda l, j: (l, 0, j)),
        out_shape=jax.ShapeDtypeStruct((DEPTH, 8, 6 * D_MODEL), F32),
        compiler_params=_params(("arbitrary", "arbitrary")),
        name="ada",
    )(cond8, w_ada, b_ada.reshape(DEPTH, 1, 6 * D_MODEL))


def _in_proj_kernel(x_ref, mod_ref, g_ref, w_ref, z_ref, u_ref, gate_ref, *kv_refs):
    h = _rms(x_ref[...], g_ref[...]) * (1.0 + mod_ref[:, D_MODEL:2 * D_MODEL]) + mod_ref[:, 0:D_MODEL]
    hb = h.astype(BF16)
    n_gate = 4 * N_HEADS
    n_head = 4 * GROUP_W
    z_ref[:, 0:n_head] = _dot(hb, w_ref[:, 0:n_head]).astype(BF16)
    tail = _dot(hb, w_ref[:, n_head:])
    gate_ref[...] = tail[:, 0:LANES]
    rest = tail[:, n_gate:n_gate + N_MAIN - n_head]
    z_ref[:, n_head:] = rest[:, 0:N_MIX - n_head].astype(BF16)
    u_ref[0] = rest[:, N_MIX - n_head:N_MIX - n_head + LANES]
    u_ref[1] = rest[:, N_MIX - n_head + LANES:]
    if kv_refs:
        kv_refs[0][...] = rest[:, 9 * GROUP_W - n_head:10 * GROUP_W - n_head]
        kv_refs[1][...] = rest[:, 10 * GROUP_W - n_head:11 * GROUP_W - n_head]


def _mod_spec(l, mod_row, tile):
    return pl.BlockSpec((None, None, 1, 6 * D_MODEL), lambda i: (l, mod_row(i * tile), 0, 0))


def _in_proj(x, mod, l, mod_row, gain, w, emit_kv):
    n = x.shape[0]
    row = lambda width: pl.BlockSpec((IN_TILE, width), lambda i: (i, 0))
    out_specs = [row(N_MIX), pl.BlockSpec((2, IN_TILE, LANES), lambda i: (0, i, 0)), row(LANES)]
    out_specs += [row(GROUP_W)] * (2 if emit_kv else 0)
    out_shape = [jax.ShapeDtypeStruct((n, N_MIX), BF16), jax.ShapeDtypeStruct((2, n, LANES), F32),
                 jax.ShapeDtypeStruct((n, LANES), F32)]
    out_shape += [jax.ShapeDtypeStruct((n, GROUP_W), F32)] * (2 if emit_kv else 0)
    return pl.pallas_call(
        _in_proj_kernel,
        grid=(n // IN_TILE,),
        in_specs=[row(D_MODEL), _mod_spec(l, mod_row, IN_TILE), _layer_spec(gain, l),
                  _layer_spec(w, l, pipeline_mode=pl.Buffered(1))],
        out_specs=out_specs, out_shape=out_shape,
        compiler_params=_params(("arbitrary",)),
        name="in_proj",
    )(x, mod, gain, w)


def _tri_masks(n):
    row = lax.broadcasted_iota(jnp.int32, (n, n), 0)
    col = lax.broadcasted_iota(jnp.int32, (n, n), 1)
    return row >= col, row <= col


def _head_mean_matrix():
    r = lax.broadcasted_iota(jnp.int32, (GROUP_W, GROUP_W), 0) // HEAD_DIM
    c = lax.broadcasted_iota(jnp.int32, (GROUP_W, GROUP_W), 1) // HEAD_DIM
    return jnp.where(r == c, 1.0 / HEAD_DIM, 0.0).astype(BF16)


def _head_mean(x, j):
    hi, lo = _split_bf16(x)
    return _dot(hi, j) + _dot(lo, j)


def _head_groupnorm(x, g, j):
    xc = x - _head_mean(x, j)
    return xc * lax.rsqrt(_head_mean(xc * xc, j) + EPS) * g


def _head_rms(x, g, j):
    return x * lax.rsqrt(_head_mean(x * x, j) + EPS) * g


def _hs(h):
    return slice(HEAD_DIM * h, HEAD_DIM * (h + 1))


def _v_ext(v_ref, rows, hs):
    ones = jnp.ones((rows.stop - rows.start, HEAD_DIM), BF16)
    return jnp.concatenate([v_ref[rows, hs].astype(BF16), ones], axis=1)


def _scan_max(x, reverse):
    n = x.shape[1]
    lane = lax.broadcasted_iota(jnp.int32, x.shape, 1)
    sh = 1
    while sh < n:
        if reverse:
            x = jnp.maximum(x, jnp.where(lane < n - sh, pltpu.roll(x, n - sh, 1), NEG))
        else:
            x = jnp.maximum(x, jnp.where(lane >= sh, pltpu.roll(x, sh, 1), NEG))
        sh *= 2
    return x


def _ends(x, is_fwd):
    return jnp.where(is_fwd, x[:, x.shape[1] - 1:], x[:, 0:1])


def _mlstm_kernel(*refs, n_chunk, has_state):
    if has_state:
        q_ref, k_ref, v_ref, o_ref, g_ref, bias_ref, gain_ref, s0_ref, m0_ref, out_ref = refs
    else:
        q_ref, k_ref, v_ref, o_ref, g_ref, bias_ref, gain_ref, out_ref, cn_ref, nn_ref, mn_ref = refs
    L = CHUNK
    nu = 2 * N_HEADS
    tril, triu = _tri_masks(L)
    is_fwd = lax.broadcasted_iota(jnp.int32, (nu, 1), 0) < N_HEADS

    b8, w8, mcum8 = [], [], []
    for c in range(n_chunk):
        p_t = (g_ref[c * L:(c + 1) * L, :] + bias_ref[...]).T
        lf = _log_sigmoid(p_t[nu:2 * nu, :])
        b = jnp.where(is_fwd, _dot(lf, triu.astype(F32), HI), _dot(lf, tril.astype(F32), HI))
        w = p_t[0:nu, :] - b
        b8.append(b)
        w8.append(w)
        mcum8.append(jnp.where(is_fwd, _scan_max(w, False), _scan_max(w, True)))

    m0 = m0_ref[...] if has_state else jnp.zeros((nu, 1), F32)
    m_in_f, m_in_b = [None] * n_chunk, [None] * n_chunk
    m = m0
    for c in range(n_chunk):
        m_in_f[c] = m
        m = (b8[c] + jnp.maximum(mcum8[c], m))[:, L - 1:]
    m_fin_f = m
    m = m0
    for c in range(n_chunk - 1, -1, -1):
        m_in_b[c] = m
        m = (b8[c] + jnp.maximum(mcum8[c], m))[:, 0:1]
    if not has_state:
        mn_ref[...] = jnp.where(is_fwd, m_fin_f, m)

    wk8, dec8, cols = [], [], []
    for c in range(n_chunk):
        m_in = jnp.where(is_fwd, m_in_f[c], m_in_b[c])
        g = jnp.maximum(mcum8[c], m_in)
        m_row = b8[c] + g
        m_new, b_last = _ends(m_row, is_fwd), _ends(b8[c], is_fwd)
        wk8.append(jnp.exp(b_last + w8[c] - m_new))
        dec8.append(jnp.exp(b_last + m_in - m_new))
        stats = jnp.concatenate([g, jnp.exp(m_in - g), jnp.exp(-m_row), jnp.zeros((LANES - 3 * nu, L), F32)], axis=0)
        cols.append(stats.T)

    k_t = [(k_ref[c * L:(c + 1) * L, :].astype(F32) * (HEAD_DIM ** -0.5)).T for c in range(n_chunk)]

    h_all = []
    for h in range(N_HEADS):
        hs = _hs(h)
        qb = [q_ref[c * L:(c + 1) * L, hs].astype(BF16) for c in range(n_chunk)]
        kt = [k_t[c][hs, :] for c in range(n_chunk)]
        vx = [_v_ext(v_ref, slice(c * L, (c + 1) * L), hs) for c in range(n_chunk)]
        qk = [_dot(qb[c], kt[c].astype(BF16)) for c in range(n_chunk)]
        h_sum = [None] * n_chunk
        for d in range(2):
            tri = tril if d == 0 else triu
            j = N_HEADS * d + h
            s_ext = s0_ref[d, h] if has_state else None
            order = range(n_chunk) if d == 0 else range(n_chunk - 1, -1, -1)
            for ci, c in enumerate(order):
                wgt = jnp.exp(jnp.where(tri, w8[c][j:j + 1, :] - cols[c][:, j:j + 1], NEG))
                tot = _dot((qk[c] * wgt).astype(BF16), vx[c])
                if s_ext is not None:
                    tot = tot + cols[c][:, nu + j:nu + j + 1] * _dot(qb[c], s_ext.astype(BF16))
                den = jnp.maximum(jnp.abs(tot[:, HEAD_DIM:HEAD_DIM + 1]), cols[c][:, 2 * nu + j:2 * nu + j + 1])
                hd = tot[:, 0:HEAD_DIM] / den
                h_sum[c] = hd if h_sum[c] is None else h_sum[c] + hd
                if (not has_state) or ci < n_chunk - 1:
                    upd = _dot((kt[c] * wk8[c][j:j + 1, :]).astype(BF16), vx[c])
                    s_ext = upd if s_ext is None else upd + dec8[c][j:j + 1, :] * s_ext
            if not has_state:
                cn_ref[d, h] = s_ext[:, 0:HEAD_DIM]
                nn_ref[d, h] = s_ext[:, HEAD_DIM:HEAD_DIM + 1]
        h_all.append(h_sum)
    jm = _head_mean_matrix()
    for c in range(n_chunk):
        rows = slice(c * L, (c + 1) * L)
        gated = jax.nn.sigmoid(o_ref[rows, :].astype(F32)) * jnp.concatenate([h_all[h][c] for h in range(N_HEADS)], axis=1)
        out_ref[rows, :] = _head_groupnorm(gated, gain_ref[...], jm)


def _ret_kernel(*refs, n_chunk, has_state):
    if has_state:
        q_ref, k_ref, v_ref, g_ref, lg_ref, gain_ref, s0_ref, out_ref, dmat_ref, col_ref, row_ref = refs
    else:
        q_ref, k_ref, v_ref, g_ref, lg_ref, gain_ref, out_ref, sn_ref, dmat_ref, col_ref, row_ref = refs
    L = CHUNK

    @pl.when(pl.program_id(0) == 0)
    def _():
        row = lax.broadcasted_iota(jnp.int32, (L, L), 0)
        col = lax.broadcasted_iota(jnp.int32, (L, L), 1)
        diff = (row - col).astype(F32)
        log_g = _log_sigmoid(lg_ref[...])
        pos_c = lax.broadcasted_iota(jnp.int32, (L, 1), 0).astype(F32)
        pos_r = lax.broadcasted_iota(jnp.int32, (1, L), 1).astype(F32)
        for h in range(N_HEADS):
            lgf = log_g[0:1, h:h + 1]
            lgb = log_g[1:2, h:h + 1]
            dmat_ref[h] = (jnp.where(diff >= 0, jnp.exp(jnp.maximum(diff, 0.0) * lgf), 0.0)
                           + jnp.where(diff <= 0, jnp.exp(jnp.maximum(-diff, 0.0) * lgb), 0.0))
            col_ref[:, h:h + 1] = jnp.exp((pos_c + 1.0) * lgf)
            col_ref[:, N_HEADS + h:N_HEADS + h + 1] = jnp.exp((L - pos_c) * lgb)
            row_ref[h:h + 1, :] = jnp.exp((L - 1.0 - pos_r) * lgf)
            row_ref[N_HEADS + h:N_HEADS + h + 1, :] = jnp.exp(pos_r * lgb)
            row_ref[2 * N_HEADS + h:2 * N_HEADS + h + 1, :] = jnp.exp(L * lgf) + jnp.zeros((1, L), F32)
            row_ref[3 * N_HEADS + h:3 * N_HEADS + h + 1, :] = jnp.exp(L * lgb) + jnp.zeros((1, L), F32)

    k_t = [(k_ref[c * L:(c + 1) * L, :].astype(F32) * (HEAD_DIM ** -0.5)).T for c in range(n_chunk)]
    o_all = []
    for h in range(N_HEADS):
        hs = _hs(h)
        zeta_f = row_ref[h:h + 1, :]
        zeta_b = row_ref[N_HEADS + h:N_HEADS + h + 1, :]
        gl_f = row_ref[2 * N_HEADS + h:2 * N_HEADS + h + 1, 0:1]
        gl_b = row_ref[3 * N_HEADS + h:3 * N_HEADS + h + 1, 0:1]
        kt = [k_t[c][hs, :] for c in range(n_chunk)]
        vb = [v_ref[c * L:(c + 1) * L, hs].astype(BF16) for c in range(n_chunk)]
        sf = [None] * (n_chunk + 1)
        sb = [None] * (n_chunk + 1)
        if has_state:
            sf[0] = s0_ref[0, h]
            sb[n_chunk] = s0_ref[1, h]
        for c in range(n_chunk):
            if (not has_state) or c < n_chunk - 1:
                upd = _dot((kt[c] * zeta_f).astype(BF16), vb[c])
                sf[c + 1] = upd if sf[c] is None else upd + gl_f * sf[c]
        for c in range(n_chunk - 1, -1, -1):
            if (not has_state) or c > 0:
                upd = _dot((kt[c] * zeta_b).astype(BF16), vb[c])
                sb[c] = upd if sb[c + 1] is None else upd + gl_b * sb[c + 1]
        if not has_state:
            sn_ref[0, h] = sf[n_chunk]
            sn_ref[1, h] = sb[0]
        o_h = []
        for c in range(n_chunk):
            qb = q_ref[c * L:(c + 1) * L, hs].astype(BF16)
            att = _dot(qb, kt[c].astype(BF16)) * dmat_ref[h]
            o = _dot(att.astype(BF16), vb[c])
            if has_state:
                o = o + col_ref[:, h:h + 1] * _dot(qb, sf[c].astype(BF16))
                o = o + col_ref[:, N_HEADS + h:N_HEADS + h + 1] * _dot(qb, sb[c + 1].astype(BF16))
            o_h.append(o)
        o_all.append(o_h)
    jm = _head_mean_matrix()
    for c in range(n_chunk):
        rows = slice(c * L, (c + 1) * L)
        o = jnp.concatenate([o_all[h][c] for h in range(N_HEADS)], axis=1)
        out_ref[rows, :] = _silu(g_ref[rows, :].astype(F32)) * _head_groupnorm(o, gain_ref[...], jm)


def _lambda(lam_ref, lam_init):
    v = lam_ref[...]
    s1 = jnp.sum(v[0:1, :] * v[1:2, :], axis=-1, keepdims=True)
    s2 = jnp.sum(v[2:3, :] * v[3:4, :], axis=-1, keepdims=True)
    return jnp.exp(s1) - jnp.exp(s2) + lam_init


def _first_half(shape):
    lane = lax.broadcasted_iota(jnp.int32, shape, 1)
    return (lane % (A_SUB // 2)) < (A_SUB // 4)


def _rope(x, cos, sin_signed):
    n = x.shape[1]
    first_half = _first_half(x.shape)
    partner = jnp.where(first_half, pltpu.roll(x, n - A_SUB // 4, 1), pltpu.roll(x, A_SUB // 4, 1))
    return x * cos + partner * sin_signed


def _attn_kernel(*refs, t, has_cache, lam_init):
    if has_cache:
        q_ref, k_ref, v_ref, kc_ref, vc_ref, lam_ref, gain_ref, out_ref, qr_ref, kr_ref, cos_ref, sin_ref = refs
    else:
        q_ref, k_ref, v_ref, lam_ref, gain_ref, out_ref, qr_ref, kr_ref = refs
    lam = _lambda(lam_ref, lam_init)
    scale = A_SUB ** -0.5
    n_qb = t // CHUNK

    if has_cache:
        @pl.when(pl.program_id(0) == 0)
        def _():
            lane = lax.broadcasted_iota(jnp.int32, (GRID_W, GROUP_W), 1)
            pos = lax.broadcasted_iota(jnp.int32, (GRID_W, GROUP_W), 0).astype(F32)
            n_freq = A_SUB // 4
            ang = pos * jnp.exp((lane % n_freq).astype(F32) * (-math.log(ROPE_BASE) / n_freq))
            cos_t = jnp.cos(ang)
            sin_t = jnp.where(_first_half((GRID_W, GROUP_W)), -jnp.sin(ang), jnp.sin(ang))
            row_axis = (lane % A_SUB) < (A_SUB // 2)
            for r in range(t // GRID_W):
                rows = slice(r * GRID_W, (r + 1) * GRID_W)
                cos_ref[rows, :] = jnp.where(row_axis, cos_t[r:r + 1, :], cos_t)
                sin_ref[rows, :] = jnp.where(row_axis, sin_t[r:r + 1, :], sin_t)

        kr_ref[...] = _rope(k_ref[...].astype(F32), cos_ref[...], sin_ref[...]).astype(BF16)
        qr_ref[...] = (_rope(q_ref[...].astype(F32), cos_ref[...], sin_ref[...]) * scale).astype(BF16)
    else:
        kr_ref[...] = k_ref[...].astype(BF16)
        qr_ref[...] = (q_ref[...].astype(F32) * scale).astype(BF16)

    vxs = [_v_ext(v_ref, slice(0, t), _hs(h)) for h in range(N_HEADS)]
    if has_cache:
        vcxs = [_v_ext(vc_ref, slice(0, vc_ref.shape[0]), _hs(h)) for h in range(N_HEADS)]

    def softmax_v(rows, units):
        ms = [slice(HEAD_DIM * h + A_SUB * m, HEAD_DIM * h + A_SUB * (m + 1)) for h, m in units]
        qm = [qr_ref[rows, c] for c in ms]
        s_l = [_dot_nt(q, kr_ref[:, c]) for q, c in zip(qm, ms)]
        mx = [jnp.max(s, axis=-1, keepdims=True) for s in s_l]
        if has_cache:
            s_c = [_dot_nt(q, kc_ref[:, c].astype(BF16)) for q, c in zip(qm, ms)]
            mx = [jnp.maximum(a, jnp.max(s, axis=-1, keepdims=True)) for a, s in zip(mx, s_c)]
        tot = [_dot(jnp.exp(s - a).astype(BF16), vxs[h]) for s, a, (h, m) in zip(s_l, mx, units)]
        if has_cache:
            tot = [tt + _dot(jnp.exp(s - a).astype(BF16), vcxs[h]) for tt, s, a, (h, m) in zip(tot, s_c, mx, units)]
        return [tt[:, 0:HEAD_DIM] / tt[:, HEAD_DIM:HEAD_DIM + 1] for tt in tot]

    heads_per_group = 2 if has_cache else 1
    o_all = [[None] * n_qb for _ in range(N_HEADS)]
    for h0 in range(0, N_HEADS, heads_per_group):
        for qi in range(n_qb):
            rows = slice(qi * CHUNK, (qi + 1) * CHUNK)
            units = [(h, m) for h in range(h0, h0 + heads_per_group) for m in range(2)]
            o = softmax_v(rows, units) if has_cache else [softmax_v(rows, [u])[0] for u in units]
            for k in range(heads_per_group):
                o_all[h0 + k][qi] = o[2 * k] - lam * o[2 * k + 1]
    jm = _head_mean_matrix()
    for qi in range(n_qb):
        rows = slice(qi * CHUNK, (qi + 1) * CHUNK)
        o = jnp.concatenate([o_all[h][qi] for h in range(N_HEADS)], axis=1)
        out_ref[rows, :] = _head_rms(o, gain_ref[...], jm) * (1.0 - lam_init)


def _mixers_kernel(*refs, n_chunk, t, has_state, lam_init, n_in, n_out, kinds, n_sub):
    n_io = sum(n_in) + sum(n_out)
    io, scratch = refs[:n_io], refs[n_io:]
    for sub in range(n_sub):
        def view(r, kind):
            if kind == 'rows':
                return r.at[pl.ds(sub * t, t)]
            return r.at[sub] if kind == 'seq' else r

        v = [view(r, k) for r, k in zip(io, kinds)]
        ins, outs = v[:sum(n_in)], v[sum(n_in):]
        i0, o0 = 0, 0
        parts = []
        for k in range(3):
            parts.append((ins[i0:i0 + n_in[k]], outs[o0:o0 + n_out[k]]))
            i0 += n_in[k]
            o0 += n_out[k]
        a_scratch = [r.at[sub] for r in scratch[3:]]
        _mlstm_kernel(*parts[0][0], *parts[0][1], n_chunk=n_chunk, has_state=has_state)
        _ret_kernel(*parts[1][0], *parts[1][1], *scratch[0:3], n_chunk=n_chunk, has_state=has_state)
        _attn_kernel(*parts[2][0], *parts[2][1], *a_scratch, t=t, has_cache=has_state, lam_init=lam_init)


def _mixers(z, gates, wp, l, n_seq, t, st, lam_init, fuse):
    has_state = st is not None
    nu = 2 * N_HEADS
    n_sub = 2 if fuse else 1
    tt = n_sub * t
    seq = lambda width: pl.BlockSpec((tt, width), lambda b: (b, 0))
    zspec = lambda colblk: pl.BlockSpec((tt, GROUP_W), lambda b: (b, colblk))
    out_rows = jax.ShapeDtypeStruct((n_seq * t, GROUP_W), F32)
    head_state = (2, N_HEADS, HEAD_DIM, HEAD_DIM)
    lead = (n_sub,) if fuse else (None,)
    per_seq = lambda *shape: pl.BlockSpec(lead + shape, lambda b: (b,) + (0,) * len(shape))

    m_in = [z, z, z, z, gates, wp['m_bias'], wp['m_norm']]
    m_specs = [zspec(0), zspec(1), zspec(2), zspec(3), seq(LANES),
               _layer_spec(wp['m_bias'], l), _layer_spec(wp['m_norm'], l)]
    m_kinds = ['rows'] * 5 + ['shared'] * 2
    r_in = [z, z, z, z, wp['r_decay'], wp['r_norm']]
    r_specs = [zspec(4), zspec(5), zspec(6), zspec(7),
               _layer_spec(wp['r_decay'], l), _layer_spec(wp['r_norm'], l)]
    r_kinds = ['rows'] * 4 + ['shared'] * 2
    a_in = [z, z, z]
    a_specs = [zspec(8), zspec(9), zspec(10)]
    a_kinds = ['rows'] * 3
    m_out, r_out, a_out = [out_rows], [out_rows], [out_rows]
    m_ospecs, r_ospecs, a_ospecs = [seq(GROUP_W)], [seq(GROUP_W)], [seq(GROUP_W)]
    mo_kinds, ro_kinds, ao_kinds = ['rows'], ['rows'], ['rows']
    a_lead = (n_sub,) if fuse else ()
    scratch = [pltpu.VMEM((N_HEADS, CHUNK, CHUNK), F32), pltpu.VMEM((CHUNK, LANES), F32),
               pltpu.VMEM((4 * N_HEADS, CHUNK), F32),
               pltpu.VMEM(a_lead + (t, GROUP_W), BF16), pltpu.VMEM(a_lead + (t, GROUP_W), BF16)]
    if has_state:
        assert not fuse
        m_in += list(st[0:2])
        m_specs += [_seq_layer_spec(a, l) for a in st[0:2]]
        r_in.append(st[2])
        r_specs.append(_seq_layer_spec(st[2], l))
        a_in += list(st[3:5])
        a_specs += [_seq_layer_spec(a, l) for a in st[3:5]]
        scratch += [pltpu.VMEM((t, GROUP_W), F32), pltpu.VMEM((t, GROUP_W), F32)]
    else:
        m_out += [jax.ShapeDtypeStruct((n_seq,) + head_state, F32),
                  jax.ShapeDtypeStruct((n_seq, 2, N_HEADS, HEAD_DIM, 1), F32),
                  jax.ShapeDtypeStruct((n_seq, nu, 1), F32)]
        m_ospecs += [per_seq(*head_state), per_seq(2, N_HEADS, HEAD_DIM, 1), per_seq(nu, 1)]
        mo_kinds += ['seq'] * 3
        r_out.append(jax.ShapeDtypeStruct((n_seq,) + head_state, F32))
        r_ospecs.append(per_seq(*head_state))
        ro_kinds.append('seq')
    a_in += [wp['a_lam'], wp['a_norm']]
    a_specs += [_layer_spec(wp['a_lam'], l), _layer_spec(wp['a_norm'], l)]
    a_kinds += ['shared'] * 2
    n_chunk = t // CHUNK
    call = functools.partial(pl.pallas_call, grid=(n_seq // n_sub,), compiler_params=_params(("arbitrary",)))
    if fuse:
        res = call(
            functools.partial(_mixers_kernel, n_chunk=n_chunk, t=t, has_state=has_state, lam_init=lam_init,
                              n_in=(len(m_in), len(r_in), len(a_in)), n_out=(len(m_out), len(r_out), len(a_out)),
                              kinds=tuple(m_kinds + r_kinds + a_kinds + mo_kinds + ro_kinds + ao_kinds),
                              n_sub=n_sub),
            in_specs=m_specs + r_specs + a_specs, out_specs=m_ospecs + r_ospecs + a_ospecs,
            out_shape=m_out + r_out + a_out, scratch_shapes=scratch, name="mixers",
        )(*m_in, *r_in, *a_in)
        return res[:len(m_out)], res[len(m_out):len(m_out) + len(r_out)], res[-1]
    m_res = call(functools.partial(_mlstm_kernel, n_chunk=n_chunk, has_state=has_state),
                 in_specs=m_specs, out_specs=m_ospecs, out_shape=m_out, name="mlstm")(*m_in)
    r_res = call(functools.partial(_ret_kernel, n_chunk=n_chunk, has_state=has_state),
                 in_specs=r_specs, out_specs=r_ospecs, out_shape=r_out, scratch_shapes=scratch[0:3],
                 name="retention")(*r_in)
    a_res = call(functools.partial(_attn_kernel, t=t, has_cache=has_state, lam_init=lam_init),
                 in_specs=a_specs, out_specs=a_ospecs, out_shape=a_out, scratch_shapes=scratch[3:],
                 name="diff_attention")(*a_in)
    return m_res, r_res, a_res[0]


def _cmul(ar, ai, br, bi):
    return ar * br - ai * bi, ar * bi + ai * br


def _lam_bar(lre, lim, dt):
    mag = jnp.exp(lre * dt)
    return mag * jnp.cos(lim * dt), mag * jnp.sin(lim * dt)


def _cpow_int(br, bi, e, n_bits):
    pr = jnp.ones(e.shape, F32)
    pi = jnp.zeros(e.shape, F32)
    for bit in range(n_bits):
        on = ((e >> bit) & 1) == 1
        qr, qi = _cmul(pr, pi, br, bi)
        pr, pi = jnp.where(on, qr, pr), jnp.where(on, qi, pi)
        if bit + 1 < n_bits:
            br, bi = _cmul(br, bi, br, bi)
    return pr, pi


def _zoh_coef(lre, lim, br, bi):
    den = lre * lre + lim * lim
    return ((br - 1.0) * lre + bi * lim) / den, (bi * lre - (br - 1.0) * lim) / den


def _toeplitz_rows(kall):
    n = S5_L * S5_GC
    lane = lax.broadcasted_iota(jnp.int32, (S5_GC, n), 1)
    pieces = []
    for s in range(S5_L):
        shifted = kall if s == 0 else pltpu.roll(kall, S5_GC * s, 1)
        pieces.append(jnp.where(lane >= S5_GC * s, shifted, 0.0))
    return jnp.concatenate(pieces, axis=0)


def _s5_prep_group(lam_c_re, lam_c_im, lam_r_re, lam_r_im, ldt_ref, b_t_re, b_t_im, b_re, b_im,
                   c_re, c_im, c_t_re, c_t_im, d_ref, t_ref, bcf_ref, bcb_ref, ccf_ref, ccb_ref, lam_ref):
    n = S5_L * S5_GC
    lane_blk = lax.broadcasted_iota(jnp.int32, (S5_P, n), 1) // S5_GC
    bits = S5_L.bit_length()

    def bases(d):
        dt = jnp.exp(ldt_ref[d])
        return (_lam_bar(lam_c_re[d], lam_c_im[d], dt), _lam_bar(lam_r_re[d], lam_r_im[d], dt))

    def tile_rows(x):
        return jnp.concatenate([x] * S5_L, axis=0)

    def rows_pow(br, bi, reverse):
        s_idx = lax.broadcasted_iota(jnp.int32, (S5_L, S5_P), 0)
        pr, pi = _cpow_int(br, bi, S5_L - 1 - s_idx if reverse else s_idx, bits)
        rep = lambda p: jnp.concatenate(
            [jnp.broadcast_to(p[s:s + 1, :], (S5_GC, S5_P)) for s in range(S5_L)], axis=0)
        return rep(pr), rep(pi)

    def tile_lanes(ref, d):
        return jnp.concatenate([ref[d]] * S5_L, axis=1)

    ct_re, ct_im = [tile_lanes(c_t_re, d) for d in range(2)], [tile_lanes(c_t_im, d) for d in range(2)]

    (cbr, cbi), (rbr, rbi) = bases(0)
    yield
    kr, ki = _zoh_coef(lam_r_re[0], lam_r_im[0], rbr, rbi)
    bbt_re, bbt_im = _cmul(kr, ki, b_t_re[0], b_t_im[0])
    pr, pi = _cpow_int(cbr, cbi, lane_blk, bits)
    yield
    cl_re, cl_im = _cmul(ct_re[0], ct_im[0], pr, pi)
    yield
    t_f = _toeplitz_rows(_dot(bbt_re, cl_re, HI) - _dot(bbt_im, cl_im, HI))
    yield
    pr, pi = rows_pow(rbr, rbi, True)
    yield
    re, im = _cmul(tile_rows(bbt_re), tile_rows(bbt_im), pr, pi)
    bcf_ref[...] = jnp.concatenate([re, im], axis=1).astype(BF16)
    re, im = _cmul(cl_re, cl_im, cbr, cbi)
    ccf_ref[...] = jnp.concatenate([re, -im], axis=0).astype(BF16)
    pr, pi = _cpow_int(rbr, rbi, jnp.full((1, S5_P), S5_L, jnp.int32), bits)
    yield
    lam_ref[0:1, :] = jnp.concatenate([pr, pr], axis=1)
    lam_ref[1:2, :] = jnp.concatenate([-pi, pi], axis=1)

    (cbr, cbi), (rbr, rbi) = bases(1)
    yield
    kr, ki = _zoh_coef(lam_c_re[1], lam_c_im[1], cbr, cbi)
    bb_re, bb_im = _cmul(kr, ki, tile_lanes(b_re, 1), tile_lanes(b_im, 1))
    pr, pi = _cpow_int(cbr, cbi, lane_blk, bits)
    yield
    bl_re, bl_im = _cmul(bb_re, bb_im, pr, pi)
    yield
    t_b = _toeplitz_rows(_dot(c_re[1], bl_re, HI) - _dot(c_im[1], bl_im, HI)).T
    yield
    kr, ki = _zoh_coef(lam_r_re[1], lam_r_im[1], rbr, rbi)
    bbt_re, bbt_im = _cmul(kr, ki, b_t_re[1], b_t_im[1])
    pr, pi = rows_pow(rbr, rbi, False)
    yield
    re, im = _cmul(tile_rows(bbt_re), tile_rows(bbt_im), pr, pi)
    bcb_ref[...] = jnp.concatenate([re, im], axis=1).astype(BF16)
    pr, pi = _cpow_int(cbr, cbi, S5_L - lane_blk, bits)
    yield
    re, im = _cmul(ct_re[1], ct_im[1], pr, pi)
    ccb_ref[...] = jnp.concatenate([re, -im], axis=0).astype(BF16)
    pr, pi = _cpow_int(rbr, rbi, jnp.full((1, S5_P), S5_L, jnp.int32), bits)
    yield
    lam_ref[2:3, :] = jnp.concatenate([pr, pr], axis=1)
    lam_ref[3:4, :] = jnp.concatenate([-pi, pi], axis=1)
    lam_ref[4:8, :] = jnp.zeros((4, 2 * S5_P), F32)

    eye = (lax.broadcasted_iota(jnp.int32, (n, n), 0) == lax.broadcasted_iota(jnp.int32, (n, n), 1))
    d_diag = jnp.concatenate([d_ref[...]] * S5_L, axis=1)
    t_ref[...] = (t_f + t_b + jnp.where(eye, d_diag, 0.0)).astype(BF16)


def _s5_prep_kernel(*refs, n_grp):
    ins, outs = refs[:14], refs[14:]
    gens = [_s5_prep_group(*[r.at[:, gi] for r in ins[:13]], ins[13].at[gi], *[r.at[gi] for r in outs])
            for gi in range(n_grp)]
    while gens:
        for g in list(gens):
            if next(g, True):
                gens.remove(g)


def _s5_prep(lam_re, lam_im, log_dt, b_re, b_im, c_re, c_im, s5_d):
    n = S5_L * S5_GC
    swap = lambda a: jnp.swapaxes(a, -1, -2)
    ins = [lam_re[..., :, None], lam_im[..., :, None], lam_re[..., None, :], lam_im[..., None, :],
           log_dt[..., None, None], swap(b_re), swap(b_im), b_re, b_im, c_re, c_im, swap(c_re), swap(c_im)]
    d_grp = s5_d.reshape(DEPTH, S5_G, 1, S5_GC)

    n_grp = 4

    def dir_spec(a):
        return pl.BlockSpec((None, 2, n_grp) + a.shape[3:], lambda l, g: (l, 0, g, 0, 0))

    def out(rows, cols, dtype):
        return (pl.BlockSpec((None, n_grp, rows, cols), lambda l, g: (l, g, 0, 0)),
                jax.ShapeDtypeStruct((DEPTH, S5_G, rows, cols), dtype))

    outs = [out(n, n, BF16), out(n, 2 * S5_P, BF16), out(n, 2 * S5_P, BF16),
            out(2 * S5_P, n, BF16), out(2 * S5_P, n, BF16), out(8, 2 * S5_P, F32)]
    return pl.pallas_call(
        functools.partial(_s5_prep_kernel, n_grp=n_grp),
        grid=(DEPTH, S5_G // n_grp),
        in_specs=[dir_spec(a) for a in ins] + [pl.BlockSpec((None, n_grp, 1, S5_GC), lambda l, g: (l, g, 0, 0))],
        out_specs=[o[0] for o in outs], out_shape=[o[1] for o in outs],
        compiler_params=_params(("arbitrary", "arbitrary")), name="s5_prep",
    )(*ins, d_grp)


def _block_transpose(a):
    n = len(a)
    blk = lax.broadcasted_iota(jnp.int32, (1, a[0].shape[1]), 1) // S5_GC
    a = list(a)
    bit = 1
    while bit < n:
        upper = (blk & bit) != 0
        for i in range(n):
            if i & bit == 0:
                j = i | bit
                lo, hi = a[i], a[j]
                a[i] = jnp.where(upper, pltpu.roll(hi, bit * S5_GC, 1), lo)
                a[j] = jnp.where(upper, hi, pltpu.roll(lo, LANES - bit * S5_GC, 1))
        bit *= 2
    return a


def _s5_kernel(*refs, n_seq, n_k, has_state):
    if has_state:
        (u_ref, t_ref, bcf_ref, bcb_ref, ccf_ref, ccb_ref, lam_ref, x0_ref,
         y_ref, ug_ref, yg_ref, inj_f, inj_b, inj_s, xin_f, xin_b) = refs
    else:
        (u_ref, t_ref, bcf_ref, bcb_ref, ccf_ref, ccb_ref, lam_ref,
         y_ref, xf_ref, xb_ref, ug_ref, yg_ref, inj_f, inj_b, inj_s, xin_f, xin_b) = refs
    ng = S5_G // 2
    r = n_seq * n_k
    tok = lambda i: pl.ds(i, r, stride=S5_L)
    u_lo = _block_transpose([u_ref[tok(i), :] for i in range(ng)])
    u_hi = _block_transpose([u_ref[tok(i), :] for i in range(ng, S5_L)])
    for g in range(ng):
        ug_ref[g] = jnp.concatenate([u_lo[g], u_hi[g]], axis=1).astype(BF16)
        inj_f[g] = _dot(ug_ref[g], bcf_ref[g])
        inj_b[g] = _dot(ug_ref[g], bcb_ref[g])

    for d, (inj, xin) in enumerate(((inj_f, xin_f), (inj_b, xin_b))):
        a = [lam_ref[g, 2 * d:2 * d + 1, :] for g in range(ng)]
        bs = [lam_ref[g, 2 * d + 1:2 * d + 2, :] for g in range(ng)]
        for g in range(ng):
            inj_s[g] = pltpu.roll(inj[g], S5_P, 1)
        x = [x0_ref[d, g] if has_state else jnp.zeros((n_seq, 2 * S5_P), F32) for g in range(ng)]
        xs = [pltpu.roll(v, S5_P, 1) for v in x]
        for k in (range(n_k) if d == 0 else range(n_k - 1, -1, -1)):
            rows = pl.ds(k, n_seq, stride=n_k)
            for g in range(ng):
                xin[g, rows, :] = x[g]
                x[g], xs[g] = (a[g] * x[g] + bs[g] * xs[g] + inj[g, rows, :],
                               a[g] * xs[g] - bs[g] * x[g] + inj_s[g, rows, :])
        if not has_state:
            for g in range(ng):
                (xf_ref if d == 0 else xb_ref)[g] = x[g]

    for g in range(ng):
        yg_ref[g] = (_dot(ug_ref[g], t_ref[g]) + _dot(xin_f[g].astype(BF16), ccf_ref[g])
                     + _dot(xin_b[g].astype(BF16), ccb_ref[g]))
    for half in range(S5_L // ng):
        y_i = _block_transpose([yg_ref[g, :, LANES * half:LANES * (half + 1)] for g in range(ng)])
        for i in range(ng):
            y_ref[tok(ng * half + i), :] = y_i[i]


def _s5(u_halves, ops, l, n_seq, n_k, x0):
    has_state = x0 is not None
    n = S5_L * S5_GC
    r = n_seq * n_k
    w = 2 * S5_P
    ng = S5_G // 2
    half_spec = pl.BlockSpec((None, r * S5_L, LANES), lambda j: (j, 0, 0))
    op_spec = lambda a: pl.BlockSpec((None, ng) + a.shape[2:], lambda j: (l, j) + (0,) * (a.ndim - 2))
    args = [u_halves] + list(ops)
    in_specs = [half_spec] + [op_spec(a) for a in ops]
    if has_state:
        args.append(x0)
        in_specs.append(pl.BlockSpec((None, 2, ng, n_seq, w), lambda j: (l, 0, j, 0, 0)))
    out_shape = [jax.ShapeDtypeStruct((2, r * S5_L, LANES), F32)]
    out_specs = [half_spec]
    if not has_state:
        out_shape += [jax.ShapeDtypeStruct((S5_G, n_seq, w), F32)] * 2
        out_specs += [pl.BlockSpec((ng, n_seq, w), lambda j: (j, 0, 0))] * 2
    return pl.pallas_call(
        functools.partial(_s5_kernel, n_seq=n_seq, n_k=n_k, has_state=has_state),
        grid=(2,), in_specs=in_specs, out_specs=out_specs, out_shape=out_shape,
        scratch_shapes=[pltpu.VMEM((ng, r, n), BF16), pltpu.VMEM((ng, r, n), F32)]
        + [pltpu.VMEM((ng, r, w), F32)] * 5,
        compiler_params=_params(("arbitrary",)), name="s5_scan",
    )(*args)


def _out_ffn_kernel(x_ref, m_ref, r_ref, a_ref, ya_ref, yb_ref, mod_ref, wglu_ref, bglu_ref, wout_ref,
                    n_post_ref, n_pre_ref, n_fpost_ref, wg_ref, wu_ref, wd_ref, o_ref):
    def mod(i):
        return mod_ref[:, i * D_MODEL:(i + 1) * D_MODEL]

    y = jnp.concatenate([ya_ref[...], yb_ref[...]], axis=1)
    gs = 0.5 * y * (1.0 + jnp.tanh(math.sqrt(2.0 / math.pi) * (y + 0.044715 * (y * y * y))))
    s_out = gs * jax.nn.sigmoid(_dot(gs.astype(BF16), wglu_ref[...]) + bglu_ref[...])
    mixed = jnp.concatenate([m_ref[...], r_ref[...], a_ref[...], s_out], axis=1).astype(BF16)
    x1 = x_ref[...] + mod(2) * _rms(_dot(mixed, wout_ref[...]), n_post_ref[...])
    h = (_rms(x1, n_pre_ref[...]) * (1.0 + mod(4)) + mod(3)).astype(BF16)
    act = (_silu(_dot(h, wg_ref[...])) * _dot(h, wu_ref[...])).astype(BF16)
    o_ref[...] = x1 + mod(5) * _rms(_dot(act, wd_ref[...]), n_fpost_ref[...])


def _out_ffn(x, m_out, r_out, a_out, y_halves, mod, l, mod_row, wp):
    n = x.shape[0]
    row = lambda w: pl.BlockSpec((ROW_TILE, w), lambda i: (i, 0))
    half = lambda j: pl.BlockSpec((None, ROW_TILE, LANES), lambda i: (j, i, 0))
    params = [wp[k] for k in ('w_glu', 'b_glu', 'w_out', 'n_mix_post', 'n_ffn_pre', 'n_ffn_post',
                              'w_gate', 'w_up', 'w_down')]
    return pl.pallas_call(
        _out_ffn_kernel,
        grid=(n // ROW_TILE,),
        in_specs=[row(D_MODEL), row(GROUP_W), row(GROUP_W), row(GROUP_W), half(0), half(1),
                  _mod_spec(l, mod_row, ROW_TILE)] + [_layer_spec(a, l, pipeline_mode=pl.Buffered(1)) for a in params],
        out_specs=row(D_MODEL),
        out_shape=jax.ShapeDtypeStruct((n, D_MODEL), F32),
        compiler_params=_params(("arbitrary",)), name="out_ffn",
    )(x, m_out, r_out, a_out, y_halves, y_halves, mod, *params)


def _layer(x, l, n_seq, t, mod, mod_row, wp, s5_ops, lam_init, st):
    ctx = st is None
    res = _in_proj(x, mod, l, mod_row, wp['n_mix_pre'], wp['w_in'], emit_kv=ctx)
    z, gates = res[0], res[2]
    m_res, r_res, a_out = _mixers(z, gates, wp, l, n_seq, t, None if ctx else st[0:5], lam_init, fuse=ctx)
    s_res = _s5(res[1], s5_ops, l, n_seq, t // S5_L, None if ctx else st[5])
    x_new = _out_ffn(x, m_res[0], r_res[0], a_out, s_res[0], mod, l, mod_row, wp)
    if not ctx:
        return x_new, None
    return x_new, (m_res[1], m_res[2], m_res[3], r_res[1], res[3], res[4], s_res[1], s_res[2])


def _pad_to(a, rows, cols=LANES):
    pad = [(0, 0)] * (a.ndim - 2) + [(0, rows - a.shape[-2]), (0, cols - a.shape[-1])]
    return jnp.pad(a, pad)


def kernel(x_prompt, x_sample, state_mlstm_C, state_mlstm_n, state_mlstm_m, state_ret, cache_diff_k, cache_diff_v, state_s5_re, state_s5_im, c, c_ctx, w_ada, b_ada, n_mix_pre, n_mix_post, n_ffn_pre, n_ffn_post, w_in, w_out, m_gate_bias, m_norm, r_decay_logit, r_norm, a_lam_q1, a_lam_k1, a_lam_q2, a_lam_k2, a_norm, s5_lam_re, s5_lam_im, s5_log_dt, s5_b_re, s5_b_im, s5_c_re, s5_c_im, s5_d, s5_w_glu, s5_b_glu, w_ffn_gate, w_ffn_up, w_ffn_down):
    n_ctx, t_ctx, _ = x_prompt.shape
    n_lat, t_lat, _ = x_sample.shape
    past = cache_diff_k.shape[2]

    cond8 = jnp.concatenate([c, c_ctx[None, :], jnp.zeros((8 - n_lat - 1, D_MODEL), F32)], axis=0)
    mod = _ada(cond8, w_ada, b_ada).reshape(DEPTH, 8, 1, 6 * D_MODEL)
    s5_ops = _s5_prep(s5_lam_re, s5_lam_im, s5_log_dt, s5_b_re, s5_b_im, s5_c_re, s5_c_im, s5_d)

    row = lambda a: a[:, None, :]
    wp = dict(
        w_in=jnp.pad(w_in.astype(BF16), ((0, 0), (0, 0), (0, N_MAIN + LANES - w_in.shape[-1]))), w_out=w_out.astype(BF16), w_glu=s5_w_glu.astype(BF16), b_glu=row(s5_b_glu),
        w_gate=w_ffn_gate.astype(BF16), w_up=w_ffn_up.astype(BF16), w_down=w_ffn_down.astype(BF16),
        n_mix_pre=row(n_mix_pre), n_mix_post=row(n_mix_post), n_ffn_pre=row(n_ffn_pre),
        n_ffn_post=row(n_ffn_post), m_norm=row(m_norm), r_norm=row(r_norm), a_norm=row(a_norm),
        m_bias=_pad_to(row(m_gate_bias), 1), r_decay=_pad_to(r_decay_logit, 8),
        a_lam=_pad_to(jnp.stack([a_lam_q1, a_lam_k1, a_lam_q2, a_lam_k2], axis=1), 8))
    lam_inits = [0.8 - 0.6 * math.exp(-0.3 * l) for l in range(DEPTH)]

    x = x_prompt.reshape(n_ctx * t_ctx, D_MODEL)
    new_states = []
    for l in range(DEPTH):
        x, st = _layer(x, l, n_ctx, t_ctx, mod, lambda tok: n_lat, wp, s5_ops, lam_inits[l], None)
        new_states.append(st)
    y_prompt = x.reshape(n_ctx, t_ctx, D_MODEL)

    x0 = jnp.stack([state_s5_re, state_s5_im], axis=-2)
    x0 = x0.transpose(1, 2, 3, 0, 4, 5).reshape(DEPTH, 2, S5_G, n_lat, 2 * S5_P)
    s_ext0 = jnp.concatenate([state_mlstm_C, state_mlstm_n[..., None],
                              jnp.zeros(state_mlstm_n.shape + (LANES - HEAD_DIM - 1,), F32)], axis=-1)
    st = (s_ext0, state_mlstm_m.reshape(n_lat, DEPTH, 2 * N_HEADS, 1), state_ret,
          cache_diff_k.reshape(n_lat, DEPTH, past, GROUP_W), cache_diff_v.reshape(n_lat, DEPTH, past, GROUP_W), x0)
    x = x_sample.reshape(n_lat * t_lat, D_MODEL)
    for l in range(DEPTH):
        x, _ = _layer(x, l, n_lat, t_lat, mod, lambda tok: tok // t_lat, wp, s5_ops, lam_inits[l], st)
    y_sample = x.reshape(n_lat, t_lat, D_MODEL)

    stack = lambda i: jnp.stack([s[i] for s in new_states], axis=1)
    kv = lambda i: jnp.stack([s[i].reshape(n_ctx, t_ctx, N_HEADS, HEAD_DIM) for s in new_states], axis=1)
    xs = jnp.stack([stack(6), stack(7)], axis=2)
    xs = xs.reshape(S5_G, DEPTH, 2, n_ctx, 2, S5_P).transpose(3, 1, 2, 0, 4, 5)
    return (y_prompt, y_sample, stack(0), stack(1)[..., 0], stack(2).reshape(n_ctx, DEPTH, 2, N_HEADS),
            stack(3), kv(4), kv(5), xs[..., 0, :], xs[..., 1, :])
```

```python
import functools
import math

import jax
import jax.numpy as jnp
from jax import lax
from jax.experimental import pallas as pl
from jax.experimental.pallas import tpu as pltpu

F32 = jnp.float32
BF16 = jnp.bfloat16
HI = lax.Precision.HIGHEST

D_MODEL = 1024
DEPTH = 2
GRID_W = 64
HEAD_DIM = 64
GROUP_W = 256
N_HEADS = 4
A_SUB = 32
S5_GC = 16
S5_G = 16
S5_P = 64
D_FF = 2816
ROPE_BASE = 10000.0
EPS = 1e-6
N_MAIN = 12 * GROUP_W
N_MIX = 11 * GROUP_W
LANES = 128
CHUNK = 256
ROW_TILE = 512
S5_L = 16
NEG = -1e30
VMEM_LIMIT = 56 * 1024 * 1024


def _dot(a, b, precision=None):
    return jnp.dot(a, b, preferred_element_type=F32, precision=precision)


def _dot_nt(a, b):
    return lax.dot_general(a, b, (((1,), (1,)), ((), ())), preferred_element_type=F32)


def _log_sigmoid(x):
    return jnp.minimum(x, 0.0) - jnp.log(1.0 + jnp.exp(-jnp.abs(x)))


def _silu(x):
    return x * jax.nn.sigmoid(x)


def _rms(x, g):
    return x * lax.rsqrt(jnp.mean(x * x, axis=-1, keepdims=True) + EPS) * g


def _params(sem=None):
    return pltpu.CompilerParams(dimension_semantics=sem, vmem_limit_bytes=VMEM_LIMIT)


def _layer_spec(a, l, **kw):
    n = a.ndim - 1
    return pl.BlockSpec((None,) + a.shape[1:], lambda *_: (l,) + (0,) * n, **kw)


def _seq_layer_spec(a, l):
    n = a.ndim - 2
    return pl.BlockSpec((None, None) + a.shape[2:], lambda b: (b, l) + (0,) * n)


def _split_bf16(x):
    hi = x.astype(BF16)
    return hi, (x - hi.astype(F32)).astype(BF16)


def _ada_kernel(c_ref, w_ref, b_ref, o_ref):
    a_hi, a_lo = _split_bf16(_silu(c_ref[...]))
    w_hi, w_lo = _split_bf16(w_ref[...])
    o_ref[...] = _dot(a_hi, w_hi) + (_dot(a_lo, w_hi) + _dot(a_hi, w_lo)) + b_ref[...]


def _ada(cond8, w_ada, b_ada):
    tn = 1536
    return pl.pallas_call(
        _ada_kernel,
        grid=(DEPTH, 6 * D_MODEL // tn),
        in_specs=[pl.BlockSpec((8, D_MODEL), lambda l, j: (0, 0)),
                  pl.BlockSpec((None, D_MODEL, tn), lambda l, j: (l, 0, j)),
                  pl.BlockSpec((None, 1, tn), lambda l, j: (l, 0, j))],
        out_specs=pl.BlockSpec((None, 8, tn), lambda l, j: (l, 0, j)),
        out_shape=jax.ShapeDtypeStruct((DEPTH, 8, 6 * D_MODEL), F32),
        compiler_params=_params(("arbitrary", "arbitrary")),
        name="ada",
    )(cond8, w_ada, b_ada.reshape(DEPTH, 1, 6 * D_MODEL))


def _in_proj_kernel(x_ref, mod_ref, g_ref, w_ref, z_ref, u_ref, gate_ref, *kv_refs):
    h = _rms(x_ref[...], g_ref[...]) * (1.0 + mod_ref[:, D_MODEL:2 * D_MODEL]) + mod_ref[:, 0:D_MODEL]
    hb = h.astype(BF16)
    n_gate = 4 * N_HEADS
    n_head = 4 * GROUP_W
    z_ref[:, 0:n_head] = _dot(hb, w_ref[:, 0:n_head]).astype(BF16)
    tail = _dot(hb, w_ref[:, n_head:])
    gate_ref[...] = tail[:, 0:LANES]
    rest = tail[:, n_gate:n_gate + N_MAIN - n_head]
    z_ref[:, n_head:] = rest[:, 0:N_MIX - n_head].astype(BF16)
    u_ref[0] = rest[:, N_MIX - n_head:N_MIX - n_head + LANES]
    u_ref[1] = rest[:, N_MIX - n_head + LANES:]
    if kv_refs:
        kv_refs[0][...] = rest[:, 9 * GROUP_W - n_head:10 * GROUP_W - n_head]
        kv_refs[1][...] = rest[:, 10 * GROUP_W - n_head:11 * GROUP_W - n_head]


def _mod_spec(l, mod_row):
    return pl.BlockSpec((None, None, 1, 6 * D_MODEL), lambda i: (l, mod_row(i), 0, 0))


def _in_proj(x, mod, l, mod_row, gain, w, emit_kv):
    n = x.shape[0]
    row = lambda width: pl.BlockSpec((ROW_TILE, width), lambda i: (i, 0))
    out_specs = [row(N_MIX), pl.BlockSpec((2, ROW_TILE, LANES), lambda i: (0, i, 0)), row(LANES)]
    out_specs += [row(GROUP_W)] * (2 if emit_kv else 0)
    out_shape = [jax.ShapeDtypeStruct((n, N_MIX), BF16), jax.ShapeDtypeStruct((2, n, LANES), F32),
                 jax.ShapeDtypeStruct((n, LANES), F32)]
    out_shape += [jax.ShapeDtypeStruct((n, GROUP_W), F32)] * (2 if emit_kv else 0)
    return pl.pallas_call(
        _in_proj_kernel,
        grid=(n // ROW_TILE,),
        in_specs=[row(D_MODEL), _mod_spec(l, mod_row), _layer_spec(gain, l),
                  _layer_spec(w, l, pipeline_mode=pl.Buffered(1))],
        out_specs=out_specs, out_shape=out_shape,
        compiler_params=_params(("arbitrary",)),
        name="in_proj",
    )(x, mod, gain, w)


def _tri_masks(n):
    row = lax.broadcasted_iota(jnp.int32, (n, n), 0)
    col = lax.broadcasted_iota(jnp.int32, (n, n), 1)
    return row >= col, row <= col


def _head_mean_matrix():
    r = lax.broadcasted_iota(jnp.int32, (GROUP_W, GROUP_W), 0) // HEAD_DIM
    c = lax.broadcasted_iota(jnp.int32, (GROUP_W, GROUP_W), 1) // HEAD_DIM
    return jnp.where(r == c, 1.0 / HEAD_DIM, 0.0).astype(BF16)


def _head_mean(x, j):
    hi, lo = _split_bf16(x)
    return _dot(hi, j) + _dot(lo, j)


def _head_groupnorm(x, g, j):
    xc = x - _head_mean(x, j)
    return xc * lax.rsqrt(_head_mean(xc * xc, j) + EPS) * g


def _head_rms(x, g, j):
    return x * lax.rsqrt(_head_mean(x * x, j) + EPS) * g


def _hs(h):
    return slice(HEAD_DIM * h, HEAD_DIM * (h + 1))


def _v_ext(v_ref, rows, hs):
    ones = jnp.ones((rows.stop - rows.start, HEAD_DIM), BF16)
    return jnp.concatenate([v_ref[rows, hs].astype(BF16), ones], axis=1)


def _scan_max(x, reverse):
    n = x.shape[1]
    lane = lax.broadcasted_iota(jnp.int32, x.shape, 1)
    sh = 1
    while sh < n:
        if reverse:
            x = jnp.maximum(x, jnp.where(lane < n - sh, pltpu.roll(x, n - sh, 1), NEG))
        else:
            x = jnp.maximum(x, jnp.where(lane >= sh, pltpu.roll(x, sh, 1), NEG))
        sh *= 2
    return x


def _ends(x, is_fwd):
    return jnp.where(is_fwd, x[:, x.shape[1] - 1:], x[:, 0:1])


def _mlstm_kernel(*refs, n_chunk, has_state):
    if has_state:
        q_ref, k_ref, v_ref, o_ref, g_ref, bias_ref, gain_ref, s0_ref, m0_ref, out_ref = refs
    else:
        q_ref, k_ref, v_ref, o_ref, g_ref, bias_ref, gain_ref, out_ref, cn_ref, nn_ref, mn_ref = refs
    L = CHUNK
    nu = 2 * N_HEADS
    tril, triu = _tri_masks(L)
    is_fwd = lax.broadcasted_iota(jnp.int32, (nu, 1), 0) < N_HEADS

    b8, w8, mcum8 = [], [], []
    for c in range(n_chunk):
        p_t = (g_ref[c * L:(c + 1) * L, :] + bias_ref[...]).T
        lf = _log_sigmoid(p_t[nu:2 * nu, :])
        b = jnp.where(is_fwd, _dot(lf, triu.astype(F32), HI), _dot(lf, tril.astype(F32), HI))
        w = p_t[0:nu, :] - b
        b8.append(b)
        w8.append(w)
        mcum8.append(jnp.where(is_fwd, _scan_max(w, False), _scan_max(w, True)))

    m0 = m0_ref[...] if has_state else jnp.zeros((nu, 1), F32)
    m_in_f, m_in_b = [None] * n_chunk, [None] * n_chunk
    m = m0
    for c in range(n_chunk):
        m_in_f[c] = m
        m = (b8[c] + jnp.maximum(mcum8[c], m))[:, L - 1:]
    m_fin_f = m
    m = m0
    for c in range(n_chunk - 1, -1, -1):
        m_in_b[c] = m
        m = (b8[c] + jnp.maximum(mcum8[c], m))[:, 0:1]
    if not has_state:
        mn_ref[...] = jnp.where(is_fwd, m_fin_f, m)

    wk8, dec8, cols = [], [], []
    for c in range(n_chunk):
        m_in = jnp.where(is_fwd, m_in_f[c], m_in_b[c])
        g = jnp.maximum(mcum8[c], m_in)
        m_row = b8[c] + g
        m_new, b_last = _ends(m_row, is_fwd), _ends(b8[c], is_fwd)
        wk8.append(jnp.exp(b_last + w8[c] - m_new))
        dec8.append(jnp.exp(b_last + m_in - m_new))
        stats = jnp.concatenate([g, jnp.exp(m_in - g), jnp.exp(-m_row), jnp.zeros((LANES - 3 * nu, L), F32)], axis=0)
        cols.append(stats.T)

    k_t = [(k_ref[c * L:(c + 1) * L, :].astype(F32) * (HEAD_DIM ** -0.5)).T for c in range(n_chunk)]

    h_all = []
    for h in range(N_HEADS):
        hs = _hs(h)
        qb = [q_ref[c * L:(c + 1) * L, hs].astype(BF16) for c in range(n_chunk)]
        kt = [k_t[c][hs, :] for c in range(n_chunk)]
        vx = [_v_ext(v_ref, slice(c * L, (c + 1) * L), hs) for c in range(n_chunk)]
        qk = [_dot(qb[c], kt[c].astype(BF16)) for c in range(n_chunk)]
        h_sum = [None] * n_chunk
        for d in range(2):
            tri = tril if d == 0 else triu
            j = N_HEADS * d + h
            s_ext = s0_ref[d, h] if has_state else None
            order = range(n_chunk) if d == 0 else range(n_chunk - 1, -1, -1)
            for ci, c in enumerate(order):
                wgt = jnp.exp(jnp.where(tri, w8[c][j:j + 1, :] - cols[c][:, j:j + 1], NEG))
                tot = _dot((qk[c] * wgt).astype(BF16), vx[c])
                if s_ext is not None:
                    tot = tot + cols[c][:, nu + j:nu + j + 1] * _dot(qb[c], s_ext.astype(BF16))
                den = jnp.maximum(jnp.abs(tot[:, HEAD_DIM:HEAD_DIM + 1]), cols[c][:, 2 * nu + j:2 * nu + j + 1])
                hd = tot[:, 0:HEAD_DIM] / den
                h_sum[c] = hd if h_sum[c] is None else h_sum[c] + hd
                if (not has_state) or ci < n_chunk - 1:
                    upd = _dot((kt[c] * wk8[c][j:j + 1, :]).astype(BF16), vx[c])
                    s_ext = upd if s_ext is None else upd + dec8[c][j:j + 1, :] * s_ext
            if not has_state:
                cn_ref[d, h] = s_ext[:, 0:HEAD_DIM]
                nn_ref[d, h] = s_ext[:, HEAD_DIM:HEAD_DIM + 1]
        h_all.append(h_sum)
    jm = _head_mean_matrix()
    for c in range(n_chunk):
        rows = slice(c * L, (c + 1) * L)
        gated = jax.nn.sigmoid(o_ref[rows, :].astype(F32)) * jnp.concatenate([h_all[h][c] for h in range(N_HEADS)], axis=1)
        out_ref[rows, :] = _head_groupnorm(gated, gain_ref[...], jm)


def _ret_kernel(*refs, n_chunk, has_state):
    if has_state:
        q_ref, k_ref, v_ref, g_ref, lg_ref, gain_ref, s0_ref, out_ref, dmat_ref, col_ref, row_ref = refs
    else:
        q_ref, k_ref, v_ref, g_ref, lg_ref, gain_ref, out_ref, sn_ref, dmat_ref, col_ref, row_ref = refs
    L = CHUNK

    @pl.when(pl.program_id(0) == 0)
    def _():
        row = lax.broadcasted_iota(jnp.int32, (L, L), 0)
        col = lax.broadcasted_iota(jnp.int32, (L, L), 1)
        diff = (row - col).astype(F32)
        log_g = _log_sigmoid(lg_ref[...])
        pos_c = lax.broadcasted_iota(jnp.int32, (L, 1), 0).astype(F32)
        pos_r = lax.broadcasted_iota(jnp.int32, (1, L), 1).astype(F32)
        for h in range(N_HEADS):
            lgf = log_g[0:1, h:h + 1]
            lgb = log_g[1:2, h:h + 1]
            dmat_ref[h] = (jnp.where(diff >= 0, jnp.exp(jnp.maximum(diff, 0.0) * lgf), 0.0)
                           + jnp.where(diff <= 0, jnp.exp(jnp.maximum(-diff, 0.0) * lgb), 0.0))
            col_ref[:, h:h + 1] = jnp.exp((pos_c + 1.0) * lgf)
            col_ref[:, N_HEADS + h:N_HEADS + h + 1] = jnp.exp((L - pos_c) * lgb)
            row_ref[h:h + 1, :] = jnp.exp((L - 1.0 - pos_r) * lgf)
            row_ref[N_HEADS + h:N_HEADS + h + 1, :] = jnp.exp(pos_r * lgb)
            row_ref[2 * N_HEADS + h:2 * N_HEADS + h + 1, :] = jnp.exp(L * lgf) + jnp.zeros((1, L), F32)
            row_ref[3 * N_HEADS + h:3 * N_HEADS + h + 1, :] = jnp.exp(L * lgb) + jnp.zeros((1, L), F32)

    k_t = [(k_ref[c * L:(c + 1) * L, :].astype(F32) * (HEAD_DIM ** -0.5)).T for c in range(n_chunk)]
    o_all = []
    for h in range(N_HEADS):
        hs = _hs(h)
        zeta_f = row_ref[h:h + 1, :]
        zeta_b = row_ref[N_HEADS + h:N_HEADS + h + 1, :]
        gl_f = row_ref[2 * N_HEADS + h:2 * N_HEADS + h + 1, 0:1]
        gl_b = row_ref[3 * N_HEADS + h:3 * N_HEADS + h + 1, 0:1]
        kt = [k_t[c][hs, :] for c in range(n_chunk)]
        vb = [v_ref[c * L:(c + 1) * L, hs].astype(BF16) for c in range(n_chunk)]
        sf = [None] * (n_chunk + 1)
        sb = [None] * (n_chunk + 1)
        if has_state:
            sf[0] = s0_ref[0, h]
            sb[n_chunk] = s0_ref[1, h]
        for c in range(n_chunk):
            if (not has_state) or c < n_chunk - 1:
                upd = _dot((kt[c] * zeta_f).astype(BF16), vb[c])
                sf[c + 1] = upd if sf[c] is None else upd + gl_f * sf[c]
        for c in range(n_chunk - 1, -1, -1):
            if (not has_state) or c > 0:
                upd = _dot((kt[c] * zeta_b).astype(BF16), vb[c])
                sb[c] = upd if sb[c + 1] is None else upd + gl_b * sb[c + 1]
        if not has_state:
            sn_ref[0, h] = sf[n_chunk]
            sn_ref[1, h] = sb[0]
        o_h = []
        for c in range(n_chunk):
            qb = q_ref[c * L:(c + 1) * L, hs].astype(BF16)
            att = _dot(qb, kt[c].astype(BF16)) * dmat_ref[h]
            o = _dot(att.astype(BF16), vb[c])
            if has_state:
                o = o + col_ref[:, h:h + 1] * _dot(qb, sf[c].astype(BF16))
                o = o + col_ref[:, N_HEADS + h:N_HEADS + h + 1] * _dot(qb, sb[c + 1].astype(BF16))
            o_h.append(o)
        o_all.append(o_h)
    jm = _head_mean_matrix()
    for c in range(n_chunk):
        rows = slice(c * L, (c + 1) * L)
        o = jnp.concatenate([o_all[h][c] for h in range(N_HEADS)], axis=1)
        out_ref[rows, :] = _silu(g_ref[rows, :].astype(F32)) * _head_groupnorm(o, gain_ref[...], jm)


def _lambda(lam_ref, lam_init):
    v = lam_ref[...]
    s1 = jnp.sum(v[0:1, :] * v[1:2, :], axis=-1, keepdims=True)
    s2 = jnp.sum(v[2:3, :] * v[3:4, :], axis=-1, keepdims=True)
    return jnp.exp(s1) - jnp.exp(s2) + lam_init


def _first_half(shape):
    lane = lax.broadcasted_iota(jnp.int32, shape, 1)
    return (lane % (A_SUB // 2)) < (A_SUB // 4)


def _rope(x, cos, sin_signed):
    n = x.shape[1]
    first_half = _first_half(x.shape)
    partner = jnp.where(first_half, pltpu.roll(x, n - A_SUB // 4, 1), pltpu.roll(x, A_SUB // 4, 1))
    return x * cos + partner * sin_signed


def _attn_kernel(*refs, t, has_cache, lam_init):
    if has_cache:
        q_ref, k_ref, v_ref, kc_ref, vc_ref, lam_ref, gain_ref, out_ref, qr_ref, kr_ref, cos_ref, sin_ref = refs
    else:
        q_ref, k_ref, v_ref, lam_ref, gain_ref, out_ref, qr_ref, kr_ref = refs
    lam = _lambda(lam_ref, lam_init)
    scale = A_SUB ** -0.5
    qb = 2 * CHUNK if has_cache else CHUNK
    n_qb = t // qb

    if has_cache:
        @pl.when(pl.program_id(0) == 0)
        def _():
            lane = lax.broadcasted_iota(jnp.int32, (GRID_W, GROUP_W), 1)
            pos = lax.broadcasted_iota(jnp.int32, (GRID_W, GROUP_W), 0).astype(F32)
            n_freq = A_SUB // 4
            ang = pos * jnp.exp((lane % n_freq).astype(F32) * (-math.log(ROPE_BASE) / n_freq))
            cos_t = jnp.cos(ang)
            sin_t = jnp.where(_first_half((GRID_W, GROUP_W)), -jnp.sin(ang), jnp.sin(ang))
            row_axis = (lane % A_SUB) < (A_SUB // 2)
            for r in range(t // GRID_W):
                rows = slice(r * GRID_W, (r + 1) * GRID_W)
                cos_ref[rows, :] = jnp.where(row_axis, cos_t[r:r + 1, :], cos_t)
                sin_ref[rows, :] = jnp.where(row_axis, sin_t[r:r + 1, :], sin_t)

        kr_ref[...] = _rope(k_ref[...].astype(F32), cos_ref[...], sin_ref[...]).astype(BF16)
        qr_ref[...] = (_rope(q_ref[...].astype(F32), cos_ref[...], sin_ref[...]) * scale).astype(BF16)
    else:
        kr_ref[...] = k_ref[...].astype(BF16)
        qr_ref[...] = (q_ref[...].astype(F32) * scale).astype(BF16)

    vxs = [_v_ext(v_ref, slice(0, t), _hs(h)) for h in range(N_HEADS)]
    if has_cache:
        vcxs = [_v_ext(vc_ref, slice(0, vc_ref.shape[0]), _hs(h)) for h in range(N_HEADS)]

    def softmax_v(rows, units):
        ms = [slice(HEAD_DIM * h + A_SUB * m, HEAD_DIM * h + A_SUB * (m + 1)) for h, m in units]
        qm = [qr_ref[rows, c] for c in ms]
        s_l = [_dot_nt(q, kr_ref[:, c]) for q, c in zip(qm, ms)]
        mx = [jnp.max(s, axis=-1, keepdims=True) for s in s_l]
        if has_cache:
            s_c = [_dot_nt(q, kc_ref[:, c].astype(BF16)) for q, c in zip(qm, ms)]
            mx = [jnp.maximum(a, jnp.max(s, axis=-1, keepdims=True)) for a, s in zip(mx, s_c)]
        tot = [_dot(jnp.exp(s - a).astype(BF16), vxs[h]) for s, a, (h, m) in zip(s_l, mx, units)]
        if has_cache:
            tot = [tt + _dot(jnp.exp(s - a).astype(BF16), vcxs[h]) for tt, s, a, (h, m) in zip(tot, s_c, mx, units)]
        return [tt[:, 0:HEAD_DIM] / tt[:, HEAD_DIM:HEAD_DIM + 1] for tt in tot]

    o_all = [[None] * n_qb for _ in range(N_HEADS)]
    for h in range(N_HEADS):
        for qi in range(n_qb):
            rows = slice(qi * qb, (qi + 1) * qb)
            units = [(h, 0), (h, 1)]
            o = softmax_v(rows, units) if has_cache else [softmax_v(rows, [u])[0] for u in units]
            o_all[h][qi] = o[0] - lam * o[1]
    jm = _head_mean_matrix()
    for qi in range(n_qb):
        rows = slice(qi * qb, (qi + 1) * qb)
        o = jnp.concatenate([o_all[h][qi] for h in range(N_HEADS)], axis=1)
        out_ref[rows, :] = _head_rms(o, gain_ref[...], jm) * (1.0 - lam_init)


def _mixers_kernel(*refs, n_chunk, t, has_state, lam_init, n_in, n_out, kinds, n_sub):
    n_io = sum(n_in) + sum(n_out)
    io, scratch = refs[:n_io], refs[n_io:]
    for sub in range(n_sub):
        def view(r, kind):
            if kind == 'rows':
                return r.at[pl.ds(sub * t, t)]
            return r.at[sub] if kind == 'seq' else r

        v = [view(r, k) for r, k in zip(io, kinds)]
        ins, outs = v[:sum(n_in)], v[sum(n_in):]
        i0, o0 = 0, 0
        parts = []
        for k in range(3):
            parts.append((ins[i0:i0 + n_in[k]], outs[o0:o0 + n_out[k]]))
            i0 += n_in[k]
            o0 += n_out[k]
        a_scratch = [r.at[sub] for r in scratch[3:]]
        _mlstm_kernel(*parts[0][0], *parts[0][1], n_chunk=n_chunk, has_state=has_state)
        _ret_kernel(*parts[1][0], *parts[1][1], *scratch[0:3], n_chunk=n_chunk, has_state=has_state)
        _attn_kernel(*parts[2][0], *parts[2][1], *a_scratch, t=t, has_cache=has_state, lam_init=lam_init)


def _mixers(z, gates, wp, l, n_seq, t, st, lam_init, fuse):
    has_state = st is not None
    nu = 2 * N_HEADS
    n_sub = 2 if fuse else 1
    tt = n_sub * t
    seq = lambda width: pl.BlockSpec((tt, width), lambda b: (b, 0))
    zspec = lambda colblk: pl.BlockSpec((tt, GROUP_W), lambda b: (b, colblk))
    out_rows = jax.ShapeDtypeStruct((n_seq * t, GROUP_W), F32)
    head_state = (2, N_HEADS, HEAD_DIM, HEAD_DIM)
    lead = (n_sub,) if fuse else (None,)
    per_seq = lambda *shape: pl.BlockSpec(lead + shape, lambda b: (b,) + (0,) * len(shape))

    m_in = [z, z, z, z, gates, wp['m_bias'], wp['m_norm']]
    m_specs = [zspec(0), zspec(1), zspec(2), zspec(3), seq(LANES),
               _layer_spec(wp['m_bias'], l), _layer_spec(wp['m_norm'], l)]
    m_kinds = ['rows'] * 5 + ['shared'] * 2
    r_in = [z, z, z, z, wp['r_decay'], wp['r_norm']]
    r_specs = [zspec(4), zspec(5), zspec(6), zspec(7),
               _layer_spec(wp['r_decay'], l), _layer_spec(wp['r_norm'], l)]
    r_kinds = ['rows'] * 4 + ['shared'] * 2
    a_in = [z, z, z]
    a_specs = [zspec(8), zspec(9), zspec(10)]
    a_kinds = ['rows'] * 3
    m_out, r_out, a_out = [out_rows], [out_rows], [out_rows]
    m_ospecs, r_ospecs, a_ospecs = [seq(GROUP_W)], [seq(GROUP_W)], [seq(GROUP_W)]
    mo_kinds, ro_kinds, ao_kinds = ['rows'], ['rows'], ['rows']
    a_lead = (n_sub,) if fuse else ()
    scratch = [pltpu.VMEM((N_HEADS, CHUNK, CHUNK), F32), pltpu.VMEM((CHUNK, LANES), F32),
               pltpu.VMEM((4 * N_HEADS, CHUNK), F32),
               pltpu.VMEM(a_lead + (t, GROUP_W), BF16), pltpu.VMEM(a_lead + (t, GROUP_W), BF16)]
    if has_state:
        assert not fuse
        m_in += list(st[0:2])
        m_specs += [_seq_layer_spec(a, l) for a in st[0:2]]
        r_in.append(st[2])
        r_specs.append(_seq_layer_spec(st[2], l))
        a_in += list(st[3:5])
        a_specs += [_seq_layer_spec(a, l) for a in st[3:5]]
        scratch += [pltpu.VMEM((t, GROUP_W), F32), pltpu.VMEM((t, GROUP_W), F32)]
    else:
        m_out += [jax.ShapeDtypeStruct((n_seq,) + head_state, F32),
                  jax.ShapeDtypeStruct((n_seq, 2, N_HEADS, HEAD_DIM, 1), F32),
                  jax.ShapeDtypeStruct((n_seq, nu, 1), F32)]
        m_ospecs += [per_seq(*head_state), per_seq(2, N_HEADS, HEAD_DIM, 1), per_seq(nu, 1)]
        mo_kinds += ['seq'] * 3
        r_out.append(jax.ShapeDtypeStruct((n_seq,) + head_state, F32))
        r_ospecs.append(per_seq(*head_state))
        ro_kinds.append('seq')
    a_in += [wp['a_lam'], wp['a_norm']]
    a_specs += [_layer_spec(wp['a_lam'], l), _layer_spec(wp['a_norm'], l)]
    a_kinds += ['shared'] * 2
    n_chunk = t // CHUNK
    call = functools.partial(pl.pallas_call, grid=(n_seq // n_sub,), compiler_params=_params(("arbitrary",)))
    if fuse:
        res = call(
            functools.partial(_mixers_kernel, n_chunk=n_chunk, t=t, has_state=has_state, lam_init=lam_init,
                              n_in=(len(m_in), len(r_in), len(a_in)), n_out=(len(m_out), len(r_out), len(a_out)),
                              kinds=tuple(m_kinds + r_kinds + a_kinds + mo_kinds + ro_kinds + ao_kinds),
                              n_sub=n_sub),
            in_specs=m_specs + r_specs + a_specs, out_specs=m_ospecs + r_ospecs + a_ospecs,
            out_shape=m_out + r_out + a_out, scratch_shapes=scratch, name="mixers",
        )(*m_in, *r_in, *a_in)
        return res[:len(m_out)], res[len(m_out):len(m_out) + len(r_out)], res[-1]
    m_res = call(functools.partial(_mlstm_kernel, n_chunk=n_chunk, has_state=has_state),
                 in_specs=m_specs, out_specs=m_ospecs, out_shape=m_out, name="mlstm")(*m_in)
    r_res = call(functools.partial(_ret_kernel, n_chunk=n_chunk, has_state=has_state),
                 in_specs=r_specs, out_specs=r_ospecs, out_shape=r_out, scratch_shapes=scratch[0:3],
                 name="retention")(*r_in)
    a_res = call(functools.partial(_attn_kernel, t=t, has_cache=has_state, lam_init=lam_init),
                 in_specs=a_specs, out_specs=a_ospecs, out_shape=a_out, scratch_shapes=scratch[3:],
                 name="diff_attention")(*a_in)
    return m_res, r_res, a_res[0]


def _cmul(ar, ai, br, bi):
    return ar * br - ai * bi, ar * bi + ai * br


def _lam_bar(lre, lim, dt):
    mag = jnp.exp(lre * dt)
    return mag * jnp.cos(lim * dt), mag * jnp.sin(lim * dt)


def _cpow_int(br, bi, e, n_bits):
    pr = jnp.ones(e.shape, F32)
    pi = jnp.zeros(e.shape, F32)
    for bit in range(n_bits):
        on = ((e >> bit) & 1) == 1
        qr, qi = _cmul(pr, pi, br, bi)
        pr, pi = jnp.where(on, qr, pr), jnp.where(on, qi, pi)
        if bit + 1 < n_bits:
            br, bi = _cmul(br, bi, br, bi)
    return pr, pi


def _zoh_coef(lre, lim, br, bi):
    den = lre * lre + lim * lim
    return ((br - 1.0) * lre + bi * lim) / den, (bi * lre - (br - 1.0) * lim) / den


def _toeplitz_rows(kall):
    n = S5_L * S5_GC
    lane = lax.broadcasted_iota(jnp.int32, (S5_GC, n), 1)
    pieces = []
    for s in range(S5_L):
        shifted = kall if s == 0 else pltpu.roll(kall, S5_GC * s, 1)
        pieces.append(jnp.where(lane >= S5_GC * s, shifted, 0.0))
    return jnp.concatenate(pieces, axis=0)


def _s5_prep_group(lam_c_re, lam_c_im, lam_r_re, lam_r_im, ldt_ref, b_t_re, b_t_im, b_re, b_im,
                   c_re, c_im, c_t_re, c_t_im, d_ref, t_ref, bcf_ref, bcb_ref, ccf_ref, ccb_ref, lam_ref):
    n = S5_L * S5_GC
    lane_blk = lax.broadcasted_iota(jnp.int32, (S5_P, n), 1) // S5_GC
    bits = S5_L.bit_length()

    def bases(d):
        dt = jnp.exp(ldt_ref[d])
        return (_lam_bar(lam_c_re[d], lam_c_im[d], dt), _lam_bar(lam_r_re[d], lam_r_im[d], dt))

    def tile_rows(x):
        return jnp.concatenate([x] * S5_L, axis=0)

    def rows_pow(br, bi, reverse):
        s_idx = lax.broadcasted_iota(jnp.int32, (S5_L, S5_P), 0)
        pr, pi = _cpow_int(br, bi, S5_L - 1 - s_idx if reverse else s_idx, bits)
        rep = lambda p: jnp.concatenate(
            [jnp.broadcast_to(p[s:s + 1, :], (S5_GC, S5_P)) for s in range(S5_L)], axis=0)
        return rep(pr), rep(pi)

    def tile_lanes(ref, d):
        return jnp.concatenate([ref[d]] * S5_L, axis=1)

    ct_re, ct_im = [tile_lanes(c_t_re, d) for d in range(2)], [tile_lanes(c_t_im, d) for d in range(2)]

    (cbr, cbi), (rbr, rbi) = bases(0)
    yield
    kr, ki = _zoh_coef(lam_r_re[0], lam_r_im[0], rbr, rbi)
    bbt_re, bbt_im = _cmul(kr, ki, b_t_re[0], b_t_im[0])
    pr, pi = _cpow_int(cbr, cbi, lane_blk, bits)
    yield
    cl_re, cl_im = _cmul(ct_re[0], ct_im[0], pr, pi)
    yield
    t_f = _toeplitz_rows(_dot(bbt_re, cl_re, HI) - _dot(bbt_im, cl_im, HI))
    yield
    pr, pi = rows_pow(rbr, rbi, True)
    yield
    re, im = _cmul(tile_rows(bbt_re), tile_rows(bbt_im), pr, pi)
    bcf_ref[...] = jnp.concatenate([re, im], axis=1).astype(BF16)
    re, im = _cmul(cl_re, cl_im, cbr, cbi)
    ccf_ref[...] = jnp.concatenate([re, -im], axis=0).astype(BF16)
    pr, pi = _cpow_int(rbr, rbi, jnp.full((1, S5_P), S5_L, jnp.int32), bits)
    yield
    lam_ref[0:1, :] = jnp.concatenate([pr, pr], axis=1)
    lam_ref[1:2, :] = jnp.concatenate([-pi, pi], axis=1)

    (cbr, cbi), (rbr, rbi) = bases(1)
    yield
    kr, ki = _zoh_coef(lam_c_re[1], lam_c_im[1], cbr, cbi)
    bb_re, bb_im = _cmul(kr, ki, tile_lanes(b_re, 1), tile_lanes(b_im, 1))
    pr, pi = _cpow_int(cbr, cbi, lane_blk, bits)
    yield
    bl_re, bl_im = _cmul(bb_re, bb_im, pr, pi)
    yield
    t_b = _toeplitz_rows(_dot(c_re[1], bl_re, HI) - _dot(c_im[1], bl_im, HI)).T
    yield
    kr, ki = _zoh_coef(lam_r_re[1], lam_r_im[1], rbr, rbi)
    bbt_re, bbt_im = _cmul(kr, ki, b_t_re[1], b_t_im[1])
    pr, pi = rows_pow(rbr, rbi, False)
    yield
    re, im = _cmul(tile_rows(bbt_re), tile_rows(bbt_im), pr, pi)
    bcb_ref[...] = jnp.concatenate([re, im], axis=1).astype(BF16)
    pr, pi = _cpow_int(cbr, cbi, S5_L - lane_blk, bits)
    yield
    re, im = _cmul(ct_re[1], ct_im[1], pr, pi)
    ccb_ref[...] = jnp.concatenate([re, -im], axis=0).astype(BF16)
    pr, pi = _cpow_int(rbr, rbi, jnp.full((1, S5_P), S5_L, jnp.int32), bits)
    yield
    lam_ref[2:3, :] = jnp.concatenate([pr, pr], axis=1)
    lam_ref[3:4, :] = jnp.concatenate([-pi, pi], axis=1)
    lam_ref[4:8, :] = jnp.zeros((4, 2 * S5_P), F32)

    eye = (lax.broadcasted_iota(jnp.int32, (n, n), 0) == lax.broadcasted_iota(jnp.int32, (n, n), 1))
    d_diag = jnp.concatenate([d_ref[...]] * S5_L, axis=1)
    t_ref[...] = (t_f + t_b + jnp.where(eye, d_diag, 0.0)).astype(BF16)


def _s5_prep_kernel(*refs, n_grp):
    ins, outs = refs[:14], refs[14:]
    gens = [_s5_prep_group(*[r.at[:, gi] for r in ins[:13]], ins[13].at[gi], *[r.at[gi] for r in outs])
            for gi in range(n_grp)]
    while gens:
        for g in list(gens):
            if next(g, True):
                gens.remove(g)


def _s5_prep(lam_re, lam_im, log_dt, b_re, b_im, c_re, c_im, s5_d):
    n = S5_L * S5_GC
    swap = lambda a: jnp.swapaxes(a, -1, -2)
    ins = [lam_re[..., :, None], lam_im[..., :, None], lam_re[..., None, :], lam_im[..., None, :],
           log_dt[..., None, None], swap(b_re), swap(b_im), b_re, b_im, c_re, c_im, swap(c_re), swap(c_im)]
    d_grp = s5_d.reshape(DEPTH, S5_G, 1, S5_GC)

    n_grp = 4

    def dir_spec(a):
        return pl.BlockSpec((None, 2, n_grp) + a.shape[3:], lambda l, g: (l, 0, g, 0, 0))

    def out(rows, cols, dtype):
        return (pl.BlockSpec((None, n_grp, rows, cols), lambda l, g: (l, g, 0, 0)),
                jax.ShapeDtypeStruct((DEPTH, S5_G, rows, cols), dtype))

    outs = [out(n, n, BF16), out(n, 2 * S5_P, BF16), out(n, 2 * S5_P, BF16),
            out(2 * S5_P, n, BF16), out(2 * S5_P, n, BF16), out(8, 2 * S5_P, F32)]
    return pl.pallas_call(
        functools.partial(_s5_prep_kernel, n_grp=n_grp),
        grid=(DEPTH, S5_G // n_grp),
        in_specs=[dir_spec(a) for a in ins] + [pl.BlockSpec((None, n_grp, 1, S5_GC), lambda l, g: (l, g, 0, 0))],
        out_specs=[o[0] for o in outs], out_shape=[o[1] for o in outs],
        compiler_params=_params(("arbitrary", "arbitrary")), name="s5_prep",
    )(*ins, d_grp)


def _block_transpose(a):
    n = len(a)
    blk = lax.broadcasted_iota(jnp.int32, (1, a[0].shape[1]), 1) // S5_GC
    a = list(a)
    bit = 1
    while bit < n:
        upper = (blk & bit) != 0
        for i in range(n):
            if i & bit == 0:
                j = i | bit
                lo, hi = a[i], a[j]
                a[i] = jnp.where(upper, pltpu.roll(hi, bit * S5_GC, 1), lo)
                a[j] = jnp.where(upper, hi, pltpu.roll(lo, LANES - bit * S5_GC, 1))
        bit *= 2
    return a


def _s5_kernel(*refs, n_seq, n_k, has_state):
    if has_state:
        (u_ref, t_ref, bcf_ref, bcb_ref, ccf_ref, ccb_ref, lam_ref, x0_ref,
         y_ref, ug_ref, yg_ref, inj_f, inj_b, inj_s, xin_f, xin_b) = refs
    else:
        (u_ref, t_ref, bcf_ref, bcb_ref, ccf_ref, ccb_ref, lam_ref,
         y_ref, xf_ref, xb_ref, ug_ref, yg_ref, inj_f, inj_b, inj_s, xin_f, xin_b) = refs
    ng = S5_G // 2
    r = n_seq * n_k
    tok = lambda i: pl.ds(i, r, stride=S5_L)
    u_lo = _block_transpose([u_ref[tok(i), :] for i in range(ng)])
    u_hi = _block_transpose([u_ref[tok(i), :] for i in range(ng, S5_L)])
    for g in range(ng):
        ug_ref[g] = jnp.concatenate([u_lo[g], u_hi[g]], axis=1).astype(BF16)
        inj_f[g] = _dot(ug_ref[g], bcf_ref[g])
        inj_b[g] = _dot(ug_ref[g], bcb_ref[g])

    for d, (inj, xin) in enumerate(((inj_f, xin_f), (inj_b, xin_b))):
        a = [lam_ref[g, 2 * d:2 * d + 1, :] for g in range(ng)]
        bs = [lam_ref[g, 2 * d + 1:2 * d + 2, :] for g in range(ng)]
        for g in range(ng):
            inj_s[g] = pltpu.roll(inj[g], S5_P, 1)
        x = [x0_ref[d, g] if has_state else jnp.zeros((n_seq, 2 * S5_P), F32) for g in range(ng)]
        xs = [pltpu.roll(v, S5_P, 1) for v in x]
        for k in (range(n_k) if d == 0 else range(n_k - 1, -1, -1)):
            rows = pl.ds(k, n_seq, stride=n_k)
            for g in range(ng):
                xin[g, rows, :] = x[g]
                x[g], xs[g] = (a[g] * x[g] + bs[g] * xs[g] + inj[g, rows, :],
                               a[g] * xs[g] - bs[g] * x[g] + inj_s[g, rows, :])
        if not has_state:
            for g in range(ng):
                (xf_ref if d == 0 else xb_ref)[g] = x[g]

    for g in range(ng):
        yg_ref[g] = (_dot(ug_ref[g], t_ref[g]) + _dot(xin_f[g].astype(BF16), ccf_ref[g])
                     + _dot(xin_b[g].astype(BF16), ccb_ref[g]))
    for half in range(S5_L // ng):
        y_i = _block_transpose([yg_ref[g, :, LANES * half:LANES * (half + 1)] for g in range(ng)])
        for i in range(ng):
            y_ref[tok(ng * half + i), :] = y_i[i]


def _s5(u_halves, ops, l, n_seq, n_k, x0):
    has_state = x0 is not None
    n = S5_L * S5_GC
    r = n_seq * n_k
    w = 2 * S5_P
    ng = S5_G // 2
    half_spec = pl.BlockSpec((None, r * S5_L, LANES), lambda j: (j, 0, 0))
    op_spec = lambda a: pl.BlockSpec((None, ng) + a.shape[2:], lambda j: (l, j) + (0,) * (a.ndim - 2))
    args = [u_halves] + list(ops)
    in_specs = [half_spec] + [op_spec(a) for a in ops]
    if has_state:
        args.append(x0)
        in_specs.append(pl.BlockSpec((None, 2, ng, n_seq, w), lambda j: (l, 0, j, 0, 0)))
    out_shape = [jax.ShapeDtypeStruct((2, r * S5_L, LANES), F32)]
    out_specs = [half_spec]
    if not has_state:
        out_shape += [jax.ShapeDtypeStruct((S5_G, n_seq, w), F32)] * 2
        out_specs += [pl.BlockSpec((ng, n_seq, w), lambda j: (j, 0, 0))] * 2
    return pl.pallas_call(
        functools.partial(_s5_kernel, n_seq=n_seq, n_k=n_k, has_state=has_state),
        grid=(2,), in_specs=in_specs, out_specs=out_specs, out_shape=out_shape,
        scratch_shapes=[pltpu.VMEM((ng, r, n), BF16), pltpu.VMEM((ng, r, n), F32)]
        + [pltpu.VMEM((ng, r, w), F32)] * 5,
        compiler_params=_params(("arbitrary",)), name="s5_scan",
    )(*args)


def _out_ffn_kernel(x_ref, m_ref, r_ref, a_ref, ya_ref, yb_ref, mod_ref, wglu_ref, bglu_ref, wout_ref,
                    n_post_ref, n_pre_ref, n_fpost_ref, wg_ref, wu_ref, wd_ref, o_ref):
    def mod(i):
        return mod_ref[:, i * D_MODEL:(i + 1) * D_MODEL]

    y = jnp.concatenate([ya_ref[...], yb_ref[...]], axis=1)
    gs = 0.5 * y * (1.0 + jnp.tanh(math.sqrt(2.0 / math.pi) * (y + 0.044715 * (y * y * y))))
    s_out = gs * jax.nn.sigmoid(_dot(gs.astype(BF16), wglu_ref[...]) + bglu_ref[...])
    mixed = jnp.concatenate([m_ref[...], r_ref[...], a_ref[...], s_out], axis=1).astype(BF16)
    x1 = x_ref[...] + mod(2) * _rms(_dot(mixed, wout_ref[...]), n_post_ref[...])
    h = (_rms(x1, n_pre_ref[...]) * (1.0 + mod(4)) + mod(3)).astype(BF16)
    act = (_silu(_dot(h, wg_ref[...])) * _dot(h, wu_ref[...])).astype(BF16)
    o_ref[...] = x1 + mod(5) * _rms(_dot(act, wd_ref[...]), n_fpost_ref[...])


def _out_ffn(x, m_out, r_out, a_out, y_halves, mod, l, mod_row, wp):
    n = x.shape[0]
    row = lambda w: pl.BlockSpec((ROW_TILE, w), lambda i: (i, 0))
    half = lambda j: pl.BlockSpec((None, ROW_TILE, LANES), lambda i: (j, i, 0))
    params = [wp[k] for k in ('w_glu', 'b_glu', 'w_out', 'n_mix_post', 'n_ffn_pre', 'n_ffn_post',
                              'w_gate', 'w_up', 'w_down')]
    return pl.pallas_call(
        _out_ffn_kernel,
        grid=(n // ROW_TILE,),
        in_specs=[row(D_MODEL), row(GROUP_W), row(GROUP_W), row(GROUP_W), half(0), half(1),
                  _mod_spec(l, mod_row)] + [_layer_spec(a, l, pipeline_mode=pl.Buffered(1)) for a in params],
        out_specs=row(D_MODEL),
        out_shape=jax.ShapeDtypeStruct((n, D_MODEL), F32),
        compiler_params=_params(("arbitrary",)), name="out_ffn",
    )(x, m_out, r_out, a_out, y_halves, y_halves, mod, *params)


def _layer(x, l, n_seq, t, mod, mod_row, wp, s5_ops, lam_init, st):
    ctx = st is None
    res = _in_proj(x, mod, l, mod_row, wp['n_mix_pre'], wp['w_in'], emit_kv=ctx)
    z, gates = res[0], res[2]
    m_res, r_res, a_out = _mixers(z, gates, wp, l, n_seq, t, None if ctx else st[0:5], lam_init, fuse=ctx)
    s_res = _s5(res[1], s5_ops, l, n_seq, t // S5_L, None if ctx else st[5])
    x_new = _out_ffn(x, m_res[0], r_res[0], a_out, s_res[0], mod, l, mod_row, wp)
    if not ctx:
        return x_new, None
    return x_new, (m_res[1], m_res[2], m_res[3], r_res[1], res[3], res[4], s_res[1], s_res[2])


def _pad_to(a, rows, cols=LANES):
    pad = [(0, 0)] * (a.ndim - 2) + [(0, rows - a.shape[-2]), (0, cols - a.shape[-1])]
    return jnp.pad(a, pad)


def kernel(x_prompt, x_sample, state_mlstm_C, state_mlstm_n, state_mlstm_m, state_ret, cache_diff_k, cache_diff_v, state_s5_re, state_s5_im, c, c_ctx, w_ada, b_ada, n_mix_pre, n_mix_post, n_ffn_pre, n_ffn_post, w_in, w_out, m_gate_bias, m_norm, r_decay_logit, r_norm, a_lam_q1, a_lam_k1, a_lam_q2, a_lam_k2, a_norm, s5_lam_re, s5_lam_im, s5_log_dt, s5_b_re, s5_b_im, s5_c_re, s5_c_im, s5_d, s5_w_glu, s5_b_glu, w_ffn_gate, w_ffn_up, w_ffn_down):
    n_ctx, t_ctx, _ = x_prompt.shape
    n_lat, t_lat, _ = x_sample.shape
    past = cache_diff_k.shape[2]

    cond8 = jnp.concatenate([c, c_ctx[None, :], jnp.zeros((8 - n_lat - 1, D_MODEL), F32)], axis=0)
    mod = _ada(cond8, w_ada, b_ada).reshape(DEPTH, 8, 1, 6 * D_MODEL)
    s5_ops = _s5_prep(s5_lam_re, s5_lam_im, s5_log_dt, s5_b_re, s5_b_im, s5_c_re, s5_c_im, s5_d)

    row = lambda a: a[:, None, :]
    wp = dict(
        w_in=jnp.pad(w_in.astype(BF16), ((0, 0), (0, 0), (0, N_MAIN + LANES - w_in.shape[-1]))), w_out=w_out.astype(BF16), w_glu=s5_w_glu.astype(BF16), b_glu=row(s5_b_glu),
        w_gate=w_ffn_gate.astype(BF16), w_up=w_ffn_up.astype(BF16), w_down=w_ffn_down.astype(BF16),
        n_mix_pre=row(n_mix_pre), n_mix_post=row(n_mix_post), n_ffn_pre=row(n_ffn_pre),
        n_ffn_post=row(n_ffn_post), m_norm=row(m_norm), r_norm=row(r_norm), a_norm=row(a_norm),
        m_bias=_pad_to(row(m_gate_bias), 1), r_decay=_pad_to(r_decay_logit, 8),
        a_lam=_pad_to(jnp.stack([a_lam_q1, a_lam_k1, a_lam_q2, a_lam_k2], axis=1), 8))
    lam_inits = [0.8 - 0.6 * math.exp(-0.3 * l) for l in range(DEPTH)]

    x = x_prompt.reshape(n_ctx * t_ctx, D_MODEL)
    new_states = []
    for l in range(DEPTH):
        x, st = _layer(x, l, n_ctx, t_ctx, mod, lambda i: n_lat, wp, s5_ops, lam_inits[l], None)
        new_states.append(st)
    y_prompt = x.reshape(n_ctx, t_ctx, D_MODEL)

    x0 = jnp.stack([state_s5_re, state_s5_im], axis=-2)
    x0 = x0.transpose(1, 2, 3, 0, 4, 5).reshape(DEPTH, 2, S5_G, n_lat, 2 * S5_P)
    s_ext0 = jnp.concatenate([state_mlstm_C, state_mlstm_n[..., None],
                              jnp.zeros(state_mlstm_n.shape + (LANES - HEAD_DIM - 1,), F32)], axis=-1)
    st = (s_ext0, state_mlstm_m.reshape(n_lat, DEPTH, 2 * N_HEADS, 1), state_ret,
          cache_diff_k.reshape(n_lat, DEPTH, past, GROUP_W), cache_diff_v.reshape(n_lat, DEPTH, past, GROUP_W), x0)
    x = x_sample.reshape(n_lat * t_lat, D_MODEL)
    tiles_per_seq = t_lat // ROW_TILE
    for l in range(DEPTH):
        x, _ = _layer(x, l, n_lat, t_lat, mod, lambda i: i // tiles_per_seq, wp, s5_ops, lam_inits[l], st)
    y_sample = x.reshape(n_lat, t_lat, D_MODEL)

    stack = lambda i: jnp.stack([s[i] for s in new_states], axis=1)
    kv = lambda i: jnp.stack([s[i].reshape(n_ctx, t_ctx, N_HEADS, HEAD_DIM) for s in new_states], axis=1)
    xs = jnp.stack([stack(6), stack(7)], axis=2)
    xs = xs.reshape(S5_G, DEPTH, 2, n_ctx, 2, S5_P).transpose(3, 1, 2, 0, 4, 5)
    return (y_prompt, y_sample, stack(0), stack(1)[..., 0], stack(2).reshape(n_ctx, DEPTH, 2, N_HEADS),
            stack(3), kv(4), kv(5), xs[..., 0, :], xs[..., 1, :])
```

```python
import functools
import math

import jax
import jax.numpy as jnp
from jax import lax
from jax.experimental import pallas as pl
from jax.experimental.pallas import tpu as pltpu

F32 = jnp.float32
BF16 = jnp.bfloat16
HI = lax.Precision.HIGHEST

D_MODEL = 1024
DEPTH = 2
GRID_W = 64
HEAD_DIM = 64
GROUP_W = 256
N_HEADS = 4
A_SUB = 32
S5_GC = 16
S5_G = 16
S5_P = 64
D_FF = 2816
ROPE_BASE = 10000.0
EPS = 1e-6
N_MAIN = 12 * GROUP_W
N_MIX = 11 * GROUP_W
LANES = 128
CHUNK = 256
ROW_TILE = 512
S5_L = 16
NEG = -1e30
VMEM_LIMIT = 56 * 1024 * 1024


def _dot(a, b, precision=None):
    return jnp.dot(a, b, preferred_element_type=F32, precision=precision)


def _dot_nt(a, b):
    return lax.dot_general(a, b, (((1,), (1,)), ((), ())), preferred_element_type=F32)


def _log_sigmoid(x):
    return jnp.minimum(x, 0.0) - jnp.log(1.0 + jnp.exp(-jnp.abs(x)))


def _silu(x):
    return x * jax.nn.sigmoid(x)


def _rms(x, g):
    return x * lax.rsqrt(jnp.mean(x * x, axis=-1, keepdims=True) + EPS) * g


def _params(sem=None):
    return pltpu.CompilerParams(dimension_semantics=sem, vmem_limit_bytes=VMEM_LIMIT)


def _layer_spec(a, l, **kw):
    n = a.ndim - 1
    return pl.BlockSpec((None,) + a.shape[1:], lambda *_: (l,) + (0,) * n, **kw)


def _seq_layer_spec(a, l):
    n = a.ndim - 2
    return pl.BlockSpec((None, None) + a.shape[2:], lambda b: (b, l) + (0,) * n)


def _split_bf16(x):
    hi = x.astype(BF16)
    return hi, (x - hi.astype(F32)).astype(BF16)


def _ada_kernel(c_ref, w_ref, b_ref, o_ref):
    a_hi, a_lo = _split_bf16(_silu(c_ref[...]))
    w_hi, w_lo = _split_bf16(w_ref[...])
    o_ref[...] = _dot(a_hi, w_hi) + (_dot(a_lo, w_hi) + _dot(a_hi, w_lo)) + b_ref[...]


def _ada(cond8, w_ada, b_ada):
    tn = 1536
    return pl.pallas_call(
        _ada_kernel,
        grid=(DEPTH, 6 * D_MODEL // tn),
        in_specs=[pl.BlockSpec((8, D_MODEL), lambda l, j: (0, 0)),
                  pl.BlockSpec((None, D_MODEL, tn), lambda l, j: (l, 0, j)),
                  pl.BlockSpec((None, 1, tn), lambda l, j: (l, 0, j))],
        out_specs=pl.BlockSpec((None, 8, tn), lambda l, j: (l, 0, j)),
        out_shape=jax.ShapeDtypeStruct((DEPTH, 8, 6 * D_MODEL), F32),
        compiler_params=_params(("arbitrary", "arbitrary")),
        name="ada",
    )(cond8, w_ada, b_ada.reshape(DEPTH, 1, 6 * D_MODEL))


def _in_proj_kernel(x_ref, mod_ref, g_ref, w_ref, z_ref, u_ref, gate_ref, *kv_refs):
    h = _rms(x_ref[...], g_ref[...]) * (1.0 + mod_ref[:, D_MODEL:2 * D_MODEL]) + mod_ref[:, 0:D_MODEL]
    hb = h.astype(BF16)
    n_gate = 4 * N_HEADS
    n_head = 4 * GROUP_W
    z_ref[:, 0:n_head] = _dot(hb, w_ref[:, 0:n_head]).astype(BF16)
    tail = _dot(hb, w_ref[:, n_head:])
    gate_ref[...] = tail[:, 0:LANES]
    rest = tail[:, n_gate:n_gate + N_MAIN - n_head]
    z_ref[:, n_head:] = rest[:, 0:N_MIX - n_head].astype(BF16)
    u_ref[0] = rest[:, N_MIX - n_head:N_MIX - n_head + LANES]
    u_ref[1] = rest[:, N_MIX - n_head + LANES:]
    if kv_refs:
        kv_refs[0][...] = rest[:, 9 * GROUP_W - n_head:10 * GROUP_W - n_head]
        kv_refs[1][...] = rest[:, 10 * GROUP_W - n_head:11 * GROUP_W - n_head]


def _mod_spec(l, mod_row):
    return pl.BlockSpec((None, None, 1, 6 * D_MODEL), lambda i: (l, mod_row(i), 0, 0))


def _in_proj(x, mod, l, mod_row, gain, w, emit_kv):
    n = x.shape[0]
    row = lambda width: pl.BlockSpec((ROW_TILE, width), lambda i: (i, 0))
    out_specs = [row(N_MIX), pl.BlockSpec((2, ROW_TILE, LANES), lambda i: (0, i, 0)), row(LANES)]
    out_specs += [row(GROUP_W)] * (2 if emit_kv else 0)
    out_shape = [jax.ShapeDtypeStruct((n, N_MIX), BF16), jax.ShapeDtypeStruct((2, n, LANES), F32),
                 jax.ShapeDtypeStruct((n, LANES), F32)]
    out_shape += [jax.ShapeDtypeStruct((n, GROUP_W), F32)] * (2 if emit_kv else 0)
    return pl.pallas_call(
        _in_proj_kernel,
        grid=(n // ROW_TILE,),
        in_specs=[row(D_MODEL), _mod_spec(l, mod_row), _layer_spec(gain, l),
                  _layer_spec(w, l, pipeline_mode=pl.Buffered(1))],
        out_specs=out_specs, out_shape=out_shape,
        compiler_params=_params(("arbitrary",)),
        name="in_proj",
    )(x, mod, gain, w)


def _tri_masks(n):
    row = lax.broadcasted_iota(jnp.int32, (n, n), 0)
    col = lax.broadcasted_iota(jnp.int32, (n, n), 1)
    return row >= col, row <= col


def _head_mean_matrix():
    r = lax.broadcasted_iota(jnp.int32, (GROUP_W, GROUP_W), 0) // HEAD_DIM
    c = lax.broadcasted_iota(jnp.int32, (GROUP_W, GROUP_W), 1) // HEAD_DIM
    return jnp.where(r == c, 1.0 / HEAD_DIM, 0.0).astype(BF16)


def _head_mean(x, j):
    hi, lo = _split_bf16(x)
    return _dot(hi, j) + _dot(lo, j)


def _head_groupnorm(x, g, j):
    xc = x - _head_mean(x, j)
    return xc * lax.rsqrt(_head_mean(xc * xc, j) + EPS) * g


def _head_rms(x, g, j):
    return x * lax.rsqrt(_head_mean(x * x, j) + EPS) * g


def _hs(h):
    return slice(HEAD_DIM * h, HEAD_DIM * (h + 1))


def _v_ext(v_ref, rows, hs):
    ones = jnp.ones((rows.stop - rows.start, HEAD_DIM), BF16)
    return jnp.concatenate([v_ref[rows, hs].astype(BF16), ones], axis=1)


def _scan_max(x, reverse):
    n = x.shape[1]
    lane = lax.broadcasted_iota(jnp.int32, x.shape, 1)
    sh = 1
    while sh < n:
        if reverse:
            x = jnp.maximum(x, jnp.where(lane < n - sh, pltpu.roll(x, n - sh, 1), NEG))
        else:
            x = jnp.maximum(x, jnp.where(lane >= sh, pltpu.roll(x, sh, 1), NEG))
        sh *= 2
    return x


def _ends(x, is_fwd):
    return jnp.where(is_fwd, x[:, x.shape[1] - 1:], x[:, 0:1])


def _mlstm_kernel(*refs, n_chunk, has_state):
    if has_state:
        q_ref, k_ref, v_ref, o_ref, g_ref, bias_ref, gain_ref, s0_ref, m0_ref, out_ref = refs
    else:
        q_ref, k_ref, v_ref, o_ref, g_ref, bias_ref, gain_ref, out_ref, cn_ref, nn_ref, mn_ref = refs
    L = CHUNK
    nu = 2 * N_HEADS
    tril, triu = _tri_masks(L)
    is_fwd = lax.broadcasted_iota(jnp.int32, (nu, 1), 0) < N_HEADS

    b8, w8, mcum8 = [], [], []
    for c in range(n_chunk):
        p_t = (g_ref[c * L:(c + 1) * L, :] + bias_ref[...]).T
        lf = _log_sigmoid(p_t[nu:2 * nu, :])
        b = jnp.where(is_fwd, _dot(lf, triu.astype(F32), HI), _dot(lf, tril.astype(F32), HI))
        w = p_t[0:nu, :] - b
        b8.append(b)
        w8.append(w)
        mcum8.append(jnp.where(is_fwd, _scan_max(w, False), _scan_max(w, True)))

    m0 = m0_ref[...] if has_state else jnp.zeros((nu, 1), F32)
    m_in_f, m_in_b = [None] * n_chunk, [None] * n_chunk
    m = m0
    for c in range(n_chunk):
        m_in_f[c] = m
        m = (b8[c] + jnp.maximum(mcum8[c], m))[:, L - 1:]
    m_fin_f = m
    m = m0
    for c in range(n_chunk - 1, -1, -1):
        m_in_b[c] = m
        m = (b8[c] + jnp.maximum(mcum8[c], m))[:, 0:1]
    if not has_state:
        mn_ref[...] = jnp.where(is_fwd, m_fin_f, m)

    wk8, dec8, cols = [], [], []
    for c in range(n_chunk):
        m_in = jnp.where(is_fwd, m_in_f[c], m_in_b[c])
        g = jnp.maximum(mcum8[c], m_in)
        m_row = b8[c] + g
        m_new, b_last = _ends(m_row, is_fwd), _ends(b8[c], is_fwd)
        wk8.append(jnp.exp(b_last + w8[c] - m_new))
        dec8.append(jnp.exp(b_last + m_in - m_new))
        stats = jnp.concatenate([g, jnp.exp(m_in - g), jnp.exp(-m_row), jnp.zeros((LANES - 3 * nu, L), F32)], axis=0)
        cols.append(stats.T)

    k_t = [(k_ref[c * L:(c + 1) * L, :].astype(F32) * (HEAD_DIM ** -0.5)).T for c in range(n_chunk)]

    h_all = []
    for h in range(N_HEADS):
        hs = _hs(h)
        qb = [q_ref[c * L:(c + 1) * L, hs].astype(BF16) for c in range(n_chunk)]
        kt = [k_t[c][hs, :] for c in range(n_chunk)]
        vx = [_v_ext(v_ref, slice(c * L, (c + 1) * L), hs) for c in range(n_chunk)]
        qk = [_dot(qb[c], kt[c].astype(BF16)) for c in range(n_chunk)]
        h_sum = [None] * n_chunk
        for d in range(2):
            tri = tril if d == 0 else triu
            j = N_HEADS * d + h
            s_ext = s0_ref[d, h] if has_state else None
            order = range(n_chunk) if d == 0 else range(n_chunk - 1, -1, -1)
            for ci, c in enumerate(order):
                wgt = jnp.exp(jnp.where(tri, w8[c][j:j + 1, :] - cols[c][:, j:j + 1], NEG))
                tot = _dot((qk[c] * wgt).astype(BF16), vx[c])
                if s_ext is not None:
                    tot = tot + cols[c][:, nu + j:nu + j + 1] * _dot(qb[c], s_ext.astype(BF16))
                den = jnp.maximum(jnp.abs(tot[:, HEAD_DIM:HEAD_DIM + 1]), cols[c][:, 2 * nu + j:2 * nu + j + 1])
                hd = tot[:, 0:HEAD_DIM] / den
                h_sum[c] = hd if h_sum[c] is None else h_sum[c] + hd
                if (not has_state) or ci < n_chunk - 1:
                    upd = _dot((kt[c] * wk8[c][j:j + 1, :]).astype(BF16), vx[c])
                    s_ext = upd if s_ext is None else upd + dec8[c][j:j + 1, :] * s_ext
            if not has_state:
                cn_ref[d, h] = s_ext[:, 0:HEAD_DIM]
                nn_ref[d, h] = s_ext[:, HEAD_DIM:HEAD_DIM + 1]
        h_all.append(h_sum)
    jm = _head_mean_matrix()
    for c in range(n_chunk):
        rows = slice(c * L, (c + 1) * L)
        gated = jax.nn.sigmoid(o_ref[rows, :].astype(F32)) * jnp.concatenate([h_all[h][c] for h in range(N_HEADS)], axis=1)
        out_ref[rows, :] = _head_groupnorm(gated, gain_ref[...], jm)


def _ret_kernel(*refs, n_chunk, has_state):
    if has_state:
        q_ref, k_ref, v_ref, g_ref, lg_ref, gain_ref, s0_ref, out_ref, dmat_ref, col_ref, row_ref = refs
    else:
        q_ref, k_ref, v_ref, g_ref, lg_ref, gain_ref, out_ref, sn_ref, dmat_ref, col_ref, row_ref = refs
    L = CHUNK

    @pl.when(pl.program_id(0) == 0)
    def _():
        row = lax.broadcasted_iota(jnp.int32, (L, L), 0)
        col = lax.broadcasted_iota(jnp.int32, (L, L), 1)
        diff = (row - col).astype(F32)
        log_g = _log_sigmoid(lg_ref[...])
        pos_c = lax.broadcasted_iota(jnp.int32, (L, 1), 0).astype(F32)
        pos_r = lax.broadcasted_iota(jnp.int32, (1, L), 1).astype(F32)
        for h in range(N_HEADS):
            lgf = log_g[0:1, h:h + 1]
            lgb = log_g[1:2, h:h + 1]
            dmat_ref[h] = (jnp.where(diff >= 0, jnp.exp(jnp.maximum(diff, 0.0) * lgf), 0.0)
                           + jnp.where(diff <= 0, jnp.exp(jnp.maximum(-diff, 0.0) * lgb), 0.0))
            col_ref[:, h:h + 1] = jnp.exp((pos_c + 1.0) * lgf)
            col_ref[:, N_HEADS + h:N_HEADS + h + 1] = jnp.exp((L - pos_c) * lgb)
            row_ref[h:h + 1, :] = jnp.exp((L - 1.0 - pos_r) * lgf)
            row_ref[N_HEADS + h:N_HEADS + h + 1, :] = jnp.exp(pos_r * lgb)
            row_ref[2 * N_HEADS + h:2 * N_HEADS + h + 1, :] = jnp.exp(L * lgf) + jnp.zeros((1, L), F32)
            row_ref[3 * N_HEADS + h:3 * N_HEADS + h + 1, :] = jnp.exp(L * lgb) + jnp.zeros((1, L), F32)

    k_t = [(k_ref[c * L:(c + 1) * L, :].astype(F32) * (HEAD_DIM ** -0.5)).T for c in range(n_chunk)]
    o_all = []
    for h in range(N_HEADS):
        hs = _hs(h)
        zeta_f = row_ref[h:h + 1, :]
        zeta_b = row_ref[N_HEADS + h:N_HEADS + h + 1, :]
        gl_f = row_ref[2 * N_HEADS + h:2 * N_HEADS + h + 1, 0:1]
        gl_b = row_ref[3 * N_HEADS + h:3 * N_HEADS + h + 1, 0:1]
        kt = [k_t[c][hs, :] for c in range(n_chunk)]
        vb = [v_ref[c * L:(c + 1) * L, hs].astype(BF16) for c in range(n_chunk)]
        sf = [None] * (n_chunk + 1)
        sb = [None] * (n_chunk + 1)
        if has_state:
            sf[0] = s0_ref[0, h]
            sb[n_chunk] = s0_ref[1, h]
        for c in range(n_chunk):
            if (not has_state) or c < n_chunk - 1:
                upd = _dot((kt[c] * zeta_f).astype(BF16), vb[c])
                sf[c + 1] = upd if sf[c] is None else upd + gl_f * sf[c]
        for c in range(n_chunk - 1, -1, -1):
            if (not has_state) or c > 0:
                upd = _dot((kt[c] * zeta_b).astype(BF16), vb[c])
                sb[c] = upd if sb[c + 1] is None else upd + gl_b * sb[c + 1]
        if not has_state:
            sn_ref[0, h] = sf[n_chunk]
            sn_ref[1, h] = sb[0]
        o_h = []
        for c in range(n_chunk):
            qb = q_ref[c * L:(c + 1) * L, hs].astype(BF16)
            att = _dot(qb, kt[c].astype(BF16)) * dmat_ref[h]
            o = _dot(att.astype(BF16), vb[c])
            if has_state:
                o = o + col_ref[:, h:h + 1] * _dot(qb, sf[c].astype(BF16))
                o = o + col_ref[:, N_HEADS + h:N_HEADS + h + 1] * _dot(qb, sb[c + 1].astype(BF16))
            o_h.append(o)
        o_all.append(o_h)
    jm = _head_mean_matrix()
    for c in range(n_chunk):
        rows = slice(c * L, (c + 1) * L)
        o = jnp.concatenate([o_all[h][c] for h in range(N_HEADS)], axis=1)
        out_ref[rows, :] = _silu(g_ref[rows, :].astype(F32)) * _head_groupnorm(o, gain_ref[...], jm)


def _lambda(lam_ref, lam_init):
    v = lam_ref[...]
    s1 = jnp.sum(v[0:1, :] * v[1:2, :], axis=-1, keepdims=True)
    s2 = jnp.sum(v[2:3, :] * v[3:4, :], axis=-1, keepdims=True)
    return jnp.exp(s1) - jnp.exp(s2) + lam_init


def _first_half(shape):
    lane = lax.broadcasted_iota(jnp.int32, shape, 1)
    return (lane % (A_SUB // 2)) < (A_SUB // 4)


def _rope(x, cos, sin_signed):
    n = x.shape[1]
    first_half = _first_half(x.shape)
    partner = jnp.where(first_half, pltpu.roll(x, n - A_SUB // 4, 1), pltpu.roll(x, A_SUB // 4, 1))
    return x * cos + partner * sin_signed


def _attn_kernel(*refs, t, has_cache, lam_init):
    if has_cache:
        q_ref, k_ref, v_ref, kc_ref, vc_ref, lam_ref, gain_ref, out_ref, qr_ref, kr_ref, cos_ref, sin_ref = refs
    else:
        q_ref, k_ref, v_ref, lam_ref, gain_ref, out_ref, qr_ref, kr_ref = refs
    lam = _lambda(lam_ref, lam_init)
    scale = A_SUB ** -0.5
    qb = 2 * CHUNK if has_cache else CHUNK
    n_qb = t // qb

    if has_cache:
        @pl.when(pl.program_id(0) == 0)
        def _():
            lane = lax.broadcasted_iota(jnp.int32, (GRID_W, GROUP_W), 1)
            pos = lax.broadcasted_iota(jnp.int32, (GRID_W, GROUP_W), 0).astype(F32)
            n_freq = A_SUB // 4
            ang = pos * jnp.exp((lane % n_freq).astype(F32) * (-math.log(ROPE_BASE) / n_freq))
            cos_t = jnp.cos(ang)
            sin_t = jnp.where(_first_half((GRID_W, GROUP_W)), -jnp.sin(ang), jnp.sin(ang))
            row_axis = (lane % A_SUB) < (A_SUB // 2)
            for r in range(t // GRID_W):
                rows = slice(r * GRID_W, (r + 1) * GRID_W)
                cos_ref[rows, :] = jnp.where(row_axis, cos_t[r:r + 1, :], cos_t)
                sin_ref[rows, :] = jnp.where(row_axis, sin_t[r:r + 1, :], sin_t)

        kr_ref[...] = _rope(k_ref[...].astype(F32), cos_ref[...], sin_ref[...]).astype(BF16)
        qr_ref[...] = (_rope(q_ref[...].astype(F32), cos_ref[...], sin_ref[...]) * scale).astype(BF16)
    else:
        kr_ref[...] = k_ref[...].astype(BF16)
        qr_ref[...] = (q_ref[...].astype(F32) * scale).astype(BF16)

    vxs = [_v_ext(v_ref, slice(0, t), _hs(h)) for h in range(N_HEADS)]
    if has_cache:
        vcxs = [_v_ext(vc_ref, slice(0, vc_ref.shape[0]), _hs(h)) for h in range(N_HEADS)]

    def softmax_v(rows, units):
        ms = [slice(HEAD_DIM * h + A_SUB * m, HEAD_DIM * h + A_SUB * (m + 1)) for h, m in units]
        qm = [qr_ref[rows, c] for c in ms]
        s_l = [_dot_nt(q, kr_ref[:, c]) for q, c in zip(qm, ms)]
        mx = [jnp.max(s, axis=-1, keepdims=True) for s in s_l]
        if has_cache:
            s_c = [_dot_nt(q, kc_ref[:, c].astype(BF16)) for q, c in zip(qm, ms)]
            mx = [jnp.maximum(a, jnp.max(s, axis=-1, keepdims=True)) for a, s in zip(mx, s_c)]
        tot = [_dot(jnp.exp(s - a).astype(BF16), vxs[h]) for s, a, (h, m) in zip(s_l, mx, units)]
        if has_cache:
            tot = [tt + _dot(jnp.exp(s - a).astype(BF16), vcxs[h]) for tt, s, a, (h, m) in zip(tot, s_c, mx, units)]
        return [tt[:, 0:HEAD_DIM] / tt[:, HEAD_DIM:HEAD_DIM + 1] for tt in tot]

    o_all = [[None] * n_qb for _ in range(N_HEADS)]
    for h in range(N_HEADS):
        for qi in range(n_qb):
            rows = slice(qi * qb, (qi + 1) * qb)
            units = [(h, 0), (h, 1)]
            o = softmax_v(rows, units) if has_cache else [softmax_v(rows, [u])[0] for u in units]
            o_all[h][qi] = o[0] - lam * o[1]
    jm = _head_mean_matrix()
    for qi in range(n_qb):
        rows = slice(qi * qb, (qi + 1) * qb)
        o = jnp.concatenate([o_all[h][qi] for h in range(N_HEADS)], axis=1)
        out_ref[rows, :] = _head_rms(o, gain_ref[...], jm) * (1.0 - lam_init)


def _mixers_kernel(*refs, n_chunk, t, has_state, lam_init, n_in, n_out, kinds, n_sub):
    n_io = sum(n_in) + sum(n_out)
    io, scratch = refs[:n_io], refs[n_io:]
    for sub in range(n_sub):
        def view(r, kind):
            if kind == 'rows':
                return r.at[pl.ds(sub * t, t)]
            return r.at[sub] if kind == 'seq' else r

        v = [view(r, k) for r, k in zip(io, kinds)]
        ins, outs = v[:sum(n_in)], v[sum(n_in):]
        i0, o0 = 0, 0
        parts = []
        for k in range(3):
            parts.append((ins[i0:i0 + n_in[k]], outs[o0:o0 + n_out[k]]))
            i0 += n_in[k]
            o0 += n_out[k]
        a_scratch = [r.at[sub] for r in scratch[3:]]
        _mlstm_kernel(*parts[0][0], *parts[0][1], n_chunk=n_chunk, has_state=has_state)
        _ret_kernel(*parts[1][0], *parts[1][1], *scratch[0:3], n_chunk=n_chunk, has_state=has_state)
        _attn_kernel(*parts[2][0], *parts[2][1], *a_scratch, t=t, has_cache=has_state, lam_init=lam_init)


def _mixers(z, gates, wp, l, n_seq, t, st, lam_init, fuse):
    has_state = st is not None
    nu = 2 * N_HEADS
    n_sub = 2 if fuse else 1
    tt = n_sub * t
    seq = lambda width: pl.BlockSpec((tt, width), lambda b: (b, 0))
    zspec = lambda colblk: pl.BlockSpec((tt, GROUP_W), lambda b: (b, colblk))
    out_rows = jax.ShapeDtypeStruct((n_seq * t, GROUP_W), F32)
    head_state = (2, N_HEADS, HEAD_DIM, HEAD_DIM)
    lead = (n_sub,) if fuse else (None,)
    per_seq = lambda *shape: pl.BlockSpec(lead + shape, lambda b: (b,) + (0,) * len(shape))

    m_in = [z, z, z, z, gates, wp['m_bias'], wp['m_norm']]
    m_specs = [zspec(0), zspec(1), zspec(2), zspec(3), seq(LANES),
               _layer_spec(wp['m_bias'], l), _layer_spec(wp['m_norm'], l)]
    m_kinds = ['rows'] * 5 + ['shared'] * 2
    r_in = [z, z, z, z, wp['r_decay'], wp['r_norm']]
    r_specs = [zspec(4), zspec(5), zspec(6), zspec(7),
               _layer_spec(wp['r_decay'], l), _layer_spec(wp['r_norm'], l)]
    r_kinds = ['rows'] * 4 + ['shared'] * 2
    a_in = [z, z, z]
    a_specs = [zspec(8), zspec(9), zspec(10)]
    a_kinds = ['rows'] * 3
    m_out, r_out, a_out = [out_rows], [out_rows], [out_rows]
    m_ospecs, r_ospecs, a_ospecs = [seq(GROUP_W)], [seq(GROUP_W)], [seq(GROUP_W)]
    mo_kinds, ro_kinds, ao_kinds = ['rows'], ['rows'], ['rows']
    a_lead = (n_sub,) if fuse else ()
    scratch = [pltpu.VMEM((N_HEADS, CHUNK, CHUNK), F32), pltpu.VMEM((CHUNK, LANES), F32),
               pltpu.VMEM((4 * N_HEADS, CHUNK), F32),
               pltpu.VMEM(a_lead + (t, GROUP_W), BF16), pltpu.VMEM(a_lead + (t, GROUP_W), BF16)]
    if has_state:
        assert not fuse
        m_in += list(st[0:2])
        m_specs += [_seq_layer_spec(a, l) for a in st[0:2]]
        r_in.append(st[2])
        r_specs.append(_seq_layer_spec(st[2], l))
        a_in += list(st[3:5])
        a_specs += [_seq_layer_spec(a, l) for a in st[3:5]]
        scratch += [pltpu.VMEM((t, GROUP_W), F32), pltpu.VMEM((t, GROUP_W), F32)]
    else:
        m_out += [jax.ShapeDtypeStruct((n_seq,) + head_state, F32),
                  jax.ShapeDtypeStruct((n_seq, 2, N_HEADS, HEAD_DIM, 1), F32),
                  jax.ShapeDtypeStruct((n_seq, nu, 1), F32)]
        m_ospecs += [per_seq(*head_state), per_seq(2, N_HEADS, HEAD_DIM, 1), per_seq(nu, 1)]
        mo_kinds += ['seq'] * 3
        r_out.append(jax.ShapeDtypeStruct((n_seq,) + head_state, F32))
        r_ospecs.append(per_seq(*head_state))
        ro_kinds.append('seq')
    a_in += [wp['a_lam'], wp['a_norm']]
    a_specs += [_layer_spec(wp['a_lam'], l), _layer_spec(wp['a_norm'], l)]
    a_kinds += ['shared'] * 2
    n_chunk = t // CHUNK
    call = functools.partial(pl.pallas_call, grid=(n_seq // n_sub,), compiler_params=_params(("arbitrary",)))
    if fuse:
        res = call(
            functools.partial(_mixers_kernel, n_chunk=n_chunk, t=t, has_state=has_state, lam_init=lam_init,
                              n_in=(len(m_in), len(r_in), len(a_in)), n_out=(len(m_out), len(r_out), len(a_out)),
                              kinds=tuple(m_kinds + r_kinds + a_kinds + mo_kinds + ro_kinds + ao_kinds),
                              n_sub=n_sub),
            in_specs=m_specs + r_specs + a_specs, out_specs=m_ospecs + r_ospecs + a_ospecs,
            out_shape=m_out + r_out + a_out, scratch_shapes=scratch, name="mixers",
        )(*m_in, *r_in, *a_in)
        return res[:len(m_out)], res[len(m_out):len(m_out) + len(r_out)], res[-1]
    m_res = call(functools.partial(_mlstm_kernel, n_chunk=n_chunk, has_state=has_state),
                 in_specs=m_specs, out_specs=m_ospecs, out_shape=m_out, name="mlstm")(*m_in)
    r_res = call(functools.partial(_ret_kernel, n_chunk=n_chunk, has_state=has_state),
                 in_specs=r_specs, out_specs=r_ospecs, out_shape=r_out, scratch_shapes=scratch[0:3],
                 name="retention")(*r_in)
    a_res = call(functools.partial(_attn_kernel, t=t, has_cache=has_state, lam_init=lam_init),
                 in_specs=a_specs, out_specs=a_ospecs, out_shape=a_out, scratch_shapes=scratch[3:],
                 name="diff_attention")(*a_in)
    return m_res, r_res, a_res[0]


def _cmul(ar, ai, br, bi):
    return ar * br - ai * bi, ar * bi + ai * br


def _lam_bar(lre, lim, dt):
    mag = jnp.exp(lre * dt)
    return mag * jnp.cos(lim * dt), mag * jnp.sin(lim * dt)


def _cpow_int(br, bi, e, n_bits):
    pr = jnp.ones(e.shape, F32)
    pi = jnp.zeros(e.shape, F32)
    for bit in range(n_bits):
        on = ((e >> bit) & 1) == 1
        qr, qi = _cmul(pr, pi, br, bi)
        pr, pi = jnp.where(on, qr, pr), jnp.where(on, qi, pi)
        if bit + 1 < n_bits:
            br, bi = _cmul(br, bi, br, bi)
    return pr, pi


def _zoh_coef(lre, lim, br, bi):
    den = lre * lre + lim * lim
    return ((br - 1.0) * lre + bi * lim) / den, (bi * lre - (br - 1.0) * lim) / den


def _toeplitz_rows(kall):
    n = S5_L * S5_GC
    lane = lax.broadcasted_iota(jnp.int32, (S5_GC, n), 1)
    pieces = []
    for s in range(S5_L):
        shifted = kall if s == 0 else pltpu.roll(kall, S5_GC * s, 1)
        pieces.append(jnp.where(lane >= S5_GC * s, shifted, 0.0))
    return jnp.concatenate(pieces, axis=0)


def _s5_prep_group(lam_c_re, lam_c_im, lam_r_re, lam_r_im, ldt_ref, b_t_re, b_t_im, b_re, b_im,
                   c_re, c_im, c_t_re, c_t_im, d_ref, t_ref, bcf_ref, bcb_ref, ccf_ref, ccb_ref, lam_ref):
    n = S5_L * S5_GC
    lane_blk = lax.broadcasted_iota(jnp.int32, (S5_P, n), 1) // S5_GC
    bits = S5_L.bit_length()

    def bases(d):
        dt = jnp.exp(ldt_ref[d])
        return (_lam_bar(lam_c_re[d], lam_c_im[d], dt), _lam_bar(lam_r_re[d], lam_r_im[d], dt))

    def tile_rows(x):
        return jnp.concatenate([x] * S5_L, axis=0)

    def rows_pow(br, bi, reverse):
        s_idx = lax.broadcasted_iota(jnp.int32, (S5_L, S5_P), 0)
        pr, pi = _cpow_int(br, bi, S5_L - 1 - s_idx if reverse else s_idx, bits)
        rep = lambda p: jnp.concatenate(
            [jnp.broadcast_to(p[s:s + 1, :], (S5_GC, S5_P)) for s in range(S5_L)], axis=0)
        return rep(pr), rep(pi)

    def tile_lanes(ref, d):
        return jnp.concatenate([ref[d]] * S5_L, axis=1)

    ct_re, ct_im = [tile_lanes(c_t_re, d) for d in range(2)], [tile_lanes(c_t_im, d) for d in range(2)]

    (cbr, cbi), (rbr, rbi) = bases(0)
    yield
    kr, ki = _zoh_coef(lam_r_re[0], lam_r_im[0], rbr, rbi)
    bbt_re, bbt_im = _cmul(kr, ki, b_t_re[0], b_t_im[0])
    pr, pi = _cpow_int(cbr, cbi, lane_blk, bits)
    yield
    cl_re, cl_im = _cmul(ct_re[0], ct_im[0], pr, pi)
    yield
    t_f = _toeplitz_rows(_dot(bbt_re, cl_re, HI) - _dot(bbt_im, cl_im, HI))
    yield
    pr, pi = rows_pow(rbr, rbi, True)
    yield
    re, im = _cmul(tile_rows(bbt_re), tile_rows(bbt_im), pr, pi)
    bcf_ref[...] = jnp.concatenate([re, im], axis=1).astype(BF16)
    re, im = _cmul(cl_re, cl_im, cbr, cbi)
    ccf_ref[...] = jnp.concatenate([re, -im], axis=0).astype(BF16)
    pr, pi = _cpow_int(rbr, rbi, jnp.full((1, S5_P), S5_L, jnp.int32), bits)
    yield
    lam_ref[0:1, :] = jnp.concatenate([pr, pr], axis=1)
    lam_ref[1:2, :] = jnp.concatenate([-pi, pi], axis=1)

    (cbr, cbi), (rbr, rbi) = bases(1)
    yield
    kr, ki = _zoh_coef(lam_c_re[1], lam_c_im[1], cbr, cbi)
    bb_re, bb_im = _cmul(kr, ki, tile_lanes(b_re, 1), tile_lanes(b_im, 1))
    pr, pi = _cpow_int(cbr, cbi, lane_blk, bits)
    yield
    bl_re, bl_im = _cmul(bb_re, bb_im, pr, pi)
    yield
    t_b = _toeplitz_rows(_dot(c_re[1], bl_re, HI) - _dot(c_im[1], bl_im, HI)).T
    yield
    kr, ki = _zoh_coef(lam_r_re[1], lam_r_im[1], rbr, rbi)
    bbt_re, bbt_im = _cmul(kr, ki, b_t_re[1], b_t_im[1])
    pr, pi = rows_pow(rbr, rbi, False)
    yield
    re, im = _cmul(tile_rows(bbt_re), tile_rows(bbt_im), pr, pi)
    bcb_ref[...] = jnp.concatenate([re, im], axis=1).astype(BF16)
    pr, pi = _cpow_int(cbr, cbi, S5_L - lane_blk, bits)
    yield
    re, im = _cmul(ct_re[1], ct_im[1], pr, pi)
    ccb_ref[...] = jnp.concatenate([re, -im], axis=0).astype(BF16)
    pr, pi = _cpow_int(rbr, rbi, jnp.full((1, S5_P), S5_L, jnp.int32), bits)
    yield
    lam_ref[2:3, :] = jnp.concatenate([pr, pr], axis=1)
    lam_ref[3:4, :] = jnp.concatenate([-pi, pi], axis=1)
    lam_ref[4:8, :] = jnp.zeros((4, 2 * S5_P), F32)

    eye = (lax.broadcasted_iota(jnp.int32, (n, n), 0) == lax.broadcasted_iota(jnp.int32, (n, n), 1))
    d_diag = jnp.concatenate([d_ref[...]] * S5_L, axis=1)
    t_ref[...] = (t_f + t_b + jnp.where(eye, d_diag, 0.0)).astype(BF16)


def _s5_prep_kernel(*refs, n_grp):
    ins, outs = refs[:14], refs[14:]
    gens = [_s5_prep_group(*[r.at[:, gi] for r in ins[:13]], ins[13].at[gi], *[r.at[gi] for r in outs])
            for gi in range(n_grp)]
    while gens:
        for g in list(gens):
            if next(g, True):
                gens.remove(g)


def _s5_prep(lam_re, lam_im, log_dt, b_re, b_im, c_re, c_im, s5_d):
    n = S5_L * S5_GC
    swap = lambda a: jnp.swapaxes(a, -1, -2)
    ins = [lam_re[..., :, None], lam_im[..., :, None], lam_re[..., None, :], lam_im[..., None, :],
           log_dt[..., None, None], swap(b_re), swap(b_im), b_re, b_im, c_re, c_im, swap(c_re), swap(c_im)]
    d_grp = s5_d.reshape(DEPTH, S5_G, 1, S5_GC)

    n_grp = 4

    def dir_spec(a):
        return pl.BlockSpec((None, 2, n_grp) + a.shape[3:], lambda l, g: (l, 0, g, 0, 0))

    def out(rows, cols, dtype):
        return (pl.BlockSpec((None, n_grp, rows, cols), lambda l, g: (l, g, 0, 0)),
                jax.ShapeDtypeStruct((DEPTH, S5_G, rows, cols), dtype))

    outs = [out(n, n, BF16), out(n, 2 * S5_P, BF16), out(n, 2 * S5_P, BF16),
            out(2 * S5_P, n, BF16), out(2 * S5_P, n, BF16), out(8, 2 * S5_P, F32)]
    return pl.pallas_call(
        functools.partial(_s5_prep_kernel, n_grp=n_grp),
        grid=(DEPTH, S5_G // n_grp),
        in_specs=[dir_spec(a) for a in ins] + [pl.BlockSpec((None, n_grp, 1, S5_GC), lambda l, g: (l, g, 0, 0))],
        out_specs=[o[0] for o in outs], out_shape=[o[1] for o in outs],
        compiler_params=_params(("arbitrary", "arbitrary")), name="s5_prep",
    )(*ins, d_grp)


def _block_transpose(a):
    n = len(a)
    blk = lax.broadcasted_iota(jnp.int32, (1, a[0].shape[1]), 1) // S5_GC
    a = list(a)
    bit = 1
    while bit < n:
        upper = (blk & bit) != 0
        for i in range(n):
            if i & bit == 0:
                j = i | bit
                lo, hi = a[i], a[j]
                a[i] = jnp.where(upper, pltpu.roll(hi, bit * S5_GC, 1), lo)
                a[j] = jnp.where(upper, hi, pltpu.roll(lo, LANES - bit * S5_GC, 1))
        bit *= 2
    return a


def _s5_kernel(*refs, n_seq, n_k, has_state):
    if has_state:
        (u_ref, t_ref, bcf_ref, bcb_ref, ccf_ref, ccb_ref, lam_ref, x0_ref,
         y_ref, ug_ref, yg_ref, inj_f, inj_b, inj_s, xin_f, xin_b) = refs
    else:
        (u_ref, t_ref, bcf_ref, bcb_ref, ccf_ref, ccb_ref, lam_ref,
         y_ref, xf_ref, xb_ref, ug_ref, yg_ref, inj_f, inj_b, inj_s, xin_f, xin_b) = refs
    ng = S5_G // 2
    r = n_seq * n_k
    tok = lambda i: pl.ds(i, r, stride=S5_L)
    u_lo = _block_transpose([u_ref[tok(i), :] for i in range(ng)])
    u_hi = _block_transpose([u_ref[tok(i), :] for i in range(ng, S5_L)])
    for g in range(ng):
        ug_ref[g] = jnp.concatenate([u_lo[g], u_hi[g]], axis=1).astype(BF16)
        inj_f[g] = _dot(ug_ref[g], bcf_ref[g])
        inj_b[g] = _dot(ug_ref[g], bcb_ref[g])

    for d, (inj, xin) in enumerate(((inj_f, xin_f), (inj_b, xin_b))):
        a = [lam_ref[g, 2 * d:2 * d + 1, :] for g in range(ng)]
        bs = [lam_ref[g, 2 * d + 1:2 * d + 2, :] for g in range(ng)]
        for g in range(ng):
            inj_s[g] = pltpu.roll(inj[g], S5_P, 1)
        x = [x0_ref[d, g] if has_state else jnp.zeros((n_seq, 2 * S5_P), F32) for g in range(ng)]
        xs = [pltpu.roll(v, S5_P, 1) for v in x]
        for k in (range(n_k) if d == 0 else range(n_k - 1, -1, -1)):
            rows = pl.ds(k, n_seq, stride=n_k)
            for g in range(ng):
                xin[g, rows, :] = x[g]
                x[g], xs[g] = (a[g] * x[g] + bs[g] * xs[g] + inj[g, rows, :],
                               a[g] * xs[g] - bs[g] * x[g] + inj_s[g, rows, :])
        if not has_state:
            for g in range(ng):
                (xf_ref if d == 0 else xb_ref)[g] = x[g]

    for g in range(ng):
        yg_ref[g] = (_dot(ug_ref[g], t_ref[g]) + _dot(xin_f[g].astype(BF16), ccf_ref[g])
                     + _dot(xin_b[g].astype(BF16), ccb_ref[g]))
    for half in range(S5_L // ng):
        y_i = _block_transpose([yg_ref[g, :, LANES * half:LANES * (half + 1)] for g in range(ng)])
        for i in range(ng):
            y_ref[tok(ng * half + i), :] = y_i[i]


def _s5(u_halves, ops, l, n_seq, n_k, x0):
    has_state = x0 is not None
    n = S5_L * S5_GC
    r = n_seq * n_k
    w = 2 * S5_P
    ng = S5_G // 2
    half_spec = pl.BlockSpec((None, r * S5_L, LANES), lambda j: (j, 0, 0))
    op_spec = lambda a: pl.BlockSpec((None, ng) + a.shape[2:], lambda j: (l, j) + (0,) * (a.ndim - 2))
    args = [u_halves] + list(ops)
    in_specs = [half_spec] + [op_spec(a) for a in ops]
    if has_state:
        args.append(x0)
        in_specs.append(pl.BlockSpec((None, 2, ng, n_seq, w), lambda j: (l, 0, j, 0, 0)))
    out_shape = [jax.ShapeDtypeStruct((2, r * S5_L, LANES), F32)]
    out_specs = [half_spec]
    if not has_state:
        out_shape += [jax.ShapeDtypeStruct((S5_G, n_seq, w), F32)] * 2
        out_specs += [pl.BlockSpec((ng, n_seq, w), lambda j: (j, 0, 0))] * 2
    return pl.pallas_call(
        functools.partial(_s5_kernel, n_seq=n_seq, n_k=n_k, has_state=has_state),
        grid=(2,), in_specs=in_specs, out_specs=out_specs, out_shape=out_shape,
        scratch_shapes=[pltpu.VMEM((ng, r, n), BF16), pltpu.VMEM((ng, r, n), F32)]
        + [pltpu.VMEM((ng, r, w), F32)] * 5,
        compiler_params=_params(("arbitrary",)), name="s5_scan",
    )(*args)


def _out_ffn_kernel(x_ref, m_ref, r_ref, a_ref, ya_ref, yb_ref, mod_ref, wglu_ref, bglu_ref, wout_ref,
                    n_post_ref, n_pre_ref, n_fpost_ref, wg_ref, wu_ref, wd_ref, o_ref):
    def mod(i):
        return mod_ref[:, i * D_MODEL:(i + 1) * D_MODEL]

    y = jnp.concatenate([ya_ref[...], yb_ref[...]], axis=1)
    gs = 0.5 * y * (1.0 + jnp.tanh(math.sqrt(2.0 / math.pi) * (y + 0.044715 * (y * y * y))))
    s_out = gs * jax.nn.sigmoid(_dot(gs.astype(BF16), wglu_ref[...]) + bglu_ref[...])
    mixed = jnp.concatenate([m_ref[...], r_ref[...], a_ref[...], s_out], axis=1).astype(BF16)
    halves = [slice(0, ROW_TILE // 2), slice(ROW_TILE // 2, ROW_TILE)]
    x1 = [x_ref[r, :] + mod(2) * _rms(_dot(mixed[r], wout_ref[...]), n_post_ref[...]) for r in halves]
    h = [(_rms(v, n_pre_ref[...]) * (1.0 + mod(4)) + mod(3)).astype(BF16) for v in x1]
    g = [_dot(v, wg_ref[...]) for v in h]
    u = [_dot(v, wu_ref[...]) for v in h]
    act = [(_silu(a) * b).astype(BF16) for a, b in zip(g, u)]
    f = [_dot(v, wd_ref[...]) for v in act]
    for r, v, ff in zip(halves, x1, f):
        o_ref[r, :] = v + mod(5) * _rms(ff, n_fpost_ref[...])


def _out_ffn(x, m_out, r_out, a_out, y_halves, mod, l, mod_row, wp):
    n = x.shape[0]
    row = lambda w: pl.BlockSpec((ROW_TILE, w), lambda i: (i, 0))
    half = lambda j: pl.BlockSpec((None, ROW_TILE, LANES), lambda i: (j, i, 0))
    params = [wp[k] for k in ('w_glu', 'b_glu', 'w_out', 'n_mix_post', 'n_ffn_pre', 'n_ffn_post',
                              'w_gate', 'w_up', 'w_down')]
    return pl.pallas_call(
        _out_ffn_kernel,
        grid=(n // ROW_TILE,),
        in_specs=[row(D_MODEL), row(GROUP_W), row(GROUP_W), row(GROUP_W), half(0), half(1),
                  _mod_spec(l, mod_row)] + [_layer_spec(a, l, pipeline_mode=pl.Buffered(1)) for a in params],
        out_specs=row(D_MODEL),
        out_shape=jax.ShapeDtypeStruct((n, D_MODEL), F32),
        compiler_params=_params(("arbitrary",)), name="out_ffn",
    )(x, m_out, r_out, a_out, y_halves, y_halves, mod, *params)


def _layer(x, l, n_seq, t, mod, mod_row, wp, s5_ops, lam_init, st):
    ctx = st is None
    res = _in_proj(x, mod, l, mod_row, wp['n_mix_pre'], wp['w_in'], emit_kv=ctx)
    z, gates = res[0], res[2]
    m_res, r_res, a_out = _mixers(z, gates, wp, l, n_seq, t, None if ctx else st[0:5], lam_init, fuse=ctx)
    s_res = _s5(res[1], s5_ops, l, n_seq, t // S5_L, None if ctx else st[5])
    x_new = _out_ffn(x, m_res[0], r_res[0], a_out, s_res[0], mod, l, mod_row, wp)
    if not ctx:
        return x_new, None
    return x_new, (m_res[1], m_res[2], m_res[3], r_res[1], res[3], res[4], s_res[1], s_res[2])


def _pad_to(a, rows, cols=LANES):
    pad = [(0, 0)] * (a.ndim - 2) + [(0, rows - a.shape[-2]), (0, cols - a.shape[-1])]
    return jnp.pad(a, pad)


def kernel(x_prompt, x_sample, state_mlstm_C, state_mlstm_n, state_mlstm_m, state_ret, cache_diff_k, cache_diff_v, state_s5_re, state_s5_im, c, c_ctx, w_ada, b_ada, n_mix_pre, n_mix_post, n_ffn_pre, n_ffn_post, w_in, w_out, m_gate_bias, m_norm, r_decay_logit, r_norm, a_lam_q1, a_lam_k1, a_lam_q2, a_lam_k2, a_norm, s5_lam_re, s5_lam_im, s5_log_dt, s5_b_re, s5_b_im, s5_c_re, s5_c_im, s5_d, s5_w_glu, s5_b_glu, w_ffn_gate, w_ffn_up, w_ffn_down):
    n_ctx, t_ctx, _ = x_prompt.shape
    n_lat, t_lat, _ = x_sample.shape
    past = cache_diff_k.shape[2]

    cond8 = jnp.concatenate([c, c_ctx[None, :], jnp.zeros((8 - n_lat - 1, D_MODEL), F32)], axis=0)
    mod = _ada(cond8, w_ada, b_ada).reshape(DEPTH, 8, 1, 6 * D_MODEL)
    s5_ops = _s5_prep(s5_lam_re, s5_lam_im, s5_log_dt, s5_b_re, s5_b_im, s5_c_re, s5_c_im, s5_d)

    row = lambda a: a[:, None, :]
    wp = dict(
        w_in=jnp.pad(w_in.astype(BF16), ((0, 0), (0, 0), (0, N_MAIN + LANES - w_in.shape[-1]))), w_out=w_out.astype(BF16), w_glu=s5_w_glu.astype(BF16), b_glu=row(s5_b_glu),
        w_gate=w_ffn_gate.astype(BF16), w_up=w_ffn_up.astype(BF16), w_down=w_ffn_down.astype(BF16),
        n_mix_pre=row(n_mix_pre), n_mix_post=row(n_mix_post), n_ffn_pre=row(n_ffn_pre),
        n_ffn_post=row(n_ffn_post), m_norm=row(m_norm), r_norm=row(r_norm), a_norm=row(a_norm),
        m_bias=_pad_to(row(m_gate_bias), 1), r_decay=_pad_to(r_decay_logit, 8),
        a_lam=_pad_to(jnp.stack([a_lam_q1, a_lam_k1, a_lam_q2, a_lam_k2], axis=1), 8))
    lam_inits = [0.8 - 0.6 * math.exp(-0.3 * l) for l in range(DEPTH)]

    x = x_prompt.reshape(n_ctx * t_ctx, D_MODEL)
    new_states = []
    for l in range(DEPTH):
        x, st = _layer(x, l, n_ctx, t_ctx, mod, lambda i: n_lat, wp, s5_ops, lam_inits[l], None)
        new_states.append(st)
    y_prompt = x.reshape(n_ctx, t_ctx, D_MODEL)

    x0 = jnp.stack([state_s5_re, state_s5_im], axis=-2)
    x0 = x0.transpose(1, 2, 3, 0, 4, 5).reshape(DEPTH, 2, S5_G, n_lat, 2 * S5_P)
    s_ext0 = jnp.concatenate([state_mlstm_C, state_mlstm_n[..., None],
                              jnp.zeros(state_mlstm_n.shape + (LANES - HEAD_DIM - 1,), F32)], axis=-1)
    st = (s_ext0, state_mlstm_m.reshape(n_lat, DEPTH, 2 * N_HEADS, 1), state_ret,
          cache_diff_k.reshape(n_lat, DEPTH, past, GROUP_W), cache_diff_v.reshape(n_lat, DEPTH, past, GROUP_W), x0)
    x = x_sample.reshape(n_lat * t_lat, D_MODEL)
    tiles_per_seq = t_lat // ROW_TILE
    for l in range(DEPTH):
        x, _ = _layer(x, l, n_lat, t_lat, mod, lambda i: i // tiles_per_seq, wp, s5_ops, lam_inits[l], st)
    y_sample = x.reshape(n_lat, t_lat, D_MODEL)

    stack = lambda i: jnp.stack([s[i] for s in new_states], axis=1)
    kv = lambda i: jnp.stack([s[i].reshape(n_ctx, t_ctx, N_HEADS, HEAD_DIM) for s in new_states], axis=1)
    xs = jnp.stack([stack(6), stack(7)], axis=2)
    xs = xs.reshape(S5_G, DEPTH, 2, n_ctx, 2, S5_P).transpose(3, 1, 2, 0, 4, 5)
    return (y_prompt, y_sample, stack(0), stack(1)[..., 0], stack(2).reshape(n_ctx, DEPTH, 2, N_HEADS),
            stack(3), kv(4), kv(5), xs[..., 0, :], xs[..., 1, :])
```

```python
import functools
import math

import jax
import jax.numpy as jnp
from jax import lax
from jax.experimental import pallas as pl
from jax.experimental.pallas import tpu as pltpu

F32 = jnp.float32
BF16 = jnp.bfloat16
HI = lax.Precision.HIGHEST

D_MODEL = 1024
DEPTH = 2
GRID_W = 64
HEAD_DIM = 64
GROUP_W = 256
N_HEADS = 4
A_SUB = 32
S5_GC = 16
S5_G = 16
S5_P = 64
D_FF = 2816
ROPE_BASE = 10000.0
EPS = 1e-6
N_MAIN = 12 * GROUP_W
N_MIX = 11 * GROUP_W
LANES = 128
CHUNK = 256
ROW_TILE = 512
S5_L = 16
NEG = -1e30
VMEM_LIMIT = 56 * 1024 * 1024


def _dot(a, b, precision=None):
    return jnp.dot(a, b, preferred_element_type=F32, precision=precision)


def _dot_nt(a, b):
    return lax.dot_general(a, b, (((1,), (1,)), ((), ())), preferred_element_type=F32)


def _log_sigmoid(x):
    return jnp.minimum(x, 0.0) - jnp.log(1.0 + jnp.exp(-jnp.abs(x)))


def _silu(x):
    return x * jax.nn.sigmoid(x)


def _rms(x, g):
    return x * lax.rsqrt(jnp.mean(x * x, axis=-1, keepdims=True) + EPS) * g


def _params(sem=None):
    return pltpu.CompilerParams(dimension_semantics=sem, vmem_limit_bytes=VMEM_LIMIT)


def _layer_spec(a, l, **kw):
    n = a.ndim - 1
    return pl.BlockSpec((None,) + a.shape[1:], lambda *_: (l,) + (0,) * n, **kw)


def _seq_layer_spec(a, l):
    n = a.ndim - 2
    return pl.BlockSpec((None, None) + a.shape[2:], lambda b: (b, l) + (0,) * n)


def _split_bf16(x):
    hi = x.astype(BF16)
    return hi, (x - hi.astype(F32)).astype(BF16)


def _ada_kernel(c_ref, w_ref, b_ref, o_ref):
    a_hi, a_lo = _split_bf16(_silu(c_ref[...]))
    w_hi, w_lo = _split_bf16(w_ref[...])
    o_ref[...] = _dot(a_hi, w_hi) + (_dot(a_lo, w_hi) + _dot(a_hi, w_lo)) + b_ref[...]


def _ada(cond8, w_ada, b_ada):
    tn = 1536
    return pl.pallas_call(
        _ada_kernel,
        grid=(DEPTH, 6 * D_MODEL // tn),
        in_specs=[pl.BlockSpec((8, D_MODEL), lambda l, j: (0, 0)),
                  pl.BlockSpec((None, D_MODEL, tn), lambda l, j: (l, 0, j)),
                  pl.BlockSpec((None, 1, tn), lambda l, j: (l, 0, j))],
        out_specs=pl.BlockSpec((None, 8, tn), lambda l, j: (l, 0, j)),
        out_shape=jax.ShapeDtypeStruct((DEPTH, 8, 6 * D_MODEL), F32),
        compiler_params=_params(("arbitrary", "arbitrary")),
        name="ada",
    )(cond8, w_ada, b_ada.reshape(DEPTH, 1, 6 * D_MODEL))


def _in_proj_kernel(x_ref, mod_ref, g_ref, w_ref, z_ref, u_ref, gate_ref, *kv_refs):
    n_gate = 4 * N_HEADS
    n_head = 4 * GROUP_W
    halves = [slice(0, ROW_TILE // 2), slice(ROW_TILE // 2, ROW_TILE)]
    hb = [(_rms(x_ref[r, :], g_ref[...]) * (1.0 + mod_ref[:, D_MODEL:2 * D_MODEL])
           + mod_ref[:, 0:D_MODEL]).astype(BF16) for r in halves]
    head = [_dot(v, w_ref[:, 0:n_head]) for v in hb]
    tail = [_dot(v, w_ref[:, n_head:]) for v in hb]
    for r, hd, tl in zip(halves, head, tail):
        z_ref[r, 0:n_head] = hd.astype(BF16)
        gate_ref[r, :] = tl[:, 0:LANES]
        rest = tl[:, n_gate:n_gate + N_MAIN - n_head]
        z_ref[r, n_head:] = rest[:, 0:N_MIX - n_head].astype(BF16)
        u_ref[0, r, :] = rest[:, N_MIX - n_head:N_MIX - n_head + LANES]
        u_ref[1, r, :] = rest[:, N_MIX - n_head + LANES:]
        if kv_refs:
            kv_refs[0][r, :] = rest[:, 9 * GROUP_W - n_head:10 * GROUP_W - n_head]
            kv_refs[1][r, :] = rest[:, 10 * GROUP_W - n_head:11 * GROUP_W - n_head]


def _mod_spec(l, mod_row):
    return pl.BlockSpec((None, None, 1, 6 * D_MODEL), lambda i: (l, mod_row(i), 0, 0))


def _in_proj(x, mod, l, mod_row, gain, w, emit_kv):
    n = x.shape[0]
    row = lambda width: pl.BlockSpec((ROW_TILE, width), lambda i: (i, 0))
    out_specs = [row(N_MIX), pl.BlockSpec((2, ROW_TILE, LANES), lambda i: (0, i, 0)), row(LANES)]
    out_specs += [row(GROUP_W)] * (2 if emit_kv else 0)
    out_shape = [jax.ShapeDtypeStruct((n, N_MIX), BF16), jax.ShapeDtypeStruct((2, n, LANES), F32),
                 jax.ShapeDtypeStruct((n, LANES), F32)]
    out_shape += [jax.ShapeDtypeStruct((n, GROUP_W), F32)] * (2 if emit_kv else 0)
    return pl.pallas_call(
        _in_proj_kernel,
        grid=(n // ROW_TILE,),
        in_specs=[row(D_MODEL), _mod_spec(l, mod_row), _layer_spec(gain, l),
                  _layer_spec(w, l, pipeline_mode=pl.Buffered(1))],
        out_specs=out_specs, out_shape=out_shape,
        compiler_params=_params(("arbitrary",)),
        name="in_proj",
    )(x, mod, gain, w)


def _tri_masks(n):
    row = lax.broadcasted_iota(jnp.int32, (n, n), 0)
    col = lax.broadcasted_iota(jnp.int32, (n, n), 1)
    return row >= col, row <= col


def _head_mean_matrix():
    r = lax.broadcasted_iota(jnp.int32, (GROUP_W, GROUP_W), 0) // HEAD_DIM
    c = lax.broadcasted_iota(jnp.int32, (GROUP_W, GROUP_W), 1) // HEAD_DIM
    return jnp.where(r == c, 1.0 / HEAD_DIM, 0.0).astype(BF16)


def _head_mean(x, j):
    hi, lo = _split_bf16(x)
    return _dot(hi, j) + _dot(lo, j)


def _head_groupnorm(x, g, j):
    xc = x - _head_mean(x, j)
    return xc * lax.rsqrt(_head_mean(xc * xc, j) + EPS) * g


def _head_rms(x, g, j):
    return x * lax.rsqrt(_head_mean(x * x, j) + EPS) * g


def _hs(h):
    return slice(HEAD_DIM * h, HEAD_DIM * (h + 1))


def _v_ext(v_ref, rows, hs):
    ones = jnp.ones((rows.stop - rows.start, HEAD_DIM), BF16)
    return jnp.concatenate([v_ref[rows, hs].astype(BF16), ones], axis=1)


def _scan_max(x, reverse):
    n = x.shape[1]
    lane = lax.broadcasted_iota(jnp.int32, x.shape, 1)
    sh = 1
    while sh < n:
        if reverse:
            x = jnp.maximum(x, jnp.where(lane < n - sh, pltpu.roll(x, n - sh, 1), NEG))
        else:
            x = jnp.maximum(x, jnp.where(lane >= sh, pltpu.roll(x, sh, 1), NEG))
        sh *= 2
    return x


def _ends(x, is_fwd):
    return jnp.where(is_fwd, x[:, x.shape[1] - 1:], x[:, 0:1])


def _mlstm_kernel(*refs, n_chunk, has_state):
    if has_state:
        q_ref, k_ref, v_ref, o_ref, g_ref, bias_ref, gain_ref, s0_ref, m0_ref, out_ref = refs
    else:
        q_ref, k_ref, v_ref, o_ref, g_ref, bias_ref, gain_ref, out_ref, cn_ref, nn_ref, mn_ref = refs
    L = CHUNK
    nu = 2 * N_HEADS
    tril, triu = _tri_masks(L)
    is_fwd = lax.broadcasted_iota(jnp.int32, (nu, 1), 0) < N_HEADS

    b8, w8, mcum8 = [], [], []
    for c in range(n_chunk):
        p_t = (g_ref[c * L:(c + 1) * L, :] + bias_ref[...]).T
        lf = _log_sigmoid(p_t[nu:2 * nu, :])
        b = jnp.where(is_fwd, _dot(lf, triu.astype(F32), HI), _dot(lf, tril.astype(F32), HI))
        w = p_t[0:nu, :] - b
        b8.append(b)
        w8.append(w)
        mcum8.append(jnp.where(is_fwd, _scan_max(w, False), _scan_max(w, True)))

    m0 = m0_ref[...] if has_state else jnp.zeros((nu, 1), F32)
    m_in_f, m_in_b = [None] * n_chunk, [None] * n_chunk
    m = m0
    for c in range(n_chunk):
        m_in_f[c] = m
        m = (b8[c] + jnp.maximum(mcum8[c], m))[:, L - 1:]
    m_fin_f = m
    m = m0
    for c in range(n_chunk - 1, -1, -1):
        m_in_b[c] = m
        m = (b8[c] + jnp.maximum(mcum8[c], m))[:, 0:1]
    if not has_state:
        mn_ref[...] = jnp.where(is_fwd, m_fin_f, m)

    wk8, dec8, cols = [], [], []
    for c in range(n_chunk):
        m_in = jnp.where(is_fwd, m_in_f[c], m_in_b[c])
        g = jnp.maximum(mcum8[c], m_in)
        m_row = b8[c] + g
        m_new, b_last = _ends(m_row, is_fwd), _ends(b8[c], is_fwd)
        wk8.append(jnp.exp(b_last + w8[c] - m_new))
        dec8.append(jnp.exp(b_last + m_in - m_new))
        stats = jnp.concatenate([g, jnp.exp(m_in - g), jnp.exp(-m_row), jnp.zeros((LANES - 3 * nu, L), F32)], axis=0)
        cols.append(stats.T)

    k_t = [(k_ref[c * L:(c + 1) * L, :].astype(F32) * (HEAD_DIM ** -0.5)).T for c in range(n_chunk)]

    h_all = []
    for h in range(N_HEADS):
        hs = _hs(h)
        qb = [q_ref[c * L:(c + 1) * L, hs].astype(BF16) for c in range(n_chunk)]
        kt = [k_t[c][hs, :] for c in range(n_chunk)]
        vx = [_v_ext(v_ref, slice(c * L, (c + 1) * L), hs) for c in range(n_chunk)]
        qk = [_dot(qb[c], kt[c].astype(BF16)) for c in range(n_chunk)]
        h_sum = [None] * n_chunk
        for d in range(2):
            tri = tril if d == 0 else triu
            j = N_HEADS * d + h
            s_ext = s0_ref[d, h] if has_state else None
            order = range(n_chunk) if d == 0 else range(n_chunk - 1, -1, -1)
            for ci, c in enumerate(order):
                wgt = jnp.exp(jnp.where(tri, w8[c][j:j + 1, :] - cols[c][:, j:j + 1], NEG))
                tot = _dot((qk[c] * wgt).astype(BF16), vx[c])
                if s_ext is not None:
                    tot = tot + cols[c][:, nu + j:nu + j + 1] * _dot(qb[c], s_ext.astype(BF16))
                den = jnp.maximum(jnp.abs(tot[:, HEAD_DIM:HEAD_DIM + 1]), cols[c][:, 2 * nu + j:2 * nu + j + 1])
                hd = tot[:, 0:HEAD_DIM] / den
                h_sum[c] = hd if h_sum[c] is None else h_sum[c] + hd
                if (not has_state) or ci < n_chunk - 1:
                    upd = _dot((kt[c] * wk8[c][j:j + 1, :]).astype(BF16), vx[c])
                    s_ext = upd if s_ext is None else upd + dec8[c][j:j + 1, :] * s_ext
            if not has_state:
                cn_ref[d, h] = s_ext[:, 0:HEAD_DIM]
                nn_ref[d, h] = s_ext[:, HEAD_DIM:HEAD_DIM + 1]
        h_all.append(h_sum)
    jm = _head_mean_matrix()
    for c in range(n_chunk):
        rows = slice(c * L, (c + 1) * L)
        gated = jax.nn.sigmoid(o_ref[rows, :].astype(F32)) * jnp.concatenate([h_all[h][c] for h in range(N_HEADS)], axis=1)
        out_ref[rows, :] = _head_groupnorm(gated, gain_ref[...], jm)


def _ret_kernel(*refs, n_chunk, has_state):
    if has_state:
        q_ref, k_ref, v_ref, g_ref, lg_ref, gain_ref, s0_ref, out_ref, dmat_ref, col_ref, row_ref = refs
    else:
        q_ref, k_ref, v_ref, g_ref, lg_ref, gain_ref, out_ref, sn_ref, dmat_ref, col_ref, row_ref = refs
    L = CHUNK

    @pl.when(pl.program_id(0) == 0)
    def _():
        row = lax.broadcasted_iota(jnp.int32, (L, L), 0)
        col = lax.broadcasted_iota(jnp.int32, (L, L), 1)
        diff = (row - col).astype(F32)
        log_g = _log_sigmoid(lg_ref[...])
        pos_c = lax.broadcasted_iota(jnp.int32, (L, 1), 0).astype(F32)
        pos_r = lax.broadcasted_iota(jnp.int32, (1, L), 1).astype(F32)
        for h in range(N_HEADS):
            lgf = log_g[0:1, h:h + 1]
            lgb = log_g[1:2, h:h + 1]
            dmat_ref[h] = (jnp.where(diff >= 0, jnp.exp(jnp.maximum(diff, 0.0) * lgf), 0.0)
                           + jnp.where(diff <= 0, jnp.exp(jnp.maximum(-diff, 0.0) * lgb), 0.0))
            col_ref[:, h:h + 1] = jnp.exp((pos_c + 1.0) * lgf)
            col_ref[:, N_HEADS + h:N_HEADS + h + 1] = jnp.exp((L - pos_c) * lgb)
            row_ref[h:h + 1, :] = jnp.exp((L - 1.0 - pos_r) * lgf)
            row_ref[N_HEADS + h:N_HEADS + h + 1, :] = jnp.exp(pos_r * lgb)
            row_ref[2 * N_HEADS + h:2 * N_HEADS + h + 1, :] = jnp.exp(L * lgf) + jnp.zeros((1, L), F32)
            row_ref[3 * N_HEADS + h:3 * N_HEADS + h + 1, :] = jnp.exp(L * lgb) + jnp.zeros((1, L), F32)

    k_t = [(k_ref[c * L:(c + 1) * L, :].astype(F32) * (HEAD_DIM ** -0.5)).T for c in range(n_chunk)]
    o_all = []
    for h in range(N_HEADS):
        hs = _hs(h)
        zeta_f = row_ref[h:h + 1, :]
        zeta_b = row_ref[N_HEADS + h:N_HEADS + h + 1, :]
        gl_f = row_ref[2 * N_HEADS + h:2 * N_HEADS + h + 1, 0:1]
        gl_b = row_ref[3 * N_HEADS + h:3 * N_HEADS + h + 1, 0:1]
        kt = [k_t[c][hs, :] for c in range(n_chunk)]
        vb = [v_ref[c * L:(c + 1) * L, hs].astype(BF16) for c in range(n_chunk)]
        sf = [None] * (n_chunk + 1)
        sb = [None] * (n_chunk + 1)
        if has_state:
            sf[0] = s0_ref[0, h]
            sb[n_chunk] = s0_ref[1, h]
        for c in range(n_chunk):
            if (not has_state) or c < n_chunk - 1:
                upd = _dot((kt[c] * zeta_f).astype(BF16), vb[c])
                sf[c + 1] = upd if sf[c] is None else upd + gl_f * sf[c]
        for c in range(n_chunk - 1, -1, -1):
            if (not has_state) or c > 0:
                upd = _dot((kt[c] * zeta_b).astype(BF16), vb[c])
                sb[c] = upd if sb[c + 1] is None else upd + gl_b * sb[c + 1]
        if not has_state:
            sn_ref[0, h] = sf[n_chunk]
            sn_ref[1, h] = sb[0]
        o_h = []
        for c in range(n_chunk):
            qb = q_ref[c * L:(c + 1) * L, hs].astype(BF16)
            att = _dot(qb, kt[c].astype(BF16)) * dmat_ref[h]
            o = _dot(att.astype(BF16), vb[c])
            if has_state:
                o = o + col_ref[:, h:h + 1] * _dot(qb, sf[c].astype(BF16))
                o = o + col_ref[:, N_HEADS + h:N_HEADS + h + 1] * _dot(qb, sb[c + 1].astype(BF16))
            o_h.append(o)
        o_all.append(o_h)
    jm = _head_mean_matrix()
    for c in range(n_chunk):
        rows = slice(c * L, (c + 1) * L)
        o = jnp.concatenate([o_all[h][c] for h in range(N_HEADS)], axis=1)
        out_ref[rows, :] = _silu(g_ref[rows, :].astype(F32)) * _head_groupnorm(o, gain_ref[...], jm)


def _lambda(lam_ref, lam_init):
    v = lam_ref[...]
    s1 = jnp.sum(v[0:1, :] * v[1:2, :], axis=-1, keepdims=True)
    s2 = jnp.sum(v[2:3, :] * v[3:4, :], axis=-1, keepdims=True)
    return jnp.exp(s1) - jnp.exp(s2) + lam_init


def _first_half(shape):
    lane = lax.broadcasted_iota(jnp.int32, shape, 1)
    return (lane % (A_SUB // 2)) < (A_SUB // 4)


def _rope(x, cos, sin_signed):
    n = x.shape[1]
    first_half = _first_half(x.shape)
    partner = jnp.where(first_half, pltpu.roll(x, n - A_SUB // 4, 1), pltpu.roll(x, A_SUB // 4, 1))
    return x * cos + partner * sin_signed


def _attn_kernel(*refs, t, has_cache, lam_init):
    if has_cache:
        q_ref, k_ref, v_ref, kc_ref, vc_ref, lam_ref, gain_ref, out_ref, qr_ref, kr_ref, cos_ref, sin_ref = refs
    else:
        q_ref, k_ref, v_ref, lam_ref, gain_ref, out_ref, qr_ref, kr_ref = refs
    lam = _lambda(lam_ref, lam_init)
    scale = A_SUB ** -0.5
    qb = 2 * CHUNK if has_cache else CHUNK
    n_qb = t // qb

    if has_cache:
        @pl.when(pl.program_id(0) == 0)
        def _():
            lane = lax.broadcasted_iota(jnp.int32, (GRID_W, GROUP_W), 1)
            pos = lax.broadcasted_iota(jnp.int32, (GRID_W, GROUP_W), 0).astype(F32)
            n_freq = A_SUB // 4
            ang = pos * jnp.exp((lane % n_freq).astype(F32) * (-math.log(ROPE_BASE) / n_freq))
            cos_t = jnp.cos(ang)
            sin_t = jnp.where(_first_half((GRID_W, GROUP_W)), -jnp.sin(ang), jnp.sin(ang))
            row_axis = (lane % A_SUB) < (A_SUB // 2)
            for r in range(t // GRID_W):
                rows = slice(r * GRID_W, (r + 1) * GRID_W)
                cos_ref[rows, :] = jnp.where(row_axis, cos_t[r:r + 1, :], cos_t)
                sin_ref[rows, :] = jnp.where(row_axis, sin_t[r:r + 1, :], sin_t)

        kr_ref[...] = _rope(k_ref[...].astype(F32), cos_ref[...], sin_ref[...]).astype(BF16)
        qr_ref[...] = (_rope(q_ref[...].astype(F32), cos_ref[...], sin_ref[...]) * scale).astype(BF16)
    else:
        kr_ref[...] = k_ref[...].astype(BF16)
        qr_ref[...] = (q_ref[...].astype(F32) * scale).astype(BF16)

    vxs = [_v_ext(v_ref, slice(0, t), _hs(h)) for h in range(N_HEADS)]
    if has_cache:
        vcxs = [_v_ext(vc_ref, slice(0, vc_ref.shape[0]), _hs(h)) for h in range(N_HEADS)]

    def softmax_v(rows, units):
        ms = [slice(HEAD_DIM * h + A_SUB * m, HEAD_DIM * h + A_SUB * (m + 1)) for h, m in units]
        qm = [qr_ref[rows, c] for c in ms]
        s_l = [_dot_nt(q, kr_ref[:, c]) for q, c in zip(qm, ms)]
        mx = [jnp.max(s, axis=-1, keepdims=True) for s in s_l]
        if has_cache:
            s_c = [_dot_nt(q, kc_ref[:, c].astype(BF16)) for q, c in zip(qm, ms)]
            mx = [jnp.maximum(a, jnp.max(s, axis=-1, keepdims=True)) for a, s in zip(mx, s_c)]
        tot = [_dot(jnp.exp(s - a).astype(BF16), vxs[h]) for s, a, (h, m) in zip(s_l, mx, units)]
        if has_cache:
            tot = [tt + _dot(jnp.exp(s - a).astype(BF16), vcxs[h]) for tt, s, a, (h, m) in zip(tot, s_c, mx, units)]
        return [tt[:, 0:HEAD_DIM] / tt[:, HEAD_DIM:HEAD_DIM + 1] for tt in tot]

    o_all = [[None] * n_qb for _ in range(N_HEADS)]
    for h in range(N_HEADS):
        for qi in range(n_qb):
            rows = slice(qi * qb, (qi + 1) * qb)
            units = [(h, 0), (h, 1)]
            o = softmax_v(rows, units) if has_cache else [softmax_v(rows, [u])[0] for u in units]
            o_all[h][qi] = o[0] - lam * o[1]
    jm = _head_mean_matrix()
    for qi in range(n_qb):
        rows = slice(qi * qb, (qi + 1) * qb)
        o = jnp.concatenate([o_all[h][qi] for h in range(N_HEADS)], axis=1)
        out_ref[rows, :] = _head_rms(o, gain_ref[...], jm) * (1.0 - lam_init)


def _mixers_kernel(*refs, n_chunk, t, has_state, lam_init, n_in, n_out, kinds, n_sub):
    n_io = sum(n_in) + sum(n_out)
    io, scratch = refs[:n_io], refs[n_io:]
    for sub in range(n_sub):
        def view(r, kind):
            if kind == 'rows':
                return r.at[pl.ds(sub * t, t)]
            return r.at[sub] if kind == 'seq' else r

        v = [view(r, k) for r, k in zip(io, kinds)]
        ins, outs = v[:sum(n_in)], v[sum(n_in):]
        i0, o0 = 0, 0
        parts = []
        for k in range(3):
            parts.append((ins[i0:i0 + n_in[k]], outs[o0:o0 + n_out[k]]))
            i0 += n_in[k]
            o0 += n_out[k]
        a_scratch = [r.at[sub] for r in scratch[3:]]
        _mlstm_kernel(*parts[0][0], *parts[0][1], n_chunk=n_chunk, has_state=has_state)
        _ret_kernel(*parts[1][0], *parts[1][1], *scratch[0:3], n_chunk=n_chunk, has_state=has_state)
        _attn_kernel(*parts[2][0], *parts[2][1], *a_scratch, t=t, has_cache=has_state, lam_init=lam_init)


def _mixers(z, gates, wp, l, n_seq, t, st, lam_init, fuse):
    has_state = st is not None
    nu = 2 * N_HEADS
    n_sub = 2 if fuse else 1
    tt = n_sub * t
    seq = lambda width: pl.BlockSpec((tt, width), lambda b: (b, 0))
    zspec = lambda colblk: pl.BlockSpec((tt, GROUP_W), lambda b: (b, colblk))
    out_rows = jax.ShapeDtypeStruct((n_seq * t, GROUP_W), F32)
    head_state = (2, N_HEADS, HEAD_DIM, HEAD_DIM)
    lead = (n_sub,) if fuse else (None,)
    per_seq = lambda *shape: pl.BlockSpec(lead + shape, lambda b: (b,) + (0,) * len(shape))

    m_in = [z, z, z, z, gates, wp['m_bias'], wp['m_norm']]
    m_specs = [zspec(0), zspec(1), zspec(2), zspec(3), seq(LANES),
               _layer_spec(wp['m_bias'], l), _layer_spec(wp['m_norm'], l)]
    m_kinds = ['rows'] * 5 + ['shared'] * 2
    r_in = [z, z, z, z, wp['r_decay'], wp['r_norm']]
    r_specs = [zspec(4), zspec(5), zspec(6), zspec(7),
               _layer_spec(wp['r_decay'], l), _layer_spec(wp['r_norm'], l)]
    r_kinds = ['rows'] * 4 + ['shared'] * 2
    a_in = [z, z, z]
    a_specs = [zspec(8), zspec(9), zspec(10)]
    a_kinds = ['rows'] * 3
    m_out, r_out, a_out = [out_rows], [out_rows], [out_rows]
    m_ospecs, r_ospecs, a_ospecs = [seq(GROUP_W)], [seq(GROUP_W)], [seq(GROUP_W)]
    mo_kinds, ro_kinds, ao_kinds = ['rows'], ['rows'], ['rows']
    a_lead = (n_sub,) if fuse else ()
    scratch = [pltpu.VMEM((N_HEADS, CHUNK, CHUNK), F32), pltpu.VMEM((CHUNK, LANES), F32),
               pltpu.VMEM((4 * N_HEADS, CHUNK), F32),
               pltpu.VMEM(a_lead + (t, GROUP_W), BF16), pltpu.VMEM(a_lead + (t, GROUP_W), BF16)]
    if has_state:
        assert not fuse
        m_in += list(st[0:2])
        m_specs += [_seq_layer_spec(a, l) for a in st[0:2]]
        r_in.append(st[2])
        r_specs.append(_seq_layer_spec(st[2], l))
        a_in += list(st[3:5])
        a_specs += [_seq_layer_spec(a, l) for a in st[3:5]]
        scratch += [pltpu.VMEM((t, GROUP_W), F32), pltpu.VMEM((t, GROUP_W), F32)]
    else:
        m_out += [jax.ShapeDtypeStruct((n_seq,) + head_state, F32),
                  jax.ShapeDtypeStruct((n_seq, 2, N_HEADS, HEAD_DIM, 1), F32),
                  jax.ShapeDtypeStruct((n_seq, nu, 1), F32)]
        m_ospecs += [per_seq(*head_state), per_seq(2, N_HEADS, HEAD_DIM, 1), per_seq(nu, 1)]
        mo_kinds += ['seq'] * 3
        r_out.append(jax.ShapeDtypeStruct((n_seq,) + head_state, F32))
        r_ospecs.append(per_seq(*head_state))
        ro_kinds.append('seq')
    a_in += [wp['a_lam'], wp['a_norm']]
    a_specs += [_layer_spec(wp['a_lam'], l), _layer_spec(wp['a_norm'], l)]
    a_kinds += ['shared'] * 2
    n_chunk = t // CHUNK
    call = functools.partial(pl.pallas_call, grid=(n_seq // n_sub,), compiler_params=_params(("arbitrary",)))
    if fuse:
        res = call(
            functools.partial(_mixers_kernel, n_chunk=n_chunk, t=t, has_state=has_state, lam_init=lam_init,
                              n_in=(len(m_in), len(r_in), len(a_in)), n_out=(len(m_out), len(r_out), len(a_out)),
                              kinds=tuple(m_kinds + r_kinds + a_kinds + mo_kinds + ro_kinds + ao_kinds),
                              n_sub=n_sub),
            in_specs=m_specs + r_specs + a_specs, out_specs=m_ospecs + r_ospecs + a_ospecs,
            out_shape=m_out + r_out + a_out, scratch_shapes=scratch, name="mixers",
        )(*m_in, *r_in, *a_in)
        return res[:len(m_out)], res[len(m_out):len(m_out) + len(r_out)], res[-1]
    m_res = call(functools.partial(_mlstm_kernel, n_chunk=n_chunk, has_state=has_state),
                 in_specs=m_specs, out_specs=m_ospecs, out_shape=m_out, name="mlstm")(*m_in)
    r_res = call(functools.partial(_ret_kernel, n_chunk=n_chunk, has_state=has_state),
                 in_specs=r_specs, out_specs=r_ospecs, out_shape=r_out, scratch_shapes=scratch[0:3],
                 name="retention")(*r_in)
    a_res = call(functools.partial(_attn_kernel, t=t, has_cache=has_state, lam_init=lam_init),
                 in_specs=a_specs, out_specs=a_ospecs, out_shape=a_out, scratch_shapes=scratch[3:],
                 name="diff_attention")(*a_in)
    return m_res, r_res, a_res[0]


def _cmul(ar, ai, br, bi):
    return ar * br - ai * bi, ar * bi + ai * br


def _lam_bar(lre, lim, dt):
    mag = jnp.exp(lre * dt)
    return mag * jnp.cos(lim * dt), mag * jnp.sin(lim * dt)


def _cpow_int(br, bi, e, n_bits):
    pr = jnp.ones(e.shape, F32)
    pi = jnp.zeros(e.shape, F32)
    for bit in range(n_bits):
        on = ((e >> bit) & 1) == 1
        qr, qi = _cmul(pr, pi, br, bi)
        pr, pi = jnp.where(on, qr, pr), jnp.where(on, qi, pi)
        if bit + 1 < n_bits:
            br, bi = _cmul(br, bi, br, bi)
    return pr, pi


def _zoh_coef(lre, lim, br, bi):
    den = lre * lre + lim * lim
    return ((br - 1.0) * lre + bi * lim) / den, (bi * lre - (br - 1.0) * lim) / den


def _toeplitz_rows(kall):
    n = S5_L * S5_GC
    lane = lax.broadcasted_iota(jnp.int32, (S5_GC, n), 1)
    pieces = []
    for s in range(S5_L):
        shifted = kall if s == 0 else pltpu.roll(kall, S5_GC * s, 1)
        pieces.append(jnp.where(lane >= S5_GC * s, shifted, 0.0))
    return jnp.concatenate(pieces, axis=0)


def _s5_prep_group(lam_c_re, lam_c_im, lam_r_re, lam_r_im, ldt_ref, b_t_re, b_t_im, b_re, b_im,
                   c_re, c_im, c_t_re, c_t_im, d_ref, t_ref, bcf_ref, bcb_ref, ccf_ref, ccb_ref, lam_ref):
    n = S5_L * S5_GC
    lane_blk = lax.broadcasted_iota(jnp.int32, (S5_P, n), 1) // S5_GC
    bits = S5_L.bit_length()

    def bases(d):
        dt = jnp.exp(ldt_ref[d])
        return (_lam_bar(lam_c_re[d], lam_c_im[d], dt), _lam_bar(lam_r_re[d], lam_r_im[d], dt))

    def tile_rows(x):
        return jnp.concatenate([x] * S5_L, axis=0)

    def rows_pow(br, bi, reverse):
        s_idx = lax.broadcasted_iota(jnp.int32, (S5_L, S5_P), 0)
        pr, pi = _cpow_int(br, bi, S5_L - 1 - s_idx if reverse else s_idx, bits)
        rep = lambda p: jnp.concatenate(
            [jnp.broadcast_to(p[s:s + 1, :], (S5_GC, S5_P)) for s in range(S5_L)], axis=0)
        return rep(pr), rep(pi)

    def tile_lanes(ref, d):
        return jnp.concatenate([ref[d]] * S5_L, axis=1)

    ct_re, ct_im = [tile_lanes(c_t_re, d) for d in range(2)], [tile_lanes(c_t_im, d) for d in range(2)]

    (cbr, cbi), (rbr, rbi) = bases(0)
    yield
    kr, ki = _zoh_coef(lam_r_re[0], lam_r_im[0], rbr, rbi)
    bbt_re, bbt_im = _cmul(kr, ki, b_t_re[0], b_t_im[0])
    pr, pi = _cpow_int(cbr, cbi, lane_blk, bits)
    yield
    cl_re, cl_im = _cmul(ct_re[0], ct_im[0], pr, pi)
    yield
    t_f = _toeplitz_rows(_dot(bbt_re, cl_re, HI) - _dot(bbt_im, cl_im, HI))
    yield
    pr, pi = rows_pow(rbr, rbi, True)
    yield
    re, im = _cmul(tile_rows(bbt_re), tile_rows(bbt_im), pr, pi)
    bcf_ref[...] = jnp.concatenate([re, im], axis=1).astype(BF16)
    re, im = _cmul(cl_re, cl_im, cbr, cbi)
    ccf_ref[...] = jnp.concatenate([re, -im], axis=0).astype(BF16)
    pr, pi = _cpow_int(rbr, rbi, jnp.full((1, S5_P), S5_L, jnp.int32), bits)
    yield
    lam_ref[0:1, :] = jnp.concatenate([pr, pr], axis=1)
    lam_ref[1:2, :] = jnp.concatenate([-pi, pi], axis=1)

    (cbr, cbi), (rbr, rbi) = bases(1)
    yield
    kr, ki = _zoh_coef(lam_c_re[1], lam_c_im[1], cbr, cbi)
    bb_re, bb_im = _cmul(kr, ki, tile_lanes(b_re, 1), tile_lanes(b_im, 1))
    pr, pi = _cpow_int(cbr, cbi, lane_blk, bits)
    yield
    bl_re, bl_im = _cmul(bb_re, bb_im, pr, pi)
    yield
    t_b = _toeplitz_rows(_dot(c_re[1], bl_re, HI) - _dot(c_im[1], bl_im, HI)).T
    yield
    kr, ki = _zoh_coef(lam_r_re[1], lam_r_im[1], rbr, rbi)
    bbt_re, bbt_im = _cmul(kr, ki, b_t_re[1], b_t_im[1])
    pr, pi = rows_pow(rbr, rbi, False)
    yield
    re, im = _cmul(tile_rows(bbt_re), tile_rows(bbt_im), pr, pi)
    bcb_ref[...] = jnp.concatenate([re, im], axis=1).astype(BF16)
    pr, pi = _cpow_int(cbr, cbi, S5_L - lane_blk, bits)
    yield
    re, im = _cmul(ct_re[1], ct_im[1], pr, pi)
    ccb_ref[...] = jnp.concatenate([re, -im], axis=0).astype(BF16)
    pr, pi = _cpow_int(rbr, rbi, jnp.full((1, S5_P), S5_L, jnp.int32), bits)
    yield
    lam_ref[2:3, :] = jnp.concatenate([pr, pr], axis=1)
    lam_ref[3:4, :] = jnp.concatenate([-pi, pi], axis=1)
    lam_ref[4:8, :] = jnp.zeros((4, 2 * S5_P), F32)

    eye = (lax.broadcasted_iota(jnp.int32, (n, n), 0) == lax.broadcasted_iota(jnp.int32, (n, n), 1))
    d_diag = jnp.concatenate([d_ref[...]] * S5_L, axis=1)
    t_ref[...] = (t_f + t_b + jnp.where(eye, d_diag, 0.0)).astype(BF16)


def _s5_prep_kernel(*refs, n_grp):
    ins, outs = refs[:14], refs[14:]
    gens = [_s5_prep_group(*[r.at[:, gi] for r in ins[:13]], ins[13].at[gi], *[r.at[gi] for r in outs])
            for gi in range(n_grp)]
    while gens:
        for g in list(gens):
            if next(g, True):
                gens.remove(g)


def _s5_prep(lam_re, lam_im, log_dt, b_re, b_im, c_re, c_im, s5_d):
    n = S5_L * S5_GC
    swap = lambda a: jnp.swapaxes(a, -1, -2)
    ins = [lam_re[..., :, None], lam_im[..., :, None], lam_re[..., None, :], lam_im[..., None, :],
           log_dt[..., None, None], swap(b_re), swap(b_im), b_re, b_im, c_re, c_im, swap(c_re), swap(c_im)]
    d_grp = s5_d.reshape(DEPTH, S5_G, 1, S5_GC)

    n_grp = 4

    def dir_spec(a):
        return pl.BlockSpec((None, 2, n_grp) + a.shape[3:], lambda l, g: (l, 0, g, 0, 0))

    def out(rows, cols, dtype):
        return (pl.BlockSpec((None, n_grp, rows, cols), lambda l, g: (l, g, 0, 0)),
                jax.ShapeDtypeStruct((DEPTH, S5_G, rows, cols), dtype))

    outs = [out(n, n, BF16), out(n, 2 * S5_P, BF16), out(n, 2 * S5_P, BF16),
            out(2 * S5_P, n, BF16), out(2 * S5_P, n, BF16), out(8, 2 * S5_P, F32)]
    return pl.pallas_call(
        functools.partial(_s5_prep_kernel, n_grp=n_grp),
        grid=(DEPTH, S5_G // n_grp),
        in_specs=[dir_spec(a) for a in ins] + [pl.BlockSpec((None, n_grp, 1, S5_GC), lambda l, g: (l, g, 0, 0))],
        out_specs=[o[0] for o in outs], out_shape=[o[1] for o in outs],
        compiler_params=_params(("arbitrary", "arbitrary")), name="s5_prep",
    )(*ins, d_grp)


def _block_transpose(a):
    n = len(a)
    blk = lax.broadcasted_iota(jnp.int32, (1, a[0].shape[1]), 1) // S5_GC
    a = list(a)
    bit = 1
    while bit < n:
        upper = (blk & bit) != 0
        for i in range(n):
            if i & bit == 0:
                j = i | bit
                lo, hi = a[i], a[j]
                a[i] = jnp.where(upper, pltpu.roll(hi, bit * S5_GC, 1), lo)
                a[j] = jnp.where(upper, hi, pltpu.roll(lo, LANES - bit * S5_GC, 1))
        bit *= 2
    return a


def _s5_kernel(*refs, n_seq, n_k, has_state):
    if has_state:
        (u_ref, t_ref, bcf_ref, bcb_ref, ccf_ref, ccb_ref, lam_ref, x0_ref,
         y_ref, ug_ref, yg_ref, inj_f, inj_b, inj_s, xin_f, xin_b) = refs
    else:
        (u_ref, t_ref, bcf_ref, bcb_ref, ccf_ref, ccb_ref, lam_ref,
         y_ref, xf_ref, xb_ref, ug_ref, yg_ref, inj_f, inj_b, inj_s, xin_f, xin_b) = refs
    ng = S5_G // 2
    r = n_seq * n_k
    tok = lambda i: pl.ds(i, r, stride=S5_L)
    u_lo = _block_transpose([u_ref[tok(i), :] for i in range(ng)])
    u_hi = _block_transpose([u_ref[tok(i), :] for i in range(ng, S5_L)])
    for g in range(ng):
        ug_ref[g] = jnp.concatenate([u_lo[g], u_hi[g]], axis=1).astype(BF16)
        inj_f[g] = _dot(ug_ref[g], bcf_ref[g])
        inj_b[g] = _dot(ug_ref[g], bcb_ref[g])

    for d, (inj, xin) in enumerate(((inj_f, xin_f), (inj_b, xin_b))):
        a = [lam_ref[g, 2 * d:2 * d + 1, :] for g in range(ng)]
        bs = [lam_ref[g, 2 * d + 1:2 * d + 2, :] for g in range(ng)]
        for g in range(ng):
            inj_s[g] = pltpu.roll(inj[g], S5_P, 1)
        x = [x0_ref[d, g] if has_state else jnp.zeros((n_seq, 2 * S5_P), F32) for g in range(ng)]
        xs = [pltpu.roll(v, S5_P, 1) for v in x]
        for k in (range(n_k) if d == 0 else range(n_k - 1, -1, -1)):
            rows = pl.ds(k, n_seq, stride=n_k)
            for g in range(ng):
                xin[g, rows, :] = x[g]
                x[g], xs[g] = (a[g] * x[g] + bs[g] * xs[g] + inj[g, rows, :],
                               a[g] * xs[g] - bs[g] * x[g] + inj_s[g, rows, :])
        if not has_state:
            for g in range(ng):
                (xf_ref if d == 0 else xb_ref)[g] = x[g]

    for g in range(ng):
        yg_ref[g] = (_dot(ug_ref[g], t_ref[g]) + _dot(xin_f[g].astype(BF16), ccf_ref[g])
                     + _dot(xin_b[g].astype(BF16), ccb_ref[g]))
    for half in range(S5_L // ng):
        y_i = _block_transpose([yg_ref[g, :, LANES * half:LANES * (half + 1)] for g in range(ng)])
        for i in range(ng):
            y_ref[tok(ng * half + i), :] = y_i[i]


def _s5(u_halves, ops, l, n_seq, n_k, x0):
    has_state = x0 is not None
    n = S5_L * S5_GC
    r = n_seq * n_k
    w = 2 * S5_P
    ng = S5_G // 2
    half_spec = pl.BlockSpec((None, r * S5_L, LANES), lambda j: (j, 0, 0))
    op_spec = lambda a: pl.BlockSpec((None, ng) + a.shape[2:], lambda j: (l, j) + (0,) * (a.ndim - 2))
    args = [u_halves] + list(ops)
    in_specs = [half_spec] + [op_spec(a) for a in ops]
    if has_state:
        args.append(x0)
        in_specs.append(pl.BlockSpec((None, 2, ng, n_seq, w), lambda j: (l, 0, j, 0, 0)))
    out_shape = [jax.ShapeDtypeStruct((2, r * S5_L, LANES), F32)]
    out_specs = [half_spec]
    if not has_state:
        out_shape += [jax.ShapeDtypeStruct((S5_G, n_seq, w), F32)] * 2
        out_specs += [pl.BlockSpec((ng, n_seq, w), lambda j: (j, 0, 0))] * 2
    return pl.pallas_call(
        functools.partial(_s5_kernel, n_seq=n_seq, n_k=n_k, has_state=has_state),
        grid=(2,), in_specs=in_specs, out_specs=out_specs, out_shape=out_shape,
        scratch_shapes=[pltpu.VMEM((ng, r, n), BF16), pltpu.VMEM((ng, r, n), F32)]
        + [pltpu.VMEM((ng, r, w), F32)] * 5,
        compiler_params=_params(("arbitrary",)), name="s5_scan",
    )(*args)


def _out_ffn_kernel(x_ref, m_ref, r_ref, a_ref, ya_ref, yb_ref, mod_ref, wglu_ref, bglu_ref, wout_ref,
                    n_post_ref, n_pre_ref, n_fpost_ref, wg_ref, wu_ref, wd_ref, o_ref):
    def mod(i):
        return mod_ref[:, i * D_MODEL:(i + 1) * D_MODEL]

    y = jnp.concatenate([ya_ref[...], yb_ref[...]], axis=1)
    gs = 0.5 * y * (1.0 + jnp.tanh(math.sqrt(2.0 / math.pi) * (y + 0.044715 * (y * y * y))))
    s_out = gs * jax.nn.sigmoid(_dot(gs.astype(BF16), wglu_ref[...]) + bglu_ref[...])
    mixed = jnp.concatenate([m_ref[...], r_ref[...], a_ref[...], s_out], axis=1).astype(BF16)
    halves = [slice(0, ROW_TILE // 2), slice(ROW_TILE // 2, ROW_TILE)]
    x1 = [x_ref[r, :] + mod(2) * _rms(_dot(mixed[r], wout_ref[...]), n_post_ref[...]) for r in halves]
    h = [(_rms(v, n_pre_ref[...]) * (1.0 + mod(4)) + mod(3)).astype(BF16) for v in x1]
    g = [_dot(v, wg_ref[...]) for v in h]
    u = [_dot(v, wu_ref[...]) for v in h]
    act = [(_silu(a) * b).astype(BF16) for a, b in zip(g, u)]
    f = [_dot(v, wd_ref[...]) for v in act]
    for r, v, ff in zip(halves, x1, f):
        o_ref[r, :] = v + mod(5) * _rms(ff, n_fpost_ref[...])


def _out_ffn(x, m_out, r_out, a_out, y_halves, mod, l, mod_row, wp):
    n = x.shape[0]
    row = lambda w: pl.BlockSpec((ROW_TILE, w), lambda i: (i, 0))
    half = lambda j: pl.BlockSpec((None, ROW_TILE, LANES), lambda i: (j, i, 0))
    params = [wp[k] for k in ('w_glu', 'b_glu', 'w_out', 'n_mix_post', 'n_ffn_pre', 'n_ffn_post',
                              'w_gate', 'w_up', 'w_down')]
    return pl.pallas_call(
        _out_ffn_kernel,
        grid=(n // ROW_TILE,),
        in_specs=[row(D_MODEL), row(GROUP_W), row(GROUP_W), row(GROUP_W), half(0), half(1),
                  _mod_spec(l, mod_row)] + [_layer_spec(a, l, pipeline_mode=pl.Buffered(1)) for a in params],
        out_specs=row(D_MODEL),
        out_shape=jax.ShapeDtypeStruct((n, D_MODEL), F32),
        compiler_params=_params(("arbitrary",)), name="out_ffn",
    )(x, m_out, r_out, a_out, y_halves, y_halves, mod, *params)


def _layer(x, l, n_seq, t, mod, mod_row, wp, s5_ops, lam_init, st):
    ctx = st is None
    res = _in_proj(x, mod, l, mod_row, wp['n_mix_pre'], wp['w_in'], emit_kv=ctx)
    z, gates = res[0], res[2]
    m_res, r_res, a_out = _mixers(z, gates, wp, l, n_seq, t, None if ctx else st[0:5], lam_init, fuse=ctx)
    s_res = _s5(res[1], s5_ops, l, n_seq, t // S5_L, None if ctx else st[5])
    x_new = _out_ffn(x, m_res[0], r_res[0], a_out, s_res[0], mod, l, mod_row, wp)
    if not ctx:
        return x_new, None
    return x_new, (m_res[1], m_res[2], m_res[3], r_res[1], res[3], res[4], s_res[1], s_res[2])


def _pad_to(a, rows, cols=LANES):
    pad = [(0, 0)] * (a.ndim - 2) + [(0, rows - a.shape[-2]), (0, cols - a.shape[-1])]
    return jnp.pad(a, pad)


def kernel(x_prompt, x_sample, state_mlstm_C, state_mlstm_n, state_mlstm_m, state_ret, cache_diff_k, cache_diff_v, state_s5_re, state_s5_im, c, c_ctx, w_ada, b_ada, n_mix_pre, n_mix_post, n_ffn_pre, n_ffn_post, w_in, w_out, m_gate_bias, m_norm, r_decay_logit, r_norm, a_lam_q1, a_lam_k1, a_lam_q2, a_lam_k2, a_norm, s5_lam_re, s5_lam_im, s5_log_dt, s5_b_re, s5_b_im, s5_c_re, s5_c_im, s5_d, s5_w_glu, s5_b_glu, w_ffn_gate, w_ffn_up, w_ffn_down):
    n_ctx, t_ctx, _ = x_prompt.shape
    n_lat, t_lat, _ = x_sample.shape
    past = cache_diff_k.shape[2]

    cond8 = jnp.concatenate([c, c_ctx[None, :], jnp.zeros((8 - n_lat - 1, D_MODEL), F32)], axis=0)
    mod = _ada(cond8, w_ada, b_ada).reshape(DEPTH, 8, 1, 6 * D_MODEL)
    s5_ops = _s5_prep(s5_lam_re, s5_lam_im, s5_log_dt, s5_b_re, s5_b_im, s5_c_re, s5_c_im, s5_d)

    row = lambda a: a[:, None, :]
    wp = dict(
        w_in=jnp.pad(w_in.astype(BF16), ((0, 0), (0, 0), (0, N_MAIN + LANES - w_in.shape[-1]))), w_out=w_out.astype(BF16), w_glu=s5_w_glu.astype(BF16), b_glu=row(s5_b_glu),
        w_gate=w_ffn_gate.astype(BF16), w_up=w_ffn_up.astype(BF16), w_down=w_ffn_down.astype(BF16),
        n_mix_pre=row(n_mix_pre), n_mix_post=row(n_mix_post), n_ffn_pre=row(n_ffn_pre),
        n_ffn_post=row(n_ffn_post), m_norm=row(m_norm), r_norm=row(r_norm), a_norm=row(a_norm),
        m_bias=_pad_to(row(m_gate_bias), 1), r_decay=_pad_to(r_decay_logit, 8),
        a_lam=_pad_to(jnp.stack([a_lam_q1, a_lam_k1, a_lam_q2, a_lam_k2], axis=1), 8))
    lam_inits = [0.8 - 0.6 * math.exp(-0.3 * l) for l in range(DEPTH)]

    x = x_prompt.reshape(n_ctx * t_ctx, D_MODEL)
    new_states = []
    for l in range(DEPTH):
        x, st = _layer(x, l, n_ctx, t_ctx, mod, lambda i: n_lat, wp, s5_ops, lam_inits[l], None)
        new_states.append(st)
    y_prompt = x.reshape(n_ctx, t_ctx, D_MODEL)

    x0 = jnp.stack([state_s5_re, state_s5_im], axis=-2)
    x0 = x0.transpose(1, 2, 3, 0, 4, 5).reshape(DEPTH, 2, S5_G, n_lat, 2 * S5_P)
    s_ext0 = jnp.concatenate([state_mlstm_C, state_mlstm_n[..., None],
                              jnp.zeros(state_mlstm_n.shape + (LANES - HEAD_DIM - 1,), F32)], axis=-1)
    st = (s_ext0, state_mlstm_m.reshape(n_lat, DEPTH, 2 * N_HEADS, 1), state_ret,
          cache_diff_k.reshape(n_lat, DEPTH, past, GROUP_W), cache_diff_v.reshape(n_lat, DEPTH, past, GROUP_W), x0)
    x = x_sample.reshape(n_lat * t_lat, D_MODEL)
    tiles_per_seq = t_lat // ROW_TILE
    for l in range(DEPTH):
        x, _ = _layer(x, l, n_lat, t_lat, mod, lambda i: i // tiles_per_seq, wp, s5_ops, lam_inits[l], st)
    y_sample = x.reshape(n_lat, t_lat, D_MODEL)

    stack = lambda i: jnp.stack([s[i] for s in new_states], axis=1)
    kv = lambda i: jnp.stack([s[i].reshape(n_ctx, t_ctx, N_HEADS, HEAD_DIM) for s in new_states], axis=1)
    xs = jnp.stack([stack(6), stack(7)], axis=2)
    xs = xs.reshape(S5_G, DEPTH, 2, n_ctx, 2, S5_P).transpose(3, 1, 2, 0, 4, 5)
    return (y_prompt, y_sample, stack(0), stack(1)[..., 0], stack(2).reshape(n_ctx, DEPTH, 2, N_HEADS),
            stack(3), kv(4), kv(5), xs[..., 0, :], xs[..., 1, :])
```

```python
import functools
import math

import jax
import jax.numpy as jnp
from jax import lax
from jax.experimental import pallas as pl
from jax.experimental.pallas import tpu as pltpu

F32 = jnp.float32
BF16 = jnp.bfloat16
HI = lax.Precision.HIGHEST

D_MODEL = 1024
DEPTH = 2
GRID_W = 64
HEAD_DIM = 64
GROUP_W = 256
N_HEADS = 4
A_SUB = 32
S5_GC = 16
S5_G = 16
S5_P = 64
D_FF = 2816
ROPE_BASE = 10000.0
EPS = 1e-6
N_MAIN = 12 * GROUP_W
N_MIX = 11 * GROUP_W
LANES = 128
CHUNK = 256
ROW_TILE = 512
S5_L = 16
NEG = -1e30
VMEM_LIMIT = 56 * 1024 * 1024


def _dot(a, b, precision=None):
    return jnp.dot(a, b, preferred_element_type=F32, precision=precision)


def _dot_nt(a, b):
    return lax.dot_general(a, b, (((1,), (1,)), ((), ())), preferred_element_type=F32)


def _log_sigmoid(x):
    return jnp.minimum(x, 0.0) - jnp.log(1.0 + jnp.exp(-jnp.abs(x)))


def _silu(x):
    return x * jax.nn.sigmoid(x)


def _rms(x, g):
    return x * lax.rsqrt(jnp.mean(x * x, axis=-1, keepdims=True) + EPS) * g


def _params(sem=None):
    return pltpu.CompilerParams(dimension_semantics=sem, vmem_limit_bytes=VMEM_LIMIT)


def _layer_spec(a, l, **kw):
    n = a.ndim - 1
    return pl.BlockSpec((None,) + a.shape[1:], lambda *_: (l,) + (0,) * n, **kw)


def _seq_layer_spec(a, l):
    n = a.ndim - 2
    return pl.BlockSpec((None, None) + a.shape[2:], lambda b: (b, l) + (0,) * n)


def _split_bf16(x):
    hi = x.astype(BF16)
    return hi, (x - hi.astype(F32)).astype(BF16)


def _ada_kernel(c_ref, w_ref, b_ref, o_ref):
    a_hi, a_lo = _split_bf16(_silu(c_ref[...]))
    w_hi, w_lo = _split_bf16(w_ref[...])
    o_ref[...] = _dot(a_hi, w_hi) + (_dot(a_lo, w_hi) + _dot(a_hi, w_lo)) + b_ref[...]


def _ada(cond8, w_ada, b_ada):
    tn = 1536
    return pl.pallas_call(
        _ada_kernel,
        grid=(DEPTH, 6 * D_MODEL // tn),
        in_specs=[pl.BlockSpec((8, D_MODEL), lambda l, j: (0, 0)),
                  pl.BlockSpec((None, D_MODEL, tn), lambda l, j: (l, 0, j)),
                  pl.BlockSpec((None, 1, tn), lambda l, j: (l, 0, j))],
        out_specs=pl.BlockSpec((None, 8, tn), lambda l, j: (l, 0, j)),
        out_shape=jax.ShapeDtypeStruct((DEPTH, 8, 6 * D_MODEL), F32),
        compiler_params=_params(("arbitrary", "arbitrary")),
        name="ada",
    )(cond8, w_ada, b_ada.reshape(DEPTH, 1, 6 * D_MODEL))


def _in_proj_kernel(x_ref, mod_ref, g_ref, w_ref, z_ref, u_ref, gate_ref, *kv_refs):
    n_gate = 4 * N_HEADS
    n_head = 4 * GROUP_W
    halves = [slice(0, ROW_TILE // 2), slice(ROW_TILE // 2, ROW_TILE)]
    hb = [(_rms(x_ref[r, :], g_ref[...]) * (1.0 + mod_ref[:, D_MODEL:2 * D_MODEL])
           + mod_ref[:, 0:D_MODEL]).astype(BF16) for r in halves]
    head = [_dot(v, w_ref[:, 0:n_head]) for v in hb]
    tail = [_dot(v, w_ref[:, n_head:]) for v in hb]
    for r, hd, tl in zip(halves, head, tail):
        z_ref[r, 0:n_head] = hd.astype(BF16)
        gate_ref[r, :] = tl[:, 0:LANES]
        rest = tl[:, n_gate:n_gate + N_MAIN - n_head]
        z_ref[r, n_head:] = rest[:, 0:N_MIX - n_head].astype(BF16)
        u_ref[0, r, :] = rest[:, N_MIX - n_head:N_MIX - n_head + LANES]
        u_ref[1, r, :] = rest[:, N_MIX - n_head + LANES:]
        if kv_refs:
            kv_refs[0][r, :] = rest[:, 9 * GROUP_W - n_head:10 * GROUP_W - n_head]
            kv_refs[1][r, :] = rest[:, 10 * GROUP_W - n_head:11 * GROUP_W - n_head]


def _mod_spec(l, mod_row):
    return pl.BlockSpec((None, None, 1, 6 * D_MODEL), lambda i: (l, mod_row(i), 0, 0))


def _in_proj(x, mod, l, mod_row, gain, w, emit_kv):
    n = x.shape[0]
    row = lambda width: pl.BlockSpec((ROW_TILE, width), lambda i: (i, 0))
    out_specs = [row(N_MIX), pl.BlockSpec((2, ROW_TILE, LANES), lambda i: (0, i, 0)), row(LANES)]
    out_specs += [row(GROUP_W)] * (2 if emit_kv else 0)
    out_shape = [jax.ShapeDtypeStruct((n, N_MIX), BF16), jax.ShapeDtypeStruct((2, n, LANES), F32),
                 jax.ShapeDtypeStruct((n, LANES), F32)]
    out_shape += [jax.ShapeDtypeStruct((n, GROUP_W), F32)] * (2 if emit_kv else 0)
    return pl.pallas_call(
        _in_proj_kernel,
        grid=(n // ROW_TILE,),
        in_specs=[row(D_MODEL), _mod_spec(l, mod_row), _layer_spec(gain, l),
                  _layer_spec(w, l, pipeline_mode=pl.Buffered(1))],
        out_specs=out_specs, out_shape=out_shape,
        compiler_params=_params(("arbitrary",)),
        name="in_proj",
    )(x, mod, gain, w)


def _tri_masks(n):
    row = lax.broadcasted_iota(jnp.int32, (n, n), 0)
    col = lax.broadcasted_iota(jnp.int32, (n, n), 1)
    return row >= col, row <= col


def _head_mean_matrix():
    r = lax.broadcasted_iota(jnp.int32, (GROUP_W, GROUP_W), 0) // HEAD_DIM
    c = lax.broadcasted_iota(jnp.int32, (GROUP_W, GROUP_W), 1) // HEAD_DIM
    return jnp.where(r == c, 1.0 / HEAD_DIM, 0.0).astype(BF16)


def _head_mean(x, j):
    hi, lo = _split_bf16(x)
    return _dot(hi, j) + _dot(lo, j)


def _head_groupnorm(x, g, j):
    xc = x - _head_mean(x, j)
    return xc * lax.rsqrt(_head_mean(xc * xc, j) + EPS) * g


def _head_rms(x, g, j):
    return x * lax.rsqrt(_head_mean(x * x, j) + EPS) * g


def _hs(h):
    return slice(HEAD_DIM * h, HEAD_DIM * (h + 1))


def _v_ext(v_ref, rows, hs):
    ones = jnp.ones((rows.stop - rows.start, HEAD_DIM), BF16)
    return jnp.concatenate([v_ref[rows, hs].astype(BF16), ones], axis=1)


def _scan_max(x, reverse):
    n = x.shape[1]
    lane = lax.broadcasted_iota(jnp.int32, x.shape, 1)
    sh = 1
    while sh < n:
        if reverse:
            x = jnp.maximum(x, jnp.where(lane < n - sh, pltpu.roll(x, n - sh, 1), NEG))
        else:
            x = jnp.maximum(x, jnp.where(lane >= sh, pltpu.roll(x, sh, 1), NEG))
        sh *= 2
    return x


def _ends(x, is_fwd):
    return jnp.where(is_fwd, x[:, x.shape[1] - 1:], x[:, 0:1])


def _mlstm_kernel(*refs, n_chunk, has_state):
    if has_state:
        q_ref, k_ref, v_ref, o_ref, g_ref, bias_ref, gain_ref, s0_ref, m0_ref, out_ref = refs
    else:
        q_ref, k_ref, v_ref, o_ref, g_ref, bias_ref, gain_ref, out_ref, cn_ref, nn_ref, mn_ref = refs
    L = CHUNK
    nu = 2 * N_HEADS
    tril, triu = _tri_masks(L)
    is_fwd = lax.broadcasted_iota(jnp.int32, (nu, 1), 0) < N_HEADS

    b8, w8, mcum8 = [], [], []
    for c in range(n_chunk):
        p_t = (g_ref[c * L:(c + 1) * L, :] + bias_ref[...]).T
        lf = _log_sigmoid(p_t[nu:2 * nu, :])
        b = jnp.where(is_fwd, _dot(lf, triu.astype(F32), HI), _dot(lf, tril.astype(F32), HI))
        w = p_t[0:nu, :] - b
        b8.append(b)
        w8.append(w)
        mcum8.append(jnp.where(is_fwd, _scan_max(w, False), _scan_max(w, True)))

    m0 = m0_ref[...] if has_state else jnp.zeros((nu, 1), F32)
    m_in_f, m_in_b = [None] * n_chunk, [None] * n_chunk
    m = m0
    for c in range(n_chunk):
        m_in_f[c] = m
        m = (b8[c] + jnp.maximum(mcum8[c], m))[:, L - 1:]
    m_fin_f = m
    m = m0
    for c in range(n_chunk - 1, -1, -1):
        m_in_b[c] = m
        m = (b8[c] + jnp.maximum(mcum8[c], m))[:, 0:1]
    if not has_state:
        mn_ref[...] = jnp.where(is_fwd, m_fin_f, m)

    wk8, dec8, cols = [], [], []
    for c in range(n_chunk):
        m_in = jnp.where(is_fwd, m_in_f[c], m_in_b[c])
        g = jnp.maximum(mcum8[c], m_in)
        m_row = b8[c] + g
        m_new, b_last = _ends(m_row, is_fwd), _ends(b8[c], is_fwd)
        wk8.append(jnp.exp(b_last + w8[c] - m_new))
        dec8.append(jnp.exp(b_last + m_in - m_new))
        stats = jnp.concatenate([g, jnp.exp(m_in - g), jnp.exp(-m_row), jnp.zeros((LANES - 3 * nu, L), F32)], axis=0)
        cols.append(stats.T)

    k_t = [(k_ref[c * L:(c + 1) * L, :].astype(F32) * (HEAD_DIM ** -0.5)).T for c in range(n_chunk)]

    h_all = []
    for h in range(N_HEADS):
        hs = _hs(h)
        qb = [q_ref[c * L:(c + 1) * L, hs].astype(BF16) for c in range(n_chunk)]
        kt = [k_t[c][hs, :] for c in range(n_chunk)]
        vx = [_v_ext(v_ref, slice(c * L, (c + 1) * L), hs) for c in range(n_chunk)]
        qk = [_dot(qb[c], kt[c].astype(BF16)) for c in range(n_chunk)]
        h_sum = [None] * n_chunk
        for d in range(2):
            tri = tril if d == 0 else triu
            j = N_HEADS * d + h
            s_ext = s0_ref[d, h] if has_state else None
            order = range(n_chunk) if d == 0 else range(n_chunk - 1, -1, -1)
            for ci, c in enumerate(order):
                wgt = jnp.exp(jnp.where(tri, w8[c][j:j + 1, :] - cols[c][:, j:j + 1], NEG))
                tot = _dot((qk[c] * wgt).astype(BF16), vx[c])
                if s_ext is not None:
                    tot = tot + cols[c][:, nu + j:nu + j + 1] * _dot(qb[c], s_ext.astype(BF16))
                den = jnp.maximum(jnp.abs(tot[:, HEAD_DIM:HEAD_DIM + 1]), cols[c][:, 2 * nu + j:2 * nu + j + 1])
                hd = tot[:, 0:HEAD_DIM] / den
                h_sum[c] = hd if h_sum[c] is None else h_sum[c] + hd
                if (not has_state) or ci < n_chunk - 1:
                    upd = _dot((kt[c] * wk8[c][j:j + 1, :]).astype(BF16), vx[c])
                    s_ext = upd if s_ext is None else upd + dec8[c][j:j + 1, :] * s_ext
            if not has_state:
                cn_ref[d, h] = s_ext[:, 0:HEAD_DIM]
                nn_ref[d, h] = s_ext[:, HEAD_DIM:HEAD_DIM + 1]
        h_all.append(h_sum)
    jm = _head_mean_matrix()
    for c in range(n_chunk):
        rows = slice(c * L, (c + 1) * L)
        gated = jax.nn.sigmoid(o_ref[rows, :].astype(F32)) * jnp.concatenate([h_all[h][c] for h in range(N_HEADS)], axis=1)
        out_ref[rows, :] = _head_groupnorm(gated, gain_ref[...], jm)


def _ret_kernel(*refs, n_chunk, has_state):
    if has_state:
        q_ref, k_ref, v_ref, g_ref, lg_ref, gain_ref, s0_ref, out_ref, dmat_ref, col_ref, row_ref = refs
    else:
        q_ref, k_ref, v_ref, g_ref, lg_ref, gain_ref, out_ref, sn_ref, dmat_ref, col_ref, row_ref = refs
    L = CHUNK

    @pl.when(pl.program_id(0) == 0)
    def _():
        row = lax.broadcasted_iota(jnp.int32, (L, L), 0)
        col = lax.broadcasted_iota(jnp.int32, (L, L), 1)
        diff = (row - col).astype(F32)
        log_g = _log_sigmoid(lg_ref[...])
        pos_c = lax.broadcasted_iota(jnp.int32, (L, 1), 0).astype(F32)
        pos_r = lax.broadcasted_iota(jnp.int32, (1, L), 1).astype(F32)
        for h in range(N_HEADS):
            lgf = log_g[0:1, h:h + 1]
            lgb = log_g[1:2, h:h + 1]
            dmat_ref[h] = (jnp.where(diff >= 0, jnp.exp(jnp.maximum(diff, 0.0) * lgf), 0.0)
                           + jnp.where(diff <= 0, jnp.exp(jnp.maximum(-diff, 0.0) * lgb), 0.0))
            col_ref[:, h:h + 1] = jnp.exp((pos_c + 1.0) * lgf)
            col_ref[:, N_HEADS + h:N_HEADS + h + 1] = jnp.exp((L - pos_c) * lgb)
            row_ref[h:h + 1, :] = jnp.exp((L - 1.0 - pos_r) * lgf)
            row_ref[N_HEADS + h:N_HEADS + h + 1, :] = jnp.exp(pos_r * lgb)
            row_ref[2 * N_HEADS + h:2 * N_HEADS + h + 1, :] = jnp.exp(L * lgf) + jnp.zeros((1, L), F32)
            row_ref[3 * N_HEADS + h:3 * N_HEADS + h + 1, :] = jnp.exp(L * lgb) + jnp.zeros((1, L), F32)

    k_t = [(k_ref[c * L:(c + 1) * L, :].astype(F32) * (HEAD_DIM ** -0.5)).T for c in range(n_chunk)]
    o_all = [None] * N_HEADS

    def head_stages(h):
        hs = _hs(h)
        zeta_f = row_ref[h:h + 1, :]
        zeta_b = row_ref[N_HEADS + h:N_HEADS + h + 1, :]
        gl_f = row_ref[2 * N_HEADS + h:2 * N_HEADS + h + 1, 0:1]
        gl_b = row_ref[3 * N_HEADS + h:3 * N_HEADS + h + 1, 0:1]
        kt = [k_t[c][hs, :] for c in range(n_chunk)]
        vb = [v_ref[c * L:(c + 1) * L, hs].astype(BF16) for c in range(n_chunk)]
        sf = [None] * (n_chunk + 1)
        sb = [None] * (n_chunk + 1)
        if has_state:
            sf[0] = s0_ref[0, h]
            sb[n_chunk] = s0_ref[1, h]
        for c in range(n_chunk):
            if (not has_state) or c < n_chunk - 1:
                upd = _dot((kt[c] * zeta_f).astype(BF16), vb[c])
                sf[c + 1] = upd if sf[c] is None else upd + gl_f * sf[c]
        for c in range(n_chunk - 1, -1, -1):
            if (not has_state) or c > 0:
                upd = _dot((kt[c] * zeta_b).astype(BF16), vb[c])
                sb[c] = upd if sb[c + 1] is None else upd + gl_b * sb[c + 1]
        if not has_state:
            sn_ref[0, h] = sf[n_chunk]
            sn_ref[1, h] = sb[0]
        yield
        o_h = []
        for c in range(n_chunk):
            qb = q_ref[c * L:(c + 1) * L, hs].astype(BF16)
            att = _dot(qb, kt[c].astype(BF16)) * dmat_ref[h]
            o = _dot(att.astype(BF16), vb[c])
            if has_state:
                o = o + col_ref[:, h:h + 1] * _dot(qb, sf[c].astype(BF16))
                o = o + col_ref[:, N_HEADS + h:N_HEADS + h + 1] * _dot(qb, sb[c + 1].astype(BF16))
            o_h.append(o)
            yield
        o_all[h] = o_h

    gens = [head_stages(h) for h in range(N_HEADS)]
    while gens:
        for g in list(gens):
            if has_state:
                if next(g, True):
                    gens.remove(g)
            else:
                for _ in g:
                    pass
                gens.remove(g)
    jm = _head_mean_matrix()
    for c in range(n_chunk):
        rows = slice(c * L, (c + 1) * L)
        o = jnp.concatenate([o_all[h][c] for h in range(N_HEADS)], axis=1)
        out_ref[rows, :] = _silu(g_ref[rows, :].astype(F32)) * _head_groupnorm(o, gain_ref[...], jm)


def _lambda(lam_ref, lam_init):
    v = lam_ref[...]
    s1 = jnp.sum(v[0:1, :] * v[1:2, :], axis=-1, keepdims=True)
    s2 = jnp.sum(v[2:3, :] * v[3:4, :], axis=-1, keepdims=True)
    return jnp.exp(s1) - jnp.exp(s2) + lam_init


def _first_half(shape):
    lane = lax.broadcasted_iota(jnp.int32, shape, 1)
    return (lane % (A_SUB // 2)) < (A_SUB // 4)


def _rope(x, cos, sin_signed):
    n = x.shape[1]
    first_half = _first_half(x.shape)
    partner = jnp.where(first_half, pltpu.roll(x, n - A_SUB // 4, 1), pltpu.roll(x, A_SUB // 4, 1))
    return x * cos + partner * sin_signed


def _attn_kernel(*refs, t, has_cache, lam_init):
    if has_cache:
        q_ref, k_ref, v_ref, kc_ref, vc_ref, lam_ref, gain_ref, out_ref, qr_ref, kr_ref, cos_ref, sin_ref = refs
    else:
        q_ref, k_ref, v_ref, lam_ref, gain_ref, out_ref, qr_ref, kr_ref = refs
    lam = _lambda(lam_ref, lam_init)
    scale = A_SUB ** -0.5
    qb = 2 * CHUNK if has_cache else CHUNK
    n_qb = t // qb

    if has_cache:
        @pl.when(pl.program_id(0) == 0)
        def _():
            lane = lax.broadcasted_iota(jnp.int32, (GRID_W, GROUP_W), 1)
            pos = lax.broadcasted_iota(jnp.int32, (GRID_W, GROUP_W), 0).astype(F32)
            n_freq = A_SUB // 4
            ang = pos * jnp.exp((lane % n_freq).astype(F32) * (-math.log(ROPE_BASE) / n_freq))
            cos_t = jnp.cos(ang)
            sin_t = jnp.where(_first_half((GRID_W, GROUP_W)), -jnp.sin(ang), jnp.sin(ang))
            row_axis = (lane % A_SUB) < (A_SUB // 2)
            for r in range(t // GRID_W):
                rows = slice(r * GRID_W, (r + 1) * GRID_W)
                cos_ref[rows, :] = jnp.where(row_axis, cos_t[r:r + 1, :], cos_t)
                sin_ref[rows, :] = jnp.where(row_axis, sin_t[r:r + 1, :], sin_t)

        kr_ref[...] = _rope(k_ref[...].astype(F32), cos_ref[...], sin_ref[...]).astype(BF16)
        qr_ref[...] = (_rope(q_ref[...].astype(F32), cos_ref[...], sin_ref[...]) * scale).astype(BF16)
    else:
        kr_ref[...] = k_ref[...].astype(BF16)
        qr_ref[...] = (q_ref[...].astype(F32) * scale).astype(BF16)

    vxs = [_v_ext(v_ref, slice(0, t), _hs(h)) for h in range(N_HEADS)]
    if has_cache:
        vcxs = [_v_ext(vc_ref, slice(0, vc_ref.shape[0]), _hs(h)) for h in range(N_HEADS)]

    def softmax_v(rows, units):
        ms = [slice(HEAD_DIM * h + A_SUB * m, HEAD_DIM * h + A_SUB * (m + 1)) for h, m in units]
        qm = [qr_ref[rows, c] for c in ms]
        s_l = [_dot_nt(q, kr_ref[:, c]) for q, c in zip(qm, ms)]
        mx = [jnp.max(s, axis=-1, keepdims=True) for s in s_l]
        if has_cache:
            s_c = [_dot_nt(q, kc_ref[:, c].astype(BF16)) for q, c in zip(qm, ms)]
            mx = [jnp.maximum(a, jnp.max(s, axis=-1, keepdims=True)) for a, s in zip(mx, s_c)]
        tot = [_dot(jnp.exp(s - a).astype(BF16), vxs[h]) for s, a, (h, m) in zip(s_l, mx, units)]
        if has_cache:
            tot = [tt + _dot(jnp.exp(s - a).astype(BF16), vcxs[h]) for tt, s, a, (h, m) in zip(tot, s_c, mx, units)]
        return [tt[:, 0:HEAD_DIM] / tt[:, HEAD_DIM:HEAD_DIM + 1] for tt in tot]

    o_all = [[None] * n_qb for _ in range(N_HEADS)]
    for h in range(N_HEADS):
        for qi in range(n_qb):
            rows = slice(qi * qb, (qi + 1) * qb)
            units = [(h, 0), (h, 1)]
            o = softmax_v(rows, units) if has_cache else [softmax_v(rows, [u])[0] for u in units]
            o_all[h][qi] = o[0] - lam * o[1]
    jm = _head_mean_matrix()
    for qi in range(n_qb):
        rows = slice(qi * qb, (qi + 1) * qb)
        o = jnp.concatenate([o_all[h][qi] for h in range(N_HEADS)], axis=1)
        out_ref[rows, :] = _head_rms(o, gain_ref[...], jm) * (1.0 - lam_init)


def _mixers_kernel(*refs, n_chunk, t, has_state, lam_init, n_in, n_out, kinds, n_sub):
    n_io = sum(n_in) + sum(n_out)
    io, scratch = refs[:n_io], refs[n_io:]
    for sub in range(n_sub):
        def view(r, kind):
            if kind == 'rows':
                return r.at[pl.ds(sub * t, t)]
            return r.at[sub] if kind == 'seq' else r

        v = [view(r, k) for r, k in zip(io, kinds)]
        ins, outs = v[:sum(n_in)], v[sum(n_in):]
        i0, o0 = 0, 0
        parts = []
        for k in range(3):
            parts.append((ins[i0:i0 + n_in[k]], outs[o0:o0 + n_out[k]]))
            i0 += n_in[k]
            o0 += n_out[k]
        a_scratch = [r.at[sub] for r in scratch[3:]]
        _mlstm_kernel(*parts[0][0], *parts[0][1], n_chunk=n_chunk, has_state=has_state)
        _ret_kernel(*parts[1][0], *parts[1][1], *scratch[0:3], n_chunk=n_chunk, has_state=has_state)
        _attn_kernel(*parts[2][0], *parts[2][1], *a_scratch, t=t, has_cache=has_state, lam_init=lam_init)


def _mixers(z, gates, wp, l, n_seq, t, st, lam_init, fuse):
    has_state = st is not None
    nu = 2 * N_HEADS
    n_sub = 2 if fuse else 1
    tt = n_sub * t
    seq = lambda width: pl.BlockSpec((tt, width), lambda b: (b, 0))
    zspec = lambda colblk: pl.BlockSpec((tt, GROUP_W), lambda b: (b, colblk))
    out_rows = jax.ShapeDtypeStruct((n_seq * t, GROUP_W), F32)
    head_state = (2, N_HEADS, HEAD_DIM, HEAD_DIM)
    lead = (n_sub,) if fuse else (None,)
    per_seq = lambda *shape: pl.BlockSpec(lead + shape, lambda b: (b,) + (0,) * len(shape))

    m_in = [z, z, z, z, gates, wp['m_bias'], wp['m_norm']]
    m_specs = [zspec(0), zspec(1), zspec(2), zspec(3), seq(LANES),
               _layer_spec(wp['m_bias'], l), _layer_spec(wp['m_norm'], l)]
    m_kinds = ['rows'] * 5 + ['shared'] * 2
    r_in = [z, z, z, z, wp['r_decay'], wp['r_norm']]
    r_specs = [zspec(4), zspec(5), zspec(6), zspec(7),
               _layer_spec(wp['r_decay'], l), _layer_spec(wp['r_norm'], l)]
    r_kinds = ['rows'] * 4 + ['shared'] * 2
    a_in = [z, z, z]
    a_specs = [zspec(8), zspec(9), zspec(10)]
    a_kinds = ['rows'] * 3
    m_out, r_out, a_out = [out_rows], [out_rows], [out_rows]
    m_ospecs, r_ospecs, a_ospecs = [seq(GROUP_W)], [seq(GROUP_W)], [seq(GROUP_W)]
    mo_kinds, ro_kinds, ao_kinds = ['rows'], ['rows'], ['rows']
    a_lead = (n_sub,) if fuse else ()
    scratch = [pltpu.VMEM((N_HEADS, CHUNK, CHUNK), F32), pltpu.VMEM((CHUNK, LANES), F32),
               pltpu.VMEM((4 * N_HEADS, CHUNK), F32),
               pltpu.VMEM(a_lead + (t, GROUP_W), BF16), pltpu.VMEM(a_lead + (t, GROUP_W), BF16)]
    if has_state:
        assert not fuse
        m_in += list(st[0:2])
        m_specs += [_seq_layer_spec(a, l) for a in st[0:2]]
        r_in.append(st[2])
        r_specs.append(_seq_layer_spec(st[2], l))
        a_in += list(st[3:5])
        a_specs += [_seq_layer_spec(a, l) for a in st[3:5]]
        scratch += [pltpu.VMEM((t, GROUP_W), F32), pltpu.VMEM((t, GROUP_W), F32)]
    else:
        m_out += [jax.ShapeDtypeStruct((n_seq,) + head_state, F32),
                  jax.ShapeDtypeStruct((n_seq, 2, N_HEADS, HEAD_DIM, 1), F32),
                  jax.ShapeDtypeStruct((n_seq, nu, 1), F32)]
        m_ospecs += [per_seq(*head_state), per_seq(2, N_HEADS, HEAD_DIM, 1), per_seq(nu, 1)]
        mo_kinds += ['seq'] * 3
        r_out.append(jax.ShapeDtypeStruct((n_seq,) + head_state, F32))
        r_ospecs.append(per_seq(*head_state))
        ro_kinds.append('seq')
    a_in += [wp['a_lam'], wp['a_norm']]
    a_specs += [_layer_spec(wp['a_lam'], l), _layer_spec(wp['a_norm'], l)]
    a_kinds += ['shared'] * 2
    n_chunk = t // CHUNK
    call = functools.partial(pl.pallas_call, grid=(n_seq // n_sub,), compiler_params=_params(("arbitrary",)))
    if fuse:
        res = call(
            functools.partial(_mixers_kernel, n_chunk=n_chunk, t=t, has_state=has_state, lam_init=lam_init,
                              n_in=(len(m_in), len(r_in), len(a_in)), n_out=(len(m_out), len(r_out), len(a_out)),
                              kinds=tuple(m_kinds + r_kinds + a_kinds + mo_kinds + ro_kinds + ao_kinds),
                              n_sub=n_sub),
            in_specs=m_specs + r_specs + a_specs, out_specs=m_ospecs + r_ospecs + a_ospecs,
            out_shape=m_out + r_out + a_out, scratch_shapes=scratch, name="mixers",
        )(*m_in, *r_in, *a_in)
        return res[:len(m_out)], res[len(m_out):len(m_out) + len(r_out)], res[-1]
    m_res = call(functools.partial(_mlstm_kernel, n_chunk=n_chunk, has_state=has_state),
                 in_specs=m_specs, out_specs=m_ospecs, out_shape=m_out, name="mlstm")(*m_in)
    r_res = call(functools.partial(_ret_kernel, n_chunk=n_chunk, has_state=has_state),
                 in_specs=r_specs, out_specs=r_ospecs, out_shape=r_out, scratch_shapes=scratch[0:3],
                 name="retention")(*r_in)
    a_res = call(functools.partial(_attn_kernel, t=t, has_cache=has_state, lam_init=lam_init),
                 in_specs=a_specs, out_specs=a_ospecs, out_shape=a_out, scratch_shapes=scratch[3:],
                 name="diff_attention")(*a_in)
    return m_res, r_res, a_res[0]


def _cmul(ar, ai, br, bi):
    return ar * br - ai * bi, ar * bi + ai * br


def _lam_bar(lre, lim, dt):
    mag = jnp.exp(lre * dt)
    return mag * jnp.cos(lim * dt), mag * jnp.sin(lim * dt)


def _cpow_int(br, bi, e, n_bits):
    pr = jnp.ones(e.shape, F32)
    pi = jnp.zeros(e.shape, F32)
    for bit in range(n_bits):
        on = ((e >> bit) & 1) == 1
        qr, qi = _cmul(pr, pi, br, bi)
        pr, pi = jnp.where(on, qr, pr), jnp.where(on, qi, pi)
        if bit + 1 < n_bits:
            br, bi = _cmul(br, bi, br, bi)
    return pr, pi


def _zoh_coef(lre, lim, br, bi):
    den = lre * lre + lim * lim
    return ((br - 1.0) * lre + bi * lim) / den, (bi * lre - (br - 1.0) * lim) / den


def _toeplitz_rows(kall):
    n = S5_L * S5_GC
    lane = lax.broadcasted_iota(jnp.int32, (S5_GC, n), 1)
    pieces = []
    for s in range(S5_L):
        shifted = kall if s == 0 else pltpu.roll(kall, S5_GC * s, 1)
        pieces.append(jnp.where(lane >= S5_GC * s, shifted, 0.0))
    return jnp.concatenate(pieces, axis=0)


def _s5_prep_group(lam_c_re, lam_c_im, lam_r_re, lam_r_im, ldt_ref, b_t_re, b_t_im, b_re, b_im,
                   c_re, c_im, c_t_re, c_t_im, d_ref, t_ref, bcf_ref, bcb_ref, ccf_ref, ccb_ref, lam_ref):
    n = S5_L * S5_GC
    lane_blk = lax.broadcasted_iota(jnp.int32, (S5_P, n), 1) // S5_GC
    bits = S5_L.bit_length()

    def bases(d):
        dt = jnp.exp(ldt_ref[d])
        return (_lam_bar(lam_c_re[d], lam_c_im[d], dt), _lam_bar(lam_r_re[d], lam_r_im[d], dt))

    def tile_rows(x):
        return jnp.concatenate([x] * S5_L, axis=0)

    def rows_pow(br, bi, reverse):
        s_idx = lax.broadcasted_iota(jnp.int32, (S5_L, S5_P), 0)
        pr, pi = _cpow_int(br, bi, S5_L - 1 - s_idx if reverse else s_idx, bits)
        rep = lambda p: jnp.concatenate(
            [jnp.broadcast_to(p[s:s + 1, :], (S5_GC, S5_P)) for s in range(S5_L)], axis=0)
        return rep(pr), rep(pi)

    def tile_lanes(ref, d):
        return jnp.concatenate([ref[d]] * S5_L, axis=1)

    ct_re, ct_im = [tile_lanes(c_t_re, d) for d in range(2)], [tile_lanes(c_t_im, d) for d in range(2)]

    (cbr, cbi), (rbr, rbi) = bases(0)
    yield
    kr, ki = _zoh_coef(lam_r_re[0], lam_r_im[0], rbr, rbi)
    bbt_re, bbt_im = _cmul(kr, ki, b_t_re[0], b_t_im[0])
    pr, pi = _cpow_int(cbr, cbi, lane_blk, bits)
    yield
    cl_re, cl_im = _cmul(ct_re[0], ct_im[0], pr, pi)
    yield
    t_f = _toeplitz_rows(_dot(bbt_re, cl_re, HI) - _dot(bbt_im, cl_im, HI))
    yield
    pr, pi = rows_pow(rbr, rbi, True)
    yield
    re, im = _cmul(tile_rows(bbt_re), tile_rows(bbt_im), pr, pi)
    bcf_ref[...] = jnp.concatenate([re, im], axis=1).astype(BF16)
    re, im = _cmul(cl_re, cl_im, cbr, cbi)
    ccf_ref[...] = jnp.concatenate([re, -im], axis=0).astype(BF16)
    pr, pi = _cpow_int(rbr, rbi, jnp.full((1, S5_P), S5_L, jnp.int32), bits)
    yield
    lam_ref[0:1, :] = jnp.concatenate([pr, pr], axis=1)
    lam_ref[1:2, :] = jnp.concatenate([-pi, pi], axis=1)

    (cbr, cbi), (rbr, rbi) = bases(1)
    yield
    kr, ki = _zoh_coef(lam_c_re[1], lam_c_im[1], cbr, cbi)
    bb_re, bb_im = _cmul(kr, ki, tile_lanes(b_re, 1), tile_lanes(b_im, 1))
    pr, pi = _cpow_int(cbr, cbi, lane_blk, bits)
    yield
    bl_re, bl_im = _cmul(bb_re, bb_im, pr, pi)
    yield
    t_b = _toeplitz_rows(_dot(c_re[1], bl_re, HI) - _dot(c_im[1], bl_im, HI)).T
    yield
    kr, ki = _zoh_coef(lam_r_re[1], lam_r_im[1], rbr, rbi)
    bbt_re, bbt_im = _cmul(kr, ki, b_t_re[1], b_t_im[1])
    pr, pi = rows_pow(rbr, rbi, False)
    yield
    re, im = _cmul(tile_rows(bbt_re), tile_rows(bbt_im), pr, pi)
    bcb_ref[...] = jnp.concatenate([re, im], axis=1).astype(BF16)
    pr, pi = _cpow_int(cbr, cbi, S5_L - lane_blk, bits)
    yield
    re, im = _cmul(ct_re[1], ct_im[1], pr, pi)
    ccb_ref[...] = jnp.concatenate([re, -im], axis=0).astype(BF16)
    pr, pi = _cpow_int(rbr, rbi, jnp.full((1, S5_P), S5_L, jnp.int32), bits)
    yield
    lam_ref[2:3, :] = jnp.concatenate([pr, pr], axis=1)
    lam_ref[3:4, :] = jnp.concatenate([-pi, pi], axis=1)
    lam_ref[4:8, :] = jnp.zeros((4, 2 * S5_P), F32)

    eye = (lax.broadcasted_iota(jnp.int32, (n, n), 0) == lax.broadcasted_iota(jnp.int32, (n, n), 1))
    d_diag = jnp.concatenate([d_ref[...]] * S5_L, axis=1)
    t_ref[...] = (t_f + t_b + jnp.where(eye, d_diag, 0.0)).astype(BF16)


def _s5_prep_kernel(*refs, n_grp):
    ins, outs = refs[:14], refs[14:]
    gens = [_s5_prep_group(*[r.at[:, gi] for r in ins[:13]], ins[13].at[gi], *[r.at[gi] for r in outs])
            for gi in range(n_grp)]
    while gens:
        for g in list(gens):
            if next(g, True):
                gens.remove(g)


def _s5_prep(lam_re, lam_im, log_dt, b_re, b_im, c_re, c_im, s5_d):
    n = S5_L * S5_GC
    swap = lambda a: jnp.swapaxes(a, -1, -2)
    ins = [lam_re[..., :, None], lam_im[..., :, None], lam_re[..., None, :], lam_im[..., None, :],
           log_dt[..., None, None], swap(b_re), swap(b_im), b_re, b_im, c_re, c_im, swap(c_re), swap(c_im)]
    d_grp = s5_d.reshape(DEPTH, S5_G, 1, S5_GC)

    n_grp = 4

    def dir_spec(a):
        return pl.BlockSpec((None, 2, n_grp) + a.shape[3:], lambda l, g: (l, 0, g, 0, 0))

    def out(rows, cols, dtype):
        return (pl.BlockSpec((None, n_grp, rows, cols), lambda l, g: (l, g, 0, 0)),
                jax.ShapeDtypeStruct((DEPTH, S5_G, rows, cols), dtype))

    outs = [out(n, n, BF16), out(n, 2 * S5_P, BF16), out(n, 2 * S5_P, BF16),
            out(2 * S5_P, n, BF16), out(2 * S5_P, n, BF16), out(8, 2 * S5_P, F32)]
    return pl.pallas_call(
        functools.partial(_s5_prep_kernel, n_grp=n_grp),
        grid=(DEPTH, S5_G // n_grp),
        in_specs=[dir_spec(a) for a in ins] + [pl.BlockSpec((None, n_grp, 1, S5_GC), lambda l, g: (l, g, 0, 0))],
        out_specs=[o[0] for o in outs], out_shape=[o[1] for o in outs],
        compiler_params=_params(("arbitrary", "arbitrary")), name="s5_prep",
    )(*ins, d_grp)


def _block_transpose(a):
    n = len(a)
    blk = lax.broadcasted_iota(jnp.int32, (1, a[0].shape[1]), 1) // S5_GC
    a = list(a)
    bit = 1
    while bit < n:
        upper = (blk & bit) != 0
        for i in range(n):
            if i & bit == 0:
                j = i | bit
                lo, hi = a[i], a[j]
                a[i] = jnp.where(upper, pltpu.roll(hi, bit * S5_GC, 1), lo)
                a[j] = jnp.where(upper, hi, pltpu.roll(lo, LANES - bit * S5_GC, 1))
        bit *= 2
    return a


def _s5_kernel(*refs, n_seq, n_k, has_state):
    if has_state:
        (u_ref, t_ref, bcf_ref, bcb_ref, ccf_ref, ccb_ref, lam_ref, x0_ref,
         y_ref, ug_ref, yg_ref, inj_f, inj_b, inj_s, xin_f, xin_b) = refs
    else:
        (u_ref, t_ref, bcf_ref, bcb_ref, ccf_ref, ccb_ref, lam_ref,
         y_ref, xf_ref, xb_ref, ug_ref, yg_ref, inj_f, inj_b, inj_s, xin_f, xin_b) = refs
    ng = S5_G // 2
    r = n_seq * n_k
    tok = lambda i: pl.ds(i, r, stride=S5_L)
    u_lo = _block_transpose([u_ref[tok(i), :] for i in range(ng)])
    u_hi = _block_transpose([u_ref[tok(i), :] for i in range(ng, S5_L)])
    for g in range(ng):
        ug_ref[g] = jnp.concatenate([u_lo[g], u_hi[g]], axis=1).astype(BF16)
        inj_f[g] = _dot(ug_ref[g], bcf_ref[g])
        inj_b[g] = _dot(ug_ref[g], bcb_ref[g])

    for d, (inj, xin) in enumerate(((inj_f, xin_f), (inj_b, xin_b))):
        a = [lam_ref[g, 2 * d:2 * d + 1, :] for g in range(ng)]
        bs = [lam_ref[g, 2 * d + 1:2 * d + 2, :] for g in range(ng)]
        for g in range(ng):
            inj_s[g] = pltpu.roll(inj[g], S5_P, 1)
        x = [x0_ref[d, g] if has_state else jnp.zeros((n_seq, 2 * S5_P), F32) for g in range(ng)]
        xs = [pltpu.roll(v, S5_P, 1) for v in x]
        for k in (range(n_k) if d == 0 else range(n_k - 1, -1, -1)):
            rows = pl.ds(k, n_seq, stride=n_k)
            for g in range(ng):
                xin[g, rows, :] = x[g]
                x[g], xs[g] = (a[g] * x[g] + bs[g] * xs[g] + inj[g, rows, :],
                               a[g] * xs[g] - bs[g] * x[g] + inj_s[g, rows, :])
        if not has_state:
            for g in range(ng):
                (xf_ref if d == 0 else xb_ref)[g] = x[g]

    for g in range(ng):
        yg_ref[g] = (_dot(ug_ref[g], t_ref[g]) + _dot(xin_f[g].astype(BF16), ccf_ref[g])
                     + _dot(xin_b[g].astype(BF16), ccb_ref[g]))
    for half in range(S5_L // ng):
        y_i = _block_transpose([yg_ref[g, :, LANES * half:LANES * (half + 1)] for g in range(ng)])
        for i in range(ng):
            y_ref[tok(ng * half + i), :] = y_i[i]


def _s5(u_halves, ops, l, n_seq, n_k, x0):
    has_state = x0 is not None
    n = S5_L * S5_GC
    r = n_seq * n_k
    w = 2 * S5_P
    ng = S5_G // 2
    half_spec = pl.BlockSpec((None, r * S5_L, LANES), lambda j: (j, 0, 0))
    op_spec = lambda a: pl.BlockSpec((None, ng) + a.shape[2:], lambda j: (l, j) + (0,) * (a.ndim - 2))
    args = [u_halves] + list(ops)
    in_specs = [half_spec] + [op_spec(a) for a in ops]
    if has_state:
        args.append(x0)
        in_specs.append(pl.BlockSpec((None, 2, ng, n_seq, w), lambda j: (l, 0, j, 0, 0)))
    out_shape = [jax.ShapeDtypeStruct((2, r * S5_L, LANES), F32)]
    out_specs = [half_spec]
    if not has_state:
        out_shape += [jax.ShapeDtypeStruct((S5_G, n_seq, w), F32)] * 2
        out_specs += [pl.BlockSpec((ng, n_seq, w), lambda j: (j, 0, 0))] * 2
    return pl.pallas_call(
        functools.partial(_s5_kernel, n_seq=n_seq, n_k=n_k, has_state=has_state),
        grid=(2,), in_specs=in_specs, out_specs=out_specs, out_shape=out_shape,
        scratch_shapes=[pltpu.VMEM((ng, r, n), BF16), pltpu.VMEM((ng, r, n), F32)]
        + [pltpu.VMEM((ng, r, w), F32)] * 5,
        compiler_params=_params(("arbitrary",)), name="s5_scan",
    )(*args)


def _out_ffn_kernel(x_ref, m_ref, r_ref, a_ref, ya_ref, yb_ref, mod_ref, wglu_ref, bglu_ref, wout_ref,
                    n_post_ref, n_pre_ref, n_fpost_ref, wg_ref, wu_ref, wd_ref, o_ref):
    def mod(i):
        return mod_ref[:, i * D_MODEL:(i + 1) * D_MODEL]

    y = jnp.concatenate([ya_ref[...], yb_ref[...]], axis=1)
    gs = 0.5 * y * (1.0 + jnp.tanh(math.sqrt(2.0 / math.pi) * (y + 0.044715 * (y * y * y))))
    s_out = gs * jax.nn.sigmoid(_dot(gs.astype(BF16), wglu_ref[...]) + bglu_ref[...])
    mixed = jnp.concatenate([m_ref[...], r_ref[...], a_ref[...], s_out], axis=1).astype(BF16)
    halves = [slice(0, ROW_TILE // 2), slice(ROW_TILE // 2, ROW_TILE)]
    x1 = [x_ref[r, :] + mod(2) * _rms(_dot(mixed[r], wout_ref[...]), n_post_ref[...]) for r in halves]
    h = [(_rms(v, n_pre_ref[...]) * (1.0 + mod(4)) + mod(3)).astype(BF16) for v in x1]
    g = [_dot(v, wg_ref[...]) for v in h]
    u = [_dot(v, wu_ref[...]) for v in h]
    act = [(_silu(a) * b).astype(BF16) for a, b in zip(g, u)]
    f = [_dot(v, wd_ref[...]) for v in act]
    for r, v, ff in zip(halves, x1, f):
        o_ref[r, :] = v + mod(5) * _rms(ff, n_fpost_ref[...])


def _out_ffn(x, m_out, r_out, a_out, y_halves, mod, l, mod_row, wp):
    n = x.shape[0]
    row = lambda w: pl.BlockSpec((ROW_TILE, w), lambda i: (i, 0))
    half = lambda j: pl.BlockSpec((None, ROW_TILE, LANES), lambda i: (j, i, 0))
    params = [wp[k] for k in ('w_glu', 'b_glu', 'w_out', 'n_mix_post', 'n_ffn_pre', 'n_ffn_post',
                              'w_gate', 'w_up', 'w_down')]
    return pl.pallas_call(
        _out_ffn_kernel,
        grid=(n // ROW_TILE,),
        in_specs=[row(D_MODEL), row(GROUP_W), row(GROUP_W), row(GROUP_W), half(0), half(1),
                  _mod_spec(l, mod_row)] + [_layer_spec(a, l, pipeline_mode=pl.Buffered(1)) for a in params],
        out_specs=row(D_MODEL),
        out_shape=jax.ShapeDtypeStruct((n, D_MODEL), F32),
        compiler_params=_params(("arbitrary",)), name="out_ffn",
    )(x, m_out, r_out, a_out, y_halves, y_halves, mod, *params)


def _layer(x, l, n_seq, t, mod, mod_row, wp, s5_ops, lam_init, st):
    ctx = st is None
    res = _in_proj(x, mod, l, mod_row, wp['n_mix_pre'], wp['w_in'], emit_kv=ctx)
    z, gates = res[0], res[2]
    m_res, r_res, a_out = _mixers(z, gates, wp, l, n_seq, t, None if ctx else st[0:5], lam_init, fuse=ctx)
    s_res = _s5(res[1], s5_ops, l, n_seq, t // S5_L, None if ctx else st[5])
    x_new = _out_ffn(x, m_res[0], r_res[0], a_out, s_res[0], mod, l, mod_row, wp)
    if not ctx:
        return x_new, None
    return x_new, (m_res[1], m_res[2], m_res[3], r_res[1], res[3], res[4], s_res[1], s_res[2])


def _pad_to(a, rows, cols=LANES):
    pad = [(0, 0)] * (a.ndim - 2) + [(0, rows - a.shape[-2]), (0, cols - a.shape[-1])]
    return jnp.pad(a, pad)


def kernel(x_prompt, x_sample, state_mlstm_C, state_mlstm_n, state_mlstm_m, state_ret, cache_diff_k, cache_diff_v, state_s5_re, state_s5_im, c, c_ctx, w_ada, b_ada, n_mix_pre, n_mix_post, n_ffn_pre, n_ffn_post, w_in, w_out, m_gate_bias, m_norm, r_decay_logit, r_norm, a_lam_q1, a_lam_k1, a_lam_q2, a_lam_k2, a_norm, s5_lam_re, s5_lam_im, s5_log_dt, s5_b_re, s5_b_im, s5_c_re, s5_c_im, s5_d, s5_w_glu, s5_b_glu, w_ffn_gate, w_ffn_up, w_ffn_down):
    n_ctx, t_ctx, _ = x_prompt.shape
    n_lat, t_lat, _ = x_sample.shape
    past = cache_diff_k.shape[2]

    cond8 = jnp.concatenate([c, c_ctx[None, :], jnp.zeros((8 - n_lat - 1, D_MODEL), F32)], axis=0)
    mod = _ada(cond8, w_ada, b_ada).reshape(DEPTH, 8, 1, 6 * D_MODEL)
    s5_ops = _s5_prep(s5_lam_re, s5_lam_im, s5_log_dt, s5_b_re, s5_b_im, s5_c_re, s5_c_im, s5_d)

    row = lambda a: a[:, None, :]
    wp = dict(
        w_in=jnp.pad(w_in.astype(BF16), ((0, 0), (0, 0), (0, N_MAIN + LANES - w_in.shape[-1]))), w_out=w_out.astype(BF16), w_glu=s5_w_glu.astype(BF16), b_glu=row(s5_b_glu),
        w_gate=w_ffn_gate.astype(BF16), w_up=w_ffn_up.astype(BF16), w_down=w_ffn_down.astype(BF16),
        n_mix_pre=row(n_mix_pre), n_mix_post=row(n_mix_post), n_ffn_pre=row(n_ffn_pre),
        n_ffn_post=row(n_ffn_post), m_norm=row(m_norm), r_norm=row(r_norm), a_norm=row(a_norm),
        m_bias=_pad_to(row(m_gate_bias), 1), r_decay=_pad_to(r_decay_logit, 8),
        a_lam=_pad_to(jnp.stack([a_lam_q1, a_lam_k1, a_lam_q2, a_lam_k2], axis=1), 8))
    lam_inits = [0.8 - 0.6 * math.exp(-0.3 * l) for l in range(DEPTH)]

    x = x_prompt.reshape(n_ctx * t_ctx, D_MODEL)
    new_states = []
    for l in range(DEPTH):
        x, st = _layer(x, l, n_ctx, t_ctx, mod, lambda i: n_lat, wp, s5_ops, lam_inits[l], None)
        new_states.append(st)
    y_prompt = x.reshape(n_ctx, t_ctx, D_MODEL)

    x0 = jnp.stack([state_s5_re, state_s5_im], axis=-2)
    x0 = x0.transpose(1, 2, 3, 0, 4, 5).reshape(DEPTH, 2, S5_G, n_lat, 2 * S5_P)
    s_ext0 = jnp.concatenate([state_mlstm_C, state_mlstm_n[..., None],
                              jnp.zeros(state_mlstm_n.shape + (LANES - HEAD_DIM - 1,), F32)], axis=-1)
    st = (s_ext0, state_mlstm_m.reshape(n_lat, DEPTH, 2 * N_HEADS, 1), state_ret,
          cache_diff_k.reshape(n_lat, DEPTH, past, GROUP_W), cache_diff_v.reshape(n_lat, DEPTH, past, GROUP_W), x0)
    x = x_sample.reshape(n_lat * t_lat, D_MODEL)
    tiles_per_seq = t_lat // ROW_TILE
    for l in range(DEPTH):
        x, _ = _layer(x, l, n_lat, t_lat, mod, lambda i: i // tiles_per_seq, wp, s5_ops, lam_inits[l], st)
    y_sample = x.reshape(n_lat, t_lat, D_MODEL)

    stack = lambda i: jnp.stack([s[i] for s in new_states], axis=1)
    kv = lambda i: jnp.stack([s[i].reshape(n_ctx, t_ctx, N_HEADS, HEAD_DIM) for s in new_states], axis=1)
    xs = jnp.stack([stack(6), stack(7)], axis=2)
    xs = xs.reshape(S5_G, DEPTH, 2, n_ctx, 2, S5_P).transpose(3, 1, 2, 0, 4, 5)
    return (y_prompt, y_sample, stack(0), stack(1)[..., 0], stack(2).reshape(n_ctx, DEPTH, 2, N_HEADS),
            stack(3), kv(4), kv(5), xs[..., 0, :], xs[..., 1, :])
```
